```python
import math
import jax
import jax.numpy as jnp
from jax import lax
import numpy as np


D_MODEL = 2048
BATCH = 8
SEQ = 4096
DEPTH = 4

CTX_LEN = 256
GRID_W = 64
N_MIXERS = 3
N_ML_LAYERS = (DEPTH + 2) // 3
N_WA_LAYERS = (DEPTH + 1) // 3
N_S5_LAYERS = DEPTH // 3
N_MOD = 9
D_FF = 5632
EPS = 1e-6
NEG_BIG = -1e30
ML_HEADS = 8
ML_HEAD_DIM = D_MODEL // ML_HEADS
ML_CHUNK = 64
WA_Q_HEADS = 16
WA_KV_HEADS = 4
WA_HEAD_DIM = D_MODEL // WA_Q_HEADS
WA_WINDOW = 128
WA_BLOCK = 128
ROPE_BASE = 10000.0
S5_GROUP = 16
S5_GROUPS = D_MODEL // S5_GROUP
S5_STATE = 64
S5_CHUNK = 128
S5_DT_MIN = 1e-3
S5_DT_MAX = 1e-1

kernel_name = 'hybrid_mlstm_swa_s5_macaron_dit'


def _rmsnorm(x, g):
    xf = x.astype(jnp.float32)
    y = xf * lax.rsqrt(jnp.mean(xf * xf, axis=-1, keepdims=True) + EPS)
    return (y * g.astype(jnp.float32)).astype(x.dtype)


def _adaln(x, g, shift, scale):
    return _rmsnorm(x, g) * (1 + scale) + shift


def _swiglu(h, w_in, w_out):
    a, gate = jnp.split(h @ w_in, 2, axis=-1)
    return (a * jax.nn.silu(gate)) @ w_out


def _flip(t, axis, rev):
    return jnp.flip(t, axis) if rev else t


def _axial_rope(n):
    rows = n // GRID_W
    row = jnp.repeat(jnp.arange(rows, dtype=jnp.float32), GRID_W)
    col = jnp.tile(jnp.arange(GRID_W, dtype=jnp.float32), rows)
    n_freq = WA_HEAD_DIM // 4
    inv = ROPE_BASE ** (-jnp.arange(n_freq, dtype=jnp.float32) / n_freq)
    ang = jnp.concatenate([row[:, None] * inv, col[:, None] * inv], axis=-1)
    return jnp.cos(ang), jnp.sin(ang)


def _rope(x, cos, sin):
    half = x.shape[-1] // 2
    shp = (1, x.shape[1]) + (1,) * (x.ndim - 3) + (half,)
    cs, sn = cos.reshape(shp), sin.reshape(shp)
    xf = x.astype(jnp.float32)
    x1, x2 = xf[..., :half], xf[..., half:]
    return jnp.concatenate([x1 * cs - x2 * sn, x2 * cs + x1 * sn], axis=-1).astype(x.dtype)


def _mlstm_scan(q, k, v, logi, logf, state, with_out):
    bsz, nh, length, _ = k.shape
    nc = length // ML_CHUNK

    def chunks(t):
        t = t.reshape(t.shape[:2] + (nc, ML_CHUNK) + t.shape[3:])
        return jnp.moveaxis(t, 2, 0)

    lower = jnp.tril(jnp.ones((ML_CHUNK, ML_CHUNK), dtype=bool))

    def body(carry, xs):
        c_mem, n_mem, m_run = carry
        qc, kc, vc, ic, fc = xs
        b = jnp.cumsum(fc, axis=-1)
        b_end = b[..., -1]
        g = b_end[..., None] - b + ic
        m_new = jnp.maximum(b_end + m_run, jnp.max(g, axis=-1))
        w = jnp.exp(g - m_new[..., None])
        dec = jnp.exp(b_end + m_run - m_new)
        c_new = dec[..., None, None] * c_mem + jnp.einsum('bhs,bhsd,bhsv->bhdv', w, kc, vc)
        n_new = dec[..., None] * n_mem + jnp.einsum('bhs,bhsd->bhd', w, kc)
        if not with_out:
            return (c_new, n_new, m_new), None
        d_log = jnp.where(lower, b[..., :, None] - b[..., None, :] + ic[..., None, :], NEG_BIG)
        m_prev = b + m_run[..., None]
        m_q = jnp.maximum(m_prev, jnp.max(d_log, axis=-1))
        s = jnp.einsum('bhld,bhsd->bhls', qc, kc) * jnp.exp(d_log - m_q[..., None])
        dq = jnp.exp(m_prev - m_q)
        num = jnp.einsum('bhls,bhsv->bhlv', s, vc) + dq[..., None] * jnp.einsum('bhld,bhdv->bhlv', qc, c_mem)
        den = jnp.sum(s, axis=-1) + dq * jnp.einsum('bhld,bhd->bhl', qc, n_mem)
        h = num / jnp.maximum(jnp.abs(den), jnp.exp(-m_q))[..., None]
        return (c_new, n_new, m_new), h

    xs = (chunks(q) if with_out else None, chunks(k), chunks(v), chunks(logi), chunks(logf))
    state, hs = lax.scan(body, state, xs)
    if not with_out:
        return None, state
    return jnp.moveaxis(hs, 0, 2).reshape(bsz, nh, length, -1), state


def _mlstm_mixer(h_lat, h_ctx, w_in, b_gate, g_head, w_out, with_ctx):
    dm = D_MODEL

    def heads(t):
        bsz, length, _ = t.shape
        return jnp.moveaxis(t.reshape(bsz, length, ML_HEADS, ML_HEAD_DIM), 2, 1).astype(jnp.float32)

    def gates(z):
        bsz, length, _ = z.shape
        gt = (z.astype(jnp.float32) + b_gate.astype(jnp.float32)).reshape(bsz, length, 4, ML_HEADS)
        gt = jnp.moveaxis(gt, 1, -1)
        return gt[:, 0], jax.nn.log_sigmoid(gt[:, 1]), gt[:, 2], jax.nn.log_sigmoid(gt[:, 3])

    def project(h, with_q):
        if with_q:
            q, k, v, o, g = jnp.split(h @ w_in, [dm, 2 * dm, 3 * dm, 4 * dm], axis=-1)
            q = heads(q) * ML_HEAD_DIM ** -0.5
        else:
            k, v = jnp.split(h @ w_in[:, dm:3 * dm], 2, axis=-1)
            g = h @ w_in[:, 4 * dm:]
            q, o = None, None
        return q, heads(k), heads(v), o, gates(g)

    ql, kl, vl, ol, (il_f, fl_f, il_b, fl_b) = project(h_lat, True)
    qc, kc, vc, oc, (ic_f, fc_f, ic_b, fc_b) = project(h_ctx, with_ctx)
    bsz = h_lat.shape[0]
    zero = (jnp.zeros((bsz, ML_HEADS, ML_HEAD_DIM, ML_HEAD_DIM), jnp.float32),
            jnp.zeros((bsz, ML_HEADS, ML_HEAD_DIM), jnp.float32),
            jnp.full((bsz, ML_HEADS), NEG_BIG, jnp.float32))

    def rev(t):
        return None if t is None else jnp.flip(t, axis=2)

    hc_f, st_f = _mlstm_scan(qc, kc, vc, ic_f, fc_f, zero, with_ctx)
    hc_b, st_b = _mlstm_scan(rev(qc), rev(kc), rev(vc), rev(ic_b), rev(fc_b), zero, with_ctx)
    hl_f, _ = _mlstm_scan(ql, kl, vl, il_f, fl_f, st_f, True)
    hl_b, _ = _mlstm_scan(rev(ql), rev(kl), rev(vl), rev(il_b), rev(fl_b), st_b, True)

    def merge(h_f, h_b, o):
        hs = h_f + rev(h_b)
        hs = hs * lax.rsqrt(jnp.mean(hs * hs, axis=-1, keepdims=True) + EPS)
        bsz_, _, length, _ = hs.shape
        hs = jnp.moveaxis(hs, 1, 2).reshape(bsz_, length, dm) * g_head.astype(jnp.float32)
        return (jax.nn.sigmoid(o.astype(jnp.float32)) * hs).astype(o.dtype) @ w_out

    y_ctx = merge(hc_f, hc_b, oc) if with_ctx else None
    return merge(hl_f, hl_b, ol), y_ctx


def _sink_softmax(s, sink):
    col = jnp.broadcast_to(sink[None, :, :, None, None], s.shape[:-1] + (1,))
    return jax.nn.softmax(jnp.concatenate([s, col], axis=-1), axis=-1)[..., :-1]


def _window_gqa_mixer(h_lat, h_ctx, w_in, sink, w_out, cos, sin, with_ctx):
    bsz, n, _ = h_lat.shape
    grp = WA_Q_HEADS // WA_KV_HEADS
    qd = WA_Q_HEADS * WA_HEAD_DIM
    kd = WA_KV_HEADS * WA_HEAD_DIM
    scale = WA_HEAD_DIM ** -0.5
    span = 3 * WA_BLOCK
    sink = sink.astype(jnp.float32).reshape(WA_KV_HEADS, grp)

    def split_q(t):
        return t.reshape(t.shape[0], t.shape[1], WA_KV_HEADS, grp, WA_HEAD_DIM)

    def split_kv(t):
        return t.reshape(t.shape[0], t.shape[1], WA_KV_HEADS, WA_HEAD_DIM)

    q, k, v = jnp.split(h_lat @ w_in, [qd, qd + kd], axis=-1)
    q = _rope(split_q(q), cos, sin)
    k = _rope(split_kv(k), cos, sin)
    v = split_kv(v)
    k_ctx, v_ctx = jnp.split(h_ctx @ w_in[:, qd:], 2, axis=-1)
    k_ctx, v_ctx = split_kv(k_ctx), split_kv(v_ctx)
    pad = ((0, 0), (WA_BLOCK, WA_BLOCK), (0, 0), (0, 0))
    k_pad, v_pad = jnp.pad(k, pad), jnp.pad(v, pad)

    def block(qb):
        start = qb * WA_BLOCK
        q_blk = lax.dynamic_slice_in_dim(q, start, WA_BLOCK, axis=1)
        k_blk = lax.dynamic_slice_in_dim(k_pad, start, span, axis=1)
        v_blk = lax.dynamic_slice_in_dim(v_pad, start, span, axis=1)
        q_pos = start + jnp.arange(WA_BLOCK)
        k_pos = start - WA_BLOCK + jnp.arange(span)
        ok = (jnp.abs(q_pos[:, None] - k_pos[None, :]) <= WA_WINDOW) & (k_pos >= 0) & (k_pos < n)
        s_loc = jnp.einsum('bqkgd,bskd->bkgqs', q_blk, k_blk, preferred_element_type=jnp.float32) * scale
        s_loc = jnp.where(ok, s_loc, NEG_BIG)
        s_ctx = jnp.einsum('bqkgd,bskd->bkgqs', q_blk, k_ctx, preferred_element_type=jnp.float32) * scale
        p = _sink_softmax(jnp.concatenate([s_loc, s_ctx], axis=-1), sink).astype(v.dtype)
        return (jnp.einsum('bkgqs,bskd->bqkgd', p[..., :span], v_blk)
                + jnp.einsum('bkgqs,bskd->bqkgd', p[..., span:], v_ctx))

    y = lax.map(block, jnp.arange(n // WA_BLOCK))
    y = jnp.moveaxis(y, 0, 1).reshape(bsz, n, D_MODEL) @ w_out
    y_ctx = None
    if with_ctx:
        q_c = split_q(h_ctx @ w_in[:, :qd])
        s = jnp.einsum('bqkgd,bskd->bkgqs', q_c, k_ctx, preferred_element_type=jnp.float32) * scale
        p = _sink_softmax(s, sink).astype(v_ctx.dtype)
        y_ctx = jnp.einsum('bkgqs,bskd->bqkgd', p, v_ctx).reshape(bsz, h_ctx.shape[1], D_MODEL) @ w_out
    return y, y_ctx


def _s5_discretize(lam_re, lam_im, log_dt, b_re, b_im):
    lam_re, lam_im = lam_re.astype(jnp.float32), lam_im.astype(jnp.float32)
    b_re, b_im = b_re.astype(jnp.float32), b_im.astype(jnp.float32)
    dt = jnp.exp(log_dt.astype(jnp.float32))[:, None]
    mag = jnp.exp(lam_re * dt)
    lb_re, lb_im = mag * jnp.cos(lam_im * dt), mag * jnp.sin(lam_im * dt)
    den = lam_re * lam_re + lam_im * lam_im
    nr, ni = lb_re - 1.0, lb_im
    fr = (nr * lam_re + ni * lam_im) / den
    fi = (ni * lam_re - nr * lam_im) / den
    bb_re = fr[..., None] * b_re - fi[..., None] * b_im
    bb_im = fr[..., None] * b_im + fi[..., None] * b_re
    return lb_re, lb_im, bb_re, bb_im


def _s5_scan(u, lb_re, lb_im, bb_re, bb_im, c_re, c_im, state, with_out):
    length, bsz = u.shape[:2]
    nc = length // S5_CHUNK
    a_re = jnp.broadcast_to(lb_re, (S5_CHUNK, bsz) + lb_re.shape)
    a_im = jnp.broadcast_to(lb_im, (S5_CHUNK, bsz) + lb_im.shape)

    def combine(e1, e2):
        a1r, a1i, b1r, b1i = e1
        a2r, a2i, b2r, b2i = e2
        return (a2r * a1r - a2i * a1i, a2r * a1i + a2i * a1r,
                a2r * b1r - a2i * b1i + b2r, a2r * b1i + a2i * b1r + b2i)

    def body(carry, uk):
        s_re, s_im = carry
        bu_re = jnp.einsum('lbgc,gpc->lbgp', uk, bb_re)
        bu_im = jnp.einsum('lbgc,gpc->lbgp', uk, bb_im)
        bu_re = bu_re.at[0].add(lb_re * s_re - lb_im * s_im)
        bu_im = bu_im.at[0].add(lb_re * s_im + lb_im * s_re)
        _, _, x_re, x_im = lax.associative_scan(combine, (a_re, a_im, bu_re, bu_im), axis=0)
        new = (x_re[-1], x_im[-1])
        if not with_out:
            return new, None
        y = jnp.einsum('lbgp,gcp->lbgc', x_re, c_re) - jnp.einsum('lbgp,gcp->lbgc', x_im, c_im)
        return new, y

    state, ys = lax.scan(body, state, u.reshape((nc, S5_CHUNK) + u.shape[1:]))
    if not with_out:
        return None, state
    return ys.reshape(u.shape), state


def _s5_mixer(h_lat, h_ctx, w_in, lam_re, lam_im, log_dt, b_re, b_im, c_re, c_im, d_skip, w_out, with_ctx):
    def to_groups(u):
        bsz_, length, _ = u.shape
        return jnp.moveaxis(u.astype(jnp.float32).reshape(bsz_, length, S5_GROUPS, S5_GROUP), 1, 0)

    u_lat = h_lat @ w_in
    u_ctx = h_ctx @ w_in
    t_lat, t_ctx = to_groups(u_lat), to_groups(u_ctx)
    bsz = h_lat.shape[0]
    zero = (jnp.zeros((bsz, S5_GROUPS, S5_STATE), jnp.float32),) * 2
    y_lat, y_ctx = [], []
    for d in range(2):
        rev = d == 1
        lb_re, lb_im, bb_re, bb_im = _s5_discretize(lam_re[d], lam_im[d], log_dt[d], b_re[d], b_im[d])
        cr, ci = c_re[d].astype(jnp.float32), c_im[d].astype(jnp.float32)
        yc, s_end = _s5_scan(_flip(t_ctx, 0, rev), lb_re, lb_im, bb_re, bb_im, cr, ci, zero, with_ctx)
        yl, _ = _s5_scan(_flip(t_lat, 0, rev), lb_re, lb_im, bb_re, bb_im, cr, ci, s_end, True)
        y_lat.append(_flip(yl, 0, rev))
        if with_ctx:
            y_ctx.append(_flip(yc, 0, rev))

    def glu_out(ys, u):
        bsz_, length, _ = u.shape
        y = jnp.moveaxis(ys[0] + ys[1], 0, 1).reshape(bsz_, length, D_MODEL)
        y = y + d_skip.astype(jnp.float32) * u.astype(jnp.float32)
        z = jax.nn.gelu(y).astype(u.dtype)
        a, gate = jnp.split(z @ w_out, 2, axis=-1)
        return a * jax.nn.sigmoid(gate)

    return glu_out(y_lat, u_lat), (glu_out(y_ctx, u_ctx) if with_ctx else None)


def setup_inputs(seed: int = 0) -> dict:
    key = jax.random.key(seed)
    ks = list(jax.random.split(key, 32))

    def nrm(i, shape, std):
        return std * jax.random.normal(ks[i], shape, jnp.float32)

    dm = D_MODEL
    f_bias = jnp.linspace(3.0, 6.0, ML_HEADS, dtype=jnp.float32)
    ml_b_gate = jnp.concatenate([
        nrm(12, (N_ML_LAYERS, ML_HEADS), 0.1),
        f_bias + nrm(13, (N_ML_LAYERS, ML_HEADS), 0.1),
        nrm(14, (N_ML_LAYERS, ML_HEADS), 0.1),
        f_bias + nrm(15, (N_ML_LAYERS, ML_HEADS), 0.1)], axis=-1)
    s5_shape = (N_S5_LAYERS, 2, S5_GROUPS, S5_STATE)
    state_idx = jnp.arange(S5_STATE, dtype=jnp.float32)
    return {
        'x': nrm(0, (BATCH, SEQ, dm), 1.0),
        'c': nrm(1, (BATCH, dm), 1.0),
        'ctx': nrm(2, (BATCH, CTX_LEN, dm), 1.0),
        'c_ctx': nrm(3, (dm,), 1.0),
        'w_ada': nrm(4, (DEPTH, dm, N_MOD * dm), 0.5 * dm ** -0.5),
        'b_ada': nrm(5, (DEPTH, N_MOD * dm), 0.02),
        'g_norm': 1.0 + nrm(6, (DEPTH, 3, dm), 0.02),
        'w_ffn_in': nrm(7, (DEPTH, 2, dm, 2 * D_FF), dm ** -0.5),
        'w_ffn_out': nrm(8, (DEPTH, 2, D_FF, dm), D_FF ** -0.5),
        'g_final': 1.0 + nrm(9, (dm,), 0.02),
        'ml_w_in': nrm(10, (N_ML_LAYERS, dm, 4 * dm + 4 * ML_HEADS), dm ** -0.5),
        'ml_b_gate': ml_b_gate,
        'ml_g_head': 1.0 + nrm(11, (N_ML_LAYERS, dm), 0.02),
        'ml_w_out': nrm(16, (N_ML_LAYERS, dm, dm), dm ** -0.5),
        'wa_w_in': nrm(17, (N_WA_LAYERS, dm, (WA_Q_HEADS + 2 * WA_KV_HEADS) * WA_HEAD_DIM), dm ** -0.5),
        'wa_sink': nrm(18, (N_WA_LAYERS, WA_Q_HEADS), 0.5),
        'wa_w_out': nrm(19, (N_WA_LAYERS, dm, dm), dm ** -0.5),
        's5_w_in': nrm(20, (N_S5_LAYERS, dm, dm), dm ** -0.5),
        's5_lam_re': -0.5 + nrm(21, s5_shape, 0.01),
        's5_lam_im': math.pi * state_idx + nrm(22, s5_shape, 0.01),
        's5_log_dt': jax.random.uniform(ks[23], (N_S5_LAYERS, 2, S5_GROUPS), jnp.float32,
                                        math.log(S5_DT_MIN), math.log(S5_DT_MAX)),
        's5_b_re': nrm(24, s5_shape + (S5_GROUP,), (2 * S5_GROUP) ** -0.5),
        's5_b_im': nrm(25, s5_shape + (S5_GROUP,), (2 * S5_GROUP) ** -0.5),
        's5_c_re': nrm(26, (N_S5_LAYERS, 2, S5_GROUPS, S5_GROUP, S5_STATE), 0.5),
        's5_c_im': nrm(27, (N_S5_LAYERS, 2, S5_GROUPS, S5_GROUP, S5_STATE), 0.5),
        's5_d_skip': nrm(28, (N_S5_LAYERS, dm), 0.5),
        's5_w_out': nrm(29, (N_S5_LAYERS, dm, 2 * dm), dm ** -0.5),
    }


def reference(x, c, ctx, c_ctx, w_ada, b_ada, g_norm, w_ffn_in, w_ffn_out, g_final,
              ml_w_in, ml_b_gate, ml_g_head, ml_w_out,
              wa_w_in, wa_sink, wa_w_out,
              s5_w_in, s5_lam_re, s5_lam_im, s5_log_dt, s5_b_re, s5_b_im, s5_c_re, s5_c_im,
              s5_d_skip, s5_w_out):
    bsz, n, dm = x.shape
    cos, sin = _axial_rope(n)
    s_c = jax.nn.silu(c)
    s_cc = jax.nn.silu(c_ctx)
    h, hc = x, ctx
    for layer in range(DEPTH):
        has_next = layer < DEPTH - 1
        mod = (s_c @ w_ada[layer] + b_ada[layer]).reshape(bsz, N_MOD, 1, dm)
        mod_c = (s_cc @ w_ada[layer] + b_ada[layer]).reshape(N_MOD, dm)
        g = g_norm[layer]
        h = h + 0.5 * mod[:, 2] * _swiglu(_adaln(h, g[0], mod[:, 0], mod[:, 1]), w_ffn_in[layer, 0], w_ffn_out[layer, 0])
        hc = hc + 0.5 * mod_c[2] * _swiglu(_adaln(hc, g[0], mod_c[0], mod_c[1]), w_ffn_in[layer, 0], w_ffn_out[layer, 0])
        a_l = _adaln(h, g[1], mod[:, 3], mod[:, 4])
        a_c = _adaln(hc, g[1], mod_c[3], mod_c[4])
        kind, idx = layer % N_MIXERS, layer // N_MIXERS
        if kind == 0:
            y, y_c = _mlstm_mixer(a_l, a_c, ml_w_in[idx], ml_b_gate[idx], ml_g_head[idx], ml_w_out[idx], has_next)
        elif kind == 1:
            y, y_c = _window_gqa_mixer(a_l, a_c, wa_w_in[idx], wa_sink[idx], wa_w_out[idx], cos, sin, has_next)
        else:
            y, y_c = _s5_mixer(a_l, a_c, s5_w_in[idx], s5_lam_re[idx], s5_lam_im[idx], s5_log_dt[idx],
                               s5_b_re[idx], s5_b_im[idx], s5_c_re[idx], s5_c_im[idx],
                               s5_d_skip[idx], s5_w_out[idx], has_next)
        h = h + mod[:, 5] * y
        h = h + 0.5 * mod[:, 8] * _swiglu(_adaln(h, g[2], mod[:, 6], mod[:, 7]), w_ffn_in[layer, 1], w_ffn_out[layer, 1])
        if has_next:
            hc = hc + mod_c[5] * y_c
            hc = hc + 0.5 * mod_c[8] * _swiglu(_adaln(hc, g[2], mod_c[6], mod_c[7]), w_ffn_in[layer, 1], w_ffn_out[layer, 1])
    return _rmsnorm(h, g_final)
```

```python
import functools
import math

import jax
import jax.numpy as jnp
from jax import lax
from jax.experimental import pallas as pl
from jax.experimental.pallas import tpu as pltpu

F32 = jnp.float32
BF16 = jnp.bfloat16

EPS = 1e-6
NEG_BIG = -1e30
N_MOD = 9
ML_HEADS = 8
ML_CHUNK = 256
WA_Q_HEADS = 16
WA_KV_HEADS = 4
WA_WINDOW = 128
WA_BLOCK = 128
GRID_W = 64
ROPE_BASE = 10000.0
S5_GROUP = 16
S5_STATE = 64
S5_CHUNK = 16

LANES = 128
ROW_TILE = 512
VMEM_LIMIT = 56 * 1024 * 1024


def _cparams(*sem):
    return pltpu.CompilerParams(dimension_semantics=sem, vmem_limit_bytes=VMEM_LIMIT)


def _row_tile(rows):
    return ROW_TILE if rows % ROW_TILE == 0 else rows


def _col_tile(cols, want):
    t = min(want, cols)
    while cols % t:
        t -= LANES
    return t


def _adaln(x, g, shift, scale):
    var = jnp.mean(x * x, axis=-1, keepdims=True)
    return (x * lax.rsqrt(var + EPS) * g) * (1.0 + scale) + shift


def _mod_spec(k, d):
    return pl.BlockSpec((None, None, 1, d), lambda b, *_: (b, k, 0, 0))


def _mod_kernel(c_ref, w_ref, b_ref, o_ref):
    c = c_ref[...]
    s = (c * jax.nn.sigmoid(c)).astype(BF16)
    o_ref[...] = jnp.dot(s, w_ref[...].astype(BF16), preferred_element_type=F32) + b_ref[...]


def _mod_table(c_rows, w_ada, b_ada):
    depth, d, nd = w_ada.shape
    r = c_rows.shape[0]
    tn = _col_tile(nd, 1024)
    return pl.pallas_call(
        _mod_kernel,
        out_shape=jax.ShapeDtypeStruct((depth, r, nd), F32),
        grid=(depth, nd // tn),
        in_specs=[pl.BlockSpec((r, d), lambda l, n: (0, 0)),
                  pl.BlockSpec((None, d, tn), lambda l, n: (l, 0, n)),
                  pl.BlockSpec((None, 1, tn), lambda l, n: (l, 0, n))],
        out_specs=pl.BlockSpec((None, r, tn), lambda l, n: (l, 0, n)),
        compiler_params=_cparams("parallel", "parallel"),
        name="mod_table",
    )(c_rows, w_ada, b_ada.reshape(depth, 1, nd))


def _ffn_kernel(x_ref, g_ref, sh_ref, sc_ref, gate_ref, wa_ref, wg_ref, wo_ref, o_ref, hn_ref):
    j = pl.program_id(2)

    @pl.when(j == 0)
    def _():
        hn_ref[...] = _adaln(x_ref[...], g_ref[...], sh_ref[...], sc_ref[...]).astype(BF16)

    hn = hn_ref[...]
    a = jnp.dot(hn, wa_ref[...], preferred_element_type=F32)
    gt = jnp.dot(hn, wg_ref[...], preferred_element_type=F32)
    act = (a * (gt * jax.nn.sigmoid(gt))).astype(BF16)
    part = jnp.dot(act, wo_ref[...], preferred_element_type=F32)

    @pl.when(j == 0)
    def _():
        o_ref[...] = part

    @pl.when(j > 0)
    def _():
        o_ref[...] += part

    @pl.when(j == pl.num_programs(2) - 1)
    def _():
        o_ref[...] = x_ref[...] + (0.5 * gate_ref[...]) * o_ref[...]


def _ffn(x, g, mod, k0, w_in, w_out):
    bsz, rows, d = x.shape
    ff = w_out.shape[0]
    tm = _row_tile(rows)
    tf = _col_tile(ff, 512)
    nf = ff // tf
    row = pl.BlockSpec((None, tm, d), lambda b, i, j: (b, i, 0))
    return pl.pallas_call(
        _ffn_kernel,
        out_shape=jax.ShapeDtypeStruct(x.shape, F32),
        grid=(bsz, rows // tm, nf),
        in_specs=[row,
                  pl.BlockSpec((1, d), lambda b, i, j: (0, 0)),
                  _mod_spec(k0, d), _mod_spec(k0 + 1, d), _mod_spec(k0 + 2, d),
                  pl.BlockSpec((d, tf), lambda b, i, j: (0, j)),
                  pl.BlockSpec((d, tf), lambda b, i, j: (0, j + nf)),
                  pl.BlockSpec((tf, d), lambda b, i, j: (j, 0))],
        out_specs=row,
        scratch_shapes=[pltpu.VMEM((tm, d), BF16)],
        compiler_params=_cparams("parallel", "parallel", "arbitrary"),
        name="ffn",
    )(x, g.reshape(1, d), mod, mod, mod, w_in, w_in, w_out)


def _proj_kernel(*refs, n_rope, with_gates):
    x_ref, g_ref, sh_ref, sc_ref, w_ref = refs[:5]
    rest = refs[5:]
    if n_rope:
        cs_ref, sn_ref = rest[:2]
        rest = rest[2:]
    if with_gates:
        wg_ref, bg_ref, o_ref, og_ref, hn_ref = rest
    else:
        o_ref, hn_ref = rest
    n = pl.program_id(2)

    @pl.when(n == 0)
    def _():
        hn = _adaln(x_ref[...], g_ref[...], sh_ref[...], sc_ref[...]).astype(BF16)
        hn_ref[...] = hn
        if with_gates:
            og_ref[...] = jnp.dot(hn, wg_ref[...], preferred_element_type=F32) + bg_ref[...]

    acc = jnp.dot(hn_ref[...], w_ref[...], preferred_element_type=F32)
    if not n_rope:
        o_ref[...] = acc.astype(o_ref.dtype)
    else:
        @pl.when(n >= n_rope)
        def _():
            o_ref[...] = acc.astype(o_ref.dtype)

        @pl.when(n < n_rope)
        def _():
            cs, sn = cs_ref[...], sn_ref[...]
            hd = cs.shape[1]
            for h in range(acc.shape[1] // hd):
                r = acc[:, h * hd:(h + 1) * hd]
                r = r * cs + pltpu.roll(r, hd // 2, axis=1) * sn
                o_ref[:, h * hd:(h + 1) * hd] = r.astype(o_ref.dtype)


def _proj(x, g, mod, k0, w, out_dtype, rope=None, rope_cols=0, gates=None):
    bsz, rows, d = x.shape
    nout = w.shape[1]
    tm = _row_tile(rows)
    tn = _col_tile(math.gcd(nout, rope_cols), 512)
    n_rope = rope_cols // tn
    row = pl.BlockSpec((None, tm, d), lambda b, i, n: (b, i, 0))
    in_specs = [row, pl.BlockSpec((1, d), lambda b, i, n: (0, 0)),
                _mod_spec(k0, d), _mod_spec(k0 + 1, d),
                pl.BlockSpec((d, tn), lambda b, i, n: (0, n))]
    args = [x, g.reshape(1, d), mod, mod, w]
    if n_rope:
        cs, sn = rope
        hd = cs.shape[1]
        in_specs += [pl.BlockSpec((tm, hd), lambda b, i, n: (i, 0))] * 2
        args += [cs, sn]
    out_shape = jax.ShapeDtypeStruct((bsz, rows, nout), out_dtype)
    out_specs = pl.BlockSpec((None, tm, tn), lambda b, i, n: (b, i, n))
    if gates is not None:
        wg, bg = gates
        ng = wg.shape[1]
        in_specs += [pl.BlockSpec((d, ng), lambda b, i, n: (0, 0)),
                     pl.BlockSpec((1, ng), lambda b, i, n: (0, 0))]
        args += [wg, bg]
        out_shape = (out_shape, jax.ShapeDtypeStruct((bsz, rows, ng), F32))
        out_specs = (out_specs, pl.BlockSpec((None, tm, ng), lambda b, i, n: (b, i, 0)))
    return pl.pallas_call(
        functools.partial(_proj_kernel, n_rope=n_rope, with_gates=gates is not None),
        out_shape=out_shape,
        grid=(bsz, rows // tm, nout // tn),
        in_specs=in_specs,
        out_specs=out_specs,
        scratch_shapes=[pltpu.VMEM((tm, d), BF16)],
        compiler_params=_cparams("parallel", "parallel", "arbitrary"),
        name="proj",
    )(*args)


def _out_proj_kernel(y_ref, w_ref, h_ref, gate_ref, o_ref):
    acc = jnp.dot(y_ref[...], w_ref[...], preferred_element_type=F32)
    o_ref[...] = h_ref[...] + gate_ref[...] * acc


def _out_proj(y, w, h, mod, k):
    bsz, rows, d = h.shape
    dk = y.shape[2]
    tm = _row_tile(rows)
    return pl.pallas_call(
        _out_proj_kernel,
        out_shape=jax.ShapeDtypeStruct(h.shape, F32),
        grid=(bsz, rows // tm),
        in_specs=[pl.BlockSpec((None, tm, dk), lambda b, i: (b, i, 0)),
                  pl.BlockSpec((dk, d), lambda b, i: (0, 0)),
                  pl.BlockSpec((None, tm, d), lambda b, i: (b, i, 0)),
                  _mod_spec(k, d)],
        out_specs=pl.BlockSpec((None, tm, d), lambda b, i: (b, i, 0)),
        compiler_params=_cparams("parallel", "parallel"),
        name="out_proj",
    )(y, w, h, mod)


def _ml_chunk(q, k, v, icol, fcol, irow, frow, c_mem, n_mem, m_run, tri, tri_t):
    b_col = jnp.sum(jnp.where(tri, frow, 0.0), axis=1, keepdims=True)
    b_row = jnp.sum(jnp.where(tri_t, fcol, 0.0), axis=0, keepdims=True)
    b_end = jnp.sum(frow, axis=1, keepdims=True)
    g_col = b_end - b_col + icol
    g_row = b_end - b_row + irow
    m_new = jnp.maximum(b_end + m_run, jnp.max(g_row, axis=1, keepdims=True))
    wk = jnp.exp(g_col - m_new) * k.astype(F32)
    dec = jnp.exp(b_end + m_run - m_new)
    c_new = dec * c_mem + lax.dot_general(wk.astype(BF16), v, (((0,), (0,)), ((), ())),
                                          preferred_element_type=F32)
    n_new = dec * n_mem + jnp.sum(wk, axis=0, keepdims=True)
    d_log = jnp.where(tri, b_col - b_row + irow, NEG_BIG)
    m_prev = b_col + m_run
    m_q = jnp.maximum(m_prev, jnp.max(d_log, axis=1, keepdims=True))
    s = lax.dot_general(q, k, (((1,), (1,)), ((), ())), preferred_element_type=F32)
    s = s * jnp.exp(d_log - m_q)
    dq = jnp.exp(m_prev - m_q)
    num = (jnp.dot(s.astype(BF16), v, preferred_element_type=F32)
           + dq * jnp.dot(q, c_mem.astype(BF16), preferred_element_type=F32))
    den = (jnp.sum(s, axis=1, keepdims=True)
           + dq * jnp.sum(q.astype(F32) * n_mem, axis=1, keepdims=True))
    h = num / jnp.maximum(jnp.abs(den), jnp.exp(-m_q))
    return h, c_new, n_new, m_new


def _mlstm_kernel(cq, ck, cv, co, lq, lk, lv, lo, cgc, cgr, lgc, lgr, gh_ref,
                  yc_ref, yl_ref, hf_ref, hb_ref, c_ref, n_ref, m_ref):
    hd = lq.shape[1]
    lc = ML_CHUNK
    q_scale = hd ** -0.5
    rows = lax.broadcasted_iota(jnp.int32, (lc, lc), 0)
    cols = lax.broadcasted_iota(jnp.int32, (lc, lc), 1)
    causal = cols <= rows
    anti = cols >= rows

    c_ref[...] = jnp.zeros_like(c_ref)
    n_ref[...] = jnp.zeros_like(n_ref)
    m_ref[...] = jnp.full_like(m_ref, NEG_BIG)

    def one(q_ref, k_ref, v_ref, gc_ref, gr_ref, out_ref, d, start):
        sl = pl.ds(start, lc)
        gc = gc_ref[sl, :]
        gr = gr_ref[:, sl]
        icol, fcol = gc[:, 2 * d:2 * d + 1], jax.nn.log_sigmoid(gc[:, 2 * d + 1:2 * d + 2])
        irow, frow = gr[2 * d:2 * d + 1, :], jax.nn.log_sigmoid(gr[2 * d + 1:2 * d + 2, :])
        q = (q_ref[sl, :].astype(F32) * q_scale).astype(BF16)
        h, c_new, n_new, m_new = _ml_chunk(
            q, k_ref[sl, :], v_ref[sl, :], icol, fcol, irow, frow,
            c_ref[d], n_ref[d], m_ref[d], causal if d == 0 else anti, anti if d == 0 else causal)
        c_ref[d] = c_new
        n_ref[d] = n_new
        m_ref[d] = m_new
        out_ref[sl, :] = h

    def scan(q_ref, k_ref, v_ref, o_ref, gc_ref, gr_ref, y_ref):
        nc = q_ref.shape[0] // lc

        def body(c, carry):
            one(q_ref, k_ref, v_ref, gc_ref, gr_ref, hf_ref, 0, pl.multiple_of(c * lc, lc))
            one(q_ref, k_ref, v_ref, gc_ref, gr_ref, hb_ref, 1, pl.multiple_of((nc - 1 - c) * lc, lc))
            return carry

        lax.fori_loop(0, nc, body, 0)

        def merge(c, carry):
            sl = pl.ds(pl.multiple_of(c * lc, lc), lc)
            hs = hf_ref[sl, :] + hb_ref[sl, :]
            hs = hs * lax.rsqrt(jnp.mean(hs * hs, axis=-1, keepdims=True) + EPS) * gh_ref[...]
            y_ref[sl, :] = (jax.nn.sigmoid(o_ref[sl, :].astype(F32)) * hs).astype(y_ref.dtype)
            return carry

        lax.fori_loop(0, nc, merge, 0)

    scan(cq, ck, cv, co, cgc, cgr, yc_ref)
    scan(lq, lk, lv, lo, lgc, lgr, yl_ref)


def _mlstm_scan(zc, zl, gc, gl, g_head):
    bsz, n, d4 = zl.shape
    lctx = zc.shape[1]
    d = d4 // 4
    nh = ML_HEADS
    hd = d // nh
    assert n % ML_CHUNK == 0 and lctx % ML_CHUNK == 0

    def gate_views(g):
        length = g.shape[1]
        g = g[:, :, :4 * nh].reshape(bsz, length, 4, nh)
        return jnp.transpose(g, (0, 3, 1, 2)), jnp.transpose(g, (0, 3, 2, 1))

    cgc, cgr = gate_views(gc)
    lgc, lgr = gate_views(gl)

    def zspec(length, k):
        return pl.BlockSpec((None, length, hd), lambda b, h: (b, 0, k * nh + h))

    def gspecs(length):
        return [pl.BlockSpec((None, None, length, 4), lambda b, h: (b, h, 0, 0)),
                pl.BlockSpec((None, None, 4, length), lambda b, h: (b, h, 0, 0))]

    lmax = max(n, lctx)
    return pl.pallas_call(
        _mlstm_kernel,
        out_shape=(jax.ShapeDtypeStruct((bsz, lctx, d), BF16),
                   jax.ShapeDtypeStruct((bsz, n, d), BF16)),
        grid=(bsz, nh),
        in_specs=([zspec(lctx, k) for k in range(4)] + [zspec(n, k) for k in range(4)]
                  + gspecs(lctx) + gspecs(n)
                  + [pl.BlockSpec((None, 1, hd), lambda b, h: (h, 0, 0))]),
        out_specs=(pl.BlockSpec((None, lctx, hd), lambda b, h: (b, 0, h)),
                   pl.BlockSpec((None, n, hd), lambda b, h: (b, 0, h))),
        scratch_shapes=[pltpu.VMEM((lmax, hd), F32), pltpu.VMEM((lmax, hd), F32),
                        pltpu.VMEM((2, hd, hd), F32), pltpu.VMEM((2, 1, hd), F32),
                        pltpu.VMEM((2, 1, 1), F32)],
        compiler_params=_cparams("parallel", "parallel"),
        name="mlstm_scan",
    )(zc, zc, zc, zc, zl, zl, zl, zl, cgc, cgr, lgc, lgr, g_head.reshape(nh, 1, hd))


def _mlstm_mixer(h, hc, g, mod, mod_c, w_in, b_gate, g_head, w_out, with_ctx):
    bsz, n, d = h.shape
    lctx = hc.shape[1] // bsz
    w_main = w_in[:, :4 * d].astype(BF16)
    ng = 4 * ML_HEADS
    ngp = -(-ng // LANES) * LANES
    w_gate = jnp.pad(w_in[:, 4 * d:], ((0, 0), (0, ngp - ng))).astype(BF16)
    bias = jnp.pad(b_gate.reshape(1, ng), ((0, 0), (0, ngp - ng)))
    zl, gl = _proj(h, g, mod, 3, w_main, BF16, gates=(w_gate, bias))
    zc, gc = _proj(hc, g, mod_c, 3, w_main, BF16, gates=(w_gate, bias))
    yc, yl = _mlstm_scan(zc.reshape(bsz, lctx, 4 * d), zl, gc.reshape(bsz, lctx, ngp), gl, g_head)
    w_o = w_out.astype(BF16)
    h = _out_proj(yl, w_o, h, mod, 5)
    if with_ctx:
        hc = _out_proj(yc.reshape(1, bsz * lctx, d), w_o, hc, mod_c, 5)
    return h, hc


def _wa_kernel(sink_ref, q_ref, *refs, local, scale):
    if local:
        k_ref, v_ref, kc_ref, vc_ref, o_ref = refs
    else:
        kc_ref, vc_ref, o_ref = refs
    tq = q_ref.shape[0]
    hd = kc_ref.shape[1] // WA_KV_HEADS
    grp = WA_Q_HEADS // WA_KV_HEADS
    if local:
        n = k_ref.shape[0]
        span = 3 * WA_BLOCK
        start = pl.program_id(1) * tq
        ks = pl.multiple_of(jnp.clip(start - WA_BLOCK, 0, n - span), WA_BLOCK)
        q_pos = start + (lax.broadcasted_iota(jnp.int32, (grp * tq, span), 0) & (tq - 1))
        k_pos = ks + lax.broadcasted_iota(jnp.int32, (grp * tq, span), 1)
        ok = jnp.abs(q_pos - k_pos) <= WA_WINDOW
    dims = (((1,), (1,)), ((), ()))
    for kv in range(WA_KV_HEADS):
        cs = slice(kv * hd, (kv + 1) * hd)
        q4 = jnp.concatenate([q_ref[:, (kv * grp + j) * hd:(kv * grp + j + 1) * hd]
                              for j in range(grp)], axis=0)
        sink = jnp.concatenate([jnp.full((tq, 1), sink_ref[0, kv * grp + j], F32)
                                for j in range(grp)], axis=0)
        kc, vc = kc_ref[:, cs], vc_ref[:, cs]
        s_ctx = lax.dot_general(q4, kc, dims, preferred_element_type=F32) * scale
        m = jnp.maximum(jnp.max(s_ctx, axis=1, keepdims=True), sink)
        if local:
            kl, vl = k_ref[pl.ds(ks, span), cs], v_ref[pl.ds(ks, span), cs]
            s_loc = lax.dot_general(q4, kl, dims, preferred_element_type=F32) * scale
            s_loc = jnp.where(ok, s_loc, NEG_BIG)
            m = jnp.maximum(m, jnp.max(s_loc, axis=1, keepdims=True))
            p_loc = jnp.exp(s_loc - m)
        p_ctx = jnp.exp(s_ctx - m)
        den = jnp.sum(p_ctx, axis=1, keepdims=True) + jnp.exp(sink - m)
        if local:
            den = den + jnp.sum(p_loc, axis=1, keepdims=True)
        inv = 1.0 / den
        o = jnp.dot((p_ctx * inv).astype(BF16), vc, preferred_element_type=F32)
        if local:
            o = o + jnp.dot((p_loc * inv).astype(BF16), vl, preferred_element_type=F32)
        for j in range(grp):
            hq = kv * grp + j
            o_ref[:, hq * hd:(hq + 1) * hd] = o[j * tq:(j + 1) * tq].astype(o_ref.dtype)


def _wa_attention(sink, zq, zkv, zc, local):
    bsz, lq, _ = zq.shape
    lctx = zc.shape[1]
    hd = zq.shape[2] // (WA_Q_HEADS + 2 * WA_KV_HEADS)
    qd, kd = WA_Q_HEADS * hd, WA_KV_HEADS * hd
    tq = WA_BLOCK
    kblk = qd // kd
    in_specs = [pl.BlockSpec(memory_space=pltpu.SMEM),
                pl.BlockSpec((None, tq, qd), lambda b, i: (b, i, 0))]
    args = [sink.reshape(1, WA_Q_HEADS), zq]
    if local:
        n = zkv.shape[1]
        in_specs += [pl.BlockSpec((None, n, kd), lambda b, i: (b, 0, kblk)),
                     pl.BlockSpec((None, n, kd), lambda b, i: (b, 0, kblk + 1))]
        args += [zkv, zkv]
    in_specs += [pl.BlockSpec((None, lctx, kd), lambda b, i: (b, 0, kblk)),
                 pl.BlockSpec((None, lctx, kd), lambda b, i: (b, 0, kblk + 1))]
    args += [zc, zc]
    return pl.pallas_call(
        functools.partial(_wa_kernel, local=local, scale=hd ** -0.5),
        out_shape=jax.ShapeDtypeStruct((bsz, lq, qd), BF16),
        grid=(bsz, lq // tq),
        in_specs=in_specs,
        out_specs=pl.BlockSpec((None, tq, qd), lambda b, i: (b, i, 0)),
        compiler_params=_cparams("parallel", "parallel"),
        name="wa_attention",
    )(*args)


def _rope_tables(n, hd):
    rows = n // GRID_W
    row = jnp.repeat(jnp.arange(rows, dtype=F32), GRID_W)
    col = jnp.tile(jnp.arange(GRID_W, dtype=F32), rows)
    n_freq = hd // 4
    inv = ROPE_BASE ** (-jnp.arange(n_freq, dtype=F32) / n_freq)
    ang = jnp.concatenate([row[:, None] * inv, col[:, None] * inv], axis=-1)
    cos, sin = jnp.cos(ang), jnp.sin(ang)
    return jnp.concatenate([cos, cos], axis=-1), jnp.concatenate([-sin, sin], axis=-1)


def _wa_mixer(h, hc, g, mod, mod_c, w_in, sink, w_out, with_ctx):
    bsz, n, d = h.shape
    lctx = hc.shape[1] // bsz
    hd = d // WA_Q_HEADS
    w = w_in.astype(BF16)
    rope_cols = (WA_Q_HEADS + WA_KV_HEADS) * hd
    zl = _proj(h, g, mod, 3, w, BF16, rope=_rope_tables(n, hd), rope_cols=rope_cols)
    zc = _proj(hc, g, mod_c, 3, w, BF16).reshape(bsz, lctx, -1)
    sink = sink.astype(F32)
    w_o = w_out.astype(BF16)
    h = _out_proj(_wa_attention(sink, zl, zl, zc, True), w_o, h, mod, 5)
    if with_ctx:
        yc = _wa_attention(sink, zc, None, zc, False)
        hc = _out_proj(yc.reshape(1, bsz * lctx, d), w_o, hc, mod_c, 5)
    return h, hc


def _s5_kernel(u_ref, t_ref, q_ref, p_ref, a_ref, y_ref, s_ref, x_ref, *, n_ctx, bsz):
    u = u_ref[...]
    nj = u.shape[0] // bsz
    half = s_ref.shape[2] // 2
    for d in range(2):
        s_ref[d] = jnp.dot(u, q_ref[d], preferred_element_type=F32)
    aa = [a_ref[d, 0:1, :] for d in range(2)]
    ab = [a_ref[d, 1:2, :] for d in range(2)]

    def step(d, j, w):
        rows = pl.ds(pl.multiple_of(j * bsz, bsz), bsz)
        x_ref[d, rows, :] = w[:, :half]
        other = jnp.concatenate([w[:, half:], w[:, :half]], axis=1)
        return aa[d] * w + ab[d] * other + s_ref[d, rows, :]

    def run(lo, n, carry):
        def body(i, ws):
            return step(0, lo + i, ws[0]), step(1, lo + n - 1 - i, ws[1])
        return lax.fori_loop(0, n, body, carry)

    zero = jnp.zeros((bsz, 2 * half), F32)
    ws = run(0, n_ctx, (zero, zero))
    run(n_ctx, nj - n_ctx, ws)
    y = None
    for d in range(2):
        yd = (jnp.dot(u, t_ref[d], preferred_element_type=F32)
              + jnp.dot(x_ref[d].astype(BF16), p_ref[d], preferred_element_type=F32))
        y = yd if y is None else y + yd
    y_ref[...] = y


def _s5_scan(u, t_m, q_m, p_m, a_m, n_ctx, bsz):
    ng, rows, w = u.shape
    st2 = p_m.shape[2]
    return pl.pallas_call(
        functools.partial(_s5_kernel, n_ctx=n_ctx, bsz=bsz),
        out_shape=jax.ShapeDtypeStruct((ng, rows, w), F32),
        grid=(ng,),
        in_specs=[pl.BlockSpec((None, rows, w), lambda g: (g, 0, 0)),
                  pl.BlockSpec((2, None, w, w), lambda g: (0, g, 0, 0)),
                  pl.BlockSpec((2, None, w, 2 * st2), lambda g: (0, g, 0, 0)),
                  pl.BlockSpec((2, None, st2, w), lambda g: (0, g, 0, 0)),
                  pl.BlockSpec((2, None, 2, 2 * st2), lambda g: (0, g, 0, 0))],
        out_specs=pl.BlockSpec((None, rows, w), lambda g: (g, 0, 0)),
        scratch_shapes=[pltpu.VMEM((2, rows, 2 * st2), F32), pltpu.VMEM((2, rows, st2), F32)],
        compiler_params=_cparams("parallel"),
        name="s5_scan",
    )(u, t_m, q_m, p_m, a_m)


def _s5_operators(lam_re, lam_im, log_dt, b_re, b_im, c_re, c_im):
    hi = lax.Precision.HIGHEST
    lc = S5_CHUNK
    dt = jnp.exp(log_dt)[..., None]
    mag = jnp.exp(lam_re * dt)
    lb_re, lb_im = mag * jnp.cos(lam_im * dt), mag * jnp.sin(lam_im * dt)
    den = lam_re * lam_re + lam_im * lam_im
    nr, ni = lb_re - 1.0, lb_im
    fr = (nr * lam_re + ni * lam_im) / den
    fi = (ni * lam_re - nr * lam_im) / den
    bb_re = fr[..., None] * b_re - fi[..., None] * b_im
    bb_im = fr[..., None] * b_im + fi[..., None] * b_re
    k = jnp.arange(lc + 1, dtype=F32)[:, None, None, None]
    pmag = jnp.exp(k * (lam_re * dt))
    pw_re, pw_im = pmag * jnp.cos(k * (lam_im * dt)), pmag * jnp.sin(k * (lam_im * dt))
    lbb_re = pw_re[..., None] * bb_re - pw_im[..., None] * bb_im
    lbb_im = pw_re[..., None] * bb_im + pw_im[..., None] * bb_re
    kern = (jnp.einsum('dgop,kdgpc->kdgco', c_re, lbb_re, precision=hi)
            - jnp.einsum('dgop,kdgpc->kdgco', c_im, lbb_im, precision=hi))
    ti = jnp.arange(lc)[:, None]
    to = jnp.arange(lc)[None, :]
    ngrp, nch = lam_re.shape[1], b_re.shape[3]
    nst = lam_re.shape[2]

    def per_dir(d):
        lag = (to - ti) if d == 0 else (ti - to)
        kd = kern[jnp.clip(lag, 0, lc), d]
        kd = jnp.where((lag >= 0)[:, :, None, None, None], kd, 0.0)
        t_m = jnp.transpose(kd, (2, 0, 3, 1, 4)).reshape(ngrp, lc * nch, lc * nch)
        e_in = (lc - 1 - jnp.arange(lc)) if d == 0 else jnp.arange(lc)
        q_re = jnp.transpose(lbb_re[e_in, d], (1, 0, 3, 2)).reshape(ngrp, lc * nch, nst)
        q_im = jnp.transpose(lbb_im[e_in, d], (1, 0, 3, 2)).reshape(ngrp, lc * nch, nst)
        q_m = jnp.concatenate([q_re, q_im, q_im, q_re], axis=-1)
        e_out = (jnp.arange(lc) + 1) if d == 0 else (lc - jnp.arange(lc))
        cl_re = (c_re[d][None] * pw_re[e_out, d][:, :, None, :]
                 - c_im[d][None] * pw_im[e_out, d][:, :, None, :])
        cl_im = (c_re[d][None] * pw_im[e_out, d][:, :, None, :]
                 + c_im[d][None] * pw_re[e_out, d][:, :, None, :])
        p_re = jnp.transpose(cl_re, (1, 3, 0, 2)).reshape(ngrp, nst, lc * nch)
        p_im = jnp.transpose(cl_im, (1, 3, 0, 2)).reshape(ngrp, nst, lc * nch)
        p_m = jnp.concatenate([p_re, -p_im], axis=1)
        a_re, a_im = pw_re[lc, d], pw_im[lc, d]
        a_m = jnp.stack([jnp.concatenate([a_re] * 4, axis=-1),
                         jnp.concatenate([-a_im, a_im, a_im, -a_im], axis=-1)], axis=1)
        return t_m, q_m, p_m, a_m

    f, b = per_dir(0), per_dir(1)
    t_m, q_m, p_m, a_m = (jnp.stack([x, y]) for x, y in zip(f, b))
    return t_m.astype(BF16), q_m.astype(BF16), p_m.astype(BF16), a_m


def _s5_glu_kernel(y_ref, u_ref, ds_ref, wa_ref, wg_ref, h_ref, gate_ref, o_ref, z_ref):
    n = pl.program_id(2)

    @pl.when(n == 0)
    def _():
        z_ref[...] = jax.nn.gelu(y_ref[...] + ds_ref[...] * u_ref[...]).astype(BF16)

    z = z_ref[...]
    a = jnp.dot(z, wa_ref[...], preferred_element_type=F32)
    gt = jnp.dot(z, wg_ref[...], preferred_element_type=F32)
    o_ref[...] = h_ref[...] + gate_ref[...] * (a * jax.nn.sigmoid(gt))


def _s5_glu(y, u, d_skip, w, h, mod, k):
    bsz, rows, d = h.shape
    tm = _row_tile(rows)
    tn = _col_tile(d, 512)
    nn = d // tn
    row = pl.BlockSpec((None, tm, d), lambda b, i, n: (b, i, 0))
    col = pl.BlockSpec((None, tm, tn), lambda b, i, n: (b, i, n))
    return pl.pallas_call(
        _s5_glu_kernel,
        out_shape=jax.ShapeDtypeStruct(h.shape, F32),
        grid=(bsz, rows // tm, nn),
        in_specs=[row, row, pl.BlockSpec((1, d), lambda b, i, n: (0, 0)),
                  pl.BlockSpec((d, tn), lambda b, i, n: (0, n)),
                  pl.BlockSpec((d, tn), lambda b, i, n: (0, n + nn)),
                  col, pl.BlockSpec((None, None, 1, tn), lambda b, i, n: (b, k, 0, n))],
        out_specs=col,
        scratch_shapes=[pltpu.VMEM((tm, d), BF16)],
        compiler_params=_cparams("parallel", "parallel", "arbitrary"),
        name="s5_glu",
    )(y, u, d_skip.reshape(1, d), w, w, h, mod)


def _s5_mixer(h, hc, g, mod, mod_c, w_in, lam_re, lam_im, log_dt, b_re, b_im, c_re, c_im,
              d_skip, w_out, with_ctx):
    bsz, n, d = h.shape
    lctx = hc.shape[1] // bsz
    lc, gw = S5_CHUNK, S5_GROUP
    ng = d // gw
    w = w_in.astype(BF16)
    u_l = _proj(h, g, mod, 3, w, F32)
    u_c = _proj(hc, g, mod_c, 3, w, F32)
    ops = _s5_operators(lam_re, lam_im, log_dt, b_re, b_im, c_re, c_im)
    u = jnp.concatenate([u_c.reshape(bsz, lctx, d), u_l], axis=1).astype(BF16)
    nj = (lctx + n) // lc
    u = jnp.transpose(u.reshape(bsz, nj, lc, ng, gw), (3, 1, 0, 2, 4)).reshape(ng, nj * bsz, lc * gw)
    y = _s5_scan(u, *ops, n_ctx=lctx // lc, bsz=bsz)
    y = jnp.transpose(y.reshape(ng, nj, bsz, lc, gw), (2, 1, 3, 0, 4)).reshape(bsz, lctx + n, d)
    w_o = w_out.astype(BF16)
    h = _s5_glu(y[:, lctx:], u_l, d_skip, w_o, h, mod, 5)
    if with_ctx:
        hc = _s5_glu(y[:, :lctx].reshape(1, bsz * lctx, d), u_c, d_skip, w_o, hc, mod_c, 5)
    return h, hc


def _rmsnorm_kernel(x_ref, g_ref, o_ref):
    x = x_ref[...]
    o_ref[...] = x * lax.rsqrt(jnp.mean(x * x, axis=-1, keepdims=True) + EPS) * g_ref[...]


def _final_norm(x, g):
    bsz, rows, d = x.shape
    tm = _row_tile(rows)
    row = pl.BlockSpec((None, tm, d), lambda b, i: (b, i, 0))
    return pl.pallas_call(
        _rmsnorm_kernel,
        out_shape=jax.ShapeDtypeStruct(x.shape, F32),
        grid=(bsz, rows // tm),
        in_specs=[row, pl.BlockSpec((1, d), lambda b, i: (0, 0))],
        out_specs=row,
        compiler_params=_cparams("parallel", "parallel"),
        name="final_norm",
    )(x, g.reshape(1, d))


def kernel(x, c, ctx, c_ctx, w_ada, b_ada, g_norm, w_ffn_in, w_ffn_out, g_final, ml_w_in, ml_b_gate, ml_g_head, ml_w_out, wa_w_in, wa_sink, wa_w_out, s5_w_in, s5_lam_re, s5_lam_im, s5_log_dt, s5_b_re, s5_b_im, s5_c_re, s5_c_im, s5_d_skip, s5_w_out):
    bsz, n, d = x.shape
    depth = w_ada.shape[0]
    lctx = ctx.shape[1]
    n_rows = -(-(bsz + 1) // 16) * 16
    c_rows = jnp.concatenate([c, c_ctx[None], jnp.zeros((n_rows - bsz - 1, d), F32)], axis=0)
    mods = _mod_table(c_rows, w_ada, b_ada).reshape(depth, n_rows, N_MOD, 1, d)
    h = x
    hc = ctx.reshape(1, bsz * lctx, d)
    for layer in range(depth):
        has_next = layer < depth - 1
        mod, mod_c = mods[layer, :bsz], mods[layer, bsz:bsz + 1]
        g = g_norm[layer]
        w_in = w_ffn_in[layer].astype(BF16)
        w_out = w_ffn_out[layer].astype(BF16)
        h = _ffn(h, g[0], mod, 0, w_in[0], w_out[0])
        hc = _ffn(hc, g[0], mod_c, 0, w_in[0], w_out[0])
        kind, idx = layer % 3, layer // 3
        if kind == 0:
            h, hc = _mlstm_mixer(h, hc, g[1], mod, mod_c, ml_w_in[idx], ml_b_gate[idx],
                                 ml_g_head[idx], ml_w_out[idx], has_next)
        elif kind == 1:
            h, hc = _wa_mixer(h, hc, g[1], mod, mod_c, wa_w_in[idx], wa_sink[idx],
                              wa_w_out[idx], has_next)
        else:
            h, hc = _s5_mixer(h, hc, g[1], mod, mod_c, s5_w_in[idx], s5_lam_re[idx], s5_lam_im[idx],
                              s5_log_dt[idx], s5_b_re[idx], s5_b_im[idx], s5_c_re[idx],
                              s5_c_im[idx], s5_d_skip[idx], s5_w_out[idx], has_next)
        h = _ffn(h, g[2], mod, 6, w_in[1], w_out[1])
        if has_next:
            hc = _ffn(hc, g[2], mod_c, 6, w_in[1], w_out[1])
    return _final_norm(h, g_final)
```

```python
import functools
import math

import jax
import jax.numpy as jnp
from jax import lax
from jax.experimental import pallas as pl
from jax.experimental.pallas import tpu as pltpu

F32 = jnp.float32
BF16 = jnp.bfloat16

EPS = 1e-6
NEG_BIG = -1e30
N_MOD = 9
ML_HEADS = 8
ML_CHUNK = 256
WA_Q_HEADS = 16
WA_KV_HEADS = 4
WA_WINDOW = 128
WA_BLOCK = 128
GRID_W = 64
ROPE_BASE = 10000.0
S5_GROUP = 16
S5_STATE = 64
S5_CHUNK = 16

LANES = 128
ROW_TILE = 512
ROW_TILE_STREAM = 1024
VMEM_LIMIT = 56 * 1024 * 1024


def _cparams(*sem):
    return pltpu.CompilerParams(dimension_semantics=sem, vmem_limit_bytes=VMEM_LIMIT)


def _row_tile(rows, want):
    return want if rows % want == 0 else rows


def _col_tile(cols, want):
    t = min(want, cols)
    while cols % t:
        t -= LANES
    return t


def _adaln(x, g, shift, scale):
    var = jnp.mean(x * x, axis=-1, keepdims=True)
    return (x * lax.rsqrt(var + EPS) * g) * (1.0 + scale) + shift


ADALN_ROWS = 128


def _adaln_rows(hn_ref, x_ref, g_ref, sh_ref, sc_ref):
    rows = x_ref.shape[0]
    step = ADALN_ROWS if rows % ADALN_ROWS == 0 else rows

    def body(r, carry):
        sl = pl.ds(pl.multiple_of(r * step, step), step)
        hn_ref[sl, :] = _adaln(x_ref[sl, :], g_ref[...], sh_ref[...], sc_ref[...]).astype(BF16)
        return carry

    lax.fori_loop(0, rows // step, body, 0)


def _mod_spec(k, d):
    return pl.BlockSpec((None, None, 1, d), lambda b, *_: (b, k, 0, 0))


def _mod_kernel(c_ref, w_ref, b_ref, o_ref):
    c = c_ref[...]
    s = (c * jax.nn.sigmoid(c)).astype(BF16)
    o_ref[...] = jnp.dot(s, w_ref[...].astype(BF16), preferred_element_type=F32) + b_ref[...]


def _mod_table(c_rows, w_ada, b_ada):
    depth, d, nd = w_ada.shape
    r = c_rows.shape[0]
    tn = _col_tile(nd, 1024)
    return pl.pallas_call(
        _mod_kernel,
        out_shape=jax.ShapeDtypeStruct((depth, r, nd), F32),
        grid=(depth, nd // tn),
        in_specs=[pl.BlockSpec((r, d), lambda l, n: (0, 0)),
                  pl.BlockSpec((None, d, tn), lambda l, n: (l, 0, n)),
                  pl.BlockSpec((None, 1, tn), lambda l, n: (l, 0, n))],
        out_specs=pl.BlockSpec((None, r, tn), lambda l, n: (l, 0, n)),
        compiler_params=_cparams("parallel", "parallel"),
        name="mod_table",
    )(c_rows, w_ada, b_ada.reshape(depth, 1, nd))


def _ffn_kernel(x_ref, g_ref, sh_ref, sc_ref, gate_ref, wa_ref, wg_ref, wo_ref, o_ref,
                hn_ref, act_ref, *, nf):
    j = pl.program_id(2)
    tf = wa_ref.shape[1]
    tn = o_ref.shape[1]

    @pl.when(j == 0)
    def _():
        _adaln_rows(hn_ref, x_ref, g_ref, sh_ref, sc_ref)

    @pl.when(j < nf)
    def _():
        hn = hn_ref[...]
        a = jnp.dot(hn, wa_ref[...], preferred_element_type=F32)
        gt = jnp.dot(hn, wg_ref[...], preferred_element_type=F32)
        cols = pl.ds(pl.multiple_of(j * tf, tf), tf)
        act_ref[:, cols] = (a * (gt * jax.nn.sigmoid(gt))).astype(BF16)

    @pl.when(j >= nf)
    def _():
        y = jnp.dot(act_ref[...], wo_ref[...], preferred_element_type=F32)
        cols = pl.ds(pl.multiple_of((j - nf) * tn, tn), tn)
        o_ref[...] = x_ref[:, cols] + (0.5 * gate_ref[...]) * y


def _ffn(x, g, mod, k0, w_in, w_out):
    bsz, rows, d = x.shape
    ff = w_out.shape[0]
    tm = _row_tile(rows, ROW_TILE_STREAM)
    tf = _col_tile(ff, 512)
    tn = _col_tile(d, 256)
    nf = ff // tf

    def fill(j):
        return jnp.minimum(j, nf - 1)

    def drain(j):
        return jnp.maximum(j - nf, 0)

    return pl.pallas_call(
        functools.partial(_ffn_kernel, nf=nf),
        out_shape=jax.ShapeDtypeStruct(x.shape, F32),
        grid=(bsz, rows // tm, nf + d // tn),
        in_specs=[pl.BlockSpec((None, tm, d), lambda b, i, j: (b, i, 0)),
                  pl.BlockSpec((1, d), lambda b, i, j: (0, 0)),
                  _mod_spec(k0, d), _mod_spec(k0 + 1, d),
                  pl.BlockSpec((None, None, 1, tn), lambda b, i, j: (b, k0 + 2, 0, drain(j))),
                  pl.BlockSpec((d, tf), lambda b, i, j: (0, fill(j))),
                  pl.BlockSpec((d, tf), lambda b, i, j: (0, fill(j) + nf)),
                  pl.BlockSpec((ff, tn), lambda b, i, j: (0, drain(j)))],
        out_specs=pl.BlockSpec((None, tm, tn), lambda b, i, j: (b, i, drain(j))),
        scratch_shapes=[pltpu.VMEM((tm, d), BF16), pltpu.VMEM((tm, ff), BF16)],
        compiler_params=_cparams("parallel", "parallel", "arbitrary"),
        name="ffn",
    )(x, g.reshape(1, d), mod, mod, mod, w_in, w_in, w_out)


def _proj_kernel(*refs, n_rope, with_gates, n_scaled, scale):
    x_ref, g_ref, sh_ref, sc_ref, w_ref = refs[:5]
    rest = refs[5:]
    if n_rope:
        cs_ref, sn_ref = rest[:2]
        rest = rest[2:]
    if with_gates:
        wg_ref, bg_ref, o_ref, og_ref, hn_ref = rest
    else:
        o_ref, hn_ref = rest
    n = pl.program_id(2)

    @pl.when(n == 0)
    def _():
        _adaln_rows(hn_ref, x_ref, g_ref, sh_ref, sc_ref)
        if with_gates:
            og_ref[...] = jnp.dot(hn_ref[...], wg_ref[...], preferred_element_type=F32) + bg_ref[...]

    acc = jnp.dot(hn_ref[...], w_ref[...], preferred_element_type=F32)
    if n_scaled:
        acc = acc * jnp.where(n < n_scaled, scale, 1.0)
    if not n_rope:
        o_ref[...] = acc.astype(o_ref.dtype)
    else:
        @pl.when(n >= n_rope)
        def _():
            o_ref[...] = acc.astype(o_ref.dtype)

        @pl.when(n < n_rope)
        def _():
            cs, sn = cs_ref[...], sn_ref[...]
            hd = cs.shape[1]
            for h in range(acc.shape[1] // hd):
                r = acc[:, h * hd:(h + 1) * hd]
                r = r * cs + pltpu.roll(r, hd // 2, axis=1) * sn
                o_ref[:, h * hd:(h + 1) * hd] = r.astype(o_ref.dtype)


def _proj(x, g, mod, k0, w, out_dtype, rope=None, rope_cols=0, gates=None, scale=1.0, scale_cols=0):
    bsz, rows, d = x.shape
    nout = w.shape[1]
    tm = _row_tile(rows, ROW_TILE_STREAM)
    tn = _col_tile(math.gcd(math.gcd(nout, rope_cols), scale_cols), 512)
    n_rope = rope_cols // tn
    row = pl.BlockSpec((None, tm, d), lambda b, i, n: (b, i, 0))
    in_specs = [row, pl.BlockSpec((1, d), lambda b, i, n: (0, 0)),
                _mod_spec(k0, d), _mod_spec(k0 + 1, d),
                pl.BlockSpec((d, tn), lambda b, i, n: (0, n))]
    args = [x, g.reshape(1, d), mod, mod, w]
    if n_rope:
        cs, sn = rope
        hd = cs.shape[1]
        in_specs += [pl.BlockSpec((tm, hd), lambda b, i, n: (i, 0))] * 2
        args += [cs, sn]
    out_shape = jax.ShapeDtypeStruct((bsz, rows, nout), out_dtype)
    out_specs = pl.BlockSpec((None, tm, tn), lambda b, i, n: (b, i, n))
    if gates is not None:
        wg, bg = gates
        ng = wg.shape[1]
        in_specs += [pl.BlockSpec((d, ng), lambda b, i, n: (0, 0)),
                     pl.BlockSpec((1, ng), lambda b, i, n: (0, 0))]
        args += [wg, bg]
        out_shape = (out_shape, jax.ShapeDtypeStruct((bsz, rows, ng), F32))
        out_specs = (out_specs, pl.BlockSpec((None, tm, ng), lambda b, i, n: (b, i, 0)))
    return pl.pallas_call(
        functools.partial(_proj_kernel, n_rope=n_rope, with_gates=gates is not None,
                          n_scaled=scale_cols // tn, scale=scale),
        out_shape=out_shape,
        grid=(bsz, rows // tm, nout // tn),
        in_specs=in_specs,
        out_specs=out_specs,
        scratch_shapes=[pltpu.VMEM((tm, d), BF16)],
        compiler_params=_cparams("parallel", "parallel", "arbitrary"),
        name="proj",
    )(*args)


def _out_proj_kernel(y_ref, w_ref, h_ref, gate_ref, o_ref):
    acc = jnp.dot(y_ref[...], w_ref[...], preferred_element_type=F32)
    o_ref[...] = h_ref[...] + gate_ref[...] * acc


def _out_proj(y, w, h, mod, k):
    bsz, rows, d = h.shape
    dk = y.shape[2]
    tm = _row_tile(rows, ROW_TILE)
    return pl.pallas_call(
        _out_proj_kernel,
        out_shape=jax.ShapeDtypeStruct(h.shape, F32),
        grid=(bsz, rows // tm),
        in_specs=[pl.BlockSpec((None, tm, dk), lambda b, i: (b, i, 0)),
                  pl.BlockSpec((dk, d), lambda b, i: (0, 0)),
                  pl.BlockSpec((None, tm, d), lambda b, i: (b, i, 0)),
                  _mod_spec(k, d)],
        out_specs=pl.BlockSpec((None, tm, d), lambda b, i: (b, i, 0)),
        compiler_params=_cparams("parallel", "parallel"),
        name="out_proj",
    )(y, w, h, mod)


def _ml_chunk(q, k, v, icol, fcol, irow, frow, c_mem, n_mem, m_run, tri, tri_t):
    b_col = jnp.sum(jnp.where(tri, frow, 0.0), axis=1, keepdims=True)
    b_row = jnp.sum(jnp.where(tri_t, fcol, 0.0), axis=0, keepdims=True)
    b_end = jnp.sum(frow, axis=1, keepdims=True)
    g_col = b_end - b_col + icol
    g_row = b_end - b_row + irow
    m_new = jnp.maximum(b_end + m_run, jnp.max(g_row, axis=1, keepdims=True))
    wk = jnp.exp(g_col - m_new) * k.astype(F32)
    dec = jnp.exp(b_end + m_run - m_new)
    c_new = dec * c_mem + lax.dot_general(wk.astype(BF16), v, (((0,), (0,)), ((), ())),
                                          preferred_element_type=F32)
    n_new = dec * n_mem + jnp.sum(wk, axis=0, keepdims=True)
    d_log = jnp.where(tri, b_col - b_row + irow, NEG_BIG)
    m_prev = b_col + m_run
    m_q = jnp.maximum(m_prev, jnp.max(d_log, axis=1, keepdims=True))
    s = lax.dot_general(q, k, (((1,), (1,)), ((), ())), preferred_element_type=F32)
    s = s * jnp.exp(d_log - m_q)
    dq = jnp.exp(m_prev - m_q)
    num = (jnp.dot(s.astype(BF16), v, preferred_element_type=F32)
           + dq * jnp.dot(q, c_mem.astype(BF16), preferred_element_type=F32))
    den = (jnp.sum(s, axis=1, keepdims=True)
           + dq * jnp.sum(q.astype(F32) * n_mem, axis=1, keepdims=True))
    h = num / jnp.maximum(jnp.abs(den), jnp.exp(-m_q))
    return h, c_new, n_new, m_new


def _mlstm_kernel(cq, ck, cv, co, lq, lk, lv, lo, cgc, cgr, lgc, lgr, gh_ref,
                  yc_ref, yl_ref, hf_ref, hb_ref, c_ref, n_ref, m_ref):
    hd = lq.shape[1]
    lc = ML_CHUNK
    rows = lax.broadcasted_iota(jnp.int32, (lc, lc), 0)
    cols = lax.broadcasted_iota(jnp.int32, (lc, lc), 1)
    causal = cols <= rows
    anti = cols >= rows

    c_ref[...] = jnp.zeros_like(c_ref)
    n_ref[...] = jnp.zeros_like(n_ref)
    m_ref[...] = jnp.full_like(m_ref, NEG_BIG)

    def one(q_ref, k_ref, v_ref, gc_ref, gr_ref, out_ref, d, start):
        sl = pl.ds(start, lc)
        gc = gc_ref[sl, :]
        gr = gr_ref[:, sl]
        icol, fcol = gc[:, 2 * d:2 * d + 1], jax.nn.log_sigmoid(gc[:, 2 * d + 1:2 * d + 2])
        irow, frow = gr[2 * d:2 * d + 1, :], jax.nn.log_sigmoid(gr[2 * d + 1:2 * d + 2, :])
        h, c_new, n_new, m_new = _ml_chunk(
            q_ref[sl, :], k_ref[sl, :], v_ref[sl, :], icol, fcol, irow, frow,
            c_ref[d], n_ref[d], m_ref[d], causal if d == 0 else anti, anti if d == 0 else causal)
        c_ref[d] = c_new
        n_ref[d] = n_new
        m_ref[d] = m_new
        out_ref[sl, :] = h

    def scan(q_ref, k_ref, v_ref, o_ref, gc_ref, gr_ref, y_ref):
        nc = q_ref.shape[0] // lc

        def body(c, carry):
            one(q_ref, k_ref, v_ref, gc_ref, gr_ref, hf_ref, 0, pl.multiple_of(c * lc, lc))
            one(q_ref, k_ref, v_ref, gc_ref, gr_ref, hb_ref, 1, pl.multiple_of((nc - 1 - c) * lc, lc))
            return carry

        lax.fori_loop(0, nc, body, 0)

        def merge(c, carry):
            sl = pl.ds(pl.multiple_of(c * lc, lc), lc)
            hs = hf_ref[sl, :] + hb_ref[sl, :]
            hs = hs * lax.rsqrt(jnp.mean(hs * hs, axis=-1, keepdims=True) + EPS) * gh_ref[...]
            y_ref[sl, :] = (jax.nn.sigmoid(o_ref[sl, :].astype(F32)) * hs).astype(y_ref.dtype)
            return carry

        lax.fori_loop(0, nc, merge, 0)

    scan(cq, ck, cv, co, cgc, cgr, yc_ref)
    scan(lq, lk, lv, lo, lgc, lgr, yl_ref)


def _mlstm_scan(zc, zl, gc, gl, g_head):
    bsz, n, d4 = zl.shape
    lctx = zc.shape[1]
    d = d4 // 4
    nh = ML_HEADS
    hd = d // nh
    assert n % ML_CHUNK == 0 and lctx % ML_CHUNK == 0

    def gate_views(g):
        length = g.shape[1]
        g = g[:, :, :4 * nh].reshape(bsz, length, 4, nh)
        return jnp.transpose(g, (0, 3, 1, 2)), jnp.transpose(g, (0, 3, 2, 1))

    cgc, cgr = gate_views(gc)
    lgc, lgr = gate_views(gl)

    def zspec(length, k):
        return pl.BlockSpec((None, length, hd), lambda b, h: (b, 0, k * nh + h))

    def gspecs(length):
        return [pl.BlockSpec((None, None, length, 4), lambda b, h: (b, h, 0, 0)),
                pl.BlockSpec((None, None, 4, length), lambda b, h: (b, h, 0, 0))]

    lmax = max(n, lctx)
    return pl.pallas_call(
        _mlstm_kernel,
        out_shape=(jax.ShapeDtypeStruct((bsz, lctx, d), BF16),
                   jax.ShapeDtypeStruct((bsz, n, d), BF16)),
        grid=(bsz, nh),
        in_specs=([zspec(lctx, k) for k in range(4)] + [zspec(n, k) for k in range(4)]
                  + gspecs(lctx) + gspecs(n)
                  + [pl.BlockSpec((None, 1, hd), lambda b, h: (h, 0, 0))]),
        out_specs=(pl.BlockSpec((None, lctx, hd), lambda b, h: (b, 0, h)),
                   pl.BlockSpec((None, n, hd), lambda b, h: (b, 0, h))),
        scratch_shapes=[pltpu.VMEM((lmax, hd), F32), pltpu.VMEM((lmax, hd), F32),
                        pltpu.VMEM((2, hd, hd), F32), pltpu.VMEM((2, 1, hd), F32),
                        pltpu.VMEM((2, 1, 1), F32)],
        compiler_params=_cparams("parallel", "parallel"),
        name="mlstm_scan",
    )(zc, zc, zc, zc, zl, zl, zl, zl, cgc, cgr, lgc, lgr, g_head.reshape(nh, 1, hd))


def _mlstm_mixer(h, hc, g, mod, mod_c, w_in, b_gate, g_head, w_out, with_ctx):
    bsz, n, d = h.shape
    lctx = hc.shape[1] // bsz
    w_main = w_in[:, :4 * d].astype(BF16)
    ng = 4 * ML_HEADS
    ngp = -(-ng // LANES) * LANES
    w_gate = jnp.pad(w_in[:, 4 * d:], ((0, 0), (0, ngp - ng))).astype(BF16)
    bias = jnp.pad(b_gate.reshape(1, ng), ((0, 0), (0, ngp - ng)))
    q_scale = (d // ML_HEADS) ** -0.5
    zl, gl = _proj(h, g, mod, 3, w_main, BF16, gates=(w_gate, bias), scale=q_scale, scale_cols=d)
    zc, gc = _proj(hc, g, mod_c, 3, w_main, BF16, gates=(w_gate, bias), scale=q_scale, scale_cols=d)
    yc, yl = _mlstm_scan(zc.reshape(bsz, lctx, 4 * d), zl, gc.reshape(bsz, lctx, ngp), gl, g_head)
    w_o = w_out.astype(BF16)
    h = _out_proj(yl, w_o, h, mod, 5)
    if with_ctx:
        hc = _out_proj(yc.reshape(1, bsz * lctx, d), w_o, hc, mod_c, 5)
    return h, hc


def _wa_kernel(sink_ref, q_ref, *refs, local, scale):
    if local:
        k_ref, v_ref, kc_ref, vc_ref, o_ref = refs
    else:
        kc_ref, vc_ref, o_ref = refs
    tq = q_ref.shape[0]
    hd = kc_ref.shape[1] // WA_KV_HEADS
    grp = WA_Q_HEADS // WA_KV_HEADS
    if local:
        n = k_ref.shape[0]
        span = 3 * WA_BLOCK
        start = pl.program_id(1) * tq
        ks = pl.multiple_of(jnp.clip(start - WA_BLOCK, 0, n - span), WA_BLOCK)
        q_pos = start + (lax.broadcasted_iota(jnp.int32, (grp * tq, span), 0) & (tq - 1))
        k_pos = ks + lax.broadcasted_iota(jnp.int32, (grp * tq, span), 1)
        ok = jnp.abs(q_pos - k_pos) <= WA_WINDOW
    dims = (((1,), (1,)), ((), ()))
    for kv in range(WA_KV_HEADS):
        cs = slice(kv * hd, (kv + 1) * hd)
        q4 = jnp.concatenate([q_ref[:, (kv * grp + j) * hd:(kv * grp + j + 1) * hd]
                              for j in range(grp)], axis=0)
        sink = jnp.concatenate([jnp.full((tq, 1), sink_ref[0, kv * grp + j], F32)
                                for j in range(grp)], axis=0)
        kc, vc = kc_ref[:, cs], vc_ref[:, cs]
        s_ctx = lax.dot_general(q4, kc, dims, preferred_element_type=F32) * scale
        m = jnp.maximum(jnp.max(s_ctx, axis=1, keepdims=True), sink)
        if local:
            kl, vl = k_ref[pl.ds(ks, span), cs], v_ref[pl.ds(ks, span), cs]
            s_loc = lax.dot_general(q4, kl, dims, preferred_element_type=F32) * scale
            s_loc = jnp.where(ok, s_loc, NEG_BIG)
            m = jnp.maximum(m, jnp.max(s_loc, axis=1, keepdims=True))
            p_loc = jnp.exp(s_loc - m)
        p_ctx = jnp.exp(s_ctx - m)
        den = jnp.sum(p_ctx, axis=1, keepdims=True) + jnp.exp(sink - m)
        if local:
            den = den + jnp.sum(p_loc, axis=1, keepdims=True)
        inv = 1.0 / den
        o = jnp.dot((p_ctx * inv).astype(BF16), vc, preferred_element_type=F32)
        if local:
            o = o + jnp.dot((p_loc * inv).astype(BF16), vl, preferred_element_type=F32)
        for j in range(grp):
            hq = kv * grp + j
            o_ref[:, hq * hd:(hq + 1) * hd] = o[j * tq:(j + 1) * tq].astype(o_ref.dtype)


def _wa_attention(sink, zq, zkv, zc, local):
    bsz, lq, _ = zq.shape
    lctx = zc.shape[1]
    hd = zq.shape[2] // (WA_Q_HEADS + 2 * WA_KV_HEADS)
    qd, kd = WA_Q_HEADS * hd, WA_KV_HEADS * hd
    tq = WA_BLOCK
    kblk = qd // kd
    in_specs = [pl.BlockSpec(memory_space=pltpu.SMEM),
                pl.BlockSpec((None, tq, qd), lambda b, i: (b, i, 0))]
    args = [sink.reshape(1, WA_Q_HEADS), zq]
    if local:
        n = zkv.shape[1]
        in_specs += [pl.BlockSpec((None, n, kd), lambda b, i: (b, 0, kblk)),
                     pl.BlockSpec((None, n, kd), lambda b, i: (b, 0, kblk + 1))]
        args += [zkv, zkv]
    in_specs += [pl.BlockSpec((None, lctx, kd), lambda b, i: (b, 0, kblk)),
                 pl.BlockSpec((None, lctx, kd), lambda b, i: (b, 0, kblk + 1))]
    args += [zc, zc]
    return pl.pallas_call(
        functools.partial(_wa_kernel, local=local, scale=hd ** -0.5),
        out_shape=jax.ShapeDtypeStruct((bsz, lq, qd), BF16),
        grid=(bsz, lq // tq),
        in_specs=in_specs,
        out_specs=pl.BlockSpec((None, tq, qd), lambda b, i: (b, i, 0)),
        compiler_params=_cparams("parallel", "parallel"),
        name="wa_attention",
    )(*args)


def _rope_tables(n, hd):
    rows = n // GRID_W
    row = jnp.repeat(jnp.arange(rows, dtype=F32), GRID_W)
    col = jnp.tile(jnp.arange(GRID_W, dtype=F32), rows)
    n_freq = hd // 4
    inv = ROPE_BASE ** (-jnp.arange(n_freq, dtype=F32) / n_freq)
    ang = jnp.concatenate([row[:, None] * inv, col[:, None] * inv], axis=-1)
    cos, sin = jnp.cos(ang), jnp.sin(ang)
    return jnp.concatenate([cos, cos], axis=-1), jnp.concatenate([-sin, sin], axis=-1)


def _wa_mixer(h, hc, g, mod, mod_c, w_in, sink, w_out, with_ctx):
    bsz, n, d = h.shape
    lctx = hc.shape[1] // bsz
    hd = d // WA_Q_HEADS
    w = w_in.astype(BF16)
    rope_cols = (WA_Q_HEADS + WA_KV_HEADS) * hd
    zl = _proj(h, g, mod, 3, w, BF16, rope=_rope_tables(n, hd), rope_cols=rope_cols)
    zc = _proj(hc, g, mod_c, 3, w, BF16).reshape(bsz, lctx, -1)
    sink = sink.astype(F32)
    w_o = w_out.astype(BF16)
    h = _out_proj(_wa_attention(sink, zl, zl, zc, True), w_o, h, mod, 5)
    if with_ctx:
        yc = _wa_attention(sink, zc, None, zc, False)
        hc = _out_proj(yc.reshape(1, bsz * lctx, d), w_o, hc, mod_c, 5)
    return h, hc


def _s5_kernel(u_ref, t_ref, q_ref, p_ref, a_ref, y_ref, s_ref, x_ref, *, n_ctx, bsz):
    u = u_ref[...]
    nj = u.shape[0] // bsz
    half = s_ref.shape[2] // 2
    for d in range(2):
        s_ref[d] = jnp.dot(u, q_ref[d], preferred_element_type=F32)
    aa = [a_ref[d, 0:1, :] for d in range(2)]
    ab = [a_ref[d, 1:2, :] for d in range(2)]

    def step(d, j, w):
        rows = pl.ds(pl.multiple_of(j * bsz, bsz), bsz)
        x_ref[d, rows, :] = w[:, :half]
        other = jnp.concatenate([w[:, half:], w[:, :half]], axis=1)
        return aa[d] * w + ab[d] * other + s_ref[d, rows, :]

    def run(lo, n, carry):
        def body(i, ws):
            return step(0, lo + i, ws[0]), step(1, lo + n - 1 - i, ws[1])
        return lax.fori_loop(0, n, body, carry)

    zero = jnp.zeros((bsz, 2 * half), F32)
    ws = run(0, n_ctx, (zero, zero))
    run(n_ctx, nj - n_ctx, ws)
    y = None
    for d in range(2):
        yd = (jnp.dot(u, t_ref[d], preferred_element_type=F32)
              + jnp.dot(x_ref[d].astype(BF16), p_ref[d], preferred_element_type=F32))
        y = yd if y is None else y + yd
    y_ref[...] = y


def _s5_scan(u, t_m, q_m, p_m, a_m, n_ctx, bsz):
    ng, rows, w = u.shape
    st2 = p_m.shape[2]
    return pl.pallas_call(
        functools.partial(_s5_kernel, n_ctx=n_ctx, bsz=bsz),
        out_shape=jax.ShapeDtypeStruct((ng, rows, w), F32),
        grid=(ng,),
        in_specs=[pl.BlockSpec((None, rows, w), lambda g: (g, 0, 0)),
                  pl.BlockSpec((2, None, w, w), lambda g: (0, g, 0, 0)),
                  pl.BlockSpec((2, None, w, 2 * st2), lambda g: (0, g, 0, 0)),
                  pl.BlockSpec((2, None, st2, w), lambda g: (0, g, 0, 0)),
                  pl.BlockSpec((2, None, 2, 2 * st2), lambda g: (0, g, 0, 0))],
        out_specs=pl.BlockSpec((None, rows, w), lambda g: (g, 0, 0)),
        scratch_shapes=[pltpu.VMEM((2, rows, 2 * st2), F32), pltpu.VMEM((2, rows, st2), F32)],
        compiler_params=_cparams("parallel"),
        name="s5_scan",
    )(u, t_m, q_m, p_m, a_m)


def _s5_operators(lam_re, lam_im, log_dt, b_re, b_im, c_re, c_im):
    hi = lax.Precision.HIGHEST
    lc = S5_CHUNK
    dt = jnp.exp(log_dt)[..., None]
    mag = jnp.exp(lam_re * dt)
    lb_re, lb_im = mag * jnp.cos(lam_im * dt), mag * jnp.sin(lam_im * dt)
    den = lam_re * lam_re + lam_im * lam_im
    nr, ni = lb_re - 1.0, lb_im
    fr = (nr * lam_re + ni * lam_im) / den
    fi = (ni * lam_re - nr * lam_im) / den
    bb_re = fr[..., None] * b_re - fi[..., None] * b_im
    bb_im = fr[..., None] * b_im + fi[..., None] * b_re
    k = jnp.arange(lc + 1, dtype=F32)[:, None, None, None]
    pmag = jnp.exp(k * (lam_re * dt))
    pw_re, pw_im = pmag * jnp.cos(k * (lam_im * dt)), pmag * jnp.sin(k * (lam_im * dt))
    lbb_re = pw_re[..., None] * bb_re - pw_im[..., None] * bb_im
    lbb_im = pw_re[..., None] * bb_im + pw_im[..., None] * bb_re
    kern = (jnp.einsum('dgop,kdgpc->kdgco', c_re, lbb_re, precision=hi)
            - jnp.einsum('dgop,kdgpc->kdgco', c_im, lbb_im, precision=hi))
    ti = jnp.arange(lc)[:, None]
    to = jnp.arange(lc)[None, :]
    ngrp, nch = lam_re.shape[1], b_re.shape[3]
    nst = lam_re.shape[2]

    def per_dir(d):
        lag = (to - ti) if d == 0 else (ti - to)
        kd = kern[jnp.clip(lag, 0, lc), d]
        kd = jnp.where((lag >= 0)[:, :, None, None, None], kd, 0.0)
        t_m = jnp.transpose(kd, (2, 0, 3, 1, 4)).reshape(ngrp, lc * nch, lc * nch)
        e_in = (lc - 1 - jnp.arange(lc)) if d == 0 else jnp.arange(lc)
        q_re = jnp.transpose(lbb_re[e_in, d], (1, 0, 3, 2)).reshape(ngrp, lc * nch, nst)
        q_im = jnp.transpose(lbb_im[e_in, d], (1, 0, 3, 2)).reshape(ngrp, lc * nch, nst)
        q_m = jnp.concatenate([q_re, q_im, q_im, q_re], axis=-1)
        e_out = (jnp.arange(lc) + 1) if d == 0 else (lc - jnp.arange(lc))
        cl_re = (c_re[d][None] * pw_re[e_out, d][:, :, None, :]
                 - c_im[d][None] * pw_im[e_out, d][:, :, None, :])
        cl_im = (c_re[d][None] * pw_im[e_out, d][:, :, None, :]
                 + c_im[d][None] * pw_re[e_out, d][:, :, None, :])
        p_re = jnp.transpose(cl_re, (1, 3, 0, 2)).reshape(ngrp, nst, lc * nch)
        p_im = jnp.transpose(cl_im, (1, 3, 0, 2)).reshape(ngrp, nst, lc * nch)
        p_m = jnp.concatenate([p_re, -p_im], axis=1)
        a_re, a_im = pw_re[lc, d], pw_im[lc, d]
        a_m = jnp.stack([jnp.concatenate([a_re] * 4, axis=-1),
                         jnp.concatenate([-a_im, a_im, a_im, -a_im], axis=-1)], axis=1)
        return t_m, q_m, p_m, a_m

    f, b = per_dir(0), per_dir(1)
    t_m, q_m, p_m, a_m = (jnp.stack([x, y]) for x, y in zip(f, b))
    return t_m.astype(BF16), q_m.astype(BF16), p_m.astype(BF16), a_m


def _s5_glu_kernel(y_ref, u_ref, ds_ref, wa_ref, wg_ref, h_ref, gate_ref, o_ref, z_ref):
    n = pl.program_id(2)

    @pl.when(n == 0)
    def _():
        z_ref[...] = jax.nn.gelu(y_ref[...] + ds_ref[...] * u_ref[...]).astype(BF16)

    z = z_ref[...]
    a = jnp.dot(z, wa_ref[...], preferred_element_type=F32)
    gt = jnp.dot(z, wg_ref[...], preferred_element_type=F32)
    o_ref[...] = h_ref[...] + gate_ref[...] * (a * jax.nn.sigmoid(gt))


def _s5_glu(y, u, d_skip, w, h, mod, k):
    bsz, rows, d = h.shape
    tm = _row_tile(rows, ROW_TILE)
    tn = _col_tile(d, 512)
    nn = d // tn
    row = pl.BlockSpec((None, tm, d), lambda b, i, n: (b, i, 0))
    col = pl.BlockSpec((None, tm, tn), lambda b, i, n: (b, i, n))
    return pl.pallas_call(
        _s5_glu_kernel,
        out_shape=jax.ShapeDtypeStruct(h.shape, F32),
        grid=(bsz, rows // tm, nn),
        in_specs=[row, row, pl.BlockSpec((1, d), lambda b, i, n: (0, 0)),
                  pl.BlockSpec((d, tn), lambda b, i, n: (0, n)),
                  pl.BlockSpec((d, tn), lambda b, i, n: (0, n + nn)),
                  col, pl.BlockSpec((None, None, 1, tn), lambda b, i, n: (b, k, 0, n))],
        out_specs=col,
        scratch_shapes=[pltpu.VMEM((tm, d), BF16)],
        compiler_params=_cparams("parallel", "parallel", "arbitrary"),
        name="s5_glu",
    )(y, u, d_skip.reshape(1, d), w, w, h, mod)


def _s5_mixer(h, hc, g, mod, mod_c, w_in, lam_re, lam_im, log_dt, b_re, b_im, c_re, c_im,
              d_skip, w_out, with_ctx):
    bsz, n, d = h.shape
    lctx = hc.shape[1] // bsz
    lc, gw = S5_CHUNK, S5_GROUP
    ng = d // gw
    w = w_in.astype(BF16)
    u_l = _proj(h, g, mod, 3, w, F32)
    u_c = _proj(hc, g, mod_c, 3, w, F32)
    ops = _s5_operators(lam_re, lam_im, log_dt, b_re, b_im, c_re, c_im)
    u = jnp.concatenate([u_c.reshape(bsz, lctx, d), u_l], axis=1).astype(BF16)
    nj = (lctx + n) // lc
    u = jnp.transpose(u.reshape(bsz, nj, lc, ng, gw), (3, 1, 0, 2, 4)).reshape(ng, nj * bsz, lc * gw)
    y = _s5_scan(u, *ops, n_ctx=lctx // lc, bsz=bsz)
    y = jnp.transpose(y.reshape(ng, nj, bsz, lc, gw), (2, 1, 3, 0, 4)).reshape(bsz, lctx + n, d)
    w_o = w_out.astype(BF16)
    h = _s5_glu(y[:, lctx:], u_l, d_skip, w_o, h, mod, 5)
    if with_ctx:
        hc = _s5_glu(y[:, :lctx].reshape(1, bsz * lctx, d), u_c, d_skip, w_o, hc, mod_c, 5)
    return h, hc


def _rmsnorm_kernel(x_ref, g_ref, o_ref):
    x = x_ref[...]
    o_ref[...] = x * lax.rsqrt(jnp.mean(x * x, axis=-1, keepdims=True) + EPS) * g_ref[...]


def _final_norm(x, g):
    bsz, rows, d = x.shape
    tm = _row_tile(rows, ROW_TILE)
    row = pl.BlockSpec((None, tm, d), lambda b, i: (b, i, 0))
    return pl.pallas_call(
        _rmsnorm_kernel,
        out_shape=jax.ShapeDtypeStruct(x.shape, F32),
        grid=(bsz, rows // tm),
        in_specs=[row, pl.BlockSpec((1, d), lambda b, i: (0, 0))],
        out_specs=row,
        compiler_params=_cparams("parallel", "parallel"),
        name="final_norm",
    )(x, g.reshape(1, d))


def kernel(x, c, ctx, c_ctx, w_ada, b_ada, g_norm, w_ffn_in, w_ffn_out, g_final, ml_w_in, ml_b_gate, ml_g_head, ml_w_out, wa_w_in, wa_sink, wa_w_out, s5_w_in, s5_lam_re, s5_lam_im, s5_log_dt, s5_b_re, s5_b_im, s5_c_re, s5_c_im, s5_d_skip, s5_w_out):
    bsz, n, d = x.shape
    depth = w_ada.shape[0]
    lctx = ctx.shape[1]
    n_rows = -(-(bsz + 1) // 16) * 16
    c_rows = jnp.concatenate([c, c_ctx[None], jnp.zeros((n_rows - bsz - 1, d), F32)], axis=0)
    mods = _mod_table(c_rows, w_ada, b_ada).reshape(depth, n_rows, N_MOD, 1, d)
    h = x
    hc = ctx.reshape(1, bsz * lctx, d)
    for layer in range(depth):
        has_next = layer < depth - 1
        mod, mod_c = mods[layer, :bsz], mods[layer, bsz:bsz + 1]
        g = g_norm[layer]
        w_in = w_ffn_in[layer].astype(BF16)
        w_out = w_ffn_out[layer].astype(BF16)
        h = _ffn(h, g[0], mod, 0, w_in[0], w_out[0])
        hc = _ffn(hc, g[0], mod_c, 0, w_in[0], w_out[0])
        kind, idx = layer % 3, layer // 3
        if kind == 0:
            h, hc = _mlstm_mixer(h, hc, g[1], mod, mod_c, ml_w_in[idx], ml_b_gate[idx],
                                 ml_g_head[idx], ml_w_out[idx], has_next)
        elif kind == 1:
            h, hc = _wa_mixer(h, hc, g[1], mod, mod_c, wa_w_in[idx], wa_sink[idx],
                              wa_w_out[idx], has_next)
        else:
            h, hc = _s5_mixer(h, hc, g[1], mod, mod_c, s5_w_in[idx], s5_lam_re[idx], s5_lam_im[idx],
                              s5_log_dt[idx], s5_b_re[idx], s5_b_im[idx], s5_c_re[idx],
                              s5_c_im[idx], s5_d_skip[idx], s5_w_out[idx], has_next)
        h = _ffn(h, g[2], mod, 6, w_in[1], w_out[1])
        if has_next:
            hc = _ffn(hc, g[2], mod_c, 6, w_in[1], w_out[1])
    return _final_norm(h, g_final)
```

```python
import functools
import math

import jax
import jax.numpy as jnp
from jax import lax
from jax.experimental import pallas as pl
from jax.experimental.pallas import tpu as pltpu

F32 = jnp.float32
BF16 = jnp.bfloat16

EPS = 1e-6
NEG_BIG = -1e30
N_MOD = 9
ML_HEADS = 8
ML_CHUNK = 256
WA_Q_HEADS = 16
WA_KV_HEADS = 4
WA_WINDOW = 128
WA_BLOCK = 128
GRID_W = 64
ROPE_BASE = 10000.0
S5_GROUP = 16
S5_STATE = 64
S5_CHUNK = 16

LANES = 128
ROW_TILE = 512
ROW_TILE_STREAM = 1024
VMEM_LIMIT = 56 * 1024 * 1024


def _cparams(*sem):
    return pltpu.CompilerParams(dimension_semantics=sem, vmem_limit_bytes=VMEM_LIMIT)


def _row_tile(rows, want):
    return want if rows % want == 0 else rows


def _col_tile(cols, want):
    t = min(want, cols)
    while cols % t:
        t -= LANES
    return t


def _adaln(x, g, shift, scale):
    var = jnp.mean(x * x, axis=-1, keepdims=True)
    return (x * lax.rsqrt(var + EPS) * g) * (1.0 + scale) + shift


ADALN_ROWS = 128


def _adaln_rows(hn_ref, x_ref, g_ref, sh_ref, sc_ref):
    rows = x_ref.shape[0]
    step = ADALN_ROWS if rows % ADALN_ROWS == 0 else rows

    def body(r, carry):
        sl = pl.ds(pl.multiple_of(r * step, step), step)
        hn_ref[sl, :] = _adaln(x_ref[sl, :], g_ref[...], sh_ref[...], sc_ref[...]).astype(BF16)
        return carry

    lax.fori_loop(0, rows // step, body, 0)


def _mod_spec(k, d, shared=False):
    return pl.BlockSpec((None, None, 1, d), lambda b, *_: (0 if shared else b, k, 0, 0))


def _mod_kernel(c_ref, w_ref, b_ref, o_ref):
    c = c_ref[...]
    s = (c * jax.nn.sigmoid(c)).astype(BF16)
    o_ref[...] = jnp.dot(s, w_ref[...].astype(BF16), preferred_element_type=F32) + b_ref[...]


def _mod_table(c_rows, w_ada, b_ada):
    depth, d, nd = w_ada.shape
    r = c_rows.shape[0]
    tn = _col_tile(nd, 1024)
    return pl.pallas_call(
        _mod_kernel,
        out_shape=jax.ShapeDtypeStruct((depth, r, nd), F32),
        grid=(depth, nd // tn),
        in_specs=[pl.BlockSpec((r, d), lambda l, n: (0, 0)),
                  pl.BlockSpec((None, d, tn), lambda l, n: (l, 0, n)),
                  pl.BlockSpec((None, 1, tn), lambda l, n: (l, 0, n))],
        out_specs=pl.BlockSpec((None, r, tn), lambda l, n: (l, 0, n)),
        compiler_params=_cparams("parallel", "parallel"),
        name="mod_table",
    )(c_rows, w_ada, b_ada.reshape(depth, 1, nd))


def _ffn_kernel(x_ref, g_ref, sh_ref, sc_ref, gate_ref, wa_ref, wg_ref, wo_ref, o_ref,
                hn_ref, act_ref, *, nf):
    j = pl.program_id(2)
    tf = wa_ref.shape[1]
    tn = o_ref.shape[1]

    @pl.when(j == 0)
    def _():
        _adaln_rows(hn_ref, x_ref, g_ref, sh_ref, sc_ref)

    @pl.when(j < nf)
    def _():
        hn = hn_ref[...]
        a = jnp.dot(hn, wa_ref[...], preferred_element_type=F32)
        gt = jnp.dot(hn, wg_ref[...], preferred_element_type=F32)
        cols = pl.ds(pl.multiple_of(j * tf, tf), tf)
        act_ref[:, cols] = (a * (gt * jax.nn.sigmoid(gt))).astype(BF16)

    @pl.when(j >= nf)
    def _():
        y = jnp.dot(act_ref[...], wo_ref[...], preferred_element_type=F32)
        cols = pl.ds(pl.multiple_of((j - nf) * tn, tn), tn)
        o_ref[...] = x_ref[:, cols] + (0.5 * gate_ref[...]) * y


def _ffn(x, g, mod, k0, w_in, w_out):
    bsz, rows, d = x.shape
    ff = w_out.shape[0]
    tm = _row_tile(rows, ROW_TILE_STREAM)
    tf = _col_tile(ff, 512)
    tn = _col_tile(d, 256)
    nf = ff // tf

    def fill(j):
        return jnp.minimum(j, nf - 1)

    def drain(j):
        return jnp.maximum(j - nf, 0)

    return pl.pallas_call(
        functools.partial(_ffn_kernel, nf=nf),
        out_shape=jax.ShapeDtypeStruct(x.shape, F32),
        grid=(bsz, rows // tm, nf + d // tn),
        in_specs=[pl.BlockSpec((None, tm, d), lambda b, i, j: (b, i, 0)),
                  pl.BlockSpec((1, d), lambda b, i, j: (0, 0)),
                  _mod_spec(k0, d), _mod_spec(k0 + 1, d),
                  pl.BlockSpec((None, None, 1, tn), lambda b, i, j: (b, k0 + 2, 0, drain(j))),
                  pl.BlockSpec((d, tf), lambda b, i, j: (0, fill(j))),
                  pl.BlockSpec((d, tf), lambda b, i, j: (0, fill(j) + nf)),
                  pl.BlockSpec((ff, tn), lambda b, i, j: (0, drain(j)))],
        out_specs=pl.BlockSpec((None, tm, tn), lambda b, i, j: (b, i, drain(j))),
        scratch_shapes=[pltpu.VMEM((tm, d), BF16), pltpu.VMEM((tm, ff), BF16)],
        compiler_params=_cparams("parallel", "parallel", "arbitrary"),
        name="ffn",
    )(x, g.reshape(1, d), mod, mod, mod, w_in, w_in, w_out)


def _proj_kernel(*refs, n_rope, with_gates, n_scaled, scale):
    x_ref, g_ref, sh_ref, sc_ref, w_ref = refs[:5]
    rest = refs[5:]
    if n_rope:
        cs_ref, sn_ref = rest[:2]
        rest = rest[2:]
    if with_gates:
        wg_ref, bg_ref, o_ref, og_ref, hn_ref = rest
    else:
        o_ref, hn_ref = rest
    n = pl.program_id(2)

    @pl.when(n == 0)
    def _():
        _adaln_rows(hn_ref, x_ref, g_ref, sh_ref, sc_ref)
        if with_gates:
            og_ref[...] = jnp.dot(hn_ref[...], wg_ref[...], preferred_element_type=F32) + bg_ref[...]

    acc = jnp.dot(hn_ref[...], w_ref[...], preferred_element_type=F32)
    if n_scaled:
        acc = acc * jnp.where(n < n_scaled, scale, 1.0)
    if not n_rope:
        o_ref[...] = acc.astype(o_ref.dtype)
    else:
        @pl.when(n >= n_rope)
        def _():
            o_ref[...] = acc.astype(o_ref.dtype)

        @pl.when(n < n_rope)
        def _():
            cs, sn = cs_ref[...], sn_ref[...]
            hd = cs.shape[1]
            for h in range(acc.shape[1] // hd):
                r = acc[:, h * hd:(h + 1) * hd]
                r = r * cs + pltpu.roll(r, hd // 2, axis=1) * sn
                o_ref[:, h * hd:(h + 1) * hd] = r.astype(o_ref.dtype)


def _proj(x, g, mod, k0, w, out_dtype, rope=None, rope_cols=0, gates=None, scale=1.0, scale_cols=0):
    bsz, rows, d = x.shape
    nout = w.shape[1]
    tm = _row_tile(rows, ROW_TILE_STREAM)
    tn = _col_tile(math.gcd(math.gcd(nout, rope_cols), scale_cols), 512)
    n_rope = rope_cols // tn
    row = pl.BlockSpec((None, tm, d), lambda b, i, n: (b, i, 0))
    in_specs = [row, pl.BlockSpec((1, d), lambda b, i, n: (0, 0)),
                _mod_spec(k0, d), _mod_spec(k0 + 1, d),
                pl.BlockSpec((d, tn), lambda b, i, n: (0, n))]
    args = [x, g.reshape(1, d), mod, mod, w]
    if n_rope:
        cs, sn = rope
        hd = cs.shape[1]
        in_specs += [pl.BlockSpec((tm, hd), lambda b, i, n: (i, 0))] * 2
        args += [cs, sn]
    out_shape = jax.ShapeDtypeStruct((bsz, rows, nout), out_dtype)
    out_specs = pl.BlockSpec((None, tm, tn), lambda b, i, n: (b, i, n))
    if gates is not None:
        wg, bg = gates
        ng = wg.shape[1]
        in_specs += [pl.BlockSpec((d, ng), lambda b, i, n: (0, 0)),
                     pl.BlockSpec((1, ng), lambda b, i, n: (0, 0))]
        args += [wg, bg]
        out_shape = (out_shape, jax.ShapeDtypeStruct((bsz, rows, ng), F32))
        out_specs = (out_specs, pl.BlockSpec((None, tm, ng), lambda b, i, n: (b, i, 0)))
    return pl.pallas_call(
        functools.partial(_proj_kernel, n_rope=n_rope, with_gates=gates is not None,
                          n_scaled=scale_cols // tn, scale=scale),
        out_shape=out_shape,
        grid=(bsz, rows // tm, nout // tn),
        in_specs=in_specs,
        out_specs=out_specs,
        scratch_shapes=[pltpu.VMEM((tm, d), BF16)],
        compiler_params=_cparams("parallel", "parallel", "arbitrary"),
        name="proj",
    )(*args)


def _out_proj_kernel(y_ref, w_ref, h_ref, gate_ref, o_ref):
    acc = jnp.dot(y_ref[...], w_ref[...], preferred_element_type=F32)
    o_ref[...] = h_ref[...] + gate_ref[...] * acc


def _out_proj(y, w, h, mod, k):
    bsz, rows, d = h.shape
    dk = y.shape[2]
    tm = _row_tile(rows, ROW_TILE)
    return pl.pallas_call(
        _out_proj_kernel,
        out_shape=jax.ShapeDtypeStruct(h.shape, F32),
        grid=(bsz, rows // tm),
        in_specs=[pl.BlockSpec((None, tm, dk), lambda b, i: (b, i, 0)),
                  pl.BlockSpec((dk, d), lambda b, i: (0, 0)),
                  pl.BlockSpec((None, tm, d), lambda b, i: (b, i, 0)),
                  _mod_spec(k, d)],
        out_specs=pl.BlockSpec((None, tm, d), lambda b, i: (b, i, 0)),
        compiler_params=_cparams("parallel", "parallel"),
        name="out_proj",
    )(y, w, h, mod)


def _ml_chunk(q, k, v, icol, fcol, irow, frow, c_mem, n_mem, m_run, tri, tri_t):
    b_col = jnp.sum(jnp.where(tri, frow, 0.0), axis=1, keepdims=True)
    b_row = jnp.sum(jnp.where(tri_t, fcol, 0.0), axis=0, keepdims=True)
    b_end = jnp.sum(frow, axis=1, keepdims=True)
    g_col = b_end - b_col + icol
    g_row = b_end - b_row + irow
    m_new = jnp.maximum(b_end + m_run, jnp.max(g_row, axis=1, keepdims=True))
    wk = jnp.exp(g_col - m_new) * k.astype(F32)
    dec = jnp.exp(b_end + m_run - m_new)
    c_new = dec * c_mem + lax.dot_general(wk.astype(BF16), v, (((0,), (0,)), ((), ())),
                                          preferred_element_type=F32)
    n_new = dec * n_mem + jnp.sum(wk, axis=0, keepdims=True)
    d_log = jnp.where(tri, b_col - b_row + irow, NEG_BIG)
    m_prev = b_col + m_run
    m_q = jnp.maximum(m_prev, jnp.max(d_log, axis=1, keepdims=True))
    s = lax.dot_general(q, k, (((1,), (1,)), ((), ())), preferred_element_type=F32)
    s = s * jnp.exp(d_log - m_q)
    dq = jnp.exp(m_prev - m_q)
    num = (jnp.dot(s.astype(BF16), v, preferred_element_type=F32)
           + dq * jnp.dot(q, c_mem.astype(BF16), preferred_element_type=F32))
    den = (jnp.sum(s, axis=1, keepdims=True)
           + dq * jnp.sum(q.astype(F32) * n_mem, axis=1, keepdims=True))
    h = num / jnp.maximum(jnp.abs(den), jnp.exp(-m_q))
    return h, c_new, n_new, m_new


def _mlstm_kernel(cq, ck, cv, co, lq, lk, lv, lo, cgc, cgr, lgc, lgr, gh_ref,
                  yc_ref, yl_ref, hf_ref, hb_ref, c_ref, n_ref, m_ref):
    hd = lq.shape[1]
    lc = ML_CHUNK
    rows = lax.broadcasted_iota(jnp.int32, (lc, lc), 0)
    cols = lax.broadcasted_iota(jnp.int32, (lc, lc), 1)
    causal = cols <= rows
    anti = cols >= rows

    c_ref[...] = jnp.zeros_like(c_ref)
    n_ref[...] = jnp.zeros_like(n_ref)
    m_ref[...] = jnp.full_like(m_ref, NEG_BIG)

    def one(q_ref, k_ref, v_ref, gc_ref, gr_ref, out_ref, d, start):
        sl = pl.ds(start, lc)
        gc = gc_ref[sl, :]
        gr = gr_ref[:, sl]
        icol, fcol = gc[:, 2 * d:2 * d + 1], jax.nn.log_sigmoid(gc[:, 2 * d + 1:2 * d + 2])
        irow, frow = gr[2 * d:2 * d + 1, :], jax.nn.log_sigmoid(gr[2 * d + 1:2 * d + 2, :])
        h, c_new, n_new, m_new = _ml_chunk(
            q_ref[sl, :], k_ref[sl, :], v_ref[sl, :], icol, fcol, irow, frow,
            c_ref[d], n_ref[d], m_ref[d], causal if d == 0 else anti, anti if d == 0 else causal)
        c_ref[d] = c_new
        n_ref[d] = n_new
        m_ref[d] = m_new
        out_ref[sl, :] = h

    def scan(q_ref, k_ref, v_ref, o_ref, gc_ref, gr_ref, y_ref):
        nc = q_ref.shape[0] // lc

        def body(c, carry):
            one(q_ref, k_ref, v_ref, gc_ref, gr_ref, hf_ref, 0, pl.multiple_of(c * lc, lc))
            one(q_ref, k_ref, v_ref, gc_ref, gr_ref, hb_ref, 1, pl.multiple_of((nc - 1 - c) * lc, lc))
            return carry

        lax.fori_loop(0, nc, body, 0)

        def merge(c, carry):
            sl = pl.ds(pl.multiple_of(c * lc, lc), lc)
            hs = hf_ref[sl, :] + hb_ref[sl, :]
            hs = hs * lax.rsqrt(jnp.mean(hs * hs, axis=-1, keepdims=True) + EPS) * gh_ref[...]
            y_ref[sl, :] = (jax.nn.sigmoid(o_ref[sl, :].astype(F32)) * hs).astype(y_ref.dtype)
            return carry

        lax.fori_loop(0, nc, merge, 0)

    scan(cq, ck, cv, co, cgc, cgr, yc_ref)
    scan(lq, lk, lv, lo, lgc, lgr, yl_ref)


def _mlstm_scan(zc, zl, gc, gl, g_head):
    bsz, n, d4 = zl.shape
    lctx = zc.shape[1]
    d = d4 // 4
    nh = ML_HEADS
    hd = d // nh
    assert n % ML_CHUNK == 0 and lctx % ML_CHUNK == 0

    def gate_views(g):
        length = g.shape[1]
        g = g[:, :, :4 * nh].reshape(bsz, length, 4, nh)
        return jnp.transpose(g, (0, 3, 1, 2)), jnp.transpose(g, (0, 3, 2, 1))

    cgc, cgr = gate_views(gc)
    lgc, lgr = gate_views(gl)

    def zspec(length, k):
        return pl.BlockSpec((None, length, hd), lambda b, h: (b, 0, k * nh + h))

    def gspecs(length):
        return [pl.BlockSpec((None, None, length, 4), lambda b, h: (b, h, 0, 0)),
                pl.BlockSpec((None, None, 4, length), lambda b, h: (b, h, 0, 0))]

    lmax = max(n, lctx)
    return pl.pallas_call(
        _mlstm_kernel,
        out_shape=(jax.ShapeDtypeStruct((bsz, lctx, d), BF16),
                   jax.ShapeDtypeStruct((bsz, n, d), BF16)),
        grid=(bsz, nh),
        in_specs=([zspec(lctx, k) for k in range(4)] + [zspec(n, k) for k in range(4)]
                  + gspecs(lctx) + gspecs(n)
                  + [pl.BlockSpec((None, 1, hd), lambda b, h: (h, 0, 0))]),
        out_specs=(pl.BlockSpec((None, lctx, hd), lambda b, h: (b, 0, h)),
                   pl.BlockSpec((None, n, hd), lambda b, h: (b, 0, h))),
        scratch_shapes=[pltpu.VMEM((lmax, hd), F32), pltpu.VMEM((lmax, hd), F32),
                        pltpu.VMEM((2, hd, hd), F32), pltpu.VMEM((2, 1, hd), F32),
                        pltpu.VMEM((2, 1, 1), F32)],
        compiler_params=_cparams("parallel", "parallel"),
        name="mlstm_scan",
    )(zc, zc, zc, zc, zl, zl, zl, zl, cgc, cgr, lgc, lgr, g_head.reshape(nh, 1, hd))


def _mlstm_mixer(h, hc, g, mod, mod_c, w_in, b_gate, g_head, w_out, with_ctx):
    bsz, n, d = h.shape
    lctx = hc.shape[1] // bsz
    w_main = w_in[:, :4 * d].astype(BF16)
    ng = 4 * ML_HEADS
    ngp = -(-ng // LANES) * LANES
    w_gate = jnp.pad(w_in[:, 4 * d:], ((0, 0), (0, ngp - ng))).astype(BF16)
    bias = jnp.pad(b_gate.reshape(1, ng), ((0, 0), (0, ngp - ng)))
    q_scale = (d // ML_HEADS) ** -0.5
    zl, gl = _proj(h, g, mod, 3, w_main, BF16, gates=(w_gate, bias), scale=q_scale, scale_cols=d)
    zc, gc = _proj(hc, g, mod_c, 3, w_main, BF16, gates=(w_gate, bias), scale=q_scale, scale_cols=d)
    yc, yl = _mlstm_scan(zc.reshape(bsz, lctx, 4 * d), zl, gc.reshape(bsz, lctx, ngp), gl, g_head)
    w_o = w_out.astype(BF16)
    h = _out_proj(yl, w_o, h, mod, 5)
    if with_ctx:
        hc = _out_proj(yc.reshape(1, bsz * lctx, d), w_o, hc, mod_c, 5)
    return h, hc


def _wa_kernel(sink_ref, q_ref, *refs, local, scale):
    if local:
        k_ref, v_ref, kc_ref, vc_ref, o_ref = refs
    else:
        kc_ref, vc_ref, o_ref = refs
    tq = q_ref.shape[0]
    hd = kc_ref.shape[1] // WA_KV_HEADS
    grp = WA_Q_HEADS // WA_KV_HEADS
    if local:
        n = k_ref.shape[0]
        span = 3 * WA_BLOCK
        start = pl.program_id(1) * tq
        ks = pl.multiple_of(jnp.clip(start - WA_BLOCK, 0, n - span), WA_BLOCK)
        q_pos = start + (lax.broadcasted_iota(jnp.int32, (grp * tq, span), 0) & (tq - 1))
        k_pos = ks + lax.broadcasted_iota(jnp.int32, (grp * tq, span), 1)
        ok = jnp.abs(q_pos - k_pos) <= WA_WINDOW
    dims = (((1,), (1,)), ((), ()))
    for kv in range(WA_KV_HEADS):
        cs = slice(kv * hd, (kv + 1) * hd)
        q4 = jnp.concatenate([q_ref[:, (kv * grp + j) * hd:(kv * grp + j + 1) * hd]
                              for j in range(grp)], axis=0)
        sink = jnp.concatenate([jnp.full((tq, 1), sink_ref[0, kv * grp + j], F32)
                                for j in range(grp)], axis=0)
        kc, vc = kc_ref[:, cs], vc_ref[:, cs]
        s_ctx = lax.dot_general(q4, kc, dims, preferred_element_type=F32) * scale
        m = jnp.maximum(jnp.max(s_ctx, axis=1, keepdims=True), sink)
        if local:
            kl, vl = k_ref[pl.ds(ks, span), cs], v_ref[pl.ds(ks, span), cs]
            s_loc = lax.dot_general(q4, kl, dims, preferred_element_type=F32) * scale
            s_loc = jnp.where(ok, s_loc, NEG_BIG)
            m = jnp.maximum(m, jnp.max(s_loc, axis=1, keepdims=True))
            p_loc = jnp.exp(s_loc - m)
        p_ctx = jnp.exp(s_ctx - m)
        den = jnp.sum(p_ctx, axis=1, keepdims=True) + jnp.exp(sink - m)
        if local:
            den = den + jnp.sum(p_loc, axis=1, keepdims=True)
        inv = 1.0 / den
        o = jnp.dot((p_ctx * inv).astype(BF16), vc, preferred_element_type=F32)
        if local:
            o = o + jnp.dot((p_loc * inv).astype(BF16), vl, preferred_element_type=F32)
        for j in range(grp):
            hq = kv * grp + j
            o_ref[:, hq * hd:(hq + 1) * hd] = o[j * tq:(j + 1) * tq].astype(o_ref.dtype)


def _wa_attention(sink, zq, zkv, zc, local):
    bsz, lq, _ = zq.shape
    lctx = zc.shape[1]
    hd = zq.shape[2] // (WA_Q_HEADS + 2 * WA_KV_HEADS)
    qd, kd = WA_Q_HEADS * hd, WA_KV_HEADS * hd
    tq = WA_BLOCK
    kblk = qd // kd
    in_specs = [pl.BlockSpec(memory_space=pltpu.SMEM),
                pl.BlockSpec((None, tq, qd), lambda b, i: (b, i, 0))]
    args = [sink.reshape(1, WA_Q_HEADS), zq]
    if local:
        n = zkv.shape[1]
        in_specs += [pl.BlockSpec((None, n, kd), lambda b, i: (b, 0, kblk)),
                     pl.BlockSpec((None, n, kd), lambda b, i: (b, 0, kblk + 1))]
        args += [zkv, zkv]
    in_specs += [pl.BlockSpec((None, lctx, kd), lambda b, i: (b, 0, kblk)),
                 pl.BlockSpec((None, lctx, kd), lambda b, i: (b, 0, kblk + 1))]
    args += [zc, zc]
    return pl.pallas_call(
        functools.partial(_wa_kernel, local=local, scale=hd ** -0.5),
        out_shape=jax.ShapeDtypeStruct((bsz, lq, qd), BF16),
        grid=(bsz, lq // tq),
        in_specs=in_specs,
        out_specs=pl.BlockSpec((None, tq, qd), lambda b, i: (b, i, 0)),
        compiler_params=_cparams("parallel", "parallel"),
        name="wa_attention",
    )(*args)


def _rope_tables(n, hd):
    rows = n // GRID_W
    row = jnp.repeat(jnp.arange(rows, dtype=F32), GRID_W)
    col = jnp.tile(jnp.arange(GRID_W, dtype=F32), rows)
    n_freq = hd // 4
    inv = ROPE_BASE ** (-jnp.arange(n_freq, dtype=F32) / n_freq)
    ang = jnp.concatenate([row[:, None] * inv, col[:, None] * inv], axis=-1)
    cos, sin = jnp.cos(ang), jnp.sin(ang)
    return jnp.concatenate([cos, cos], axis=-1), jnp.concatenate([-sin, sin], axis=-1)


def _wa_mixer(h, hc, g, mod, mod_c, w_in, sink, w_out, with_ctx):
    bsz, n, d = h.shape
    lctx = hc.shape[1] // bsz
    hd = d // WA_Q_HEADS
    w = w_in.astype(BF16)
    rope_cols = (WA_Q_HEADS + WA_KV_HEADS) * hd
    zl = _proj(h, g, mod, 3, w, BF16, rope=_rope_tables(n, hd), rope_cols=rope_cols)
    zc = _proj(hc, g, mod_c, 3, w, BF16).reshape(bsz, lctx, -1)
    sink = sink.astype(F32)
    w_o = w_out.astype(BF16)
    h = _out_proj(_wa_attention(sink, zl, zl, zc, True), w_o, h, mod, 5)
    if with_ctx:
        yc = _wa_attention(sink, zc, None, zc, False)
        hc = _out_proj(yc.reshape(1, bsz * lctx, d), w_o, hc, mod_c, 5)
    return h, hc


def _s5_proj_kernel(x_ref, g_ref, sh_ref, sc_ref, w_ref, o_ref, of_ref, hn_ref):
    n = pl.program_id(2)

    @pl.when(n == 0)
    def _():
        _adaln_rows(hn_ref, x_ref, g_ref, sh_ref, sc_ref)

    acc = jnp.dot(hn_ref[...], w_ref[...], preferred_element_type=F32)
    o_ref[...] = acc
    tm = acc.shape[0]
    for k in range(acc.shape[1] // LANES):
        blk = acc[:, k * LANES:(k + 1) * LANES].astype(BF16)
        of_ref[k] = blk.reshape(tm // S5_CHUNK, S5_CHUNK, LANES)


def _s5_proj(x, g, mod, k0, w, shared_mod):
    bsz, rows, d = x.shape
    tm = _row_tile(rows, ROW_TILE_STREAM)
    tn = _col_tile(d, 512)
    lc = S5_CHUNK
    return pl.pallas_call(
        _s5_proj_kernel,
        out_shape=(jax.ShapeDtypeStruct((bsz, rows, d), F32),
                   jax.ShapeDtypeStruct((d // LANES, rows // lc, bsz, lc, LANES), BF16)),
        grid=(bsz, rows // tm, d // tn),
        in_specs=[pl.BlockSpec((None, tm, d), lambda b, i, n: (b, i, 0)),
                  pl.BlockSpec((1, d), lambda b, i, n: (0, 0)),
                  _mod_spec(k0, d, shared_mod), _mod_spec(k0 + 1, d, shared_mod),
                  pl.BlockSpec((d, tn), lambda b, i, n: (0, n))],
        out_specs=(pl.BlockSpec((None, tm, tn), lambda b, i, n: (b, i, n)),
                   pl.BlockSpec((tn // LANES, tm // lc, None, lc, LANES), lambda b, i, n: (n, i, b, 0, 0))),
        scratch_shapes=[pltpu.VMEM((tm, d), BF16)],
        compiler_params=_cparams("parallel", "parallel", "arbitrary"),
        name="s5_proj",
    )(x, g.reshape(1, d), mod, mod, w)


S5_PACKETS = LANES // S5_GROUP
S5_REGROUP_ROWS = 16


def _packet_transpose(vs):
    lane = lax.broadcasted_iota(jnp.int32, vs[0].shape, 1)
    d = S5_PACKETS // 2
    while d:
        low = (lane & (d * S5_GROUP)) == 0
        nxt = list(vs)
        for i in range(S5_PACKETS):
            if not i & d:
                nxt[i] = jnp.where(low, vs[i], pltpu.roll(vs[i + d], d * S5_GROUP, axis=1))
                nxt[i + d] = jnp.where(low, pltpu.roll(vs[i], LANES - d * S5_GROUP, axis=1), vs[i + d])
        vs = nxt
        d //= 2
    return vs


def _s5_group_kernel(u_ref, o_ref):
    def body(r, carry):
        rows = pl.ds(pl.multiple_of(r * S5_REGROUP_ROWS, S5_REGROUP_ROWS), S5_REGROUP_ROWS)
        for hf in range(S5_CHUNK // S5_PACKETS):
            base = hf * S5_PACKETS
            vs = [pltpu.bitcast(u_ref[rows, (base + t) * LANES:(base + t + 1) * LANES], jnp.uint32)
                  for t in range(S5_PACKETS)]
            for gq, v in enumerate(_packet_transpose(vs)):
                o_ref[gq, rows, hf * LANES:(hf + 1) * LANES] = pltpu.bitcast(v, o_ref.dtype)
        return carry

    lax.fori_loop(0, u_ref.shape[0] // S5_REGROUP_ROWS, body, 0, unroll=4)


def _s5_group(u):
    g8, rows, wide = u.shape
    tr = _row_tile(rows, ROW_TILE)
    w = wide // S5_PACKETS
    return pl.pallas_call(
        _s5_group_kernel,
        out_shape=jax.ShapeDtypeStruct((g8 * S5_PACKETS, rows, w), u.dtype),
        grid=(g8, rows // tr),
        in_specs=[pl.BlockSpec((None, tr, wide), lambda g, i: (g, i, 0))],
        out_specs=pl.BlockSpec((S5_PACKETS, tr, w), lambda g, i: (g, i, 0)),
        compiler_params=_cparams("parallel", "parallel"),
        name="s5_group",
    )(u)


def _s5_ungroup_kernel(y_ref, o_ref):
    def body(r, carry):
        rows = pl.ds(pl.multiple_of(r * S5_REGROUP_ROWS, S5_REGROUP_ROWS), S5_REGROUP_ROWS)
        for hf in range(S5_CHUNK // S5_PACKETS):
            base = hf * S5_PACKETS
            vs = [pltpu.bitcast(y_ref[gq, rows, hf * LANES:(hf + 1) * LANES], jnp.uint32)
                  for gq in range(S5_PACKETS)]
            for t, v in enumerate(_packet_transpose(vs)):
                o_ref[rows, (base + t) * LANES:(base + t + 1) * LANES] = pltpu.bitcast(v, o_ref.dtype)
        return carry

    lax.fori_loop(0, o_ref.shape[0] // S5_REGROUP_ROWS, body, 0, unroll=2)


def _s5_ungroup(y):
    ng, rows, w = y.shape
    tr = _row_tile(rows, ROW_TILE)
    wide = w * S5_PACKETS
    return pl.pallas_call(
        _s5_ungroup_kernel,
        out_shape=jax.ShapeDtypeStruct((ng // S5_PACKETS, rows, wide), y.dtype),
        grid=(ng // S5_PACKETS, rows // tr),
        in_specs=[pl.BlockSpec((S5_PACKETS, tr, w), lambda g, i: (g, i, 0))],
        out_specs=pl.BlockSpec((None, tr, wide), lambda g, i: (g, i, 0)),
        compiler_params=_cparams("parallel", "parallel"),
        name="s5_ungroup",
    )(y)


def _s5_kernel(uc_ref, ul_ref, t_ref, q_ref, p_ref, a_ref, yc_ref, yl_ref,
               sc_ref, sl_ref, xc_ref, xl_ref, *, bsz):
    half = sc_ref.shape[2] // 2
    for u_ref, s_ref in ((uc_ref, sc_ref), (ul_ref, sl_ref)):
        for d in range(2):
            s_ref[d] = jnp.dot(u_ref[...], q_ref[d], preferred_element_type=F32)
    aa = [a_ref[d, 0:1, :] for d in range(2)]
    ab = [a_ref[d, 1:2, :] for d in range(2)]

    def run(s_ref, x_ref, carry):
        n = s_ref.shape[1] // bsz

        def step(d, j, w):
            rows = pl.ds(pl.multiple_of(j * bsz, bsz), bsz)
            x_ref[d, rows, :] = w[:, :half]
            other = jnp.concatenate([w[:, half:], w[:, :half]], axis=1)
            return aa[d] * w + ab[d] * other + s_ref[d, rows, :]

        def body(i, ws):
            return step(0, i, ws[0]), step(1, n - 1 - i, ws[1])

        return lax.fori_loop(0, n, body, carry)

    zero = jnp.zeros((bsz, 2 * half), F32)
    run(sl_ref, xl_ref, run(sc_ref, xc_ref, (zero, zero)))
    for u_ref, x_ref, y_ref in ((uc_ref, xc_ref, yc_ref), (ul_ref, xl_ref, yl_ref)):
        y = jnp.dot(u_ref[...], t_ref[...], preferred_element_type=F32)
        for d in range(2):
            y = y + jnp.dot(x_ref[d].astype(BF16), p_ref[d], preferred_element_type=F32)
        y_ref[...] = y


def _s5_scan(uc, ul, t_m, q_m, p_m, a_m, bsz):
    ng, rc, w = uc.shape
    rl = ul.shape[1]
    st2 = p_m.shape[2]

    def rows_spec(r):
        return pl.BlockSpec((None, r, w), lambda g: (g, 0, 0))

    return pl.pallas_call(
        functools.partial(_s5_kernel, bsz=bsz),
        out_shape=(jax.ShapeDtypeStruct((ng, rc, w), F32), jax.ShapeDtypeStruct((ng, rl, w), F32)),
        grid=(ng,),
        in_specs=[rows_spec(rc), rows_spec(rl),
                  pl.BlockSpec((None, w, w), lambda g: (g, 0, 0)),
                  pl.BlockSpec((2, None, w, 2 * st2), lambda g: (0, g, 0, 0)),
                  pl.BlockSpec((2, None, st2, w), lambda g: (0, g, 0, 0)),
                  pl.BlockSpec((2, None, 2, 2 * st2), lambda g: (0, g, 0, 0))],
        out_specs=(rows_spec(rc), rows_spec(rl)),
        scratch_shapes=[pltpu.VMEM((2, rc, 2 * st2), F32), pltpu.VMEM((2, rl, 2 * st2), F32),
                        pltpu.VMEM((2, rc, st2), F32), pltpu.VMEM((2, rl, st2), F32)],
        compiler_params=_cparams("parallel"),
        name="s5_scan",
    )(uc, ul, t_m, q_m, p_m, a_m)


def _s5_operators(lam_re, lam_im, log_dt, b_re, b_im, c_re, c_im):
    hi = lax.Precision.HIGHEST
    lc = S5_CHUNK
    dt = jnp.exp(log_dt)[..., None]
    mag = jnp.exp(lam_re * dt)
    lb_re, lb_im = mag * jnp.cos(lam_im * dt), mag * jnp.sin(lam_im * dt)
    den = lam_re * lam_re + lam_im * lam_im
    nr, ni = lb_re - 1.0, lb_im
    fr = (nr * lam_re + ni * lam_im) / den
    fi = (ni * lam_re - nr * lam_im) / den
    bb_re = fr[..., None] * b_re - fi[..., None] * b_im
    bb_im = fr[..., None] * b_im + fi[..., None] * b_re
    k = jnp.arange(lc + 1, dtype=F32)[:, None, None, None]
    pmag = jnp.exp(k * (lam_re * dt))
    pw_re, pw_im = pmag * jnp.cos(k * (lam_im * dt)), pmag * jnp.sin(k * (lam_im * dt))
    lbb_re = pw_re[..., None] * bb_re - pw_im[..., None] * bb_im
    lbb_im = pw_re[..., None] * bb_im + pw_im[..., None] * bb_re
    kern = (jnp.einsum('dgop,kdgpc->kdgco', c_re, lbb_re, precision=hi)
            - jnp.einsum('dgop,kdgpc->kdgco', c_im, lbb_im, precision=hi))
    ti = jnp.arange(lc)[:, None]
    to = jnp.arange(lc)[None, :]
    ngrp, nch = lam_re.shape[1], b_re.shape[3]
    nst = lam_re.shape[2]

    def per_dir(d):
        lag = (to - ti) if d == 0 else (ti - to)
        kd = kern[jnp.clip(lag, 0, lc), d]
        kd = jnp.where((lag >= 0)[:, :, None, None, None], kd, 0.0)
        t_m = jnp.transpose(kd, (2, 0, 3, 1, 4)).reshape(ngrp, lc * nch, lc * nch)
        e_in = (lc - 1 - jnp.arange(lc)) if d == 0 else jnp.arange(lc)
        q_re = jnp.transpose(lbb_re[e_in, d], (1, 0, 3, 2)).reshape(ngrp, lc * nch, nst)
        q_im = jnp.transpose(lbb_im[e_in, d], (1, 0, 3, 2)).reshape(ngrp, lc * nch, nst)
        q_m = jnp.concatenate([q_re, q_im, q_im, q_re], axis=-1)
        e_out = (jnp.arange(lc) + 1) if d == 0 else (lc - jnp.arange(lc))
        cl_re = (c_re[d][None] * pw_re[e_out, d][:, :, None, :]
                 - c_im[d][None] * pw_im[e_out, d][:, :, None, :])
        cl_im = (c_re[d][None] * pw_im[e_out, d][:, :, None, :]
                 + c_im[d][None] * pw_re[e_out, d][:, :, None, :])
        p_re = jnp.transpose(cl_re, (1, 3, 0, 2)).reshape(ngrp, nst, lc * nch)
        p_im = jnp.transpose(cl_im, (1, 3, 0, 2)).reshape(ngrp, nst, lc * nch)
        p_m = jnp.concatenate([p_re, -p_im], axis=1)
        a_re, a_im = pw_re[lc, d], pw_im[lc, d]
        a_m = jnp.stack([jnp.concatenate([a_re] * 4, axis=-1),
                         jnp.concatenate([-a_im, a_im, a_im, -a_im], axis=-1)], axis=1)
        return t_m, q_m, p_m, a_m

    f, b = per_dir(0), per_dir(1)
    q_m, p_m, a_m = (jnp.stack([x, y]) for x, y in zip(f[1:], b[1:]))
    return (f[0] + b[0]).astype(BF16), q_m.astype(BF16), p_m.astype(BF16), a_m


def _s5_glu_kernel(y_ref, u_ref, ds_ref, wa_ref, wg_ref, h_ref, gate_ref, o_ref, z_ref):
    n = pl.program_id(2)

    @pl.when(n == 0)
    def _():
        tm = u_ref.shape[0]
        for k in range(y_ref.shape[0]):
            cols = slice(k * LANES, (k + 1) * LANES)
            y = y_ref[k].reshape(tm, LANES)
            z_ref[:, cols] = jax.nn.gelu(y + ds_ref[:, cols] * u_ref[:, cols]).astype(BF16)

    z = z_ref[...]
    a = jnp.dot(z, wa_ref[...], preferred_element_type=F32)
    gt = jnp.dot(z, wg_ref[...], preferred_element_type=F32)
    o_ref[...] = h_ref[...] + gate_ref[...] * (a * jax.nn.sigmoid(gt))


def _s5_glu(y, u, d_skip, w, h, mod, k, shared_mod):
    bsz, rows, d = h.shape
    tm = _row_tile(rows, ROW_TILE)
    tn = _col_tile(d, 512)
    nn = d // tn
    lc = S5_CHUNK
    row = pl.BlockSpec((None, tm, d), lambda b, i, n: (b, i, 0))
    col = pl.BlockSpec((None, tm, tn), lambda b, i, n: (b, i, n))
    return pl.pallas_call(
        _s5_glu_kernel,
        out_shape=jax.ShapeDtypeStruct(h.shape, F32),
        grid=(bsz, rows // tm, nn),
        in_specs=[pl.BlockSpec((d // LANES, tm // lc, None, lc, LANES), lambda b, i, n: (0, i, b, 0, 0)),
                  row, pl.BlockSpec((1, d), lambda b, i, n: (0, 0)),
                  pl.BlockSpec((d, tn), lambda b, i, n: (0, n)),
                  pl.BlockSpec((d, tn), lambda b, i, n: (0, n + nn)),
                  col, pl.BlockSpec((None, None, 1, tn),
                                    lambda b, i, n: (0 if shared_mod else b, k, 0, n))],
        out_specs=col,
        scratch_shapes=[pltpu.VMEM((tm, d), BF16)],
        compiler_params=_cparams("parallel", "parallel", "arbitrary"),
        name="s5_glu",
    )(y, u, d_skip.reshape(1, d), w, w, h, mod)


def _s5_mixer(h, hc, g, mod, mod_c, w_in, lam_re, lam_im, log_dt, b_re, b_im, c_re, c_im,
              d_skip, w_out, with_ctx):
    bsz, n, d = h.shape
    lctx = hc.shape[1] // bsz
    lc = S5_CHUNK
    g8 = d // LANES
    w = w_in.astype(BF16)
    hc3 = hc.reshape(bsz, lctx, d)
    u_l, uf_l = _s5_proj(h, g, mod, 3, w, False)
    u_c, uf_c = _s5_proj(hc3, g, mod_c, 3, w, True)
    ops = _s5_operators(lam_re, lam_im, log_dt, b_re, b_im, c_re, c_im)
    up_l = _s5_group(uf_l.reshape(g8, n // lc * bsz, lc * LANES))
    up_c = _s5_group(uf_c.reshape(g8, lctx // lc * bsz, lc * LANES))
    yp_c, yp_l = _s5_scan(up_c, up_l, *ops, bsz=bsz)
    y_l = _s5_ungroup(yp_l).reshape(g8, n // lc, bsz, lc, LANES)
    w_o = w_out.astype(BF16)
    h = _s5_glu(y_l, u_l, d_skip, w_o, h, mod, 5, False)
    if with_ctx:
        y_c = _s5_ungroup(yp_c).reshape(g8, lctx // lc, bsz, lc, LANES)
        hc = _s5_glu(y_c, u_c, d_skip, w_o, hc3, mod_c, 5, True).reshape(1, bsz * lctx, d)
    return h, hc


def _rmsnorm_kernel(x_ref, g_ref, o_ref):
    x = x_ref[...]
    o_ref[...] = x * lax.rsqrt(jnp.mean(x * x, axis=-1, keepdims=True) + EPS) * g_ref[...]


def _final_norm(x, g):
    bsz, rows, d = x.shape
    tm = _row_tile(rows, ROW_TILE)
    row = pl.BlockSpec((None, tm, d), lambda b, i: (b, i, 0))
    return pl.pallas_call(
        _rmsnorm_kernel,
        out_shape=jax.ShapeDtypeStruct(x.shape, F32),
        grid=(bsz, rows // tm),
        in_specs=[row, pl.BlockSpec((1, d), lambda b, i: (0, 0))],
        out_specs=row,
        compiler_params=_cparams("parallel", "parallel"),
        name="final_norm",
    )(x, g.reshape(1, d))


def kernel(x, c, ctx, c_ctx, w_ada, b_ada, g_norm, w_ffn_in, w_ffn_out, g_final, ml_w_in, ml_b_gate, ml_g_head, ml_w_out, wa_w_in, wa_sink, wa_w_out, s5_w_in, s5_lam_re, s5_lam_im, s5_log_dt, s5_b_re, s5_b_im, s5_c_re, s5_c_im, s5_d_skip, s5_w_out):
    bsz, n, d = x.shape
    depth = w_ada.shape[0]
    lctx = ctx.shape[1]
    n_rows = -(-(bsz + 1) // 16) * 16
    c_rows = jnp.concatenate([c, c_ctx[None], jnp.zeros((n_rows - bsz - 1, d), F32)], axis=0)
    mods = _mod_table(c_rows, w_ada, b_ada).reshape(depth, n_rows, N_MOD, 1, d)
    h = x
    hc = ctx.reshape(1, bsz * lctx, d)
    for layer in range(depth):
        has_next = layer < depth - 1
        mod, mod_c = mods[layer, :bsz], mods[layer, bsz:bsz + 1]
        g = g_norm[layer]
        w_in = w_ffn_in[layer].astype(BF16)
        w_out = w_ffn_out[layer].astype(BF16)
        h = _ffn(h, g[0], mod, 0, w_in[0], w_out[0])
        hc = _ffn(hc, g[0], mod_c, 0, w_in[0], w_out[0])
        kind, idx = layer % 3, layer // 3
        if kind == 0:
            h, hc = _mlstm_mixer(h, hc, g[1], mod, mod_c, ml_w_in[idx], ml_b_gate[idx],
                                 ml_g_head[idx], ml_w_out[idx], has_next)
        elif kind == 1:
            h, hc = _wa_mixer(h, hc, g[1], mod, mod_c, wa_w_in[idx], wa_sink[idx],
                              wa_w_out[idx], has_next)
        else:
            h, hc = _s5_mixer(h, hc, g[1], mod, mod_c, s5_w_in[idx], s5_lam_re[idx], s5_lam_im[idx],
                              s5_log_dt[idx], s5_b_re[idx], s5_b_im[idx], s5_c_re[idx],
                              s5_c_im[idx], s5_d_skip[idx], s5_w_out[idx], has_next)
        h = _ffn(h, g[2], mod, 6, w_in[1], w_out[1])
        if has_next:
            hc = _ffn(hc, g[2], mod_c, 6, w_in[1], w_out[1])
    return _final_norm(h, g_final)
```

```python
import functools
import math

import jax
import jax.numpy as jnp
from jax import lax
from jax.experimental import pallas as pl
from jax.experimental.pallas import tpu as pltpu

F32 = jnp.float32
BF16 = jnp.bfloat16

EPS = 1e-6
NEG_BIG = -1e30
N_MOD = 9
ML_HEADS = 8
ML_CHUNK = 512
ML_MERGE_ROWS = 256
WA_Q_HEADS = 16
WA_KV_HEADS = 4
WA_WINDOW = 128
WA_BLOCK = 128
GRID_W = 64
ROPE_BASE = 10000.0
S5_GROUP = 16
S5_STATE = 64
S5_CHUNK = 16

LANES = 128
ROW_TILE = 512
ROW_TILE_STREAM = 1024
VMEM_LIMIT = 56 * 1024 * 1024


def _cparams(*sem):
    return pltpu.CompilerParams(dimension_semantics=sem, vmem_limit_bytes=VMEM_LIMIT)


def _row_tile(rows, want):
    return want if rows % want == 0 else rows


def _col_tile(cols, want):
    t = min(want, cols)
    while cols % t:
        t -= LANES
    return t


def _adaln(x, g, shift, scale):
    var = jnp.mean(x * x, axis=-1, keepdims=True)
    return (x * lax.rsqrt(var + EPS) * g) * (1.0 + scale) + shift


ADALN_ROWS = 128


def _adaln_rows(hn_ref, x_ref, g_ref, sh_ref, sc_ref):
    rows = x_ref.shape[0]
    step = ADALN_ROWS if rows % ADALN_ROWS == 0 else rows

    def body(r, carry):
        sl = pl.ds(pl.multiple_of(r * step, step), step)
        hn_ref[sl, :] = _adaln(x_ref[sl, :], g_ref[...], sh_ref[...], sc_ref[...]).astype(BF16)
        return carry

    lax.fori_loop(0, rows // step, body, 0)


def _mod_spec(k, d, shared=False):
    return pl.BlockSpec((None, None, 1, d), lambda b, *_: (0 if shared else b, k, 0, 0))


def _mod_kernel(c_ref, w_ref, b_ref, o_ref):
    c = c_ref[...]
    s = (c * jax.nn.sigmoid(c)).astype(BF16)
    o_ref[...] = jnp.dot(s, w_ref[...].astype(BF16), preferred_element_type=F32) + b_ref[...]


def _mod_table(c_rows, w_ada, b_ada):
    depth, d, nd = w_ada.shape
    r = c_rows.shape[0]
    tn = _col_tile(nd, 1024)
    return pl.pallas_call(
        _mod_kernel,
        out_shape=jax.ShapeDtypeStruct((depth, r, nd), F32),
        grid=(depth, nd // tn),
        in_specs=[pl.BlockSpec((r, d), lambda l, n: (0, 0)),
                  pl.BlockSpec((None, d, tn), lambda l, n: (l, 0, n)),
                  pl.BlockSpec((None, 1, tn), lambda l, n: (l, 0, n))],
        out_specs=pl.BlockSpec((None, r, tn), lambda l, n: (l, 0, n)),
        compiler_params=_cparams("parallel", "parallel"),
        name="mod_table",
    )(c_rows, w_ada, b_ada.reshape(depth, 1, nd))


def _ffn_kernel(x_ref, g_ref, sh_ref, sc_ref, gate_ref, wa_ref, wg_ref, wo_ref, o_ref,
                hn_ref, act_ref, *, nf):
    j = pl.program_id(2)
    tf = wa_ref.shape[1]
    tn = o_ref.shape[1]

    @pl.when(j == 0)
    def _():
        _adaln_rows(hn_ref, x_ref, g_ref, sh_ref, sc_ref)

    @pl.when(j < nf)
    def _():
        hn = hn_ref[...]
        a = jnp.dot(hn, wa_ref[...], preferred_element_type=F32)
        gt = jnp.dot(hn, wg_ref[...], preferred_element_type=F32)
        cols = pl.ds(pl.multiple_of(j * tf, tf), tf)
        act_ref[:, cols] = (a * (gt * jax.nn.sigmoid(gt))).astype(BF16)

    @pl.when(j >= nf)
    def _():
        y = jnp.dot(act_ref[...], wo_ref[...], preferred_element_type=F32)
        cols = pl.ds(pl.multiple_of((j - nf) * tn, tn), tn)
        o_ref[...] = x_ref[:, cols] + (0.5 * gate_ref[...]) * y


def _ffn(x, g, mod, k0, w_in, w_out):
    bsz, rows, d = x.shape
    ff = w_out.shape[0]
    tm = _row_tile(rows, ROW_TILE_STREAM)
    tf = _col_tile(ff, 512)
    tn = _col_tile(d, 256)
    nf = ff // tf

    def fill(j):
        return jnp.minimum(j, nf - 1)

    def drain(j):
        return jnp.maximum(j - nf, 0)

    return pl.pallas_call(
        functools.partial(_ffn_kernel, nf=nf),
        out_shape=jax.ShapeDtypeStruct(x.shape, F32),
        grid=(bsz, rows // tm, nf + d // tn),
        in_specs=[pl.BlockSpec((None, tm, d), lambda b, i, j: (b, i, 0)),
                  pl.BlockSpec((1, d), lambda b, i, j: (0, 0)),
                  _mod_spec(k0, d), _mod_spec(k0 + 1, d),
                  pl.BlockSpec((None, None, 1, tn), lambda b, i, j: (b, k0 + 2, 0, drain(j))),
                  pl.BlockSpec((d, tf), lambda b, i, j: (0, fill(j))),
                  pl.BlockSpec((d, tf), lambda b, i, j: (0, fill(j) + nf)),
                  pl.BlockSpec((ff, tn), lambda b, i, j: (0, drain(j)))],
        out_specs=pl.BlockSpec((None, tm, tn), lambda b, i, j: (b, i, drain(j))),
        scratch_shapes=[pltpu.VMEM((tm, d), BF16), pltpu.VMEM((tm, ff), BF16)],
        compiler_params=_cparams("parallel", "parallel", "arbitrary"),
        name="ffn",
    )(x, g.reshape(1, d), mod, mod, mod, w_in, w_in, w_out)


def _proj_kernel(*refs, n_rope, with_gates, n_scaled, scale):
    x_ref, g_ref, sh_ref, sc_ref, w_ref = refs[:5]
    rest = refs[5:]
    if n_rope:
        cs_ref, sn_ref = rest[:2]
        rest = rest[2:]
    if with_gates:
        wg_ref, bg_ref, o_ref, og_ref, hn_ref = rest
    else:
        o_ref, hn_ref = rest
    n = pl.program_id(2)

    @pl.when(n == 0)
    def _():
        _adaln_rows(hn_ref, x_ref, g_ref, sh_ref, sc_ref)
        if with_gates:
            og_ref[...] = jnp.dot(hn_ref[...], wg_ref[...], preferred_element_type=F32) + bg_ref[...]

    acc = jnp.dot(hn_ref[...], w_ref[...], preferred_element_type=F32)
    if n_scaled:
        acc = acc * jnp.where(n < n_scaled, scale, 1.0)
    if not n_rope:
        o_ref[...] = acc.astype(o_ref.dtype)
    else:
        @pl.when(n >= n_rope)
        def _():
            o_ref[...] = acc.astype(o_ref.dtype)

        @pl.when(n < n_rope)
        def _():
            cs, sn = cs_ref[...], sn_ref[...]
            hd = cs.shape[1]
            for h in range(acc.shape[1] // hd):
                r = acc[:, h * hd:(h + 1) * hd]
                r = r * cs + pltpu.roll(r, hd // 2, axis=1) * sn
                o_ref[:, h * hd:(h + 1) * hd] = r.astype(o_ref.dtype)


def _proj(x, g, mod, k0, w, out_dtype, rope=None, rope_cols=0, gates=None, scale=1.0, scale_cols=0):
    bsz, rows, d = x.shape
    nout = w.shape[1]
    tm = _row_tile(rows, ROW_TILE_STREAM)
    tn = _col_tile(math.gcd(math.gcd(nout, rope_cols), scale_cols), 1024)
    n_rope = rope_cols // tn
    row = pl.BlockSpec((None, tm, d), lambda b, i, n: (b, i, 0))
    in_specs = [row, pl.BlockSpec((1, d), lambda b, i, n: (0, 0)),
                _mod_spec(k0, d), _mod_spec(k0 + 1, d),
                pl.BlockSpec((d, tn), lambda b, i, n: (0, n))]
    args = [x, g.reshape(1, d), mod, mod, w]
    if n_rope:
        cs, sn = rope
        hd = cs.shape[1]
        in_specs += [pl.BlockSpec((tm, hd), lambda b, i, n: (i, 0))] * 2
        args += [cs, sn]
    out_shape = jax.ShapeDtypeStruct((bsz, rows, nout), out_dtype)
    out_specs = pl.BlockSpec((None, tm, tn), lambda b, i, n: (b, i, n))
    if gates is not None:
        wg, bg = gates
        ng = wg.shape[1]
        in_specs += [pl.BlockSpec((d, ng), lambda b, i, n: (0, 0)),
                     pl.BlockSpec((1, ng), lambda b, i, n: (0, 0))]
        args += [wg, bg]
        out_shape = (out_shape, jax.ShapeDtypeStruct((bsz, rows, ng), F32))
        out_specs = (out_specs, pl.BlockSpec((None, tm, ng), lambda b, i, n: (b, i, 0)))
    return pl.pallas_call(
        functools.partial(_proj_kernel, n_rope=n_rope, with_gates=gates is not None,
                          n_scaled=scale_cols // tn, scale=scale),
        out_shape=out_shape,
        grid=(bsz, rows // tm, nout // tn),
        in_specs=in_specs,
        out_specs=out_specs,
        scratch_shapes=[pltpu.VMEM((tm, d), BF16)],
        compiler_params=_cparams("parallel", "parallel", "arbitrary"),
        name="proj",
    )(*args)


def _out_proj_kernel(y_ref, w_ref, h_ref, gate_ref, o_ref):
    acc = jnp.dot(y_ref[...], w_ref[...], preferred_element_type=F32)
    o_ref[...] = h_ref[...] + gate_ref[...] * acc


def _out_proj(y, w, h, mod, k):
    bsz, rows, d = h.shape
    dk = y.shape[2]
    tm = _row_tile(rows, ROW_TILE)
    return pl.pallas_call(
        _out_proj_kernel,
        out_shape=jax.ShapeDtypeStruct(h.shape, F32),
        grid=(bsz, rows // tm),
        in_specs=[pl.BlockSpec((None, tm, dk), lambda b, i: (b, i, 0)),
                  pl.BlockSpec((dk, d), lambda b, i: (0, 0)),
                  pl.BlockSpec((None, tm, d), lambda b, i: (b, i, 0)),
                  _mod_spec(k, d)],
        out_specs=pl.BlockSpec((None, tm, d), lambda b, i: (b, i, 0)),
        compiler_params=_cparams("parallel", "parallel"),
        name="out_proj",
    )(y, w, h, mod)


def _ml_chunk_len(length):
    return min(ML_CHUNK, length)


def _ml_gate_kernel(g_ref, o_ref, *, nh, lc):
    row = lax.broadcasted_iota(jnp.int32, (lc, lc), 0)
    col = lax.broadcasted_iota(jnp.int32, (lc, lc), 1)
    lower = (col <= row).astype(BF16)
    upper = (col >= row).astype(BF16)
    lane = lax.broadcasted_iota(jnp.int32, (lc, LANES), 1)
    rix = lax.broadcasted_iota(jnp.int32, (lc, LANES), 0)

    def cumsum(tri, parts):
        return sum(jnp.dot(tri, p, preferred_element_type=F32) for p in parts)

    for c in range(g_ref.shape[0] // lc):
        x = g_ref[c * lc:(c + 1) * lc, :]
        ls = jax.nn.log_sigmoid(x)
        hi = ls.astype(BF16)
        r1 = ls - hi.astype(F32)
        mid = r1.astype(BF16)
        lo = (r1 - mid.astype(F32)).astype(BF16)
        pre = cumsum(lower, (hi, mid, lo))
        suf = cumsum(upper, (hi, mid, lo))
        a_f = x - pltpu.roll(pre, LANES - nh, axis=1)
        a_b = x - pltpu.roll(suf, LANES - nh, axis=1)
        cm_f, cm_b = a_f, a_b
        sh = 1
        while sh < lc:
            cm_f = jnp.maximum(cm_f, jnp.where(rix >= sh, pltpu.roll(cm_f, sh, axis=0), NEG_BIG))
            cm_b = jnp.maximum(cm_b, jnp.where(rix < lc - sh, pltpu.roll(cm_b, lc - sh, axis=0), NEG_BIG))
            sh *= 2
        out = jnp.where(lane < nh, a_f,
              jnp.where(lane < 2 * nh, pre,
              jnp.where(lane < 3 * nh, a_b,
              jnp.where(lane < 4 * nh, suf,
              jnp.where(lane < 5 * nh, pltpu.roll(cm_f, 4 * nh, axis=1),
                        pltpu.roll(cm_b, 3 * nh, axis=1))))))
        o_ref[c * lc:(c + 1) * lc, :] = out


def _ml_gates(g):
    bsz, rows, w = g.shape
    tm = _row_tile(rows, ROW_TILE_STREAM)
    blk = pl.BlockSpec((None, tm, w), lambda b, i: (b, i, 0))
    return pl.pallas_call(
        functools.partial(_ml_gate_kernel, nh=ML_HEADS, lc=_ml_chunk_len(rows)),
        out_shape=jax.ShapeDtypeStruct(g.shape, F32),
        grid=(bsz, rows // tm),
        in_specs=[blk],
        out_specs=blk,
        compiler_params=_cparams("parallel", "parallel"),
        name="ml_gates",
    )(g)


def _ml_chunk(q, k, v_aug, a_col, b_col, cm_col, a_row, c_mem, m_run, tri, last):
    hd = q.shape[1]
    b_end = b_col[last:last + 1, :]
    m_new = b_end + jnp.maximum(m_run, cm_col[last:last + 1, :])
    wk = k * jnp.exp(a_col + (b_end - m_new)).astype(BF16)
    dec = jnp.exp(b_end + m_run - m_new)
    c_new = dec * c_mem + lax.dot_general(wk, v_aug, (((0,), (0,)), ((), ())),
                                          preferred_element_type=F32)
    r_col = jnp.maximum(m_run, cm_col)
    s = lax.dot_general(q, k, (((1,), (1,)), ((), ())), preferred_element_type=F32)
    s = s * jnp.exp(jnp.where(tri, a_row - r_col, NEG_BIG))
    dq = jnp.exp(m_run - r_col)
    acc = (jnp.dot(s.astype(BF16), v_aug, preferred_element_type=F32)
           + dq * jnp.dot(q, c_mem.astype(BF16), preferred_element_type=F32))
    den = acc[:, hd:hd + 1]
    h = acc[:, :hd] / jnp.maximum(jnp.abs(den), jnp.exp(-(b_col + r_col)))
    return h, c_new, m_new


def _mlstm_kernel(cq, ck, cv, co, lq, lk, lv, lo, cgc, cgr, lgc, lgr, gh_ref,
                  yc_ref, yl_ref, hf_ref, hb_ref, c_ref, m_ref):
    c_ref[...] = jnp.zeros_like(c_ref)
    m_ref[...] = jnp.full_like(m_ref, NEG_BIG)

    def scan(q_ref, k_ref, v_ref, o_ref, gc_ref, gr_ref, y_ref):
        lc = _ml_chunk_len(q_ref.shape[0])
        nc = q_ref.shape[0] // lc
        rows = lax.broadcasted_iota(jnp.int32, (lc, lc), 0)
        cols = lax.broadcasted_iota(jnp.int32, (lc, lc), 1)
        tri = (cols <= rows, cols >= rows)
        ones_blk = jnp.where(lax.broadcasted_iota(jnp.int32, (lc, LANES), 1) == 0, 1.0, 0.0).astype(BF16)

        def one(out_ref, d, start):
            sl = pl.ds(start, lc)
            gc = gc_ref[sl, :]
            a_row = gr_ref[d:d + 1, sl]
            v_aug = jnp.concatenate([v_ref[sl, :], ones_blk], axis=1)
            h, c_new, m_new = _ml_chunk(
                q_ref[sl, :], k_ref[sl, :], v_aug, gc[:, 2 * d:2 * d + 1], gc[:, 2 * d + 1:2 * d + 2],
                gc[:, 4 + d:5 + d], a_row, c_ref[d], m_ref[d], tri[d], lc - 1 if d == 0 else 0)
            c_ref[d] = c_new
            m_ref[d] = m_new
            out_ref[sl, :] = h

        def body(c, carry):
            one(hf_ref, 0, pl.multiple_of(c * lc, lc))
            one(hb_ref, 1, pl.multiple_of((nc - 1 - c) * lc, lc))
            return carry

        lax.fori_loop(0, nc, body, 0)

        lm = ML_MERGE_ROWS

        def merge(c, carry):
            sl = pl.ds(pl.multiple_of(c * lm, lm), lm)
            hs = hf_ref[sl, :] + hb_ref[sl, :]
            hs = hs * lax.rsqrt(jnp.mean(hs * hs, axis=-1, keepdims=True) + EPS) * gh_ref[...]
            y_ref[sl, :] = (jax.nn.sigmoid(o_ref[sl, :].astype(F32)) * hs).astype(y_ref.dtype)
            return carry

        lax.fori_loop(0, q_ref.shape[0] // lm, merge, 0)

    scan(cq, ck, cv, co, cgc, cgr, yc_ref)
    scan(lq, lk, lv, lo, lgc, lgr, yl_ref)


def _mlstm_scan(zc, zl, gc, gl, g_head):
    bsz, n, d4 = zl.shape
    lctx = zc.shape[1]
    d = d4 // 4
    nh = ML_HEADS
    hd = d // nh
    assert n % _ml_chunk_len(n) == 0 and n % ML_MERGE_ROWS == 0 and lctx % ML_MERGE_ROWS == 0

    def gate_views(g):
        length = g.shape[1]
        g = g[:, :, :6 * nh].reshape(bsz, length, 6, nh)
        return jnp.transpose(g, (0, 3, 1, 2)), jnp.transpose(g[:, :, 0:4:2], (0, 3, 2, 1))

    cgc, cgr = gate_views(gc)
    lgc, lgr = gate_views(gl)

    def zspec(length, k):
        return pl.BlockSpec((None, length, hd), lambda b, h: (b, 0, k * nh + h))

    def gspecs(length):
        return [pl.BlockSpec((None, None, length, 6), lambda b, h: (b, h, 0, 0)),
                pl.BlockSpec((None, None, 2, length), lambda b, h: (b, h, 0, 0))]

    lmax = max(n, lctx)
    return pl.pallas_call(
        _mlstm_kernel,
        out_shape=(jax.ShapeDtypeStruct((bsz, lctx, d), BF16),
                   jax.ShapeDtypeStruct((bsz, n, d), BF16)),
        grid=(bsz, nh),
        in_specs=([zspec(lctx, k) for k in range(4)] + [zspec(n, k) for k in range(4)]
                  + gspecs(lctx) + gspecs(n)
                  + [pl.BlockSpec((None, 1, hd), lambda b, h: (h, 0, 0))]),
        out_specs=(pl.BlockSpec((None, lctx, hd), lambda b, h: (b, 0, h)),
                   pl.BlockSpec((None, n, hd), lambda b, h: (b, 0, h))),
        scratch_shapes=[pltpu.VMEM((lmax, hd), F32), pltpu.VMEM((lmax, hd), F32),
                        pltpu.VMEM((2, hd, hd + LANES), F32), pltpu.VMEM((2, 1, 1), F32)],
        compiler_params=_cparams("parallel", "parallel"),
        name="mlstm_scan",
    )(zc, zc, zc, zc, zl, zl, zl, zl, cgc, cgr, lgc, lgr, g_head.reshape(nh, 1, hd))


def _mlstm_mixer(h, hc, g, mod, mod_c, w_in, b_gate, g_head, w_out, with_ctx):
    bsz, n, d = h.shape
    lctx = hc.shape[1] // bsz
    w_main = w_in[:, :4 * d].astype(BF16)
    ng = 4 * ML_HEADS
    ngp = -(-ng // LANES) * LANES
    w_gate = jnp.pad(w_in[:, 4 * d:], ((0, 0), (0, ngp - ng))).astype(BF16)
    bias = jnp.pad(b_gate.reshape(1, ng), ((0, 0), (0, ngp - ng)))
    q_scale = (d // ML_HEADS) ** -0.5
    zl, gl = _proj(h, g, mod, 3, w_main, BF16, gates=(w_gate, bias), scale=q_scale, scale_cols=d)
    zc, gc = _proj(hc, g, mod_c, 3, w_main, BF16, gates=(w_gate, bias), scale=q_scale, scale_cols=d)
    yc, yl = _mlstm_scan(zc.reshape(bsz, lctx, 4 * d), zl,
                         _ml_gates(gc.reshape(bsz, lctx, ngp)), _ml_gates(gl), g_head)
    w_o = w_out.astype(BF16)
    h = _out_proj(yl, w_o, h, mod, 5)
    if with_ctx:
        hc = _out_proj(yc.reshape(1, bsz * lctx, d), w_o, hc, mod_c, 5)
    return h, hc


def _wa_kernel(sink_ref, q_ref, *refs, local, scale):
    if local:
        k_ref, v_ref, kc_ref, vc_ref, o_ref = refs
    else:
        kc_ref, vc_ref, o_ref = refs
    tq = q_ref.shape[0]
    hd = kc_ref.shape[1] // WA_KV_HEADS
    grp = WA_Q_HEADS // WA_KV_HEADS
    if local:
        n = k_ref.shape[0]
        span = 3 * WA_BLOCK
        start = pl.program_id(1) * tq
        ks = pl.multiple_of(jnp.clip(start - WA_BLOCK, 0, n - span), WA_BLOCK)
        q_pos = start + (lax.broadcasted_iota(jnp.int32, (grp * tq, span), 0) & (tq - 1))
        k_pos = ks + lax.broadcasted_iota(jnp.int32, (grp * tq, span), 1)
        ok = jnp.abs(q_pos - k_pos) <= WA_WINDOW
    dims = (((1,), (1,)), ((), ()))
    for kv in range(WA_KV_HEADS):
        cs = slice(kv * hd, (kv + 1) * hd)
        q4 = jnp.concatenate([q_ref[:, (kv * grp + j) * hd:(kv * grp + j + 1) * hd]
                              for j in range(grp)], axis=0)
        sink = jnp.concatenate([jnp.full((tq, 1), sink_ref[0, kv * grp + j], F32)
                                for j in range(grp)], axis=0)
        kc, vc = kc_ref[:, cs], vc_ref[:, cs]
        s_ctx = lax.dot_general(q4, kc, dims, preferred_element_type=F32) * scale
        m = jnp.maximum(jnp.max(s_ctx, axis=1, keepdims=True), sink)
        if local:
            kl, vl = k_ref[pl.ds(ks, span), cs], v_ref[pl.ds(ks, span), cs]
            s_loc = lax.dot_general(q4, kl, dims, preferred_element_type=F32) * scale
            s_loc = jnp.where(ok, s_loc, NEG_BIG)
            m = jnp.maximum(m, jnp.max(s_loc, axis=1, keepdims=True))
            p_loc = jnp.exp(s_loc - m)
        p_ctx = jnp.exp(s_ctx - m)
        den = jnp.sum(p_ctx, axis=1, keepdims=True) + jnp.exp(sink - m)
        if local:
            den = den + jnp.sum(p_loc, axis=1, keepdims=True)
        inv = 1.0 / den
        o = jnp.dot((p_ctx * inv).astype(BF16), vc, preferred_element_type=F32)
        if local:
            o = o + jnp.dot((p_loc * inv).astype(BF16), vl, preferred_element_type=F32)
        for j in range(grp):
            hq = kv * grp + j
            o_ref[:, hq * hd:(hq + 1) * hd] = o[j * tq:(j + 1) * tq].astype(o_ref.dtype)


def _wa_attention(sink, zq, zkv, zc, local):
    bsz, lq, _ = zq.shape
    lctx = zc.shape[1]
    hd = zq.shape[2] // (WA_Q_HEADS + 2 * WA_KV_HEADS)
    qd, kd = WA_Q_HEADS * hd, WA_KV_HEADS * hd
    tq = WA_BLOCK
    kblk = qd // kd
    in_specs = [pl.BlockSpec(memory_space=pltpu.SMEM),
                pl.BlockSpec((None, tq, qd), lambda b, i: (b, i, 0))]
    args = [sink.reshape(1, WA_Q_HEADS), zq]
    if local:
        n = zkv.shape[1]
        in_specs += [pl.BlockSpec((None, n, kd), lambda b, i: (b, 0, kblk)),
                     pl.BlockSpec((None, n, kd), lambda b, i: (b, 0, kblk + 1))]
        args += [zkv, zkv]
    in_specs += [pl.BlockSpec((None, lctx, kd), lambda b, i: (b, 0, kblk)),
                 pl.BlockSpec((None, lctx, kd), lambda b, i: (b, 0, kblk + 1))]
    args += [zc, zc]
    return pl.pallas_call(
        functools.partial(_wa_kernel, local=local, scale=hd ** -0.5),
        out_shape=jax.ShapeDtypeStruct((bsz, lq, qd), BF16),
        grid=(bsz, lq // tq),
        in_specs=in_specs,
        out_specs=pl.BlockSpec((None, tq, qd), lambda b, i: (b, i, 0)),
        compiler_params=_cparams("parallel", "parallel"),
        name="wa_attention",
    )(*args)


def _rope_tables(n, hd):
    rows = n // GRID_W
    row = jnp.repeat(jnp.arange(rows, dtype=F32), GRID_W)
    col = jnp.tile(jnp.arange(GRID_W, dtype=F32), rows)
    n_freq = hd // 4
    inv = ROPE_BASE ** (-jnp.arange(n_freq, dtype=F32) / n_freq)
    ang = jnp.concatenate([row[:, None] * inv, col[:, None] * inv], axis=-1)
    cos, sin = jnp.cos(ang), jnp.sin(ang)
    return jnp.concatenate([cos, cos], axis=-1), jnp.concatenate([-sin, sin], axis=-1)


def _wa_mixer(h, hc, g, mod, mod_c, w_in, sink, w_out, with_ctx):
    bsz, n, d = h.shape
    lctx = hc.shape[1] // bsz
    hd = d // WA_Q_HEADS
    w = w_in.astype(BF16)
    rope_cols = (WA_Q_HEADS + WA_KV_HEADS) * hd
    zl = _proj(h, g, mod, 3, w, BF16, rope=_rope_tables(n, hd), rope_cols=rope_cols)
    zc = _proj(hc, g, mod_c, 3, w, BF16).reshape(bsz, lctx, -1)
    sink = sink.astype(F32)
    w_o = w_out.astype(BF16)
    h = _out_proj(_wa_attention(sink, zl, zl, zc, True), w_o, h, mod, 5)
    if with_ctx:
        yc = _wa_attention(sink, zc, None, zc, False)
        hc = _out_proj(yc.reshape(1, bsz * lctx, d), w_o, hc, mod_c, 5)
    return h, hc


def _s5_proj_kernel(x_ref, g_ref, sh_ref, sc_ref, w_ref, o_ref, of_ref, hn_ref):
    n = pl.program_id(2)

    @pl.when(n == 0)
    def _():
        _adaln_rows(hn_ref, x_ref, g_ref, sh_ref, sc_ref)

    acc = jnp.dot(hn_ref[...], w_ref[...], preferred_element_type=F32)
    o_ref[...] = acc
    tm = acc.shape[0]
    for k in range(acc.shape[1] // LANES):
        blk = acc[:, k * LANES:(k + 1) * LANES].astype(BF16)
        of_ref[k] = blk.reshape(tm // S5_CHUNK, S5_CHUNK, LANES)


def _s5_proj(x, g, mod, k0, w, shared_mod):
    bsz, rows, d = x.shape
    tm = _row_tile(rows, ROW_TILE_STREAM)
    tn = _col_tile(d, 1024)
    lc = S5_CHUNK
    return pl.pallas_call(
        _s5_proj_kernel,
        out_shape=(jax.ShapeDtypeStruct((bsz, rows, d), F32),
                   jax.ShapeDtypeStruct((d // LANES, rows // lc, bsz, lc, LANES), BF16)),
        grid=(bsz, rows // tm, d // tn),
        in_specs=[pl.BlockSpec((None, tm, d), lambda b, i, n: (b, i, 0)),
                  pl.BlockSpec((1, d), lambda b, i, n: (0, 0)),
                  _mod_spec(k0, d, shared_mod), _mod_spec(k0 + 1, d, shared_mod),
                  pl.BlockSpec((d, tn), lambda b, i, n: (0, n))],
        out_specs=(pl.BlockSpec((None, tm, tn), lambda b, i, n: (b, i, n)),
                   pl.BlockSpec((tn // LANES, tm // lc, None, lc, LANES), lambda b, i, n: (n, i, b, 0, 0))),
        scratch_shapes=[pltpu.VMEM((tm, d), BF16)],
        compiler_params=_cparams("parallel", "parallel", "arbitrary"),
        name="s5_proj",
    )(x, g.reshape(1, d), mod, mod, w)


S5_PACKETS = LANES // S5_GROUP
S5_REGROUP_ROWS = 16


def _packet_transpose(vs):
    lane = lax.broadcasted_iota(jnp.int32, vs[0].shape, 1)
    d = S5_PACKETS // 2
    while d:
        low = (lane & (d * S5_GROUP)) == 0
        nxt = list(vs)
        for i in range(S5_PACKETS):
            if not i & d:
                nxt[i] = jnp.where(low, vs[i], pltpu.roll(vs[i + d], d * S5_GROUP, axis=1))
                nxt[i + d] = jnp.where(low, pltpu.roll(vs[i], LANES - d * S5_GROUP, axis=1), vs[i + d])
        vs = nxt
        d //= 2
    return vs


def _s5_group_kernel(u_ref, o_ref):
    def body(r, carry):
        rows = pl.ds(pl.multiple_of(r * S5_REGROUP_ROWS, S5_REGROUP_ROWS), S5_REGROUP_ROWS)
        for hf in range(S5_CHUNK // S5_PACKETS):
            base = hf * S5_PACKETS
            vs = [pltpu.bitcast(u_ref[rows, (base + t) * LANES:(base + t + 1) * LANES], jnp.uint32)
                  for t in range(S5_PACKETS)]
            for gq, v in enumerate(_packet_transpose(vs)):
                o_ref[gq, rows, hf * LANES:(hf + 1) * LANES] = pltpu.bitcast(v, o_ref.dtype)
        return carry

    lax.fori_loop(0, u_ref.shape[0] // S5_REGROUP_ROWS, body, 0, unroll=4)


def _s5_group(u):
    g8, rows, wide = u.shape
    tr = _row_tile(rows, ROW_TILE)
    w = wide // S5_PACKETS
    return pl.pallas_call(
        _s5_group_kernel,
        out_shape=jax.ShapeDtypeStruct((g8 * S5_PACKETS, rows, w), u.dtype),
        grid=(g8, rows // tr),
        in_specs=[pl.BlockSpec((None, tr, wide), lambda g, i: (g, i, 0))],
        out_specs=pl.BlockSpec((S5_PACKETS, tr, w), lambda g, i: (g, i, 0)),
        compiler_params=_cparams("parallel", "parallel"),
        name="s5_group",
    )(u)


def _s5_ungroup_kernel(y_ref, o_ref):
    def body(r, carry):
        rows = pl.ds(pl.multiple_of(r * S5_REGROUP_ROWS, S5_REGROUP_ROWS), S5_REGROUP_ROWS)
        for hf in range(S5_CHUNK // S5_PACKETS):
            base = hf * S5_PACKETS
            vs = [pltpu.bitcast(y_ref[gq, rows, hf * LANES:(hf + 1) * LANES], jnp.uint32)
                  for gq in range(S5_PACKETS)]
            for t, v in enumerate(_packet_transpose(vs)):
                o_ref[rows, (base + t) * LANES:(base + t + 1) * LANES] = pltpu.bitcast(v, o_ref.dtype)
        return carry

    lax.fori_loop(0, o_ref.shape[0] // S5_REGROUP_ROWS, body, 0, unroll=2)


def _s5_ungroup(y):
    ng, rows, w = y.shape
    tr = _row_tile(rows, ROW_TILE)
    wide = w * S5_PACKETS
    return pl.pallas_call(
        _s5_ungroup_kernel,
        out_shape=jax.ShapeDtypeStruct((ng // S5_PACKETS, rows, wide), y.dtype),
        grid=(ng // S5_PACKETS, rows // tr),
        in_specs=[pl.BlockSpec((S5_PACKETS, tr, w), lambda g, i: (g, i, 0))],
        out_specs=pl.BlockSpec((None, tr, wide), lambda g, i: (g, i, 0)),
        compiler_params=_cparams("parallel", "parallel"),
        name="s5_ungroup",
    )(y)


def _s5_kernel(uc_ref, ul_ref, t_ref, q_ref, p_ref, a_ref, yc_ref, yl_ref,
               sc_ref, sl_ref, xc_ref, xl_ref, *, bsz):
    half = sc_ref.shape[2] // 2
    for u_ref, s_ref in ((uc_ref, sc_ref), (ul_ref, sl_ref)):
        for d in range(2):
            s_ref[d] = jnp.dot(u_ref[...], q_ref[d], preferred_element_type=F32)
    aa = [a_ref[d, 0:1, :] for d in range(2)]
    ab = [a_ref[d, 1:2, :] for d in range(2)]

    def run(s_ref, x_ref, carry):
        n = s_ref.shape[1] // bsz

        def step(d, j, w):
            rows = pl.ds(pl.multiple_of(j * bsz, bsz), bsz)
            x_ref[d, rows, :] = w[:, :half]
            other = jnp.concatenate([w[:, half:], w[:, :half]], axis=1)
            return aa[d] * w + ab[d] * other + s_ref[d, rows, :]

        def body(i, ws):
            return step(0, i, ws[0]), step(1, n - 1 - i, ws[1])

        return lax.fori_loop(0, n, body, carry)

    zero = jnp.zeros((bsz, 2 * half), F32)
    run(sl_ref, xl_ref, run(sc_ref, xc_ref, (zero, zero)))
    for u_ref, x_ref, y_ref in ((uc_ref, xc_ref, yc_ref), (ul_ref, xl_ref, yl_ref)):
        y = jnp.dot(u_ref[...], t_ref[...], preferred_element_type=F32)
        for d in range(2):
            y = y + jnp.dot(x_ref[d].astype(BF16), p_ref[d], preferred_element_type=F32)
        y_ref[...] = y


def _s5_scan(uc, ul, t_m, q_m, p_m, a_m, bsz):
    ng, rc, w = uc.shape
    rl = ul.shape[1]
    st2 = p_m.shape[2]

    def rows_spec(r):
        return pl.BlockSpec((None, r, w), lambda g: (g, 0, 0))

    return pl.pallas_call(
        functools.partial(_s5_kernel, bsz=bsz),
        out_shape=(jax.ShapeDtypeStruct((ng, rc, w), F32), jax.ShapeDtypeStruct((ng, rl, w), F32)),
        grid=(ng,),
        in_specs=[rows_spec(rc), rows_spec(rl),
                  pl.BlockSpec((None, w, w), lambda g: (g, 0, 0)),
                  pl.BlockSpec((2, None, w, 2 * st2), lambda g: (0, g, 0, 0)),
                  pl.BlockSpec((2, None, st2, w), lambda g: (0, g, 0, 0)),
                  pl.BlockSpec((2, None, 2, 2 * st2), lambda g: (0, g, 0, 0))],
        out_specs=(rows_spec(rc), rows_spec(rl)),
        scratch_shapes=[pltpu.VMEM((2, rc, 2 * st2), F32), pltpu.VMEM((2, rl, 2 * st2), F32),
                        pltpu.VMEM((2, rc, st2), F32), pltpu.VMEM((2, rl, st2), F32)],
        compiler_params=_cparams("parallel"),
        name="s5_scan",
    )(uc, ul, t_m, q_m, p_m, a_m)


def _s5_operators(lam_re, lam_im, log_dt, b_re, b_im, c_re, c_im):
    hi = lax.Precision.HIGHEST
    lc = S5_CHUNK
    dt = jnp.exp(log_dt)[..., None]
    mag = jnp.exp(lam_re * dt)
    lb_re, lb_im = mag * jnp.cos(lam_im * dt), mag * jnp.sin(lam_im * dt)
    den = lam_re * lam_re + lam_im * lam_im
    nr, ni = lb_re - 1.0, lb_im
    fr = (nr * lam_re + ni * lam_im) / den
    fi = (ni * lam_re - nr * lam_im) / den
    bb_re = fr[..., None] * b_re - fi[..., None] * b_im
    bb_im = fr[..., None] * b_im + fi[..., None] * b_re
    k = jnp.arange(lc + 1, dtype=F32)[:, None, None, None]
    pmag = jnp.exp(k * (lam_re * dt))
    pw_re, pw_im = pmag * jnp.cos(k * (lam_im * dt)), pmag * jnp.sin(k * (lam_im * dt))
    lbb_re = pw_re[..., None] * bb_re - pw_im[..., None] * bb_im
    lbb_im = pw_re[..., None] * bb_im + pw_im[..., None] * bb_re
    kern = (jnp.einsum('dgop,kdgpc->kdgco', c_re, lbb_re, precision=hi)
            - jnp.einsum('dgop,kdgpc->kdgco', c_im, lbb_im, precision=hi))
    ti = jnp.arange(lc)[:, None]
    to = jnp.arange(lc)[None, :]
    ngrp, nch = lam_re.shape[1], b_re.shape[3]
    nst = lam_re.shape[2]

    def per_dir(d):
        lag = (to - ti) if d == 0 else (ti - to)
        kd = kern[jnp.clip(lag, 0, lc), d]
        kd = jnp.where((lag >= 0)[:, :, None, None, None], kd, 0.0)
        t_m = jnp.transpose(kd, (2, 0, 3, 1, 4)).reshape(ngrp, lc * nch, lc * nch)
        e_in = (lc - 1 - jnp.arange(lc)) if d == 0 else jnp.arange(lc)
        q_re = jnp.transpose(lbb_re[e_in, d], (1, 0, 3, 2)).reshape(ngrp, lc * nch, nst)
        q_im = jnp.transpose(lbb_im[e_in, d], (1, 0, 3, 2)).reshape(ngrp, lc * nch, nst)
        q_m = jnp.concatenate([q_re, q_im, q_im, q_re], axis=-1)
        e_out = (jnp.arange(lc) + 1) if d == 0 else (lc - jnp.arange(lc))
        cl_re = (c_re[d][None] * pw_re[e_out, d][:, :, None, :]
                 - c_im[d][None] * pw_im[e_out, d][:, :, None, :])
        cl_im = (c_re[d][None] * pw_im[e_out, d][:, :, None, :]
                 + c_im[d][None] * pw_re[e_out, d][:, :, None, :])
        p_re = jnp.transpose(cl_re, (1, 3, 0, 2)).reshape(ngrp, nst, lc * nch)
        p_im = jnp.transpose(cl_im, (1, 3, 0, 2)).reshape(ngrp, nst, lc * nch)
        p_m = jnp.concatenate([p_re, -p_im], axis=1)
        a_re, a_im = pw_re[lc, d], pw_im[lc, d]
        a_m = jnp.stack([jnp.concatenate([a_re] * 4, axis=-1),
                         jnp.concatenate([-a_im, a_im, a_im, -a_im], axis=-1)], axis=1)
        return t_m, q_m, p_m, a_m

    f, b = per_dir(0), per_dir(1)
    q_m, p_m, a_m = (jnp.stack([x, y]) for x, y in zip(f[1:], b[1:]))
    return (f[0] + b[0]).astype(BF16), q_m.astype(BF16), p_m.astype(BF16), a_m


def _s5_glu_kernel(y_ref, u_ref, ds_ref, wa_ref, wg_ref, h_ref, gate_ref, o_ref, z_ref):
    n = pl.program_id(2)

    @pl.when(n == 0)
    def _():
        tm = u_ref.shape[0]
        for k in range(y_ref.shape[0]):
            cols = slice(k * LANES, (k + 1) * LANES)
            y = y_ref[k].reshape(tm, LANES)
            z_ref[:, cols] = jax.nn.gelu(y + ds_ref[:, cols] * u_ref[:, cols]).astype(BF16)

    z = z_ref[...]
    a = jnp.dot(z, wa_ref[...], preferred_element_type=F32)
    gt = jnp.dot(z, wg_ref[...], preferred_element_type=F32)
    o_ref[...] = h_ref[...] + gate_ref[...] * (a * jax.nn.sigmoid(gt))


def _s5_glu(y, u, d_skip, w, h, mod, k, shared_mod):
    bsz, rows, d = h.shape
    tm = _row_tile(rows, ROW_TILE)
    tn = _col_tile(d, 512)
    nn = d // tn
    lc = S5_CHUNK
    row = pl.BlockSpec((None, tm, d), lambda b, i, n: (b, i, 0))
    col = pl.BlockSpec((None, tm, tn), lambda b, i, n: (b, i, n))
    return pl.pallas_call(
        _s5_glu_kernel,
        out_shape=jax.ShapeDtypeStruct(h.shape, F32),
        grid=(bsz, rows // tm, nn),
        in_specs=[pl.BlockSpec((d // LANES, tm // lc, None, lc, LANES), lambda b, i, n: (0, i, b, 0, 0)),
                  row, pl.BlockSpec((1, d), lambda b, i, n: (0, 0)),
                  pl.BlockSpec((d, tn), lambda b, i, n: (0, n)),
                  pl.BlockSpec((d, tn), lambda b, i, n: (0, n + nn)),
                  col, pl.BlockSpec((None, None, 1, tn),
                                    lambda b, i, n: (0 if shared_mod else b, k, 0, n))],
        out_specs=col,
        scratch_shapes=[pltpu.VMEM((tm, d), BF16)],
        compiler_params=_cparams("parallel", "parallel", "arbitrary"),
        name="s5_glu",
    )(y, u, d_skip.reshape(1, d), w, w, h, mod)


def _s5_mixer(h, hc, g, mod, mod_c, w_in, lam_re, lam_im, log_dt, b_re, b_im, c_re, c_im,
              d_skip, w_out, with_ctx):
    bsz, n, d = h.shape
    lctx = hc.shape[1] // bsz
    lc = S5_CHUNK
    g8 = d // LANES
    w = w_in.astype(BF16)
    hc3 = hc.reshape(bsz, lctx, d)
    u_l, uf_l = _s5_proj(h, g, mod, 3, w, False)
    u_c, uf_c = _s5_proj(hc3, g, mod_c, 3, w, True)
    ops = _s5_operators(lam_re, lam_im, log_dt, b_re, b_im, c_re, c_im)
    up_l = _s5_group(uf_l.reshape(g8, n // lc * bsz, lc * LANES))
    up_c = _s5_group(uf_c.reshape(g8, lctx // lc * bsz, lc * LANES))
    yp_c, yp_l = _s5_scan(up_c, up_l, *ops, bsz=bsz)
    y_l = _s5_ungroup(yp_l).reshape(g8, n // lc, bsz, lc, LANES)
    w_o = w_out.astype(BF16)
    h = _s5_glu(y_l, u_l, d_skip, w_o, h, mod, 5, False)
    if with_ctx:
        y_c = _s5_ungroup(yp_c).reshape(g8, lctx // lc, bsz, lc, LANES)
        hc = _s5_glu(y_c, u_c, d_skip, w_o, hc3, mod_c, 5, True).reshape(1, bsz * lctx, d)
    return h, hc


def _rmsnorm_kernel(x_ref, g_ref, o_ref):
    x = x_ref[...]
    o_ref[...] = x * lax.rsqrt(jnp.mean(x * x, axis=-1, keepdims=True) + EPS) * g_ref[...]


def _final_norm(x, g):
    bsz, rows, d = x.shape
    tm = _row_tile(rows, ROW_TILE)
    row = pl.BlockSpec((None, tm, d), lambda b, i: (b, i, 0))
    return pl.pallas_call(
        _rmsnorm_kernel,
        out_shape=jax.ShapeDtypeStruct(x.shape, F32),
        grid=(bsz, rows // tm),
        in_specs=[row, pl.BlockSpec((1, d), lambda b, i: (0, 0))],
        out_specs=row,
        compiler_params=_cparams("parallel", "parallel"),
        name="final_norm",
    )(x, g.reshape(1, d))


def kernel(x, c, ctx, c_ctx, w_ada, b_ada, g_norm, w_ffn_in, w_ffn_out, g_final, ml_w_in, ml_b_gate, ml_g_head, ml_w_out, wa_w_in, wa_sink, wa_w_out, s5_w_in, s5_lam_re, s5_lam_im, s5_log_dt, s5_b_re, s5_b_im, s5_c_re, s5_c_im, s5_d_skip, s5_w_out):
    bsz, n, d = x.shape
    depth = w_ada.shape[0]
    lctx = ctx.shape[1]
    n_rows = -(-(bsz + 1) // 16) * 16
    c_rows = jnp.concatenate([c, c_ctx[None], jnp.zeros((n_rows - bsz - 1, d), F32)], axis=0)
    mods = _mod_table(c_rows, w_ada, b_ada).reshape(depth, n_rows, N_MOD, 1, d)
    h = x
    hc = ctx.reshape(1, bsz * lctx, d)
    for layer in range(depth):
        has_next = layer < depth - 1
        mod, mod_c = mods[layer, :bsz], mods[layer, bsz:bsz + 1]
        g = g_norm[layer]
        w_in = w_ffn_in[layer].astype(BF16)
        w_out = w_ffn_out[layer].astype(BF16)
        h = _ffn(h, g[0], mod, 0, w_in[0], w_out[0])
        hc = _ffn(hc, g[0], mod_c, 0, w_in[0], w_out[0])
        kind, idx = layer % 3, layer // 3
        if kind == 0:
            h, hc = _mlstm_mixer(h, hc, g[1], mod, mod_c, ml_w_in[idx], ml_b_gate[idx],
                                 ml_g_head[idx], ml_w_out[idx], has_next)
        elif kind == 1:
            h, hc = _wa_mixer(h, hc, g[1], mod, mod_c, wa_w_in[idx], wa_sink[idx],
                              wa_w_out[idx], has_next)
        else:
            h, hc = _s5_mixer(h, hc, g[1], mod, mod_c, s5_w_in[idx], s5_lam_re[idx], s5_lam_im[idx],
                              s5_log_dt[idx], s5_b_re[idx], s5_b_im[idx], s5_c_re[idx],
                              s5_c_im[idx], s5_d_skip[idx], s5_w_out[idx], has_next)
        h = _ffn(h, g[2], mod, 6, w_in[1], w_out[1])
        if has_next:
            hc = _ffn(hc, g[2], mod_c, 6, w_in[1], w_out[1])
    return _final_norm(h, g_final)
```

```python
import functools
import math

import jax
import jax.numpy as jnp
from jax import lax
from jax.experimental import pallas as pl
from jax.experimental.pallas import tpu as pltpu

F32 = jnp.float32
BF16 = jnp.bfloat16

EPS = 1e-6
NEG_BIG = -1e30
LOG2E = math.log2(math.e)
N_MOD = 9
ML_HEADS = 8
ML_CHUNK = 512
ML_MERGE_ROWS = 256
WA_Q_HEADS = 16
WA_KV_HEADS = 4
WA_WINDOW = 128
WA_BLOCK = 128
GRID_W = 64
ROPE_BASE = 10000.0
S5_GROUP = 16
S5_STATE = 64
S5_CHUNK = 16

LANES = 128
ROW_TILE = 512
ROW_TILE_STREAM = 1024
VMEM_LIMIT = 56 * 1024 * 1024


def _cparams(*sem):
    return pltpu.CompilerParams(dimension_semantics=sem, vmem_limit_bytes=VMEM_LIMIT)


def _row_tile(rows, want):
    return want if rows % want == 0 else rows


def _col_tile(cols, want):
    t = min(want, cols)
    while cols % t:
        t -= LANES
    return t


def _adaln(x, g, shift, scale):
    var = jnp.mean(x * x, axis=-1, keepdims=True)
    return (x * lax.rsqrt(var + EPS) * g) * (1.0 + scale) + shift


ADALN_ROWS = 128


def _adaln_rows(hn_ref, x_ref, g_ref, sh_ref, sc_ref):
    rows = x_ref.shape[0]
    step = ADALN_ROWS if rows % ADALN_ROWS == 0 else rows

    def body(r, carry):
        sl = pl.ds(pl.multiple_of(r * step, step), step)
        hn_ref[sl, :] = _adaln(x_ref[sl, :], g_ref[...], sh_ref[...], sc_ref[...]).astype(BF16)
        return carry

    lax.fori_loop(0, rows // step, body, 0)


def _mod_spec(k, d, shared=False):
    return pl.BlockSpec((None, None, 1, d), lambda b, *_: (0 if shared else b, k, 0, 0))


def _mod_kernel(c_ref, w_ref, b_ref, o_ref):
    c = c_ref[...]
    s = (c * jax.nn.sigmoid(c)).astype(BF16)
    o_ref[...] = jnp.dot(s, w_ref[...].astype(BF16), preferred_element_type=F32) + b_ref[...]


def _mod_table(c_rows, w_ada, b_ada):
    depth, d, nd = w_ada.shape
    r = c_rows.shape[0]
    tn = _col_tile(nd, 1024)
    return pl.pallas_call(
        _mod_kernel,
        out_shape=jax.ShapeDtypeStruct((depth, r, nd), F32),
        grid=(depth, nd // tn),
        in_specs=[pl.BlockSpec((r, d), lambda l, n: (0, 0)),
                  pl.BlockSpec((None, d, tn), lambda l, n: (l, 0, n)),
                  pl.BlockSpec((None, 1, tn), lambda l, n: (l, 0, n))],
        out_specs=pl.BlockSpec((None, r, tn), lambda l, n: (l, 0, n)),
        compiler_params=_cparams("parallel", "parallel"),
        name="mod_table",
    )(c_rows, w_ada, b_ada.reshape(depth, 1, nd))


def _ffn_kernel(x_ref, g_ref, sh_ref, sc_ref, gate_ref, wa_ref, wg_ref, wo_ref, o_ref,
                hn_ref, act_ref, *, nf):
    j = pl.program_id(2)
    tf = wa_ref.shape[1]
    tn = o_ref.shape[1]

    @pl.when(j == 0)
    def _():
        _adaln_rows(hn_ref, x_ref, g_ref, sh_ref, sc_ref)

    @pl.when(j < nf)
    def _():
        hn = hn_ref[...]
        a = jnp.dot(hn, wa_ref[...], preferred_element_type=F32)
        gt = jnp.dot(hn, wg_ref[...], preferred_element_type=F32)
        cols = pl.ds(pl.multiple_of(j * tf, tf), tf)
        act_ref[:, cols] = (a * (gt * jax.nn.sigmoid(gt))).astype(BF16)

    @pl.when(j >= nf)
    def _():
        y = jnp.dot(act_ref[...], wo_ref[...], preferred_element_type=F32)
        cols = pl.ds(pl.multiple_of((j - nf) * tn, tn), tn)
        o_ref[...] = x_ref[:, cols] + (0.5 * gate_ref[...]) * y


def _ffn(x, g, mod, k0, w_in, w_out):
    bsz, rows, d = x.shape
    ff = w_out.shape[0]
    tm = _row_tile(rows, ROW_TILE_STREAM)
    tf = _col_tile(ff, 512)
    tn = _col_tile(d, 256)
    nf = ff // tf

    def fill(j):
        return jnp.minimum(j, nf - 1)

    def drain(j):
        return jnp.maximum(j - nf, 0)

    return pl.pallas_call(
        functools.partial(_ffn_kernel, nf=nf),
        out_shape=jax.ShapeDtypeStruct(x.shape, F32),
        grid=(bsz, rows // tm, nf + d // tn),
        in_specs=[pl.BlockSpec((None, tm, d), lambda b, i, j: (b, i, 0)),
                  pl.BlockSpec((1, d), lambda b, i, j: (0, 0)),
                  _mod_spec(k0, d), _mod_spec(k0 + 1, d),
                  pl.BlockSpec((None, None, 1, tn), lambda b, i, j: (b, k0 + 2, 0, drain(j))),
                  pl.BlockSpec((d, tf), lambda b, i, j: (0, fill(j))),
                  pl.BlockSpec((d, tf), lambda b, i, j: (0, fill(j) + nf)),
                  pl.BlockSpec((ff, tn), lambda b, i, j: (0, drain(j)))],
        out_specs=pl.BlockSpec((None, tm, tn), lambda b, i, j: (b, i, drain(j))),
        scratch_shapes=[pltpu.VMEM((tm, d), BF16), pltpu.VMEM((tm, ff), BF16)],
        compiler_params=_cparams("parallel", "parallel", "arbitrary"),
        name="ffn",
    )(x, g.reshape(1, d), mod, mod, mod, w_in, w_in, w_out)


def _proj_kernel(*refs, n_rope, with_gates, n_scaled, scale):
    x_ref, g_ref, sh_ref, sc_ref, w_ref = refs[:5]
    rest = refs[5:]
    if n_rope:
        cs_ref, sn_ref = rest[:2]
        rest = rest[2:]
    if with_gates:
        wg_ref, bg_ref, o_ref, og_ref, hn_ref = rest
    else:
        o_ref, hn_ref = rest
    n = pl.program_id(2)

    @pl.when(n == 0)
    def _():
        _adaln_rows(hn_ref, x_ref, g_ref, sh_ref, sc_ref)
        if with_gates:
            og_ref[...] = jnp.dot(hn_ref[...], wg_ref[...], preferred_element_type=F32) + bg_ref[...]

    acc = jnp.dot(hn_ref[...], w_ref[...], preferred_element_type=F32)
    if n_scaled:
        acc = acc * jnp.where(n < n_scaled, scale, 1.0)
    if not n_rope:
        o_ref[...] = acc.astype(o_ref.dtype)
    else:
        @pl.when(n >= n_rope)
        def _():
            o_ref[...] = acc.astype(o_ref.dtype)

        @pl.when(n < n_rope)
        def _():
            cs, sn = cs_ref[...], sn_ref[...]
            hd = cs.shape[1]
            for h in range(acc.shape[1] // hd):
                r = acc[:, h * hd:(h + 1) * hd]
                r = r * cs + pltpu.roll(r, hd // 2, axis=1) * sn
                o_ref[:, h * hd:(h + 1) * hd] = r.astype(o_ref.dtype)


def _proj(x, g, mod, k0, w, out_dtype, rope=None, rope_cols=0, gates=None, scale=1.0, scale_cols=0):
    bsz, rows, d = x.shape
    nout = w.shape[1]
    tm = _row_tile(rows, ROW_TILE_STREAM)
    tn = _col_tile(math.gcd(math.gcd(nout, rope_cols), scale_cols), 1024)
    n_rope = rope_cols // tn
    row = pl.BlockSpec((None, tm, d), lambda b, i, n: (b, i, 0))
    in_specs = [row, pl.BlockSpec((1, d), lambda b, i, n: (0, 0)),
                _mod_spec(k0, d), _mod_spec(k0 + 1, d),
                pl.BlockSpec((d, tn), lambda b, i, n: (0, n))]
    args = [x, g.reshape(1, d), mod, mod, w]
    if n_rope:
        cs, sn = rope
        hd = cs.shape[1]
        in_specs += [pl.BlockSpec((tm, hd), lambda b, i, n: (i, 0))] * 2
        args += [cs, sn]
    out_shape = jax.ShapeDtypeStruct((bsz, rows, nout), out_dtype)
    out_specs = pl.BlockSpec((None, tm, tn), lambda b, i, n: (b, i, n))
    if gates is not None:
        wg, bg = gates
        ng = wg.shape[1]
        in_specs += [pl.BlockSpec((d, ng), lambda b, i, n: (0, 0)),
                     pl.BlockSpec((1, ng), lambda b, i, n: (0, 0))]
        args += [wg, bg]
        out_shape = (out_shape, jax.ShapeDtypeStruct((bsz, rows, ng), F32))
        out_specs = (out_specs, pl.BlockSpec((None, tm, ng), lambda b, i, n: (b, i, 0)))
    return pl.pallas_call(
        functools.partial(_proj_kernel, n_rope=n_rope, with_gates=gates is not None,
                          n_scaled=scale_cols // tn, scale=scale),
        out_shape=out_shape,
        grid=(bsz, rows // tm, nout // tn),
        in_specs=in_specs,
        out_specs=out_specs,
        scratch_shapes=[pltpu.VMEM((tm, d), BF16)],
        compiler_params=_cparams("parallel", "parallel", "arbitrary"),
        name="proj",
    )(*args)


def _out_proj_kernel(y_ref, w_ref, h_ref, gate_ref, o_ref):
    acc = jnp.dot(y_ref[...], w_ref[...], preferred_element_type=F32)
    o_ref[...] = h_ref[...] + gate_ref[...] * acc


def _out_proj(y, w, h, mod, k):
    bsz, rows, d = h.shape
    dk = y.shape[2]
    tm = _row_tile(rows, ROW_TILE)
    return pl.pallas_call(
        _out_proj_kernel,
        out_shape=jax.ShapeDtypeStruct(h.shape, F32),
        grid=(bsz, rows // tm),
        in_specs=[pl.BlockSpec((None, tm, dk), lambda b, i: (b, i, 0)),
                  pl.BlockSpec((dk, d), lambda b, i: (0, 0)),
                  pl.BlockSpec((None, tm, d), lambda b, i: (b, i, 0)),
                  _mod_spec(k, d)],
        out_specs=pl.BlockSpec((None, tm, d), lambda b, i: (b, i, 0)),
        compiler_params=_cparams("parallel", "parallel"),
        name="out_proj",
    )(y, w, h, mod)


def _ml_chunk_len(length):
    return min(ML_CHUNK, length)


def _ml_gate_kernel(g_ref, o_ref, *, nh, lc):
    row = lax.broadcasted_iota(jnp.int32, (lc, lc), 0)
    col = lax.broadcasted_iota(jnp.int32, (lc, lc), 1)
    lower = (col <= row).astype(BF16)
    upper = (col >= row).astype(BF16)
    lane = lax.broadcasted_iota(jnp.int32, (lc, LANES), 1)
    rix = lax.broadcasted_iota(jnp.int32, (lc, LANES), 0)

    def cumsum(tri, parts):
        return sum(jnp.dot(tri, p, preferred_element_type=F32) for p in parts)

    for c in range(g_ref.shape[0] // lc):
        x = g_ref[c * lc:(c + 1) * lc, :]
        ls = jax.nn.log_sigmoid(x)
        hi = ls.astype(BF16)
        r1 = ls - hi.astype(F32)
        mid = r1.astype(BF16)
        lo = (r1 - mid.astype(F32)).astype(BF16)
        pre = cumsum(lower, (hi, mid, lo))
        suf = cumsum(upper, (hi, mid, lo))
        a_f = x - pltpu.roll(pre, LANES - nh, axis=1)
        a_b = x - pltpu.roll(suf, LANES - nh, axis=1)
        cm_f, cm_b = a_f, a_b
        sh = 1
        while sh < lc:
            cm_f = jnp.maximum(cm_f, jnp.where(rix >= sh, pltpu.roll(cm_f, sh, axis=0), NEG_BIG))
            cm_b = jnp.maximum(cm_b, jnp.where(rix < lc - sh, pltpu.roll(cm_b, lc - sh, axis=0), NEG_BIG))
            sh *= 2
        out = jnp.where(lane < nh, a_f,
              jnp.where(lane < 2 * nh, pre,
              jnp.where(lane < 3 * nh, a_b,
              jnp.where(lane < 4 * nh, suf,
              jnp.where(lane < 5 * nh, pltpu.roll(cm_f, 4 * nh, axis=1),
                        pltpu.roll(cm_b, 3 * nh, axis=1))))))
        o_ref[c * lc:(c + 1) * lc, :] = out


def _ml_gates(g):
    bsz, rows, w = g.shape
    tm = _row_tile(rows, ROW_TILE_STREAM)
    blk = pl.BlockSpec((None, tm, w), lambda b, i: (b, i, 0))
    return pl.pallas_call(
        functools.partial(_ml_gate_kernel, nh=ML_HEADS, lc=_ml_chunk_len(rows)),
        out_shape=jax.ShapeDtypeStruct(g.shape, F32),
        grid=(bsz, rows // tm),
        in_specs=[blk],
        out_specs=blk,
        compiler_params=_cparams("parallel", "parallel"),
        name="ml_gates",
    )(g)


def _ml_chunk(q, k, v_aug, a_col, b_col, cm_col, a_row, c_mem, m_run, tri, last):
    dd = range(2)
    hd = q[0].shape[1]
    nt = (((1,), (1,)), ((), ()))
    tn = (((0,), (0,)), ((), ()))
    b_end = [b_col[d][last[d]:last[d] + 1, :] for d in dd]
    m_new = [b_end[d] + jnp.maximum(m_run[d], cm_col[d][last[d]:last[d] + 1, :]) for d in dd]
    r_col = [jnp.maximum(m_run[d], cm_col[d]) for d in dd]
    s = [lax.dot_general(q[d], k[d], nt, preferred_element_type=F32) for d in dd]
    c_bf = [c_mem[d].astype(BF16) for d in dd]
    qc = [jnp.dot(q[d], c_bf[d], preferred_element_type=F32) for d in dd]
    wk = [k[d] * jnp.exp(a_col[d] + (b_end[d] - m_new[d])).astype(BF16) for d in dd]
    e = [jnp.exp(jnp.where(tri[d], a_row[d] - r_col[d], NEG_BIG)) for d in dd]
    upd = [lax.dot_general(wk[d], v_aug[d], tn, preferred_element_type=F32) for d in dd]
    p = [(s[d] * e[d]).astype(BF16) for d in dd]
    dec = [jnp.exp(b_end[d] + m_run[d] - m_new[d]) for d in dd]
    dq = [jnp.exp(m_run[d] - r_col[d]) for d in dd]
    acc = [jnp.dot(p[d], v_aug[d], preferred_element_type=F32) + dq[d] * qc[d] for d in dd]
    c_new = [dec[d] * c_mem[d] + upd[d] for d in dd]
    lim = [jnp.exp(-(b_col[d] + r_col[d])) for d in dd]
    h = [acc[d][:, :hd] / jnp.maximum(jnp.abs(acc[d][:, hd:hd + 1]), lim[d]) for d in dd]
    return h, c_new, m_new


def _mlstm_kernel(cq, ck, cv, co, lq, lk, lv, lo, cgc, cgr, lgc, lgr, gh_ref,
                  yc_ref, yl_ref, hf_ref, hb_ref, c_ref, m_ref):
    c_ref[...] = jnp.zeros_like(c_ref)
    m_ref[...] = jnp.full_like(m_ref, NEG_BIG)

    def scan(q_ref, k_ref, v_ref, o_ref, gc_ref, gr_ref, y_ref):
        lc = _ml_chunk_len(q_ref.shape[0])
        nc = q_ref.shape[0] // lc
        rows = lax.broadcasted_iota(jnp.int32, (lc, lc), 0)
        cols = lax.broadcasted_iota(jnp.int32, (lc, lc), 1)
        tri = (cols <= rows, cols >= rows)
        ones_blk = jnp.where(lax.broadcasted_iota(jnp.int32, (lc, LANES), 1) == 0, 1.0, 0.0).astype(BF16)

        def body(c, carry):
            sl = (pl.ds(pl.multiple_of(c * lc, lc), lc), pl.ds(pl.multiple_of((nc - 1 - c) * lc, lc), lc))
            dd = range(2)
            gc = [gc_ref[sl[d], :] for d in dd]
            h, c_new, m_new = _ml_chunk(
                [q_ref[sl[d], :] for d in dd], [k_ref[sl[d], :] for d in dd],
                [jnp.concatenate([v_ref[sl[d], :], ones_blk], axis=1) for d in dd],
                [gc[d][:, 2 * d:2 * d + 1] for d in dd], [gc[d][:, 2 * d + 1:2 * d + 2] for d in dd],
                [gc[d][:, 4 + d:5 + d] for d in dd], [gr_ref[d:d + 1, sl[d]] for d in dd],
                [c_ref[d] for d in dd], [m_ref[d] for d in dd], tri, (lc - 1, 0))
            for d, out_ref in enumerate((hf_ref, hb_ref)):
                c_ref[d] = c_new[d]
                m_ref[d] = m_new[d]
                out_ref[sl[d], :] = h[d]
            return carry

        lax.fori_loop(0, nc, body, 0)

        lm = ML_MERGE_ROWS

        def merge(c, carry):
            sl = pl.ds(pl.multiple_of(c * lm, lm), lm)
            hs = hf_ref[sl, :] + hb_ref[sl, :]
            hs = hs * lax.rsqrt(jnp.mean(hs * hs, axis=-1, keepdims=True) + EPS) * gh_ref[...]
            y_ref[sl, :] = (jax.nn.sigmoid(o_ref[sl, :].astype(F32)) * hs).astype(y_ref.dtype)
            return carry

        lax.fori_loop(0, q_ref.shape[0] // lm, merge, 0)

    scan(cq, ck, cv, co, cgc, cgr, yc_ref)
    scan(lq, lk, lv, lo, lgc, lgr, yl_ref)


def _mlstm_scan(zc, zl, gc, gl, g_head):
    bsz, n, d4 = zl.shape
    lctx = zc.shape[1]
    d = d4 // 4
    nh = ML_HEADS
    hd = d // nh
    assert n % _ml_chunk_len(n) == 0 and n % ML_MERGE_ROWS == 0 and lctx % ML_MERGE_ROWS == 0

    def gate_views(g):
        length = g.shape[1]
        g = g[:, :, :6 * nh].reshape(bsz, length, 6, nh)
        return jnp.transpose(g, (0, 3, 1, 2)), jnp.transpose(g[:, :, 0:4:2], (0, 3, 2, 1))

    cgc, cgr = gate_views(gc)
    lgc, lgr = gate_views(gl)

    def zspec(length, k):
        return pl.BlockSpec((None, length, hd), lambda b, h: (b, 0, k * nh + h))

    def gspecs(length):
        return [pl.BlockSpec((None, None, length, 6), lambda b, h: (b, h, 0, 0)),
                pl.BlockSpec((None, None, 2, length), lambda b, h: (b, h, 0, 0))]

    lmax = max(n, lctx)
    return pl.pallas_call(
        _mlstm_kernel,
        out_shape=(jax.ShapeDtypeStruct((bsz, lctx, d), BF16),
                   jax.ShapeDtypeStruct((bsz, n, d), BF16)),
        grid=(bsz, nh),
        in_specs=([zspec(lctx, k) for k in range(4)] + [zspec(n, k) for k in range(4)]
                  + gspecs(lctx) + gspecs(n)
                  + [pl.BlockSpec((None, 1, hd), lambda b, h: (h, 0, 0))]),
        out_specs=(pl.BlockSpec((None, lctx, hd), lambda b, h: (b, 0, h)),
                   pl.BlockSpec((None, n, hd), lambda b, h: (b, 0, h))),
        scratch_shapes=[pltpu.VMEM((lmax, hd), F32), pltpu.VMEM((lmax, hd), F32),
                        pltpu.VMEM((2, hd, hd + LANES), F32), pltpu.VMEM((2, 1, 1), F32)],
        compiler_params=_cparams("parallel", "parallel"),
        name="mlstm_scan",
    )(zc, zc, zc, zc, zl, zl, zl, zl, cgc, cgr, lgc, lgr, g_head.reshape(nh, 1, hd))


def _mlstm_mixer(h, hc, g, mod, mod_c, w_in, b_gate, g_head, w_out, with_ctx):
    bsz, n, d = h.shape
    lctx = hc.shape[1] // bsz
    w_main = w_in[:, :4 * d].astype(BF16)
    ng = 4 * ML_HEADS
    ngp = -(-ng // LANES) * LANES
    w_gate = jnp.pad(w_in[:, 4 * d:], ((0, 0), (0, ngp - ng))).astype(BF16)
    bias = jnp.pad(b_gate.reshape(1, ng), ((0, 0), (0, ngp - ng)))
    q_scale = (d // ML_HEADS) ** -0.5
    zl, gl = _proj(h, g, mod, 3, w_main, BF16, gates=(w_gate, bias), scale=q_scale, scale_cols=d)
    zc, gc = _proj(hc, g, mod_c, 3, w_main, BF16, gates=(w_gate, bias), scale=q_scale, scale_cols=d)
    yc, yl = _mlstm_scan(zc.reshape(bsz, lctx, 4 * d), zl,
                         _ml_gates(gc.reshape(bsz, lctx, ngp)), _ml_gates(gl), g_head)
    w_o = w_out.astype(BF16)
    h = _out_proj(yl, w_o, h, mod, 5)
    if with_ctx:
        hc = _out_proj(yc.reshape(1, bsz * lctx, d), w_o, hc, mod_c, 5)
    return h, hc


def _wa_kernel(sink_ref, q_ref, *refs, local, scale):
    if local:
        k_ref, v_ref, kc_ref, vc_ref, o_ref = refs
    else:
        kc_ref, vc_ref, o_ref = refs
    tq = q_ref.shape[0]
    hd = kc_ref.shape[1] // WA_KV_HEADS
    grp = WA_Q_HEADS // WA_KV_HEADS
    if local:
        n = k_ref.shape[0]
        span = 3 * WA_BLOCK
        start = pl.program_id(1) * tq
        ks = pl.multiple_of(jnp.clip(start - WA_BLOCK, 0, n - span), WA_BLOCK)
        q_pos = start + (lax.broadcasted_iota(jnp.int32, (grp * tq, span), 0) & (tq - 1))
        k_pos = ks + lax.broadcasted_iota(jnp.int32, (grp * tq, span), 1)
        ok = jnp.abs(q_pos - k_pos) <= WA_WINDOW
    dims = (((1,), (1,)), ((), ()))
    heads = range(WA_KV_HEADS)
    c2 = scale * LOG2E
    cs = [slice(kv * hd, (kv + 1) * hd) for kv in heads]
    q4 = [jnp.concatenate([q_ref[:, (kv * grp + j) * hd:(kv * grp + j + 1) * hd] for j in range(grp)], axis=0)
          for kv in heads]
    sink = [jnp.concatenate([jnp.full((tq, 1), sink_ref[0, kv * grp + j], F32) for j in range(grp)], axis=0)
            * (1.0 / scale) for kv in heads]
    s_ctx = [lax.dot_general(q4[kv], kc_ref[:, cs[kv]], dims, preferred_element_type=F32) for kv in heads]
    m = [jnp.maximum(jnp.max(s_ctx[kv], axis=1, keepdims=True), sink[kv]) for kv in heads]
    if local:
        s_loc = [jnp.where(ok, lax.dot_general(q4[kv], k_ref[pl.ds(ks, span), cs[kv]], dims,
                                               preferred_element_type=F32), NEG_BIG) for kv in heads]
        m = [jnp.maximum(m[kv], jnp.max(s_loc[kv], axis=1, keepdims=True)) for kv in heads]
        p_loc = [jnp.exp2((s_loc[kv] - m[kv]) * c2) for kv in heads]
    p_ctx = [jnp.exp2((s_ctx[kv] - m[kv]) * c2) for kv in heads]
    den = [jnp.sum(p_ctx[kv], axis=1, keepdims=True) + jnp.exp2((sink[kv] - m[kv]) * c2) for kv in heads]
    if local:
        den = [den[kv] + jnp.sum(p_loc[kv], axis=1, keepdims=True) for kv in heads]
    inv = [1.0 / den[kv] for kv in heads]
    out = [jnp.dot((p_ctx[kv] * inv[kv]).astype(BF16), vc_ref[:, cs[kv]], preferred_element_type=F32)
           for kv in heads]
    if local:
        out = [out[kv] + jnp.dot((p_loc[kv] * inv[kv]).astype(BF16), v_ref[pl.ds(ks, span), cs[kv]],
                                 preferred_element_type=F32) for kv in heads]
    for kv in heads:
        for j in range(grp):
            hq = kv * grp + j
            o_ref[:, hq * hd:(hq + 1) * hd] = out[kv][j * tq:(j + 1) * tq].astype(o_ref.dtype)


def _wa_attention(sink, zq, zkv, zc, local):
    bsz, lq, _ = zq.shape
    lctx = zc.shape[1]
    hd = zq.shape[2] // (WA_Q_HEADS + 2 * WA_KV_HEADS)
    qd, kd = WA_Q_HEADS * hd, WA_KV_HEADS * hd
    tq = WA_BLOCK
    kblk = qd // kd
    in_specs = [pl.BlockSpec(memory_space=pltpu.SMEM),
                pl.BlockSpec((None, tq, qd), lambda b, i: (b, i, 0))]
    args = [sink.reshape(1, WA_Q_HEADS), zq]
    if local:
        n = zkv.shape[1]
        in_specs += [pl.BlockSpec((None, n, kd), lambda b, i: (b, 0, kblk)),
                     pl.BlockSpec((None, n, kd), lambda b, i: (b, 0, kblk + 1))]
        args += [zkv, zkv]
    in_specs += [pl.BlockSpec((None, lctx, kd), lambda b, i: (b, 0, kblk)),
                 pl.BlockSpec((None, lctx, kd), lambda b, i: (b, 0, kblk + 1))]
    args += [zc, zc]
    return pl.pallas_call(
        functools.partial(_wa_kernel, local=local, scale=hd ** -0.5),
        out_shape=jax.ShapeDtypeStruct((bsz, lq, qd), BF16),
        grid=(bsz, lq // tq),
        in_specs=in_specs,
        out_specs=pl.BlockSpec((None, tq, qd), lambda b, i: (b, i, 0)),
        compiler_params=_cparams("parallel", "parallel"),
        name="wa_attention",
    )(*args)


def _rope_tables(n, hd):
    rows = n // GRID_W
    row = jnp.repeat(jnp.arange(rows, dtype=F32), GRID_W)
    col = jnp.tile(jnp.arange(GRID_W, dtype=F32), rows)
    n_freq = hd // 4
    inv = ROPE_BASE ** (-jnp.arange(n_freq, dtype=F32) / n_freq)
    ang = jnp.concatenate([row[:, None] * inv, col[:, None] * inv], axis=-1)
    cos, sin = jnp.cos(ang), jnp.sin(ang)
    return jnp.concatenate([cos, cos], axis=-1), jnp.concatenate([-sin, sin], axis=-1)


def _wa_mixer(h, hc, g, mod, mod_c, w_in, sink, w_out, with_ctx):
    bsz, n, d = h.shape
    lctx = hc.shape[1] // bsz
    hd = d // WA_Q_HEADS
    w = w_in.astype(BF16)
    rope_cols = (WA_Q_HEADS + WA_KV_HEADS) * hd
    zl = _proj(h, g, mod, 3, w, BF16, rope=_rope_tables(n, hd), rope_cols=rope_cols)
    zc = _proj(hc, g, mod_c, 3, w, BF16).reshape(bsz, lctx, -1)
    sink = sink.astype(F32)
    w_o = w_out.astype(BF16)
    h = _out_proj(_wa_attention(sink, zl, zl, zc, True), w_o, h, mod, 5)
    if with_ctx:
        yc = _wa_attention(sink, zc, None, zc, False)
        hc = _out_proj(yc.reshape(1, bsz * lctx, d), w_o, hc, mod_c, 5)
    return h, hc


def _s5_proj_kernel(x_ref, g_ref, sh_ref, sc_ref, w_ref, of_ref, hn_ref):
    n = pl.program_id(2)

    @pl.when(n == 0)
    def _():
        _adaln_rows(hn_ref, x_ref, g_ref, sh_ref, sc_ref)

    acc = jnp.dot(hn_ref[...], w_ref[...], preferred_element_type=F32)
    tm = acc.shape[0]
    for k in range(acc.shape[1] // LANES):
        of_ref[k] = acc[:, k * LANES:(k + 1) * LANES].reshape(tm // S5_CHUNK, S5_CHUNK, LANES)


def _s5_proj(x, g, mod, k0, w, shared_mod):
    bsz, rows, d = x.shape
    tm = _row_tile(rows, ROW_TILE_STREAM)
    tn = _col_tile(d, 1024)
    lc = S5_CHUNK
    return pl.pallas_call(
        _s5_proj_kernel,
        out_shape=jax.ShapeDtypeStruct((d // LANES, rows // lc, bsz, lc, LANES), F32),
        grid=(bsz, rows // tm, d // tn),
        in_specs=[pl.BlockSpec((None, tm, d), lambda b, i, n: (b, i, 0)),
                  pl.BlockSpec((1, d), lambda b, i, n: (0, 0)),
                  _mod_spec(k0, d, shared_mod), _mod_spec(k0 + 1, d, shared_mod),
                  pl.BlockSpec((d, tn), lambda b, i, n: (0, n))],
        out_specs=pl.BlockSpec((tn // LANES, tm // lc, None, lc, LANES), lambda b, i, n: (n, i, b, 0, 0)),
        scratch_shapes=[pltpu.VMEM((tm, d), BF16)],
        compiler_params=_cparams("parallel", "parallel", "arbitrary"),
        name="s5_proj",
    )(x, g.reshape(1, d), mod, mod, w)


S5_PACKETS = LANES // S5_GROUP
S5_REGROUP_ROWS = 16


def _packet_transpose(vs):
    lane = lax.broadcasted_iota(jnp.int32, vs[0].shape, 1)
    d = S5_PACKETS // 2
    while d:
        low = (lane & (d * S5_GROUP)) == 0
        nxt = list(vs)
        for i in range(S5_PACKETS):
            if not i & d:
                nxt[i] = jnp.where(low, vs[i], pltpu.roll(vs[i + d], d * S5_GROUP, axis=1))
                nxt[i + d] = jnp.where(low, pltpu.roll(vs[i], LANES - d * S5_GROUP, axis=1), vs[i + d])
        vs = nxt
        d //= 2
    return vs


def _s5_group_kernel(u_ref, o_ref):
    def body(r, carry):
        rows = pl.ds(pl.multiple_of(r * S5_REGROUP_ROWS, S5_REGROUP_ROWS), S5_REGROUP_ROWS)
        for hf in range(S5_CHUNK // S5_PACKETS):
            base = hf * S5_PACKETS
            vs = [pltpu.bitcast(u_ref[rows, (base + t) * LANES:(base + t + 1) * LANES].astype(o_ref.dtype),
                                jnp.uint32) for t in range(S5_PACKETS)]
            for gq, v in enumerate(_packet_transpose(vs)):
                o_ref[gq, rows, hf * LANES:(hf + 1) * LANES] = pltpu.bitcast(v, o_ref.dtype)
        return carry

    lax.fori_loop(0, u_ref.shape[0] // S5_REGROUP_ROWS, body, 0, unroll=4)


def _s5_group(u):
    g8, rows, wide = u.shape
    tr = _row_tile(rows, ROW_TILE)
    w = wide // S5_PACKETS
    return pl.pallas_call(
        _s5_group_kernel,
        out_shape=jax.ShapeDtypeStruct((g8 * S5_PACKETS, rows, w), BF16),
        grid=(g8, rows // tr),
        in_specs=[pl.BlockSpec((None, tr, wide), lambda g, i: (g, i, 0))],
        out_specs=pl.BlockSpec((S5_PACKETS, tr, w), lambda g, i: (g, i, 0)),
        compiler_params=_cparams("parallel", "parallel"),
        name="s5_group",
    )(u)


def _s5_ungroup_kernel(y_ref, u_ref, ds_ref, o_ref):
    def body(r, carry):
        rows = pl.ds(pl.multiple_of(r * S5_REGROUP_ROWS, S5_REGROUP_ROWS), S5_REGROUP_ROWS)
        for hf in range(S5_CHUNK // S5_PACKETS):
            base = hf * S5_PACKETS
            vs = [pltpu.bitcast(y_ref[gq, rows, hf * LANES:(hf + 1) * LANES], jnp.uint32)
                  for gq in range(S5_PACKETS)]
            for t, v in enumerate(_packet_transpose(vs)):
                cols = slice((base + t) * LANES, (base + t + 1) * LANES)
                y = pltpu.bitcast(v, F32)
                o_ref[rows, cols] = jax.nn.gelu(y + ds_ref[...] * u_ref[rows, cols]).astype(o_ref.dtype)
        return carry

    lax.fori_loop(0, o_ref.shape[0] // S5_REGROUP_ROWS, body, 0, unroll=2)


def _s5_ungroup(y, u, d_skip):
    ng, rows, w = y.shape
    tr = _row_tile(rows, ROW_TILE)
    wide = w * S5_PACKETS
    wide_spec = pl.BlockSpec((None, tr, wide), lambda g, i: (g, i, 0))
    return pl.pallas_call(
        _s5_ungroup_kernel,
        out_shape=jax.ShapeDtypeStruct((ng // S5_PACKETS, rows, wide), BF16),
        grid=(ng // S5_PACKETS, rows // tr),
        in_specs=[pl.BlockSpec((S5_PACKETS, tr, w), lambda g, i: (g, i, 0)), wide_spec,
                  pl.BlockSpec((None, 1, LANES), lambda g, i: (g, 0, 0))],
        out_specs=wide_spec,
        compiler_params=_cparams("parallel", "parallel"),
        name="s5_ungroup",
    )(y, u, d_skip)


def _s5_kernel(uc_ref, ul_ref, kl_ref, q_ref, p_ref, a_ref, yc_ref, yl_ref,
               sc_ref, sl_ref, xc_ref, xl_ref, t_ref, *, bsz):
    half = sc_ref.shape[2] // 2
    nch = kl_ref.shape[1]
    for ti in range(S5_CHUNK):
        for to in range(S5_CHUNK):
            t_ref[ti * nch:(ti + 1) * nch, to * nch:(to + 1) * nch] = kl_ref[to - ti + S5_CHUNK - 1]
    for u_ref, s_ref in ((uc_ref, sc_ref), (ul_ref, sl_ref)):
        for d in range(2):
            s_ref[d] = jnp.dot(u_ref[...], q_ref[d], preferred_element_type=F32)
    aa = [a_ref[d, 0:1, :] for d in range(2)]
    ab = [a_ref[d, 1:2, :] for d in range(2)]

    def run(s_ref, x_ref, carry):
        n = s_ref.shape[1] // bsz

        def step(d, j, w):
            rows = pl.ds(pl.multiple_of(j * bsz, bsz), bsz)
            x_ref[d, rows, :] = w[:, :half]
            other = jnp.concatenate([w[:, half:], w[:, :half]], axis=1)
            return aa[d] * w + ab[d] * other + s_ref[d, rows, :]

        def body(i, ws):
            return step(0, i, ws[0]), step(1, n - 1 - i, ws[1])

        return lax.fori_loop(0, n, body, carry)

    zero = jnp.zeros((bsz, 2 * half), F32)
    run(sl_ref, xl_ref, run(sc_ref, xc_ref, (zero, zero)))
    t_m = t_ref[...].astype(BF16)
    for u_ref, x_ref, y_ref in ((uc_ref, xc_ref, yc_ref), (ul_ref, xl_ref, yl_ref)):
        y = jnp.dot(u_ref[...], t_m, preferred_element_type=F32)
        for d in range(2):
            y = y + jnp.dot(x_ref[d].astype(BF16), p_ref[d], preferred_element_type=F32)
        y_ref[...] = y


def _s5_scan(uc, ul, k_lag, q_m, p_m, a_m, bsz):
    ng, rc, w = uc.shape
    rl = ul.shape[1]
    st2 = p_m.shape[2]
    nlag, nch = k_lag.shape[1:3]

    def rows_spec(r):
        return pl.BlockSpec((None, r, w), lambda g: (g, 0, 0))

    return pl.pallas_call(
        functools.partial(_s5_kernel, bsz=bsz),
        out_shape=(jax.ShapeDtypeStruct((ng, rc, w), F32), jax.ShapeDtypeStruct((ng, rl, w), F32)),
        grid=(ng,),
        in_specs=[rows_spec(rc), rows_spec(rl),
                  pl.BlockSpec((None, nlag, nch, nch), lambda g: (g, 0, 0, 0)),
                  pl.BlockSpec((2, None, w, 2 * st2), lambda g: (0, g, 0, 0)),
                  pl.BlockSpec((2, None, st2, w), lambda g: (0, g, 0, 0)),
                  pl.BlockSpec((2, None, 2, 2 * st2), lambda g: (0, g, 0, 0))],
        out_specs=(rows_spec(rc), rows_spec(rl)),
        scratch_shapes=[pltpu.VMEM((2, rc, 2 * st2), F32), pltpu.VMEM((2, rl, 2 * st2), F32),
                        pltpu.VMEM((2, rc, st2), F32), pltpu.VMEM((2, rl, st2), F32),
                        pltpu.VMEM((w, w), F32)],
        compiler_params=_cparams("parallel"),
        name="s5_scan",
    )(uc, ul, k_lag, q_m, p_m, a_m)


def _s5_operators(lam_re, lam_im, log_dt, b_re, b_im, c_re, c_im):
    hi = lax.Precision.HIGHEST
    lc = S5_CHUNK
    dt = jnp.exp(log_dt)[..., None]
    mag = jnp.exp(lam_re * dt)
    lb_re, lb_im = mag * jnp.cos(lam_im * dt), mag * jnp.sin(lam_im * dt)
    den = lam_re * lam_re + lam_im * lam_im
    nr, ni = lb_re - 1.0, lb_im
    fr = (nr * lam_re + ni * lam_im) / den
    fi = (ni * lam_re - nr * lam_im) / den
    bb_re = fr[..., None] * b_re - fi[..., None] * b_im
    bb_im = fr[..., None] * b_im + fi[..., None] * b_re
    k = jnp.arange(lc + 1, dtype=F32)[:, None, None, None]
    pmag = jnp.exp(k * (lam_re * dt))
    pw_re, pw_im = pmag * jnp.cos(k * (lam_im * dt)), pmag * jnp.sin(k * (lam_im * dt))
    lbb_re = pw_re[..., None] * bb_re - pw_im[..., None] * bb_im
    lbb_im = pw_re[..., None] * bb_im + pw_im[..., None] * bb_re
    kern = (jnp.einsum('dgop,kdgpc->kdgco', c_re, lbb_re, precision=hi)
            - jnp.einsum('dgop,kdgpc->kdgco', c_im, lbb_im, precision=hi))
    ngrp, nch = lam_re.shape[1], b_re.shape[3]
    nst = lam_re.shape[2]

    k_lag = jnp.concatenate([jnp.flip(kern[1:lc, 1], axis=0), (kern[0, 0] + kern[0, 1])[None],
                             kern[1:lc, 0]], axis=0)
    k_lag = jnp.transpose(k_lag, (1, 0, 2, 3))

    def per_dir(d):
        e_in = (lc - 1 - jnp.arange(lc)) if d == 0 else jnp.arange(lc)
        q_re = jnp.transpose(lbb_re[e_in, d], (1, 0, 3, 2)).reshape(ngrp, lc * nch, nst)
        q_im = jnp.transpose(lbb_im[e_in, d], (1, 0, 3, 2)).reshape(ngrp, lc * nch, nst)
        q_m = jnp.concatenate([q_re, q_im, q_im, q_re], axis=-1)
        e_out = (jnp.arange(lc) + 1) if d == 0 else (lc - jnp.arange(lc))
        cl_re = (c_re[d][None] * pw_re[e_out, d][:, :, None, :]
                 - c_im[d][None] * pw_im[e_out, d][:, :, None, :])
        cl_im = (c_re[d][None] * pw_im[e_out, d][:, :, None, :]
                 + c_im[d][None] * pw_re[e_out, d][:, :, None, :])
        p_re = jnp.transpose(cl_re, (1, 3, 0, 2)).reshape(ngrp, nst, lc * nch)
        p_im = jnp.transpose(cl_im, (1, 3, 0, 2)).reshape(ngrp, nst, lc * nch)
        p_m = jnp.concatenate([p_re, -p_im], axis=1)
        a_re, a_im = pw_re[lc, d], pw_im[lc, d]
        a_m = jnp.stack([jnp.concatenate([a_re] * 4, axis=-1),
                         jnp.concatenate([-a_im, a_im, a_im, -a_im], axis=-1)], axis=1)
        return q_m, p_m, a_m

    q_m, p_m, a_m = (jnp.stack([x, y]) for x, y in zip(per_dir(0), per_dir(1)))
    return k_lag, q_m.astype(BF16), p_m.astype(BF16), a_m


def _s5_glu_kernel(zf_ref, wa_ref, wg_ref, h_ref, gate_ref, o_ref, z_ref):
    n = pl.program_id(2)

    @pl.when(n == 0)
    def _():
        tm = z_ref.shape[0]
        for k in range(zf_ref.shape[0]):
            z_ref[:, k * LANES:(k + 1) * LANES] = zf_ref[k].reshape(tm, LANES)

    z = z_ref[...]
    a = jnp.dot(z, wa_ref[...], preferred_element_type=F32)
    gt = jnp.dot(z, wg_ref[...], preferred_element_type=F32)
    o_ref[...] = h_ref[...] + gate_ref[...] * (a * jax.nn.sigmoid(gt))


def _s5_glu(z, w, h, mod, k, shared_mod):
    bsz, rows, d = h.shape
    tm = _row_tile(rows, ROW_TILE_STREAM)
    tn = _col_tile(d, 512)
    nn = d // tn
    lc = S5_CHUNK
    col = pl.BlockSpec((None, tm, tn), lambda b, i, n: (b, i, n))
    return pl.pallas_call(
        _s5_glu_kernel,
        out_shape=jax.ShapeDtypeStruct(h.shape, F32),
        grid=(bsz, rows // tm, nn),
        in_specs=[pl.BlockSpec((d // LANES, tm // lc, None, lc, LANES), lambda b, i, n: (0, i, b, 0, 0)),
                  pl.BlockSpec((d, tn), lambda b, i, n: (0, n)),
                  pl.BlockSpec((d, tn), lambda b, i, n: (0, n + nn)),
                  col, pl.BlockSpec((None, None, 1, tn),
                                    lambda b, i, n: (0 if shared_mod else b, k, 0, n))],
        out_specs=col,
        scratch_shapes=[pltpu.VMEM((tm, d), BF16)],
        compiler_params=_cparams("parallel", "parallel", "arbitrary"),
        name="s5_glu",
    )(z, w, w, h, mod)


def _s5_mixer(h, hc, g, mod, mod_c, w_in, lam_re, lam_im, log_dt, b_re, b_im, c_re, c_im,
              d_skip, w_out, with_ctx):
    bsz, n, d = h.shape
    lctx = hc.shape[1] // bsz
    lc = S5_CHUNK
    g8 = d // LANES
    w = w_in.astype(BF16)
    hc3 = hc.reshape(bsz, lctx, d)
    u_l = _s5_proj(h, g, mod, 3, w, False).reshape(g8, n // lc * bsz, lc * LANES)
    u_c = _s5_proj(hc3, g, mod_c, 3, w, True).reshape(g8, lctx // lc * bsz, lc * LANES)
    ops = _s5_operators(lam_re, lam_im, log_dt, b_re, b_im, c_re, c_im)
    yp_c, yp_l = _s5_scan(_s5_group(u_c), _s5_group(u_l), *ops, bsz=bsz)
    ds = d_skip.astype(F32).reshape(g8, 1, LANES)
    w_o = w_out.astype(BF16)
    z_l = _s5_ungroup(yp_l, u_l, ds).reshape(g8, n // lc, bsz, lc, LANES)
    h = _s5_glu(z_l, w_o, h, mod, 5, False)
    if with_ctx:
        z_c = _s5_ungroup(yp_c, u_c, ds).reshape(g8, lctx // lc, bsz, lc, LANES)
        hc = _s5_glu(z_c, w_o, hc3, mod_c, 5, True).reshape(1, bsz * lctx, d)
    return h, hc


def _rmsnorm_kernel(x_ref, g_ref, o_ref):
    x = x_ref[...]
    o_ref[...] = x * lax.rsqrt(jnp.mean(x * x, axis=-1, keepdims=True) + EPS) * g_ref[...]


def _final_norm(x, g):
    bsz, rows, d = x.shape
    tm = _row_tile(rows, ROW_TILE)
    row = pl.BlockSpec((None, tm, d), lambda b, i: (b, i, 0))
    return pl.pallas_call(
        _rmsnorm_kernel,
        out_shape=jax.ShapeDtypeStruct(x.shape, F32),
        grid=(bsz, rows // tm),
        in_specs=[row, pl.BlockSpec((1, d), lambda b, i: (0, 0))],
        out_specs=row,
        compiler_params=_cparams("parallel", "parallel"),
        name="final_norm",
    )(x, g.reshape(1, d))


def kernel(x, c, ctx, c_ctx, w_ada, b_ada, g_norm, w_ffn_in, w_ffn_out, g_final, ml_w_in, ml_b_gate, ml_g_head, ml_w_out, wa_w_in, wa_sink, wa_w_out, s5_w_in, s5_lam_re, s5_lam_im, s5_log_dt, s5_b_re, s5_b_im, s5_c_re, s5_c_im, s5_d_skip, s5_w_out):
    bsz, n, d = x.shape
    depth = w_ada.shape[0]
    lctx = ctx.shape[1]
    n_rows = -(-(bsz + 1) // 16) * 16
    c_rows = jnp.concatenate([c, c_ctx[None], jnp.zeros((n_rows - bsz - 1, d), F32)], axis=0)
    mods = _mod_table(c_rows, w_ada, b_ada).reshape(depth, n_rows, N_MOD, 1, d)
    h = x
    hc = ctx.reshape(1, bsz * lctx, d)
    for layer in range(depth):
        has_next = layer < depth - 1
        mod, mod_c = mods[layer, :bsz], mods[layer, bsz:bsz + 1]
        g = g_norm[layer]
        w_in = w_ffn_in[layer].astype(BF16)
        w_out = w_ffn_out[layer].astype(BF16)
        h = _ffn(h, g[0], mod, 0, w_in[0], w_out[0])
        hc = _ffn(hc, g[0], mod_c, 0, w_in[0], w_out[0])
        kind, idx = layer % 3, layer // 3
        if kind == 0:
            h, hc = _mlstm_mixer(h, hc, g[1], mod, mod_c, ml_w_in[idx], ml_b_gate[idx],
                                 ml_g_head[idx], ml_w_out[idx], has_next)
        elif kind == 1:
            h, hc = _wa_mixer(h, hc, g[1], mod, mod_c, wa_w_in[idx], wa_sink[idx],
                              wa_w_out[idx], has_next)
        else:
            h, hc = _s5_mixer(h, hc, g[1], mod, mod_c, s5_w_in[idx], s5_lam_re[idx], s5_lam_im[idx],
                              s5_log_dt[idx], s5_b_re[idx], s5_b_im[idx], s5_c_re[idx],
                              s5_c_im[idx], s5_d_skip[idx], s5_w_out[idx], has_next)
        h = _ffn(h, g[2], mod, 6, w_in[1], w_out[1])
        if has_next:
            hc = _ffn(hc, g[2], mod_c, 6, w_in[1], w_out[1])
    return _final_norm(h, g_final)
```

```python
import functools
import math

import jax
import jax.numpy as jnp
from jax import lax
from jax.experimental import pallas as pl
from jax.experimental.pallas import tpu as pltpu

F32 = jnp.float32
BF16 = jnp.bfloat16

EPS = 1e-6
NEG_BIG = -1e30
LOG2E = math.log2(math.e)
N_MOD = 9
ML_HEADS = 8
ML_CHUNK = 512
ML_MERGE_ROWS = 256
WA_Q_HEADS = 16
WA_KV_HEADS = 4
WA_WINDOW = 128
WA_BLOCK = 128
GRID_W = 64
ROPE_BASE = 10000.0
S5_GROUP = 16
S5_STATE = 64
S5_CHUNK = 16

LANES = 128
ROW_TILE = 512
ROW_TILE_STREAM = 1024
VMEM_LIMIT = 56 * 1024 * 1024


def _cparams(*sem):
    return pltpu.CompilerParams(dimension_semantics=sem, vmem_limit_bytes=VMEM_LIMIT)


def _row_tile(rows, want):
    return want if rows % want == 0 else rows


def _col_tile(cols, want):
    t = min(want, cols)
    while cols % t:
        t -= LANES
    return t


def _adaln(x, g, shift, scale):
    var = jnp.mean(x * x, axis=-1, keepdims=True)
    return (x * lax.rsqrt(var + EPS) * g) * (1.0 + scale) + shift


ADALN_ROWS = 128


def _adaln_rows(hn_ref, x_ref, g_ref, sh_ref, sc_ref):
    rows = x_ref.shape[0]
    step = ADALN_ROWS if rows % ADALN_ROWS == 0 else rows

    def body(r, carry):
        sl = pl.ds(pl.multiple_of(r * step, step), step)
        hn_ref[sl, :] = _adaln(x_ref[sl, :], g_ref[...], sh_ref[...], sc_ref[...]).astype(BF16)
        return carry

    lax.fori_loop(0, rows // step, body, 0)


def _mod_spec(k, d, shared=False):
    return pl.BlockSpec((None, None, 1, d), lambda b, *_: (0 if shared else b, k, 0, 0))


def _mod_kernel(c_ref, w_ref, b_ref, o_ref):
    c = c_ref[...]
    s = (c * jax.nn.sigmoid(c)).astype(BF16)
    o_ref[...] = jnp.dot(s, w_ref[...].astype(BF16), preferred_element_type=F32) + b_ref[...]


def _mod_table(c_rows, w_ada, b_ada):
    depth, d, nd = w_ada.shape
    r = c_rows.shape[0]
    tn = _col_tile(nd, 1024)
    return pl.pallas_call(
        _mod_kernel,
        out_shape=jax.ShapeDtypeStruct((depth, r, nd), F32),
        grid=(depth, nd // tn),
        in_specs=[pl.BlockSpec((r, d), lambda l, n: (0, 0)),
                  pl.BlockSpec((None, d, tn), lambda l, n: (l, 0, n)),
                  pl.BlockSpec((None, 1, tn), lambda l, n: (l, 0, n))],
        out_specs=pl.BlockSpec((None, r, tn), lambda l, n: (l, 0, n)),
        compiler_params=_cparams("parallel", "parallel"),
        name="mod_table",
    )(c_rows, w_ada, b_ada.reshape(depth, 1, nd))


def _ffn_kernel(xn_ref, g_ref, sh_ref, sc_ref, x_ref, gate_ref, wa_ref, wg_ref, wo_ref, o_ref,
                hn_ref, act_ref, *, nf):
    j = pl.program_id(2)
    last = pl.num_programs(2) - 1
    tf = wa_ref.shape[1]

    @pl.when((j == 0) & (pl.program_id(0) == 0) & (pl.program_id(1) == 0))
    def _():
        _adaln_rows(hn_ref, xn_ref, g_ref, sh_ref, sc_ref)

    @pl.when(j < nf)
    def _():
        hn = hn_ref[...]
        a = jnp.dot(hn, wa_ref[...], preferred_element_type=F32)
        gt = jnp.dot(hn, wg_ref[...], preferred_element_type=F32)
        cols = pl.ds(pl.multiple_of(j * tf, tf), tf)
        act_ref[:, cols] = (a * (gt * jax.nn.sigmoid(gt))).astype(BF16)

    def drain():
        y = jnp.dot(act_ref[...], wo_ref[...], preferred_element_type=F32)
        o_ref[...] = x_ref[...] + (0.5 * gate_ref[...]) * y

    @pl.when((j >= nf) & (j < last))
    def _():
        drain()

    @pl.when(j == last)
    def _():
        drain()
        step = ADALN_ROWS if xn_ref.shape[0] % ADALN_ROWS == 0 else xn_ref.shape[0]
        for r in range(xn_ref.shape[0] // step):
            rows = slice(r * step, (r + 1) * step)
            hn_ref[rows, :] = _adaln(xn_ref[rows, :], g_ref[...], sh_ref[...], sc_ref[...]).astype(BF16)


def _ffn(x, g, mod, k0, w_in, w_out):
    bsz, rows, d = x.shape
    ff = w_out.shape[0]
    tm = _row_tile(rows, ROW_TILE_STREAM)
    tf = _col_tile(ff, 512)
    tn = _col_tile(d, 256)
    nf = ff // tf
    nt = rows // tm
    steps = nf + d // tn

    def fill(j):
        return jnp.minimum(j, nf - 1)

    def drain(j):
        return jnp.maximum(j - nf, 0)

    def ahead(b, i, j):
        r = jnp.minimum(b * nt + i + (j == steps - 1).astype(jnp.int32), bsz * nt - 1)
        return r // nt, r % nt

    def mod_ahead(k):
        return pl.BlockSpec((None, None, 1, d), lambda b, i, j: (ahead(b, i, j)[0], k, 0, 0))

    return pl.pallas_call(
        functools.partial(_ffn_kernel, nf=nf),
        out_shape=jax.ShapeDtypeStruct(x.shape, F32),
        grid=(bsz, nt, steps),
        in_specs=[pl.BlockSpec((None, tm, d), lambda b, i, j: (*ahead(b, i, j), 0)),
                  pl.BlockSpec((1, d), lambda b, i, j: (0, 0)),
                  mod_ahead(k0), mod_ahead(k0 + 1),
                  pl.BlockSpec((None, tm, tn), lambda b, i, j: (b, i, drain(j))),
                  pl.BlockSpec((None, None, 1, tn), lambda b, i, j: (b, k0 + 2, 0, drain(j))),
                  pl.BlockSpec((d, tf), lambda b, i, j: (0, fill(j))),
                  pl.BlockSpec((d, tf), lambda b, i, j: (0, fill(j) + nf)),
                  pl.BlockSpec((ff, tn), lambda b, i, j: (0, drain(j)))],
        out_specs=pl.BlockSpec((None, tm, tn), lambda b, i, j: (b, i, drain(j))),
        scratch_shapes=[pltpu.VMEM((tm, d), BF16), pltpu.VMEM((tm, ff), BF16)],
        compiler_params=_cparams("arbitrary", "arbitrary", "arbitrary"),
        name="ffn",
    )(x, g.reshape(1, d), mod, mod, x, mod, w_in, w_in, w_out)


def _proj_kernel(*refs, n_rope, with_gates, n_scaled, scale):
    x_ref, g_ref, sh_ref, sc_ref, w_ref = refs[:5]
    rest = refs[5:]
    if n_rope:
        cs_ref, sn_ref = rest[:2]
        rest = rest[2:]
    if with_gates:
        wg_ref, bg_ref, o_ref, og_ref, hn_ref = rest
    else:
        o_ref, hn_ref = rest
    n = pl.program_id(2)

    @pl.when(n == 0)
    def _():
        _adaln_rows(hn_ref, x_ref, g_ref, sh_ref, sc_ref)
        if with_gates:
            og_ref[...] = jnp.dot(hn_ref[...], wg_ref[...], preferred_element_type=F32) + bg_ref[...]

    acc = jnp.dot(hn_ref[...], w_ref[...], preferred_element_type=F32)
    if n_scaled:
        acc = acc * jnp.where(n < n_scaled, scale, 1.0)
    if not n_rope:
        o_ref[...] = acc.astype(o_ref.dtype)
    else:
        @pl.when(n >= n_rope)
        def _():
            o_ref[...] = acc.astype(o_ref.dtype)

        @pl.when(n < n_rope)
        def _():
            cs, sn = cs_ref[...], sn_ref[...]
            hd = cs.shape[1]
            for h in range(acc.shape[1] // hd):
                r = acc[:, h * hd:(h + 1) * hd]
                r = r * cs + pltpu.roll(r, hd // 2, axis=1) * sn
                o_ref[:, h * hd:(h + 1) * hd] = r.astype(o_ref.dtype)


def _proj(x, g, mod, k0, w, out_dtype, rope=None, rope_cols=0, gates=None, scale=1.0, scale_cols=0):
    bsz, rows, d = x.shape
    nout = w.shape[1]
    tm = _row_tile(rows, ROW_TILE_STREAM)
    tn = _col_tile(math.gcd(math.gcd(nout, rope_cols), scale_cols), 1024)
    n_rope = rope_cols // tn
    row = pl.BlockSpec((None, tm, d), lambda b, i, n: (b, i, 0))
    in_specs = [row, pl.BlockSpec((1, d), lambda b, i, n: (0, 0)),
                _mod_spec(k0, d), _mod_spec(k0 + 1, d),
                pl.BlockSpec((d, tn), lambda b, i, n: (0, n))]
    args = [x, g.reshape(1, d), mod, mod, w]
    if n_rope:
        cs, sn = rope
        hd = cs.shape[1]
        in_specs += [pl.BlockSpec((tm, hd), lambda b, i, n: (i, 0))] * 2
        args += [cs, sn]
    out_shape = jax.ShapeDtypeStruct((bsz, rows, nout), out_dtype)
    out_specs = pl.BlockSpec((None, tm, tn), lambda b, i, n: (b, i, n))
    if gates is not None:
        wg, bg = gates
        ng = wg.shape[1]
        in_specs += [pl.BlockSpec((d, ng), lambda b, i, n: (0, 0)),
                     pl.BlockSpec((1, ng), lambda b, i, n: (0, 0))]
        args += [wg, bg]
        out_shape = (out_shape, jax.ShapeDtypeStruct((bsz, rows, ng), F32))
        out_specs = (out_specs, pl.BlockSpec((None, tm, ng), lambda b, i, n: (b, i, 0)))
    return pl.pallas_call(
        functools.partial(_proj_kernel, n_rope=n_rope, with_gates=gates is not None,
                          n_scaled=scale_cols // tn, scale=scale),
        out_shape=out_shape,
        grid=(bsz, rows // tm, nout // tn),
        in_specs=in_specs,
        out_specs=out_specs,
        scratch_shapes=[pltpu.VMEM((tm, d), BF16)],
        compiler_params=_cparams("parallel", "parallel", "arbitrary"),
        name="proj",
    )(*args)


def _out_proj_kernel(y_ref, w_ref, h_ref, gate_ref, o_ref):
    acc = jnp.dot(y_ref[...], w_ref[...], preferred_element_type=F32)
    o_ref[...] = h_ref[...] + gate_ref[...] * acc


def _out_proj(y, w, h, mod, k):
    bsz, rows, d = h.shape
    dk = y.shape[2]
    tm = _row_tile(rows, ROW_TILE)
    return pl.pallas_call(
        _out_proj_kernel,
        out_shape=jax.ShapeDtypeStruct(h.shape, F32),
        grid=(bsz, rows // tm),
        in_specs=[pl.BlockSpec((None, tm, dk), lambda b, i: (b, i, 0)),
                  pl.BlockSpec((dk, d), lambda b, i: (0, 0)),
                  pl.BlockSpec((None, tm, d), lambda b, i: (b, i, 0)),
                  _mod_spec(k, d)],
        out_specs=pl.BlockSpec((None, tm, d), lambda b, i: (b, i, 0)),
        compiler_params=_cparams("parallel", "parallel"),
        name="out_proj",
    )(y, w, h, mod)


def _ml_chunk_len(length):
    return min(ML_CHUNK, length)


def _ml_gate_kernel(g_ref, o_ref, *, nh, lc):
    row = lax.broadcasted_iota(jnp.int32, (lc, lc), 0)
    col = lax.broadcasted_iota(jnp.int32, (lc, lc), 1)
    lower = (col <= row).astype(BF16)
    upper = (col >= row).astype(BF16)
    lane = lax.broadcasted_iota(jnp.int32, (lc, LANES), 1)
    rix = lax.broadcasted_iota(jnp.int32, (lc, LANES), 0)

    def cumsum(tri, parts):
        return sum(jnp.dot(tri, p, preferred_element_type=F32) for p in parts)

    for c in range(g_ref.shape[0] // lc):
        x = g_ref[c * lc:(c + 1) * lc, :]
        ls = jax.nn.log_sigmoid(x)
        hi = ls.astype(BF16)
        r1 = ls - hi.astype(F32)
        mid = r1.astype(BF16)
        lo = (r1 - mid.astype(F32)).astype(BF16)
        pre = cumsum(lower, (hi, mid, lo))
        suf = cumsum(upper, (hi, mid, lo))
        a_f = x - pltpu.roll(pre, LANES - nh, axis=1)
        a_b = x - pltpu.roll(suf, LANES - nh, axis=1)
        cm_f, cm_b = a_f, a_b
        sh = 1
        while sh < lc:
            cm_f = jnp.maximum(cm_f, jnp.where(rix >= sh, pltpu.roll(cm_f, sh, axis=0), NEG_BIG))
            cm_b = jnp.maximum(cm_b, jnp.where(rix < lc - sh, pltpu.roll(cm_b, lc - sh, axis=0), NEG_BIG))
            sh *= 2
        out = jnp.where(lane < nh, a_f,
              jnp.where(lane < 2 * nh, pre,
              jnp.where(lane < 3 * nh, a_b,
              jnp.where(lane < 4 * nh, suf,
              jnp.where(lane < 5 * nh, pltpu.roll(cm_f, 4 * nh, axis=1),
                        pltpu.roll(cm_b, 3 * nh, axis=1))))))
        o_ref[c * lc:(c + 1) * lc, :] = out


def _ml_gates(g):
    bsz, rows, w = g.shape
    tm = _row_tile(rows, ROW_TILE_STREAM)
    blk = pl.BlockSpec((None, tm, w), lambda b, i: (b, i, 0))
    return pl.pallas_call(
        functools.partial(_ml_gate_kernel, nh=ML_HEADS, lc=_ml_chunk_len(rows)),
        out_shape=jax.ShapeDtypeStruct(g.shape, F32),
        grid=(bsz, rows // tm),
        in_specs=[blk],
        out_specs=blk,
        compiler_params=_cparams("parallel", "parallel"),
        name="ml_gates",
    )(g)


def _ml_chunk(q, k, v_aug, a_col, b_col, cm_col, a_row, c_mem, m_run, tri, last):
    dd = range(2)
    hd = q[0].shape[1]
    nt = (((1,), (1,)), ((), ()))
    tn = (((0,), (0,)), ((), ()))
    b_end = [b_col[d][last[d]:last[d] + 1, :] for d in dd]
    m_new = [b_end[d] + jnp.maximum(m_run[d], cm_col[d][last[d]:last[d] + 1, :]) for d in dd]
    r_col = [jnp.maximum(m_run[d], cm_col[d]) for d in dd]
    s = [lax.dot_general(q[d], k[d], nt, preferred_element_type=F32) for d in dd]
    c_bf = [c_mem[d].astype(BF16) for d in dd]
    qc = [jnp.dot(q[d], c_bf[d], preferred_element_type=F32) for d in dd]
    wk = [k[d] * jnp.exp(a_col[d] + (b_end[d] - m_new[d])).astype(BF16) for d in dd]
    e = [jnp.exp(jnp.where(tri[d], a_row[d] - r_col[d], NEG_BIG)) for d in dd]
    upd = [lax.dot_general(wk[d], v_aug[d], tn, preferred_element_type=F32) for d in dd]
    p = [(s[d] * e[d]).astype(BF16) for d in dd]
    dec = [jnp.exp(b_end[d] + m_run[d] - m_new[d]) for d in dd]
    dq = [jnp.exp(m_run[d] - r_col[d]) for d in dd]
    acc = [jnp.dot(p[d], v_aug[d], preferred_element_type=F32) + dq[d] * qc[d] for d in dd]
    c_new = [dec[d] * c_mem[d] + upd[d] for d in dd]
    lim = [jnp.exp(-(b_col[d] + r_col[d])) for d in dd]
    h = [acc[d][:, :hd] / jnp.maximum(jnp.abs(acc[d][:, hd:hd + 1]), lim[d]) for d in dd]
    return h, c_new, m_new


def _mlstm_kernel(cq, ck, cv, co, lq, lk, lv, lo, cgc, cgr, lgc, lgr, gh_ref,
                  yc_ref, yl_ref, hf_ref, hb_ref, c_ref, m_ref):
    c_ref[...] = jnp.zeros_like(c_ref)
    m_ref[...] = jnp.full_like(m_ref, NEG_BIG)

    def scan(q_ref, k_ref, v_ref, o_ref, gc_ref, gr_ref, y_ref):
        lc = _ml_chunk_len(q_ref.shape[0])
        nc = q_ref.shape[0] // lc
        rows = lax.broadcasted_iota(jnp.int32, (lc, lc), 0)
        cols = lax.broadcasted_iota(jnp.int32, (lc, lc), 1)
        tri = (cols <= rows, cols >= rows)
        ones_blk = jnp.where(lax.broadcasted_iota(jnp.int32, (lc, LANES), 1) == 0, 1.0, 0.0).astype(BF16)

        def body(c, carry):
            sl = (pl.ds(pl.multiple_of(c * lc, lc), lc), pl.ds(pl.multiple_of((nc - 1 - c) * lc, lc), lc))
            dd = range(2)
            gc = [gc_ref[sl[d], :] for d in dd]
            h, c_new, m_new = _ml_chunk(
                [q_ref[sl[d], :] for d in dd], [k_ref[sl[d], :] for d in dd],
                [jnp.concatenate([v_ref[sl[d], :], ones_blk], axis=1) for d in dd],
                [gc[d][:, 2 * d:2 * d + 1] for d in dd], [gc[d][:, 2 * d + 1:2 * d + 2] for d in dd],
                [gc[d][:, 4 + d:5 + d] for d in dd], [gr_ref[d:d + 1, sl[d]] for d in dd],
                [c_ref[d] for d in dd], [m_ref[d] for d in dd], tri, (lc - 1, 0))
            for d, out_ref in enumerate((hf_ref, hb_ref)):
                c_ref[d] = c_new[d]
                m_ref[d] = m_new[d]
                out_ref[sl[d], :] = h[d]
            return carry

        lax.fori_loop(0, nc, body, 0)

        lm = ML_MERGE_ROWS

        def merge(c, carry):
            sl = pl.ds(pl.multiple_of(c * lm, lm), lm)
            hs = hf_ref[sl, :] + hb_ref[sl, :]
            hs = hs * lax.rsqrt(jnp.mean(hs * hs, axis=-1, keepdims=True) + EPS) * gh_ref[...]
            y_ref[sl, :] = (jax.nn.sigmoid(o_ref[sl, :].astype(F32)) * hs).astype(y_ref.dtype)
            return carry

        lax.fori_loop(0, q_ref.shape[0] // lm, merge, 0)

    scan(cq, ck, cv, co, cgc, cgr, yc_ref)
    scan(lq, lk, lv, lo, lgc, lgr, yl_ref)


def _mlstm_scan(zc, zl, gc, gl, g_head):
    bsz, n, d4 = zl.shape
    lctx = zc.shape[1]
    d = d4 // 4
    nh = ML_HEADS
    hd = d // nh
    assert n % _ml_chunk_len(n) == 0 and n % ML_MERGE_ROWS == 0 and lctx % ML_MERGE_ROWS == 0

    def gate_views(g):
        length = g.shape[1]
        g = g[:, :, :6 * nh].reshape(bsz, length, 6, nh)
        return jnp.transpose(g, (0, 3, 1, 2)), jnp.transpose(g[:, :, 0:4:2], (0, 3, 2, 1))

    cgc, cgr = gate_views(gc)
    lgc, lgr = gate_views(gl)

    def zspec(length, k):
        return pl.BlockSpec((None, length, hd), lambda b, h: (b, 0, k * nh + h))

    def gspecs(length):
        return [pl.BlockSpec((None, None, length, 6), lambda b, h: (b, h, 0, 0)),
                pl.BlockSpec((None, None, 2, length), lambda b, h: (b, h, 0, 0))]

    lmax = max(n, lctx)
    return pl.pallas_call(
        _mlstm_kernel,
        out_shape=(jax.ShapeDtypeStruct((bsz, lctx, d), BF16),
                   jax.ShapeDtypeStruct((bsz, n, d), BF16)),
        grid=(bsz, nh),
        in_specs=([zspec(lctx, k) for k in range(4)] + [zspec(n, k) for k in range(4)]
                  + gspecs(lctx) + gspecs(n)
                  + [pl.BlockSpec((None, 1, hd), lambda b, h: (h, 0, 0))]),
        out_specs=(pl.BlockSpec((None, lctx, hd), lambda b, h: (b, 0, h)),
                   pl.BlockSpec((None, n, hd), lambda b, h: (b, 0, h))),
        scratch_shapes=[pltpu.VMEM((lmax, hd), F32), pltpu.VMEM((lmax, hd), F32),
                        pltpu.VMEM((2, hd, hd + LANES), F32), pltpu.VMEM((2, 1, 1), F32)],
        compiler_params=_cparams("parallel", "parallel"),
        name="mlstm_scan",
    )(zc, zc, zc, zc, zl, zl, zl, zl, cgc, cgr, lgc, lgr, g_head.reshape(nh, 1, hd))


def _mlstm_mixer(h, hc, g, mod, mod_c, w_in, b_gate, g_head, w_out, with_ctx):
    bsz, n, d = h.shape
    lctx = hc.shape[1] // bsz
    w_main = w_in[:, :4 * d].astype(BF16)
    ng = 4 * ML_HEADS
    ngp = -(-ng // LANES) * LANES
    w_gate = jnp.pad(w_in[:, 4 * d:], ((0, 0), (0, ngp - ng))).astype(BF16)
    bias = jnp.pad(b_gate.reshape(1, ng), ((0, 0), (0, ngp - ng)))
    q_scale = (d // ML_HEADS) ** -0.5
    zl, gl = _proj(h, g, mod, 3, w_main, BF16, gates=(w_gate, bias), scale=q_scale, scale_cols=d)
    zc, gc = _proj(hc, g, mod_c, 3, w_main, BF16, gates=(w_gate, bias), scale=q_scale, scale_cols=d)
    yc, yl = _mlstm_scan(zc.reshape(bsz, lctx, 4 * d), zl,
                         _ml_gates(gc.reshape(bsz, lctx, ngp)), _ml_gates(gl), g_head)
    w_o = w_out.astype(BF16)
    h = _out_proj(yl, w_o, h, mod, 5)
    if with_ctx:
        hc = _out_proj(yc.reshape(1, bsz * lctx, d), w_o, hc, mod_c, 5)
    return h, hc


def _wa_kernel(sink_ref, q_ref, *refs, local, scale):
    if local:
        k_ref, v_ref, kc_ref, vc_ref, o_ref = refs
    else:
        kc_ref, vc_ref, o_ref = refs
    tq = q_ref.shape[0]
    hd = kc_ref.shape[1] // WA_KV_HEADS
    grp = WA_Q_HEADS // WA_KV_HEADS
    if local:
        n = k_ref.shape[0]
        span = 3 * WA_BLOCK
        start = pl.program_id(1) * tq
        ks = pl.multiple_of(jnp.clip(start - WA_BLOCK, 0, n - span), WA_BLOCK)
        q_pos = start + (lax.broadcasted_iota(jnp.int32, (grp * tq, span), 0) & (tq - 1))
        k_pos = ks + lax.broadcasted_iota(jnp.int32, (grp * tq, span), 1)
        ok = jnp.abs(q_pos - k_pos) <= WA_WINDOW
    dims = (((1,), (1,)), ((), ()))
    heads = range(WA_KV_HEADS)
    c2 = scale * LOG2E
    cs = [slice(kv * hd, (kv + 1) * hd) for kv in heads]
    q4 = [jnp.concatenate([q_ref[:, (kv * grp + j) * hd:(kv * grp + j + 1) * hd] for j in range(grp)], axis=0)
          for kv in heads]
    sink = [jnp.concatenate([jnp.full((tq, 1), sink_ref[0, kv * grp + j], F32) for j in range(grp)], axis=0)
            * (1.0 / scale) for kv in heads]
    s_ctx = [lax.dot_general(q4[kv], kc_ref[:, cs[kv]], dims, preferred_element_type=F32) for kv in heads]
    m = [jnp.maximum(jnp.max(s_ctx[kv], axis=1, keepdims=True), sink[kv]) for kv in heads]
    if local:
        s_loc = [jnp.where(ok, lax.dot_general(q4[kv], k_ref[pl.ds(ks, span), cs[kv]], dims,
                                               preferred_element_type=F32), NEG_BIG) for kv in heads]
        m = [jnp.maximum(m[kv], jnp.max(s_loc[kv], axis=1, keepdims=True)) for kv in heads]
        p_loc = [jnp.exp2((s_loc[kv] - m[kv]) * c2) for kv in heads]
    p_ctx = [jnp.exp2((s_ctx[kv] - m[kv]) * c2) for kv in heads]
    den = [jnp.sum(p_ctx[kv], axis=1, keepdims=True) + jnp.exp2((sink[kv] - m[kv]) * c2) for kv in heads]
    if local:
        den = [den[kv] + jnp.sum(p_loc[kv], axis=1, keepdims=True) for kv in heads]
    inv = [1.0 / den[kv] for kv in heads]
    out = [jnp.dot((p_ctx[kv] * inv[kv]).astype(BF16), vc_ref[:, cs[kv]], preferred_element_type=F32)
           for kv in heads]
    if local:
        out = [out[kv] + jnp.dot((p_loc[kv] * inv[kv]).astype(BF16), v_ref[pl.ds(ks, span), cs[kv]],
                                 preferred_element_type=F32) for kv in heads]
    for kv in heads:
        for j in range(grp):
            hq = kv * grp + j
            o_ref[:, hq * hd:(hq + 1) * hd] = out[kv][j * tq:(j + 1) * tq].astype(o_ref.dtype)


def _wa_attention(sink, zq, zkv, zc, local):
    bsz, lq, _ = zq.shape
    lctx = zc.shape[1]
    hd = zq.shape[2] // (WA_Q_HEADS + 2 * WA_KV_HEADS)
    qd, kd = WA_Q_HEADS * hd, WA_KV_HEADS * hd
    tq = WA_BLOCK
    kblk = qd // kd
    in_specs = [pl.BlockSpec(memory_space=pltpu.SMEM),
                pl.BlockSpec((None, tq, qd), lambda b, i: (b, i, 0))]
    args = [sink.reshape(1, WA_Q_HEADS), zq]
    if local:
        n = zkv.shape[1]
        in_specs += [pl.BlockSpec((None, n, kd), lambda b, i: (b, 0, kblk)),
                     pl.BlockSpec((None, n, kd), lambda b, i: (b, 0, kblk + 1))]
        args += [zkv, zkv]
    in_specs += [pl.BlockSpec((None, lctx, kd), lambda b, i: (b, 0, kblk)),
                 pl.BlockSpec((None, lctx, kd), lambda b, i: (b, 0, kblk + 1))]
    args += [zc, zc]
    return pl.pallas_call(
        functools.partial(_wa_kernel, local=local, scale=hd ** -0.5),
        out_shape=jax.ShapeDtypeStruct((bsz, lq, qd), BF16),
        grid=(bsz, lq // tq),
        in_specs=in_specs,
        out_specs=pl.BlockSpec((None, tq, qd), lambda b, i: (b, i, 0)),
        compiler_params=_cparams("parallel", "parallel"),
        name="wa_attention",
    )(*args)


def _rope_tables(n, hd):
    rows = n // GRID_W
    row = jnp.repeat(jnp.arange(rows, dtype=F32), GRID_W)
    col = jnp.tile(jnp.arange(GRID_W, dtype=F32), rows)
    n_freq = hd // 4
    inv = ROPE_BASE ** (-jnp.arange(n_freq, dtype=F32) / n_freq)
    ang = jnp.concatenate([row[:, None] * inv, col[:, None] * inv], axis=-1)
    cos, sin = jnp.cos(ang), jnp.sin(ang)
    return jnp.concatenate([cos, cos], axis=-1), jnp.concatenate([-sin, sin], axis=-1)


def _wa_mixer(h, hc, g, mod, mod_c, w_in, sink, w_out, with_ctx):
    bsz, n, d = h.shape
    lctx = hc.shape[1] // bsz
    hd = d // WA_Q_HEADS
    w = w_in.astype(BF16)
    rope_cols = (WA_Q_HEADS + WA_KV_HEADS) * hd
    zl = _proj(h, g, mod, 3, w, BF16, rope=_rope_tables(n, hd), rope_cols=rope_cols)
    zc = _proj(hc, g, mod_c, 3, w, BF16).reshape(bsz, lctx, -1)
    sink = sink.astype(F32)
    w_o = w_out.astype(BF16)
    h = _out_proj(_wa_attention(sink, zl, zl, zc, True), w_o, h, mod, 5)
    if with_ctx:
        yc = _wa_attention(sink, zc, None, zc, False)
        hc = _out_proj(yc.reshape(1, bsz * lctx, d), w_o, hc, mod_c, 5)
    return h, hc


def _s5_proj_kernel(x_ref, g_ref, sh_ref, sc_ref, w_ref, of_ref, hn_ref):
    n = pl.program_id(2)

    @pl.when(n == 0)
    def _():
        _adaln_rows(hn_ref, x_ref, g_ref, sh_ref, sc_ref)

    acc = jnp.dot(hn_ref[...], w_ref[...], preferred_element_type=F32)
    tm = acc.shape[0]
    for k in range(acc.shape[1] // LANES):
        of_ref[k] = acc[:, k * LANES:(k + 1) * LANES].reshape(tm // S5_CHUNK, S5_CHUNK, LANES)


def _s5_proj(x, g, mod, k0, w, shared_mod):
    bsz, rows, d = x.shape
    tm = _row_tile(rows, ROW_TILE_STREAM)
    tn = _col_tile(d, 1024)
    lc = S5_CHUNK
    return pl.pallas_call(
        _s5_proj_kernel,
        out_shape=jax.ShapeDtypeStruct((d // LANES, rows // lc, bsz, lc, LANES), F32),
        grid=(bsz, rows // tm, d // tn),
        in_specs=[pl.BlockSpec((None, tm, d), lambda b, i, n: (b, i, 0)),
                  pl.BlockSpec((1, d), lambda b, i, n: (0, 0)),
                  _mod_spec(k0, d, shared_mod), _mod_spec(k0 + 1, d, shared_mod),
                  pl.BlockSpec((d, tn), lambda b, i, n: (0, n))],
        out_specs=pl.BlockSpec((tn // LANES, tm // lc, None, lc, LANES), lambda b, i, n: (n, i, b, 0, 0)),
        scratch_shapes=[pltpu.VMEM((tm, d), BF16)],
        compiler_params=_cparams("parallel", "parallel", "arbitrary"),
        name="s5_proj",
    )(x, g.reshape(1, d), mod, mod, w)


S5_PACKETS = LANES // S5_GROUP
S5_REGROUP_ROWS = 16


def _packet_transpose(vs):
    lane = lax.broadcasted_iota(jnp.int32, vs[0].shape, 1)
    d = S5_PACKETS // 2
    while d:
        low = (lane & (d * S5_GROUP)) == 0
        nxt = list(vs)
        for i in range(S5_PACKETS):
            if not i & d:
                nxt[i] = jnp.where(low, vs[i], pltpu.roll(vs[i + d], d * S5_GROUP, axis=1))
                nxt[i + d] = jnp.where(low, pltpu.roll(vs[i], LANES - d * S5_GROUP, axis=1), vs[i + d])
        vs = nxt
        d //= 2
    return vs


def _s5_group_kernel(u_ref, o_ref):
    def body(r, carry):
        rows = pl.ds(pl.multiple_of(r * S5_REGROUP_ROWS, S5_REGROUP_ROWS), S5_REGROUP_ROWS)
        for hf in range(S5_CHUNK // S5_PACKETS):
            base = hf * S5_PACKETS
            vs = [pltpu.bitcast(u_ref[rows, (base + t) * LANES:(base + t + 1) * LANES].astype(o_ref.dtype),
                                jnp.uint32) for t in range(S5_PACKETS)]
            for gq, v in enumerate(_packet_transpose(vs)):
                o_ref[gq, rows, hf * LANES:(hf + 1) * LANES] = pltpu.bitcast(v, o_ref.dtype)
        return carry

    lax.fori_loop(0, u_ref.shape[0] // S5_REGROUP_ROWS, body, 0, unroll=4)


def _s5_group(u):
    g8, rows, wide = u.shape
    tr = _row_tile(rows, ROW_TILE)
    w = wide // S5_PACKETS
    return pl.pallas_call(
        _s5_group_kernel,
        out_shape=jax.ShapeDtypeStruct((g8 * S5_PACKETS, rows, w), BF16),
        grid=(g8, rows // tr),
        in_specs=[pl.BlockSpec((None, tr, wide), lambda g, i: (g, i, 0))],
        out_specs=pl.BlockSpec((S5_PACKETS, tr, w), lambda g, i: (g, i, 0)),
        compiler_params=_cparams("parallel", "parallel"),
        name="s5_group",
    )(u)


def _s5_ungroup_kernel(y_ref, u_ref, ds_ref, o_ref):
    def body(r, carry):
        rows = pl.ds(pl.multiple_of(r * S5_REGROUP_ROWS, S5_REGROUP_ROWS), S5_REGROUP_ROWS)
        for hf in range(S5_CHUNK // S5_PACKETS):
            base = hf * S5_PACKETS
            vs = [pltpu.bitcast(y_ref[gq, rows, hf * LANES:(hf + 1) * LANES], jnp.uint32)
                  for gq in range(S5_PACKETS)]
            for t, v in enumerate(_packet_transpose(vs)):
                cols = slice((base + t) * LANES, (base + t + 1) * LANES)
                y = pltpu.bitcast(v, F32)
                o_ref[rows, cols] = jax.nn.gelu(y + ds_ref[...] * u_ref[rows, cols]).astype(o_ref.dtype)
        return carry

    lax.fori_loop(0, o_ref.shape[0] // S5_REGROUP_ROWS, body, 0, unroll=2)


def _s5_ungroup(y, u, d_skip):
    ng, rows, w = y.shape
    tr = _row_tile(rows, ROW_TILE)
    wide = w * S5_PACKETS
    wide_spec = pl.BlockSpec((None, tr, wide), lambda g, i: (g, i, 0))
    return pl.pallas_call(
        _s5_ungroup_kernel,
        out_shape=jax.ShapeDtypeStruct((ng // S5_PACKETS, rows, wide), BF16),
        grid=(ng // S5_PACKETS, rows // tr),
        in_specs=[pl.BlockSpec((S5_PACKETS, tr, w), lambda g, i: (g, i, 0)), wide_spec,
                  pl.BlockSpec((None, 1, LANES), lambda g, i: (g, 0, 0))],
        out_specs=wide_spec,
        compiler_params=_cparams("parallel", "parallel"),
        name="s5_ungroup",
    )(y, u, d_skip)


def _s5_kernel(uc_ref, ul_ref, kl_ref, q_ref, p_ref, a_ref, yc_ref, yl_ref,
               sc_ref, sl_ref, xc_ref, xl_ref, t_ref, *, bsz):
    half = sc_ref.shape[2] // 2
    nch = kl_ref.shape[1]
    for ti in range(S5_CHUNK):
        for to in range(S5_CHUNK):
            t_ref[ti * nch:(ti + 1) * nch, to * nch:(to + 1) * nch] = kl_ref[to - ti + S5_CHUNK - 1]
    for u_ref, s_ref in ((uc_ref, sc_ref), (ul_ref, sl_ref)):
        for d in range(2):
            s_ref[d] = jnp.dot(u_ref[...], q_ref[d], preferred_element_type=F32)
    aa = [a_ref[d, 0:1, :] for d in range(2)]
    ab = [a_ref[d, 1:2, :] for d in range(2)]

    def run(s_ref, x_ref, carry):
        n = s_ref.shape[1] // bsz

        def step(d, j, w):
            rows = pl.ds(pl.multiple_of(j * bsz, bsz), bsz)
            x_ref[d, rows, :] = w[:, :half]
            other = jnp.concatenate([w[:, half:], w[:, :half]], axis=1)
            return aa[d] * w + ab[d] * other + s_ref[d, rows, :]

        def body(i, ws):
            return step(0, i, ws[0]), step(1, n - 1 - i, ws[1])

        return lax.fori_loop(0, n, body, carry)

    zero = jnp.zeros((bsz, 2 * half), F32)
    run(sl_ref, xl_ref, run(sc_ref, xc_ref, (zero, zero)))
    t_m = t_ref[...].astype(BF16)
    for u_ref, x_ref, y_ref in ((uc_ref, xc_ref, yc_ref), (ul_ref, xl_ref, yl_ref)):
        y = jnp.dot(u_ref[...], t_m, preferred_element_type=F32)
        for d in range(2):
            y = y + lax.dot_general(x_ref[d].astype(BF16), p_ref[d], (((1,), (1,)), ((), ())),
                                    preferred_element_type=F32)
        y_ref[...] = y


def _s5_scan(uc, ul, k_lag, q_m, p_m, a_m, bsz):
    ng, rc, w = uc.shape
    rl = ul.shape[1]
    st2 = p_m.shape[3]
    nlag, nch = k_lag.shape[1:3]

    def rows_spec(r):
        return pl.BlockSpec((None, r, w), lambda g: (g, 0, 0))

    return pl.pallas_call(
        functools.partial(_s5_kernel, bsz=bsz),
        out_shape=(jax.ShapeDtypeStruct((ng, rc, w), F32), jax.ShapeDtypeStruct((ng, rl, w), F32)),
        grid=(ng,),
        in_specs=[rows_spec(rc), rows_spec(rl),
                  pl.BlockSpec((None, nlag, nch, nch), lambda g: (g, 0, 0, 0)),
                  pl.BlockSpec((2, None, w, 2 * st2), lambda g: (0, g, 0, 0)),
                  pl.BlockSpec((2, None, w, st2), lambda g: (0, g, 0, 0)),
                  pl.BlockSpec((2, None, 2, 2 * st2), lambda g: (0, g, 0, 0))],
        out_specs=(rows_spec(rc), rows_spec(rl)),
        scratch_shapes=[pltpu.VMEM((2, rc, 2 * st2), F32), pltpu.VMEM((2, rl, 2 * st2), F32),
                        pltpu.VMEM((2, rc, st2), F32), pltpu.VMEM((2, rl, st2), F32),
                        pltpu.VMEM((w, w), F32)],
        compiler_params=_cparams("parallel"),
        name="s5_scan",
    )(uc, ul, k_lag, q_m, p_m, a_m)


def _s5_operators(lam_re, lam_im, log_dt, b_re, b_im, c_re, c_im):
    hi = lax.Precision.HIGHEST
    lc = S5_CHUNK
    dt = jnp.exp(log_dt)[..., None]
    mag = jnp.exp(lam_re * dt)
    lb_re, lb_im = mag * jnp.cos(lam_im * dt), mag * jnp.sin(lam_im * dt)
    den = lam_re * lam_re + lam_im * lam_im
    nr, ni = lb_re - 1.0, lb_im
    fr = (nr * lam_re + ni * lam_im) / den
    fi = (ni * lam_re - nr * lam_im) / den
    bb_re = fr[..., None] * b_re - fi[..., None] * b_im
    bb_im = fr[..., None] * b_im + fi[..., None] * b_re
    k = jnp.arange(lc + 1, dtype=F32)[:, None, None, None]
    pmag = jnp.exp(k * (lam_re * dt))
    pw_re, pw_im = pmag * jnp.cos(k * (lam_im * dt)), pmag * jnp.sin(k * (lam_im * dt))
    lbb_re = pw_re[..., None] * bb_re - pw_im[..., None] * bb_im
    lbb_im = pw_re[..., None] * bb_im + pw_im[..., None] * bb_re
    kern = (jnp.einsum('dgop,kdgpc->kdgco', c_re, lbb_re, precision=hi)
            - jnp.einsum('dgop,kdgpc->kdgco', c_im, lbb_im, precision=hi))
    ngrp, nch = lam_re.shape[1], b_re.shape[3]
    nst = lam_re.shape[2]

    k_lag = jnp.concatenate([jnp.flip(kern[1:lc, 1], axis=0), (kern[0, 0] + kern[0, 1])[None],
                             kern[1:lc, 0]], axis=0)
    k_lag = jnp.transpose(k_lag, (1, 0, 2, 3))

    pw_t = (jnp.transpose(pw_re, (1, 2, 0, 3)), jnp.transpose(pw_im, (1, 2, 0, 3)))
    bb_t = (jnp.transpose(bb_re, (0, 1, 3, 2)), jnp.transpose(bb_im, (0, 1, 3, 2)))

    def per_dir(d):
        e_in = (lc - 1 - jnp.arange(lc)) if d == 0 else jnp.arange(lc)
        l_re, l_im = (pw[d][:, e_in][:, :, None, :] for pw in pw_t)
        bt_re, bt_im = (bb[d][:, None] for bb in bb_t)
        q_re = (l_re * bt_re - l_im * bt_im).reshape(ngrp, lc * nch, nst)
        q_im = (l_re * bt_im + l_im * bt_re).reshape(ngrp, lc * nch, nst)
        q_m = jnp.concatenate([q_re, q_im, q_im, q_re], axis=-1)
        e_out = (jnp.arange(lc) + 1) if d == 0 else (lc - jnp.arange(lc))
        l_re, l_im = (pw[d][:, e_out][:, :, None, :] for pw in pw_t)
        cd_re, cd_im = c_re[d][:, None], c_im[d][:, None]
        p_re = (cd_re * l_re - cd_im * l_im).reshape(ngrp, lc * nch, nst)
        p_im = (cd_re * l_im + cd_im * l_re).reshape(ngrp, lc * nch, nst)
        p_m = jnp.concatenate([p_re, -p_im], axis=-1)
        a_re, a_im = pw_re[lc, d], pw_im[lc, d]
        a_m = jnp.stack([jnp.concatenate([a_re] * 4, axis=-1),
                         jnp.concatenate([-a_im, a_im, a_im, -a_im], axis=-1)], axis=1)
        return q_m, p_m, a_m

    q_m, p_m, a_m = (jnp.stack([x, y]) for x, y in zip(per_dir(0), per_dir(1)))
    return k_lag, q_m.astype(BF16), p_m.astype(BF16), a_m


def _s5_glu_kernel(zf_ref, wa_ref, wg_ref, h_ref, gate_ref, o_ref, z_ref):
    n = pl.program_id(2)

    @pl.when(n == 0)
    def _():
        tm = z_ref.shape[0]
        for k in range(zf_ref.shape[0]):
            z_ref[:, k * LANES:(k + 1) * LANES] = zf_ref[k].reshape(tm, LANES)

    z = z_ref[...]
    a = jnp.dot(z, wa_ref[...], preferred_element_type=F32)
    gt = jnp.dot(z, wg_ref[...], preferred_element_type=F32)
    o_ref[...] = h_ref[...] + gate_ref[...] * (a * jax.nn.sigmoid(gt))


def _s5_glu(z, w, h, mod, k, shared_mod):
    bsz, rows, d = h.shape
    tm = _row_tile(rows, ROW_TILE_STREAM)
    tn = _col_tile(d, 512)
    nn = d // tn
    lc = S5_CHUNK
    col = pl.BlockSpec((None, tm, tn), lambda b, i, n: (b, i, n))
    return pl.pallas_call(
        _s5_glu_kernel,
        out_shape=jax.ShapeDtypeStruct(h.shape, F32),
        grid=(bsz, rows // tm, nn),
        in_specs=[pl.BlockSpec((d // LANES, tm // lc, None, lc, LANES), lambda b, i, n: (0, i, b, 0, 0)),
                  pl.BlockSpec((d, tn), lambda b, i, n: (0, n)),
                  pl.BlockSpec((d, tn), lambda b, i, n: (0, n + nn)),
                  col, pl.BlockSpec((None, None, 1, tn),
                                    lambda b, i, n: (0 if shared_mod else b, k, 0, n))],
        out_specs=col,
        scratch_shapes=[pltpu.VMEM((tm, d), BF16)],
        compiler_params=_cparams("parallel", "parallel", "arbitrary"),
        name="s5_glu",
    )(z, w, w, h, mod)


def _s5_mixer(h, hc, g, mod, mod_c, w_in, lam_re, lam_im, log_dt, b_re, b_im, c_re, c_im,
              d_skip, w_out, with_ctx):
    bsz, n, d = h.shape
    lctx = hc.shape[1] // bsz
    lc = S5_CHUNK
    g8 = d // LANES
    w = w_in.astype(BF16)
    hc3 = hc.reshape(bsz, lctx, d)
    u_l = _s5_proj(h, g, mod, 3, w, False).reshape(g8, n // lc * bsz, lc * LANES)
    u_c = _s5_proj(hc3, g, mod_c, 3, w, True).reshape(g8, lctx // lc * bsz, lc * LANES)
    ops = _s5_operators(lam_re, lam_im, log_dt, b_re, b_im, c_re, c_im)
    yp_c, yp_l = _s5_scan(_s5_group(u_c), _s5_group(u_l), *ops, bsz=bsz)
    ds = d_skip.astype(F32).reshape(g8, 1, LANES)
    w_o = w_out.astype(BF16)
    z_l = _s5_ungroup(yp_l, u_l, ds).reshape(g8, n // lc, bsz, lc, LANES)
    h = _s5_glu(z_l, w_o, h, mod, 5, False)
    if with_ctx:
        z_c = _s5_ungroup(yp_c, u_c, ds).reshape(g8, lctx // lc, bsz, lc, LANES)
        hc = _s5_glu(z_c, w_o, hc3, mod_c, 5, True).reshape(1, bsz * lctx, d)
    return h, hc


def _rmsnorm_kernel(x_ref, g_ref, o_ref):
    x = x_ref[...]
    o_ref[...] = x * lax.rsqrt(jnp.mean(x * x, axis=-1, keepdims=True) + EPS) * g_ref[...]


def _final_norm(x, g):
    bsz, rows, d = x.shape
    tm = _row_tile(rows, ROW_TILE)
    row = pl.BlockSpec((None, tm, d), lambda b, i: (b, i, 0))
    return pl.pallas_call(
        _rmsnorm_kernel,
        out_shape=jax.ShapeDtypeStruct(x.shape, F32),
        grid=(bsz, rows // tm),
        in_specs=[row, pl.BlockSpec((1, d), lambda b, i: (0, 0))],
        out_specs=row,
        compiler_params=_cparams("parallel", "parallel"),
        name="final_norm",
    )(x, g.reshape(1, d))


def kernel(x, c, ctx, c_ctx, w_ada, b_ada, g_norm, w_ffn_in, w_ffn_out, g_final, ml_w_in, ml_b_gate, ml_g_head, ml_w_out, wa_w_in, wa_sink, wa_w_out, s5_w_in, s5_lam_re, s5_lam_im, s5_log_dt, s5_b_re, s5_b_im, s5_c_re, s5_c_im, s5_d_skip, s5_w_out):
    bsz, n, d = x.shape
    depth = w_ada.shape[0]
    lctx = ctx.shape[1]
    n_rows = -(-(bsz + 1) // 16) * 16
    c_rows = jnp.concatenate([c, c_ctx[None], jnp.zeros((n_rows - bsz - 1, d), F32)], axis=0)
    mods = _mod_table(c_rows, w_ada, b_ada).reshape(depth, n_rows, N_MOD, 1, d)
    h = x
    hc = ctx.reshape(1, bsz * lctx, d)
    for layer in range(depth):
        has_next = layer < depth - 1
        mod, mod_c = mods[layer, :bsz], mods[layer, bsz:bsz + 1]
        g = g_norm[layer]
        w_in = w_ffn_in[layer].astype(BF16)
        w_out = w_ffn_out[layer].astype(BF16)
        h = _ffn(h, g[0], mod, 0, w_in[0], w_out[0])
        hc = _ffn(hc, g[0], mod_c, 0, w_in[0], w_out[0])
        kind, idx = layer % 3, layer // 3
        if kind == 0:
            h, hc = _mlstm_mixer(h, hc, g[1], mod, mod_c, ml_w_in[idx], ml_b_gate[idx],
                                 ml_g_head[idx], ml_w_out[idx], has_next)
        elif kind == 1:
            h, hc = _wa_mixer(h, hc, g[1], mod, mod_c, wa_w_in[idx], wa_sink[idx],
                              wa_w_out[idx], has_next)
        else:
            h, hc = _s5_mixer(h, hc, g[1], mod, mod_c, s5_w_in[idx], s5_lam_re[idx], s5_lam_im[idx],
                              s5_log_dt[idx], s5_b_re[idx], s5_b_im[idx], s5_c_re[idx],
                              s5_c_im[idx], s5_d_skip[idx], s5_w_out[idx], has_next)
        h = _ffn(h, g[2], mod, 6, w_in[1], w_out[1])
        if has_next:
            hc = _ffn(hc, g[2], mod_c, 6, w_in[1], w_out[1])
    return _final_norm(h, g_final)
```

```python
import functools
import math

import jax
import jax.numpy as jnp
from jax import lax
from jax.experimental import pallas as pl
from jax.experimental.pallas import tpu as pltpu

F32 = jnp.float32
BF16 = jnp.bfloat16

EPS = 1e-6
NEG_BIG = -1e30
LOG2E = math.log2(math.e)
N_MOD = 9
ML_HEADS = 8
ML_CHUNK = 512
ML_MERGE_ROWS = 256
WA_Q_HEADS = 16
WA_KV_HEADS = 4
WA_WINDOW = 128
WA_BLOCK = 128
GRID_W = 64
ROPE_BASE = 10000.0
S5_GROUP = 16
S5_STATE = 64
S5_CHUNK = 16

LANES = 128
ROW_TILE = 512
ROW_TILE_STREAM = 1024
CAST_BLOCK_BYTES = 8 * 1024 * 1024
VMEM_LIMIT = 56 * 1024 * 1024


def _cparams(*sem):
    return pltpu.CompilerParams(dimension_semantics=sem, vmem_limit_bytes=VMEM_LIMIT)


def _row_tile(rows, want):
    return want if rows % want == 0 else rows


def _col_tile(cols, want):
    t = min(want, cols)
    while cols % t:
        t -= LANES
    return t


def _adaln(x, g, shift, scale):
    var = jnp.mean(x * x, axis=-1, keepdims=True)
    return (x * lax.rsqrt(var + EPS) * g) * (1.0 + scale) + shift


ADALN_ROWS = 128


def _adaln_rows(hn_ref, x_ref, g_ref, sh_ref, sc_ref):
    rows = x_ref.shape[0]
    step = ADALN_ROWS if rows % ADALN_ROWS == 0 else rows

    def body(r, carry):
        sl = pl.ds(pl.multiple_of(r * step, step), step)
        hn_ref[sl, :] = _adaln(x_ref[sl, :], g_ref[...], sh_ref[...], sc_ref[...]).astype(BF16)
        return carry

    lax.fori_loop(0, rows // step, body, 0)


def _mod_spec(k, d, shared=False):
    return pl.BlockSpec((None, None, 1, d), lambda b, *_: (0 if shared else b, k, 0, 0))


def _mod_kernel(c_ref, w_ref, b_ref, o_ref):
    c = c_ref[...]
    s = (c * jax.nn.sigmoid(c)).astype(BF16)
    o_ref[...] = jnp.dot(s, w_ref[...].astype(BF16), preferred_element_type=F32) + b_ref[...]


def _mod_table(c_rows, w_ada, b_ada):
    depth, d, nd = w_ada.shape
    r = c_rows.shape[0]
    tn = _col_tile(nd, 1024)
    return pl.pallas_call(
        _mod_kernel,
        out_shape=jax.ShapeDtypeStruct((depth, r, nd), F32),
        grid=(depth, nd // tn),
        in_specs=[pl.BlockSpec((r, d), lambda l, n: (0, 0)),
                  pl.BlockSpec((None, d, tn), lambda l, n: (l, 0, n)),
                  pl.BlockSpec((None, 1, tn), lambda l, n: (l, 0, n))],
        out_specs=pl.BlockSpec((None, r, tn), lambda l, n: (l, 0, n)),
        compiler_params=_cparams("parallel", "parallel"),
        name="mod_table",
    )(c_rows, w_ada, b_ada.reshape(depth, 1, nd))


def _ffn_kernel(xn_ref, g_ref, sh_ref, sc_ref, x_ref, gate_ref, wa_ref, wg_ref, wo_ref, o_ref,
                hn_ref, act_ref, *, nf):
    j = pl.program_id(2)
    last = pl.num_programs(2) - 1
    tf = wa_ref.shape[1]

    @pl.when((j == 0) & (pl.program_id(0) == 0) & (pl.program_id(1) == 0))
    def _():
        _adaln_rows(hn_ref, xn_ref, g_ref, sh_ref, sc_ref)

    @pl.when(j < nf)
    def _():
        hn = hn_ref[...]
        a = jnp.dot(hn, wa_ref[...], preferred_element_type=F32)
        gt = jnp.dot(hn, wg_ref[...], preferred_element_type=F32)
        cols = pl.ds(pl.multiple_of(j * tf, tf), tf)
        act_ref[:, cols] = (a * (gt * jax.nn.sigmoid(gt))).astype(BF16)

    def drain():
        y = jnp.dot(act_ref[...], wo_ref[...], preferred_element_type=F32)
        o_ref[...] = x_ref[...] + (0.5 * gate_ref[...]) * y

    @pl.when((j >= nf) & (j < last))
    def _():
        drain()

    @pl.when(j == last)
    def _():
        drain()
        step = ADALN_ROWS if xn_ref.shape[0] % ADALN_ROWS == 0 else xn_ref.shape[0]
        for r in range(xn_ref.shape[0] // step):
            rows = slice(r * step, (r + 1) * step)
            hn_ref[rows, :] = _adaln(xn_ref[rows, :], g_ref[...], sh_ref[...], sc_ref[...]).astype(BF16)


def _cast_kernel(x_ref, o_ref):
    o_ref[...] = x_ref[...].astype(o_ref.dtype)


def _to_bf16(w):
    shape = w.shape
    cols = shape[-1]
    w2 = w.reshape(-1, cols)
    tr = 1 << ((CAST_BLOCK_BYTES // (4 * cols)).bit_length() - 1)
    while w2.shape[0] % tr:
        tr //= 2
    blk = pl.BlockSpec((tr, cols), lambda i: (i, 0))
    out = pl.pallas_call(
        _cast_kernel,
        out_shape=jax.ShapeDtypeStruct(w2.shape, BF16),
        grid=(w2.shape[0] // tr,),
        in_specs=[blk],
        out_specs=blk,
        compiler_params=_cparams("parallel"),
        name="to_bf16",
    )(w2)
    return out.reshape(shape)


def _ffn(x, g, mod, k0, w_in, w_out, layer, half):
    bsz, rows, d = x.shape
    ff = w_out.shape[2]
    tm = _row_tile(rows, ROW_TILE_STREAM)
    tf = _col_tile(ff, 512)
    tn = _col_tile(d, 256)
    nf = ff // tf
    nt = rows // tm
    steps = nf + d // tn

    def fill(j):
        return jnp.minimum(j, nf - 1)

    def drain(j):
        return jnp.maximum(j - nf, 0)

    def ahead(b, i, j):
        r = jnp.minimum(b * nt + i + (j == steps - 1).astype(jnp.int32), bsz * nt - 1)
        return r // nt, r % nt

    def mod_ahead(k):
        return pl.BlockSpec((None, None, 1, d), lambda b, i, j: (ahead(b, i, j)[0], k, 0, 0))

    return pl.pallas_call(
        functools.partial(_ffn_kernel, nf=nf),
        out_shape=jax.ShapeDtypeStruct(x.shape, F32),
        grid=(bsz, nt, steps),
        in_specs=[pl.BlockSpec((None, tm, d), lambda b, i, j: (*ahead(b, i, j), 0)),
                  pl.BlockSpec((1, d), lambda b, i, j: (0, 0)),
                  mod_ahead(k0), mod_ahead(k0 + 1),
                  pl.BlockSpec((None, tm, tn), lambda b, i, j: (b, i, drain(j))),
                  pl.BlockSpec((None, None, 1, tn), lambda b, i, j: (b, k0 + 2, 0, drain(j))),
                  pl.BlockSpec((None, None, d, tf), lambda b, i, j: (layer, half, 0, fill(j))),
                  pl.BlockSpec((None, None, d, tf), lambda b, i, j: (layer, half, 0, fill(j) + nf)),
                  pl.BlockSpec((None, None, ff, tn), lambda b, i, j: (layer, half, 0, drain(j)))],
        out_specs=pl.BlockSpec((None, tm, tn), lambda b, i, j: (b, i, drain(j))),
        scratch_shapes=[pltpu.VMEM((tm, d), BF16), pltpu.VMEM((tm, ff), BF16)],
        compiler_params=_cparams("arbitrary", "arbitrary", "arbitrary"),
        name="ffn",
    )(x, g.reshape(1, d), mod, mod, x, mod, w_in, w_in, w_out)


def _proj_kernel(*refs, n_rope, with_gates, n_scaled, scale):
    x_ref, g_ref, sh_ref, sc_ref, w_ref = refs[:5]
    rest = refs[5:]
    if n_rope:
        cs_ref, sn_ref = rest[:2]
        rest = rest[2:]
    if with_gates:
        wg_ref, bg_ref, o_ref, og_ref, hn_ref = rest
    else:
        o_ref, hn_ref = rest
    n = pl.program_id(2)

    @pl.when(n == 0)
    def _():
        _adaln_rows(hn_ref, x_ref, g_ref, sh_ref, sc_ref)
        if with_gates:
            og_ref[...] = jnp.dot(hn_ref[...], wg_ref[...], preferred_element_type=F32) + bg_ref[...]

    acc = jnp.dot(hn_ref[...], w_ref[...], preferred_element_type=F32)
    if n_scaled:
        acc = acc * jnp.where(n < n_scaled, scale, 1.0)
    if not n_rope:
        o_ref[...] = acc.astype(o_ref.dtype)
    else:
        @pl.when(n >= n_rope)
        def _():
            o_ref[...] = acc.astype(o_ref.dtype)

        @pl.when(n < n_rope)
        def _():
            cs, sn = cs_ref[...], sn_ref[...]
            hd = cs.shape[1]
            for h in range(acc.shape[1] // hd):
                r = acc[:, h * hd:(h + 1) * hd]
                r = r * cs + pltpu.roll(r, hd // 2, axis=1) * sn
                o_ref[:, h * hd:(h + 1) * hd] = r.astype(o_ref.dtype)


def _proj(x, g, mod, k0, w, out_dtype, rope=None, rope_cols=0, gates=None, scale=1.0, scale_cols=0):
    bsz, rows, d = x.shape
    nout = w.shape[1]
    tm = _row_tile(rows, ROW_TILE_STREAM)
    tn = _col_tile(math.gcd(math.gcd(nout, rope_cols), scale_cols), 1024)
    n_rope = rope_cols // tn
    row = pl.BlockSpec((None, tm, d), lambda b, i, n: (b, i, 0))
    in_specs = [row, pl.BlockSpec((1, d), lambda b, i, n: (0, 0)),
                _mod_spec(k0, d), _mod_spec(k0 + 1, d),
                pl.BlockSpec((d, tn), lambda b, i, n: (0, n))]
    args = [x, g.reshape(1, d), mod, mod, w]
    if n_rope:
        cs, sn = rope
        hd = cs.shape[1]
        in_specs += [pl.BlockSpec((tm, hd), lambda b, i, n: (i, 0))] * 2
        args += [cs, sn]
    out_shape = jax.ShapeDtypeStruct((bsz, rows, nout), out_dtype)
    out_specs = pl.BlockSpec((None, tm, tn), lambda b, i, n: (b, i, n))
    if gates is not None:
        wg, bg = gates
        ng = wg.shape[1]
        in_specs += [pl.BlockSpec((d, ng), lambda b, i, n: (0, 0)),
                     pl.BlockSpec((1, ng), lambda b, i, n: (0, 0))]
        args += [wg, bg]
        out_shape = (out_shape, jax.ShapeDtypeStruct((bsz, rows, ng), F32))
        out_specs = (out_specs, pl.BlockSpec((None, tm, ng), lambda b, i, n: (b, i, 0)))
    return pl.pallas_call(
        functools.partial(_proj_kernel, n_rope=n_rope, with_gates=gates is not None,
                          n_scaled=scale_cols // tn, scale=scale),
        out_shape=out_shape,
        grid=(bsz, rows // tm, nout // tn),
        in_specs=in_specs,
        out_specs=out_specs,
        scratch_shapes=[pltpu.VMEM((tm, d), BF16)],
        compiler_params=_cparams("parallel", "parallel", "arbitrary"),
        name="proj",
    )(*args)


def _out_proj_kernel(y_ref, w_ref, h_ref, gate_ref, o_ref):
    acc = jnp.dot(y_ref[...], w_ref[...], preferred_element_type=F32)
    o_ref[...] = h_ref[...] + gate_ref[...] * acc


def _out_proj(y, w, h, mod, k):
    bsz, rows, d = h.shape
    dk = y.shape[2]
    tm = _row_tile(rows, ROW_TILE)
    return pl.pallas_call(
        _out_proj_kernel,
        out_shape=jax.ShapeDtypeStruct(h.shape, F32),
        grid=(bsz, rows // tm),
        in_specs=[pl.BlockSpec((None, tm, dk), lambda b, i: (b, i, 0)),
                  pl.BlockSpec((dk, d), lambda b, i: (0, 0)),
                  pl.BlockSpec((None, tm, d), lambda b, i: (b, i, 0)),
                  _mod_spec(k, d)],
        out_specs=pl.BlockSpec((None, tm, d), lambda b, i: (b, i, 0)),
        compiler_params=_cparams("parallel", "parallel"),
        name="out_proj",
    )(y, w, h, mod)


def _ml_chunk_len(length):
    return min(ML_CHUNK, length)


ML_GATE_SLOTS = 8


def _ml_gate_kernel(g_ref, o_ref, *, lc):
    row = lax.broadcasted_iota(jnp.int32, (lc, lc), 0)
    col = lax.broadcasted_iota(jnp.int32, (lc, lc), 1)
    lower = (col <= row).astype(BF16)
    upper = (col >= row).astype(BF16)
    slot = lax.broadcasted_iota(jnp.int32, (lc, LANES), 1) & (ML_GATE_SLOTS - 1)
    rix = lax.broadcasted_iota(jnp.int32, (lc, LANES), 0)

    def cumsum(tri, parts):
        return sum(jnp.dot(tri, p, preferred_element_type=F32) for p in parts)

    for c in range(g_ref.shape[0] // lc):
        x = g_ref[c * lc:(c + 1) * lc, :]
        ls = jax.nn.log_sigmoid(x)
        hi = ls.astype(BF16)
        r1 = ls - hi.astype(F32)
        mid = r1.astype(BF16)
        lo = (r1 - mid.astype(F32)).astype(BF16)
        pre = cumsum(lower, (hi, mid, lo))
        suf = cumsum(upper, (hi, mid, lo))
        a_f = x - pltpu.roll(pre, LANES - 1, axis=1)
        a_b = x - pltpu.roll(suf, LANES - 1, axis=1)
        cm_f, cm_b = a_f, a_b
        sh = 1
        while sh < lc:
            cm_f = jnp.maximum(cm_f, jnp.where(rix >= sh, pltpu.roll(cm_f, sh, axis=0), NEG_BIG))
            cm_b = jnp.maximum(cm_b, jnp.where(rix < lc - sh, pltpu.roll(cm_b, lc - sh, axis=0), NEG_BIG))
            sh *= 2
        out = jnp.where(slot == 0, a_f,
              jnp.where(slot == 1, pre,
              jnp.where(slot == 2, a_b,
              jnp.where(slot == 3, suf,
              jnp.where(slot == 4, pltpu.roll(cm_f, 4, axis=1), pltpu.roll(cm_b, 3, axis=1))))))
        o_ref[:, c * lc:(c + 1) * lc] = jnp.transpose(out)


def _ml_gates(g):
    bsz, rows, w = g.shape
    tm = _row_tile(rows, ROW_TILE_STREAM)
    return pl.pallas_call(
        functools.partial(_ml_gate_kernel, lc=_ml_chunk_len(rows)),
        out_shape=jax.ShapeDtypeStruct((bsz, w, rows), F32),
        grid=(bsz, rows // tm),
        in_specs=[pl.BlockSpec((None, tm, w), lambda b, i: (b, i, 0))],
        out_specs=pl.BlockSpec((None, w, tm), lambda b, i: (b, 0, i)),
        compiler_params=_cparams("parallel", "parallel"),
        name="ml_gates",
    )(g)


def _ml_chunk(q, k, v_aug, a_col, b_col, cm_col, a_row, c_mem, m_run, tri, last):
    dd = range(2)
    hd = q[0].shape[1]
    nt = (((1,), (1,)), ((), ()))
    tn = (((0,), (0,)), ((), ()))
    b_end = [b_col[d][last[d]:last[d] + 1, :] for d in dd]
    m_new = [b_end[d] + jnp.maximum(m_run[d], cm_col[d][last[d]:last[d] + 1, :]) for d in dd]
    r_col = [jnp.maximum(m_run[d], cm_col[d]) for d in dd]
    s = [lax.dot_general(q[d], k[d], nt, preferred_element_type=F32) for d in dd]
    c_bf = [c_mem[d].astype(BF16) for d in dd]
    qc = [jnp.dot(q[d], c_bf[d], preferred_element_type=F32) for d in dd]
    wk = [k[d] * jnp.exp(a_col[d] + (b_end[d] - m_new[d])).astype(BF16) for d in dd]
    e = [jnp.exp(jnp.where(tri[d], a_row[d] - r_col[d], NEG_BIG)) for d in dd]
    upd = [lax.dot_general(wk[d], v_aug[d], tn, preferred_element_type=F32) for d in dd]
    p = [(s[d] * e[d]).astype(BF16) for d in dd]
    dec = [jnp.exp(b_end[d] + m_run[d] - m_new[d]) for d in dd]
    dq = [jnp.exp(m_run[d] - r_col[d]) for d in dd]
    acc = [jnp.dot(p[d], v_aug[d], preferred_element_type=F32) + dq[d] * qc[d] for d in dd]
    c_new = [dec[d] * c_mem[d] + upd[d] for d in dd]
    lim = [jnp.exp(-(b_col[d] + r_col[d])) for d in dd]
    h = [acc[d][:, :hd] / jnp.maximum(jnp.abs(acc[d][:, hd:hd + 1]), lim[d]) for d in dd]
    return h, c_new, m_new


def _mlstm_kernel(cq, ck, cv, co, lq, lk, lv, lo, cgr, lgr, gh_ref,
                  yc_ref, yl_ref, hf_ref, hb_ref, c_ref, m_ref):
    c_ref[...] = jnp.zeros_like(c_ref)
    m_ref[...] = jnp.full_like(m_ref, NEG_BIG)

    def scan(q_ref, k_ref, v_ref, o_ref, gr_ref, y_ref):
        lc = _ml_chunk_len(q_ref.shape[0])
        nc = q_ref.shape[0] // lc
        rows = lax.broadcasted_iota(jnp.int32, (lc, lc), 0)
        cols = lax.broadcasted_iota(jnp.int32, (lc, lc), 1)
        tri = (cols <= rows, cols >= rows)
        ones_blk = jnp.where(lax.broadcasted_iota(jnp.int32, (lc, LANES), 1) == 0, 1.0, 0.0).astype(BF16)

        def body(c, carry):
            sl = (pl.ds(pl.multiple_of(c * lc, lc), lc), pl.ds(pl.multiple_of((nc - 1 - c) * lc, lc), lc))
            dd = range(2)
            gc = [jnp.transpose(gr_ref[:, sl[d]]) for d in dd]
            h, c_new, m_new = _ml_chunk(
                [q_ref[sl[d], :] for d in dd], [k_ref[sl[d], :] for d in dd],
                [jnp.concatenate([v_ref[sl[d], :], ones_blk], axis=1) for d in dd],
                [gc[d][:, 2 * d:2 * d + 1] for d in dd], [gc[d][:, 2 * d + 1:2 * d + 2] for d in dd],
                [gc[d][:, 4 + d:5 + d] for d in dd], [gr_ref[2 * d:2 * d + 1, sl[d]] for d in dd],
                [c_ref[d] for d in dd], [m_ref[d] for d in dd], tri, (lc - 1, 0))
            for d, out_ref in enumerate((hf_ref, hb_ref)):
                c_ref[d] = c_new[d]
                m_ref[d] = m_new[d]
                out_ref[sl[d], :] = h[d]
            return carry

        lax.fori_loop(0, nc, body, 0)

        lm = ML_MERGE_ROWS

        def merge(c, carry):
            sl = pl.ds(pl.multiple_of(c * lm, lm), lm)
            hs = hf_ref[sl, :] + hb_ref[sl, :]
            hs = hs * lax.rsqrt(jnp.mean(hs * hs, axis=-1, keepdims=True) + EPS) * gh_ref[...]
            y_ref[sl, :] = (jax.nn.sigmoid(o_ref[sl, :].astype(F32)) * hs).astype(y_ref.dtype)
            return carry

        lax.fori_loop(0, q_ref.shape[0] // lm, merge, 0)

    scan(cq, ck, cv, co, cgr, yc_ref)
    scan(lq, lk, lv, lo, lgr, yl_ref)


def _mlstm_scan(zc, zl, gc, gl, g_head):
    bsz, n, d4 = zl.shape
    lctx = zc.shape[1]
    d = d4 // 4
    nh = ML_HEADS
    hd = d // nh
    assert n % _ml_chunk_len(n) == 0 and n % ML_MERGE_ROWS == 0 and lctx % ML_MERGE_ROWS == 0

    def zspec(length, k):
        return pl.BlockSpec((None, length, hd), lambda b, h: (b, 0, k * nh + h))

    def gspec(length):
        return pl.BlockSpec((None, ML_GATE_SLOTS, length), lambda b, h: (b, h, 0))

    lmax = max(n, lctx)
    return pl.pallas_call(
        _mlstm_kernel,
        out_shape=(jax.ShapeDtypeStruct((bsz, lctx, d), BF16),
                   jax.ShapeDtypeStruct((bsz, n, d), BF16)),
        grid=(bsz, nh),
        in_specs=([zspec(lctx, k) for k in range(4)] + [zspec(n, k) for k in range(4)]
                  + [gspec(lctx), gspec(n)]
                  + [pl.BlockSpec((None, 1, hd), lambda b, h: (h, 0, 0))]),
        out_specs=(pl.BlockSpec((None, lctx, hd), lambda b, h: (b, 0, h)),
                   pl.BlockSpec((None, n, hd), lambda b, h: (b, 0, h))),
        scratch_shapes=[pltpu.VMEM((lmax, hd), F32), pltpu.VMEM((lmax, hd), F32),
                        pltpu.VMEM((2, hd, hd + LANES), F32), pltpu.VMEM((2, 1, 1), F32)],
        compiler_params=_cparams("parallel", "parallel"),
        name="mlstm_scan",
    )(zc, zc, zc, zc, zl, zl, zl, zl, gc, gl, g_head.reshape(nh, 1, hd))


def _mlstm_mixer(h, hc, g, mod, mod_c, w_in, b_gate, g_head, w_out, with_ctx):
    bsz, n, d = h.shape
    lctx = hc.shape[1] // bsz
    w_main = w_in[:, :4 * d].astype(BF16)
    nh = ML_HEADS
    ngp = -(-nh * ML_GATE_SLOTS // LANES) * LANES

    def head_major(t):
        t = jnp.swapaxes(t.reshape(t.shape[:-1] + (4, nh)), -1, -2)
        t = jnp.pad(t, [(0, 0)] * (t.ndim - 1) + [(0, ML_GATE_SLOTS - 4)])
        t = t.reshape(t.shape[:-2] + (nh * ML_GATE_SLOTS,))
        return jnp.pad(t, [(0, 0)] * (t.ndim - 1) + [(0, ngp - nh * ML_GATE_SLOTS)])

    w_gate = head_major(w_in[:, 4 * d:]).astype(BF16)
    bias = head_major(b_gate.reshape(1, 4 * nh))
    q_scale = (d // ML_HEADS) ** -0.5
    zl, gl = _proj(h, g, mod, 3, w_main, BF16, gates=(w_gate, bias), scale=q_scale, scale_cols=d)
    zc, gc = _proj(hc, g, mod_c, 3, w_main, BF16, gates=(w_gate, bias), scale=q_scale, scale_cols=d)
    yc, yl = _mlstm_scan(zc.reshape(bsz, lctx, 4 * d), zl,
                         _ml_gates(gc.reshape(bsz, lctx, ngp)), _ml_gates(gl), g_head)
    w_o = w_out.astype(BF16)
    h = _out_proj(yl, w_o, h, mod, 5)
    if with_ctx:
        hc = _out_proj(yc.reshape(1, bsz * lctx, d), w_o, hc, mod_c, 5)
    return h, hc


def _wa_kernel(sink_ref, q_ref, *refs, local, scale):
    if local:
        k_ref, v_ref, kc_ref, vc_ref, o_ref = refs
    else:
        kc_ref, vc_ref, o_ref = refs
    tq = q_ref.shape[0]
    hd = kc_ref.shape[1] // WA_KV_HEADS
    grp = WA_Q_HEADS // WA_KV_HEADS
    if local:
        n = k_ref.shape[0]
        span = 3 * WA_BLOCK
        start = pl.program_id(1) * tq
        ks = pl.multiple_of(jnp.clip(start - WA_BLOCK, 0, n - span), WA_BLOCK)
        q_pos = start + (lax.broadcasted_iota(jnp.int32, (grp * tq, span), 0) & (tq - 1))
        k_pos = ks + lax.broadcasted_iota(jnp.int32, (grp * tq, span), 1)
        ok = jnp.abs(q_pos - k_pos) <= WA_WINDOW
    dims = (((1,), (1,)), ((), ()))
    heads = range(WA_KV_HEADS)
    c2 = scale * LOG2E
    cs = [slice(kv * hd, (kv + 1) * hd) for kv in heads]
    q4 = [jnp.concatenate([q_ref[:, (kv * grp + j) * hd:(kv * grp + j + 1) * hd] for j in range(grp)], axis=0)
          for kv in heads]
    sink = [jnp.concatenate([jnp.full((tq, 1), sink_ref[0, kv * grp + j], F32) for j in range(grp)], axis=0)
            * (1.0 / scale) for kv in heads]
    s_ctx = [lax.dot_general(q4[kv], kc_ref[:, cs[kv]], dims, preferred_element_type=F32) for kv in heads]
    m = [jnp.maximum(jnp.max(s_ctx[kv], axis=1, keepdims=True), sink[kv]) for kv in heads]
    if local:
        s_loc = [jnp.where(ok, lax.dot_general(q4[kv], k_ref[pl.ds(ks, span), cs[kv]], dims,
                                               preferred_element_type=F32), NEG_BIG) for kv in heads]
        m = [jnp.maximum(m[kv], jnp.max(s_loc[kv], axis=1, keepdims=True)) for kv in heads]
        p_loc = [jnp.exp2((s_loc[kv] - m[kv]) * c2) for kv in heads]
    p_ctx = [jnp.exp2((s_ctx[kv] - m[kv]) * c2) for kv in heads]
    den = [jnp.sum(p_ctx[kv], axis=1, keepdims=True) + jnp.exp2((sink[kv] - m[kv]) * c2) for kv in heads]
    if local:
        den = [den[kv] + jnp.sum(p_loc[kv], axis=1, keepdims=True) for kv in heads]
    inv = [1.0 / den[kv] for kv in heads]
    out = [jnp.dot((p_ctx[kv] * inv[kv]).astype(BF16), vc_ref[:, cs[kv]], preferred_element_type=F32)
           for kv in heads]
    if local:
        out = [out[kv] + jnp.dot((p_loc[kv] * inv[kv]).astype(BF16), v_ref[pl.ds(ks, span), cs[kv]],
                                 preferred_element_type=F32) for kv in heads]
    for kv in heads:
        for j in range(grp):
            hq = kv * grp + j
            o_ref[:, hq * hd:(hq + 1) * hd] = out[kv][j * tq:(j + 1) * tq].astype(o_ref.dtype)


def _wa_attention(sink, zq, zkv, zc, local):
    bsz, lq, _ = zq.shape
    lctx = zc.shape[1]
    hd = zq.shape[2] // (WA_Q_HEADS + 2 * WA_KV_HEADS)
    qd, kd = WA_Q_HEADS * hd, WA_KV_HEADS * hd
    tq = WA_BLOCK
    kblk = qd // kd
    in_specs = [pl.BlockSpec(memory_space=pltpu.SMEM),
                pl.BlockSpec((None, tq, qd), lambda b, i: (b, i, 0))]
    args = [sink.reshape(1, WA_Q_HEADS), zq]
    if local:
        n = zkv.shape[1]
        in_specs += [pl.BlockSpec((None, n, kd), lambda b, i: (b, 0, kblk)),
                     pl.BlockSpec((None, n, kd), lambda b, i: (b, 0, kblk + 1))]
        args += [zkv, zkv]
    in_specs += [pl.BlockSpec((None, lctx, kd), lambda b, i: (b, 0, kblk)),
                 pl.BlockSpec((None, lctx, kd), lambda b, i: (b, 0, kblk + 1))]
    args += [zc, zc]
    return pl.pallas_call(
        functools.partial(_wa_kernel, local=local, scale=hd ** -0.5),
        out_shape=jax.ShapeDtypeStruct((bsz, lq, qd), BF16),
        grid=(bsz, lq // tq),
        in_specs=in_specs,
        out_specs=pl.BlockSpec((None, tq, qd), lambda b, i: (b, i, 0)),
        compiler_params=_cparams("parallel", "parallel"),
        name="wa_attention",
    )(*args)


def _rope_tables(n, hd):
    rows = n // GRID_W
    row = jnp.repeat(jnp.arange(rows, dtype=F32), GRID_W)
    col = jnp.tile(jnp.arange(GRID_W, dtype=F32), rows)
    n_freq = hd // 4
    inv = ROPE_BASE ** (-jnp.arange(n_freq, dtype=F32) / n_freq)
    ang = jnp.concatenate([row[:, None] * inv, col[:, None] * inv], axis=-1)
    cos, sin = jnp.cos(ang), jnp.sin(ang)
    return jnp.concatenate([cos, cos], axis=-1), jnp.concatenate([-sin, sin], axis=-1)


def _wa_mixer(h, hc, g, mod, mod_c, w_in, sink, w_out, with_ctx):
    bsz, n, d = h.shape
    lctx = hc.shape[1] // bsz
    hd = d // WA_Q_HEADS
    w = w_in.astype(BF16)
    rope_cols = (WA_Q_HEADS + WA_KV_HEADS) * hd
    zl = _proj(h, g, mod, 3, w, BF16, rope=_rope_tables(n, hd), rope_cols=rope_cols)
    zc = _proj(hc, g, mod_c, 3, w, BF16).reshape(bsz, lctx, -1)
    sink = sink.astype(F32)
    w_o = w_out.astype(BF16)
    h = _out_proj(_wa_attention(sink, zl, zl, zc, True), w_o, h, mod, 5)
    if with_ctx:
        yc = _wa_attention(sink, zc, None, zc, False)
        hc = _out_proj(yc.reshape(1, bsz * lctx, d), w_o, hc, mod_c, 5)
    return h, hc


def _s5_proj_kernel(x_ref, g_ref, sh_ref, sc_ref, w_ref, of_ref, hn_ref):
    n = pl.program_id(2)

    @pl.when(n == 0)
    def _():
        _adaln_rows(hn_ref, x_ref, g_ref, sh_ref, sc_ref)

    acc = jnp.dot(hn_ref[...], w_ref[...], preferred_element_type=F32)
    tm = acc.shape[0]
    for k in range(acc.shape[1] // LANES):
        of_ref[k] = acc[:, k * LANES:(k + 1) * LANES].reshape(tm // S5_CHUNK, S5_CHUNK, LANES)


def _s5_proj(x, g, mod, k0, w, shared_mod):
    bsz, rows, d = x.shape
    tm = _row_tile(rows, ROW_TILE_STREAM)
    tn = _col_tile(d, 1024)
    lc = S5_CHUNK
    return pl.pallas_call(
        _s5_proj_kernel,
        out_shape=jax.ShapeDtypeStruct((d // LANES, rows // lc, bsz, lc, LANES), F32),
        grid=(bsz, rows // tm, d // tn),
        in_specs=[pl.BlockSpec((None, tm, d), lambda b, i, n: (b, i, 0)),
                  pl.BlockSpec((1, d), lambda b, i, n: (0, 0)),
                  _mod_spec(k0, d, shared_mod), _mod_spec(k0 + 1, d, shared_mod),
                  pl.BlockSpec((d, tn), lambda b, i, n: (0, n))],
        out_specs=pl.BlockSpec((tn // LANES, tm // lc, None, lc, LANES), lambda b, i, n: (n, i, b, 0, 0)),
        scratch_shapes=[pltpu.VMEM((tm, d), BF16)],
        compiler_params=_cparams("parallel", "parallel", "arbitrary"),
        name="s5_proj",
    )(x, g.reshape(1, d), mod, mod, w)


S5_PACKETS = LANES // S5_GROUP
S5_REGROUP_ROWS = 16


def _packet_transpose(vs):
    lane = lax.broadcasted_iota(jnp.int32, vs[0].shape, 1)
    d = S5_PACKETS // 2
    while d:
        low = (lane & (d * S5_GROUP)) == 0
        nxt = list(vs)
        for i in range(S5_PACKETS):
            if not i & d:
                nxt[i] = jnp.where(low, vs[i], pltpu.roll(vs[i + d], d * S5_GROUP, axis=1))
                nxt[i + d] = jnp.where(low, pltpu.roll(vs[i], LANES - d * S5_GROUP, axis=1), vs[i + d])
        vs = nxt
        d //= 2
    return vs


def _s5_group_kernel(u_ref, o_ref):
    def body(r, carry):
        rows = pl.ds(pl.multiple_of(r * S5_REGROUP_ROWS, S5_REGROUP_ROWS), S5_REGROUP_ROWS)
        for hf in range(S5_CHUNK // S5_PACKETS):
            base = hf * S5_PACKETS
            vs = [pltpu.bitcast(u_ref[rows, (base + t) * LANES:(base + t + 1) * LANES].astype(o_ref.dtype),
                                jnp.uint32) for t in range(S5_PACKETS)]
            for gq, v in enumerate(_packet_transpose(vs)):
                o_ref[gq, rows, hf * LANES:(hf + 1) * LANES] = pltpu.bitcast(v, o_ref.dtype)
        return carry

    lax.fori_loop(0, u_ref.shape[0] // S5_REGROUP_ROWS, body, 0, unroll=4)


def _s5_group(u):
    g8, rows, wide = u.shape
    tr = _row_tile(rows, ROW_TILE)
    w = wide // S5_PACKETS
    return pl.pallas_call(
        _s5_group_kernel,
        out_shape=jax.ShapeDtypeStruct((g8 * S5_PACKETS, rows, w), BF16),
        grid=(g8, rows // tr),
        in_specs=[pl.BlockSpec((None, tr, wide), lambda g, i: (g, i, 0))],
        out_specs=pl.BlockSpec((S5_PACKETS, tr, w), lambda g, i: (g, i, 0)),
        compiler_params=_cparams("parallel", "parallel"),
        name="s5_group",
    )(u)


def _s5_ungroup_kernel(y_ref, u_ref, ds_ref, o_ref):
    def body(r, carry):
        rows = pl.ds(pl.multiple_of(r * S5_REGROUP_ROWS, S5_REGROUP_ROWS), S5_REGROUP_ROWS)
        for hf in range(S5_CHUNK // S5_PACKETS):
            base = hf * S5_PACKETS
            vs = [pltpu.bitcast(y_ref[gq, rows, hf * LANES:(hf + 1) * LANES], jnp.uint32)
                  for gq in range(S5_PACKETS)]
            for t, v in enumerate(_packet_transpose(vs)):
                cols = slice((base + t) * LANES, (base + t + 1) * LANES)
                y = pltpu.bitcast(v, F32)
                o_ref[rows, cols] = jax.nn.gelu(y + ds_ref[...] * u_ref[rows, cols]).astype(o_ref.dtype)
        return carry

    lax.fori_loop(0, o_ref.shape[0] // S5_REGROUP_ROWS, body, 0, unroll=2)


def _s5_ungroup(y, u, d_skip):
    ng, rows, w = y.shape
    tr = _row_tile(rows, ROW_TILE)
    wide = w * S5_PACKETS
    wide_spec = pl.BlockSpec((None, tr, wide), lambda g, i: (g, i, 0))
    return pl.pallas_call(
        _s5_ungroup_kernel,
        out_shape=jax.ShapeDtypeStruct((ng // S5_PACKETS, rows, wide), BF16),
        grid=(ng // S5_PACKETS, rows // tr),
        in_specs=[pl.BlockSpec((S5_PACKETS, tr, w), lambda g, i: (g, i, 0)), wide_spec,
                  pl.BlockSpec((None, 1, LANES), lambda g, i: (g, 0, 0))],
        out_specs=wide_spec,
        compiler_params=_cparams("parallel", "parallel"),
        name="s5_ungroup",
    )(y, u, d_skip)


def _s5_kernel(uc_ref, ul_ref, kl_ref, q_ref, p_ref, a_ref, yc_ref, yl_ref,
               sc_ref, sl_ref, xc_ref, xl_ref, t_ref, *, bsz):
    half = sc_ref.shape[2] // 2
    nch = kl_ref.shape[1]
    for ti in range(S5_CHUNK):
        for to in range(S5_CHUNK):
            t_ref[ti * nch:(ti + 1) * nch, to * nch:(to + 1) * nch] = kl_ref[to - ti + S5_CHUNK - 1]
    for u_ref, s_ref in ((uc_ref, sc_ref), (ul_ref, sl_ref)):
        for d in range(2):
            s_ref[d] = jnp.dot(u_ref[...], q_ref[d], preferred_element_type=F32)
    aa = [a_ref[d, 0:1, :] for d in range(2)]
    ab = [a_ref[d, 1:2, :] for d in range(2)]

    def run(s_ref, x_ref, carry):
        n = s_ref.shape[1] // bsz

        def step(d, j, w):
            rows = pl.ds(pl.multiple_of(j * bsz, bsz), bsz)
            x_ref[d, rows, :] = w[:, :half]
            other = jnp.concatenate([w[:, half:], w[:, :half]], axis=1)
            return aa[d] * w + ab[d] * other + s_ref[d, rows, :]

        def body(i, ws):
            return step(0, i, ws[0]), step(1, n - 1 - i, ws[1])

        return lax.fori_loop(0, n, body, carry)

    zero = jnp.zeros((bsz, 2 * half), F32)
    run(sl_ref, xl_ref, run(sc_ref, xc_ref, (zero, zero)))
    t_m = t_ref[...].astype(BF16)
    for u_ref, x_ref, y_ref in ((uc_ref, xc_ref, yc_ref), (ul_ref, xl_ref, yl_ref)):
        y = jnp.dot(u_ref[...], t_m, preferred_element_type=F32)
        for d in range(2):
            y = y + lax.dot_general(x_ref[d].astype(BF16), p_ref[d], (((1,), (1,)), ((), ())),
                                    preferred_element_type=F32)
        y_ref[...] = y


def _s5_scan(uc, ul, k_lag, q_m, p_m, a_m, bsz):
    ng, rc, w = uc.shape
    rl = ul.shape[1]
    st2 = p_m.shape[3]
    nlag, nch = k_lag.shape[1:3]

    def rows_spec(r):
        return pl.BlockSpec((None, r, w), lambda g: (g, 0, 0))

    return pl.pallas_call(
        functools.partial(_s5_kernel, bsz=bsz),
        out_shape=(jax.ShapeDtypeStruct((ng, rc, w), F32), jax.ShapeDtypeStruct((ng, rl, w), F32)),
        grid=(ng,),
        in_specs=[rows_spec(rc), rows_spec(rl),
                  pl.BlockSpec((None, nlag, nch, nch), lambda g: (g, 0, 0, 0)),
                  pl.BlockSpec((2, None, w, 2 * st2), lambda g: (0, g, 0, 0)),
                  pl.BlockSpec((2, None, w, st2), lambda g: (0, g, 0, 0)),
                  pl.BlockSpec((2, None, 2, 2 * st2), lambda g: (0, g, 0, 0))],
        out_specs=(rows_spec(rc), rows_spec(rl)),
        scratch_shapes=[pltpu.VMEM((2, rc, 2 * st2), F32), pltpu.VMEM((2, rl, 2 * st2), F32),
                        pltpu.VMEM((2, rc, st2), F32), pltpu.VMEM((2, rl, st2), F32),
                        pltpu.VMEM((w, w), F32)],
        compiler_params=_cparams("parallel"),
        name="s5_scan",
    )(uc, ul, k_lag, q_m, p_m, a_m)


def _s5_operators(lam_re, lam_im, log_dt, b_re, b_im, c_re, c_im):
    hi = lax.Precision.HIGHEST
    lc = S5_CHUNK
    dt = jnp.exp(log_dt)[..., None]
    mag = jnp.exp(lam_re * dt)
    lb_re, lb_im = mag * jnp.cos(lam_im * dt), mag * jnp.sin(lam_im * dt)
    den = lam_re * lam_re + lam_im * lam_im
    nr, ni = lb_re - 1.0, lb_im
    fr = (nr * lam_re + ni * lam_im) / den
    fi = (ni * lam_re - nr * lam_im) / den
    bb_re = fr[..., None] * b_re - fi[..., None] * b_im
    bb_im = fr[..., None] * b_im + fi[..., None] * b_re
    k = jnp.arange(lc + 1, dtype=F32)[:, None, None, None]
    pmag = jnp.exp(k * (lam_re * dt))
    pw_re, pw_im = pmag * jnp.cos(k * (lam_im * dt)), pmag * jnp.sin(k * (lam_im * dt))
    lbb_re = pw_re[..., None] * bb_re - pw_im[..., None] * bb_im
    lbb_im = pw_re[..., None] * bb_im + pw_im[..., None] * bb_re
    kern = (jnp.einsum('dgop,kdgpc->kdgco', c_re, lbb_re, precision=hi)
            - jnp.einsum('dgop,kdgpc->kdgco', c_im, lbb_im, precision=hi))
    ngrp, nch = lam_re.shape[1], b_re.shape[3]
    nst = lam_re.shape[2]

    k_lag = jnp.concatenate([jnp.flip(kern[1:lc, 1], axis=0), (kern[0, 0] + kern[0, 1])[None],
                             kern[1:lc, 0]], axis=0)
    k_lag = jnp.transpose(k_lag, (1, 0, 2, 3))

    pw_t = (jnp.transpose(pw_re, (1, 2, 0, 3)), jnp.transpose(pw_im, (1, 2, 0, 3)))
    bb_t = (jnp.transpose(bb_re, (0, 1, 3, 2)), jnp.transpose(bb_im, (0, 1, 3, 2)))

    def per_dir(d):
        e_in = (lc - 1 - jnp.arange(lc)) if d == 0 else jnp.arange(lc)
        l_re, l_im = (pw[d][:, e_in][:, :, None, :] for pw in pw_t)
        bt_re, bt_im = (bb[d][:, None] for bb in bb_t)
        q_re = (l_re * bt_re - l_im * bt_im).reshape(ngrp, lc * nch, nst)
        q_im = (l_re * bt_im + l_im * bt_re).reshape(ngrp, lc * nch, nst)
        q_m = jnp.concatenate([q_re, q_im, q_im, q_re], axis=-1)
        e_out = (jnp.arange(lc) + 1) if d == 0 else (lc - jnp.arange(lc))
        l_re, l_im = (pw[d][:, e_out][:, :, None, :] for pw in pw_t)
        cd_re, cd_im = c_re[d][:, None], c_im[d][:, None]
        p_re = (cd_re * l_re - cd_im * l_im).reshape(ngrp, lc * nch, nst)
        p_im = (cd_re * l_im + cd_im * l_re).reshape(ngrp, lc * nch, nst)
        p_m = jnp.concatenate([p_re, -p_im], axis=-1)
        a_re, a_im = pw_re[lc, d], pw_im[lc, d]
        a_m = jnp.stack([jnp.concatenate([a_re] * 4, axis=-1),
                         jnp.concatenate([-a_im, a_im, a_im, -a_im], axis=-1)], axis=1)
        return q_m, p_m, a_m

    q_m, p_m, a_m = (jnp.stack([x, y]) for x, y in zip(per_dir(0), per_dir(1)))
    return k_lag, q_m.astype(BF16), p_m.astype(BF16), a_m


def _s5_glu_kernel(zf_ref, wa_ref, wg_ref, h_ref, gate_ref, o_ref, z_ref):
    n = pl.program_id(2)

    @pl.when(n == 0)
    def _():
        tm = z_ref.shape[0]
        for k in range(zf_ref.shape[0]):
            z_ref[:, k * LANES:(k + 1) * LANES] = zf_ref[k].reshape(tm, LANES)

    z = z_ref[...]
    a = jnp.dot(z, wa_ref[...], preferred_element_type=F32)
    gt = jnp.dot(z, wg_ref[...], preferred_element_type=F32)
    o_ref[...] = h_ref[...] + gate_ref[...] * (a * jax.nn.sigmoid(gt))


def _s5_glu(z, w, h, mod, k, shared_mod):
    bsz, rows, d = h.shape
    tm = _row_tile(rows, ROW_TILE_STREAM)
    tn = _col_tile(d, 512)
    nn = d // tn
    lc = S5_CHUNK
    col = pl.BlockSpec((None, tm, tn), lambda b, i, n: (b, i, n))
    return pl.pallas_call(
        _s5_glu_kernel,
        out_shape=jax.ShapeDtypeStruct(h.shape, F32),
        grid=(bsz, rows // tm, nn),
        in_specs=[pl.BlockSpec((d // LANES, tm // lc, None, lc, LANES), lambda b, i, n: (0, i, b, 0, 0)),
                  pl.BlockSpec((d, tn), lambda b, i, n: (0, n)),
                  pl.BlockSpec((d, tn), lambda b, i, n: (0, n + nn)),
                  col, pl.BlockSpec((None, None, 1, tn),
                                    lambda b, i, n: (0 if shared_mod else b, k, 0, n))],
        out_specs=col,
        scratch_shapes=[pltpu.VMEM((tm, d), BF16)],
        compiler_params=_cparams("parallel", "parallel", "arbitrary"),
        name="s5_glu",
    )(z, w, w, h, mod)


def _s5_mixer(h, hc, g, mod, mod_c, w_in, lam_re, lam_im, log_dt, b_re, b_im, c_re, c_im,
              d_skip, w_out, with_ctx):
    bsz, n, d = h.shape
    lctx = hc.shape[1] // bsz
    lc = S5_CHUNK
    g8 = d // LANES
    w = w_in.astype(BF16)
    hc3 = hc.reshape(bsz, lctx, d)
    u_l = _s5_proj(h, g, mod, 3, w, False).reshape(g8, n // lc * bsz, lc * LANES)
    u_c = _s5_proj(hc3, g, mod_c, 3, w, True).reshape(g8, lctx // lc * bsz, lc * LANES)
    ops = _s5_operators(lam_re, lam_im, log_dt, b_re, b_im, c_re, c_im)
    yp_c, yp_l = _s5_scan(_s5_group(u_c), _s5_group(u_l), *ops, bsz=bsz)
    ds = d_skip.astype(F32).reshape(g8, 1, LANES)
    w_o = w_out.astype(BF16)
    z_l = _s5_ungroup(yp_l, u_l, ds).reshape(g8, n // lc, bsz, lc, LANES)
    h = _s5_glu(z_l, w_o, h, mod, 5, False)
    if with_ctx:
        z_c = _s5_ungroup(yp_c, u_c, ds).reshape(g8, lctx // lc, bsz, lc, LANES)
        hc = _s5_glu(z_c, w_o, hc3, mod_c, 5, True).reshape(1, bsz * lctx, d)
    return h, hc


def _rmsnorm_kernel(x_ref, g_ref, o_ref):
    x = x_ref[...]
    o_ref[...] = x * lax.rsqrt(jnp.mean(x * x, axis=-1, keepdims=True) + EPS) * g_ref[...]


def _final_norm(x, g):
    bsz, rows, d = x.shape
    tm = _row_tile(rows, ROW_TILE)
    row = pl.BlockSpec((None, tm, d), lambda b, i: (b, i, 0))
    return pl.pallas_call(
        _rmsnorm_kernel,
        out_shape=jax.ShapeDtypeStruct(x.shape, F32),
        grid=(bsz, rows // tm),
        in_specs=[row, pl.BlockSpec((1, d), lambda b, i: (0, 0))],
        out_specs=row,
        compiler_params=_cparams("parallel", "parallel"),
        name="final_norm",
    )(x, g.reshape(1, d))


def kernel(x, c, ctx, c_ctx, w_ada, b_ada, g_norm, w_ffn_in, w_ffn_out, g_final, ml_w_in, ml_b_gate, ml_g_head, ml_w_out, wa_w_in, wa_sink, wa_w_out, s5_w_in, s5_lam_re, s5_lam_im, s5_log_dt, s5_b_re, s5_b_im, s5_c_re, s5_c_im, s5_d_skip, s5_w_out):
    bsz, n, d = x.shape
    depth = w_ada.shape[0]
    lctx = ctx.shape[1]
    n_rows = -(-(bsz + 1) // 16) * 16
    c_rows = jnp.concatenate([c, c_ctx[None], jnp.zeros((n_rows - bsz - 1, d), F32)], axis=0)
    mods = _mod_table(c_rows, w_ada, b_ada).reshape(depth, n_rows, N_MOD, 1, d)
    w_in, w_out = _to_bf16(w_ffn_in), _to_bf16(w_ffn_out)
    h = x
    hc = ctx.reshape(1, bsz * lctx, d)
    for layer in range(depth):
        has_next = layer < depth - 1
        mod, mod_c = mods[layer, :bsz], mods[layer, bsz:bsz + 1]
        g = g_norm[layer]
        h = _ffn(h, g[0], mod, 0, w_in, w_out, layer, 0)
        hc = _ffn(hc, g[0], mod_c, 0, w_in, w_out, layer, 0)
        kind, idx = layer % 3, layer // 3
        if kind == 0:
            h, hc = _mlstm_mixer(h, hc, g[1], mod, mod_c, ml_w_in[idx], ml_b_gate[idx],
                                 ml_g_head[idx], ml_w_out[idx], has_next)
        elif kind == 1:
            h, hc = _wa_mixer(h, hc, g[1], mod, mod_c, wa_w_in[idx], wa_sink[idx],
                              wa_w_out[idx], has_next)
        else:
            h, hc = _s5_mixer(h, hc, g[1], mod, mod_c, s5_w_in[idx], s5_lam_re[idx], s5_lam_im[idx],
                              s5_log_dt[idx], s5_b_re[idx], s5_b_im[idx], s5_c_re[idx],
                              s5_c_im[idx], s5_d_skip[idx], s5_w_out[idx], has_next)
        h = _ffn(h, g[2], mod, 6, w_in, w_out, layer, 1)
        if has_next:
            hc = _ffn(hc, g[2], mod_c, 6, w_in, w_out, layer, 1)
    return _final_norm(h, g_final)
```

```python
import functools
import math

import jax
import jax.numpy as jnp
from jax import lax
from jax.experimental import pallas as pl
from jax.experimental.pallas import tpu as pltpu

F32 = jnp.float32
BF16 = jnp.bfloat16

EPS = 1e-6
NEG_BIG = -1e30
LOG2E = math.log2(math.e)
N_MOD = 9
ML_HEADS = 8
ML_CHUNK = 512
ML_MERGE_ROWS = 256
WA_Q_HEADS = 16
WA_KV_HEADS = 4
WA_WINDOW = 128
WA_BLOCK = 128
GRID_W = 64
ROPE_BASE = 10000.0
S5_GROUP = 16
S5_STATE = 64
S5_CHUNK = 16

LANES = 128
MXU_COLS = 256
ROW_TILE = 512
ROW_TILE_STREAM = 1024
CAST_BLOCK_BYTES = 8 * 1024 * 1024
VMEM_LIMIT = 56 * 1024 * 1024


def _cparams(*sem):
    return pltpu.CompilerParams(dimension_semantics=sem, vmem_limit_bytes=VMEM_LIMIT)


def _row_tile(rows, want):
    return want if rows % want == 0 else rows


def _sub_cols(cols):
    return MXU_COLS if cols % MXU_COLS == 0 else cols


def _col_tile(cols, want):
    t = min(want, cols)
    while cols % t:
        t -= LANES
    return t


def _adaln(x, g, shift, scale):
    var = jnp.mean(x * x, axis=-1, keepdims=True)
    return (x * lax.rsqrt(var + EPS) * g) * (1.0 + scale) + shift


ADALN_ROWS = 128


def _adaln_rows(hn_ref, x_ref, g_ref, sh_ref, sc_ref):
    rows = x_ref.shape[0]
    step = ADALN_ROWS if rows % ADALN_ROWS == 0 else rows

    def body(r, carry):
        sl = pl.ds(pl.multiple_of(r * step, step), step)
        hn_ref[sl, :] = _adaln(x_ref[sl, :], g_ref[...], sh_ref[...], sc_ref[...]).astype(BF16)
        return carry

    lax.fori_loop(0, rows // step, body, 0)


def _mod_spec(k, d, shared=False):
    return pl.BlockSpec((None, None, 1, d), lambda b, *_: (0 if shared else b, k, 0, 0))


def _mod_kernel(c_ref, w_ref, b_ref, o_ref):
    c = c_ref[...]
    s = (c * jax.nn.sigmoid(c)).astype(BF16)
    o_ref[...] = jnp.dot(s, w_ref[...].astype(BF16), preferred_element_type=F32) + b_ref[...]


def _mod_table(c_rows, w_ada, b_ada):
    depth, d, nd = w_ada.shape
    r = c_rows.shape[0]
    tn = _col_tile(nd, 1024)
    return pl.pallas_call(
        _mod_kernel,
        out_shape=jax.ShapeDtypeStruct((depth, r, nd), F32),
        grid=(depth, nd // tn),
        in_specs=[pl.BlockSpec((r, d), lambda l, n: (0, 0)),
                  pl.BlockSpec((None, d, tn), lambda l, n: (l, 0, n)),
                  pl.BlockSpec((None, 1, tn), lambda l, n: (l, 0, n))],
        out_specs=pl.BlockSpec((None, r, tn), lambda l, n: (l, 0, n)),
        compiler_params=_cparams("parallel", "parallel"),
        name="mod_table",
    )(c_rows, w_ada, b_ada.reshape(depth, 1, nd))


def _ffn_kernel(xn_ref, g_ref, sh_ref, sc_ref, x_ref, gate_ref, wa_ref, wg_ref, wo_ref, o_ref,
                hn_ref, act_ref, *, nf):
    j = pl.program_id(2)
    last = pl.num_programs(2) - 1
    tf = wa_ref.shape[1]

    @pl.when((j == 0) & (pl.program_id(0) == 0) & (pl.program_id(1) == 0))
    def _():
        _adaln_rows(hn_ref, xn_ref, g_ref, sh_ref, sc_ref)

    @pl.when(j < nf)
    def _():
        hn = hn_ref[...]
        sub = _sub_cols(tf)
        for c in range(tf // sub):
            a = jnp.dot(hn, wa_ref[:, c * sub:(c + 1) * sub], preferred_element_type=F32)
            gt = jnp.dot(hn, wg_ref[:, c * sub:(c + 1) * sub], preferred_element_type=F32)
            cols = pl.ds(pl.multiple_of(j * tf + c * sub, sub), sub)
            act_ref[:, cols] = (a * (gt * jax.nn.sigmoid(gt))).astype(BF16)

    def drain():
        y = jnp.dot(act_ref[...], wo_ref[...], preferred_element_type=F32)
        o_ref[...] = x_ref[...] + (0.5 * gate_ref[...]) * y

    @pl.when((j >= nf) & (j < last))
    def _():
        drain()

    @pl.when(j == last)
    def _():
        drain()
        step = ADALN_ROWS if xn_ref.shape[0] % ADALN_ROWS == 0 else xn_ref.shape[0]
        for r in range(xn_ref.shape[0] // step):
            rows = slice(r * step, (r + 1) * step)
            hn_ref[rows, :] = _adaln(xn_ref[rows, :], g_ref[...], sh_ref[...], sc_ref[...]).astype(BF16)


def _cast_kernel(x_ref, o_ref):
    o_ref[...] = x_ref[...].astype(o_ref.dtype)


def _to_bf16(w):
    shape = w.shape
    cols = shape[-1]
    w2 = w.reshape(-1, cols)
    tr = 1 << ((CAST_BLOCK_BYTES // (4 * cols)).bit_length() - 1)
    while w2.shape[0] % tr:
        tr //= 2
    blk = pl.BlockSpec((tr, cols), lambda i: (i, 0))
    out = pl.pallas_call(
        _cast_kernel,
        out_shape=jax.ShapeDtypeStruct(w2.shape, BF16),
        grid=(w2.shape[0] // tr,),
        in_specs=[blk],
        out_specs=blk,
        compiler_params=_cparams("parallel"),
        name="to_bf16",
    )(w2)
    return out.reshape(shape)


def _ffn(x, g, mod, k0, w_in, w_out, layer, half):
    bsz, rows, d = x.shape
    ff = w_out.shape[2]
    tm = _row_tile(rows, ROW_TILE_STREAM)
    tf = _col_tile(ff, 512)
    tn = _col_tile(d, 256)
    nf = ff // tf
    nt = rows // tm
    steps = nf + d // tn

    def fill(j):
        return jnp.minimum(j, nf - 1)

    def drain(j):
        return jnp.maximum(j - nf, 0)

    def ahead(b, i, j):
        r = jnp.minimum(b * nt + i + (j == steps - 1).astype(jnp.int32), bsz * nt - 1)
        return r // nt, r % nt

    def mod_ahead(k):
        return pl.BlockSpec((None, None, 1, d), lambda b, i, j: (ahead(b, i, j)[0], k, 0, 0))

    return pl.pallas_call(
        functools.partial(_ffn_kernel, nf=nf),
        out_shape=jax.ShapeDtypeStruct(x.shape, F32),
        grid=(bsz, nt, steps),
        in_specs=[pl.BlockSpec((None, tm, d), lambda b, i, j: (*ahead(b, i, j), 0)),
                  pl.BlockSpec((1, d), lambda b, i, j: (0, 0)),
                  mod_ahead(k0), mod_ahead(k0 + 1),
                  pl.BlockSpec((None, tm, tn), lambda b, i, j: (b, i, drain(j))),
                  pl.BlockSpec((None, None, 1, tn), lambda b, i, j: (b, k0 + 2, 0, drain(j))),
                  pl.BlockSpec((None, None, d, tf), lambda b, i, j: (layer, half, 0, fill(j))),
                  pl.BlockSpec((None, None, d, tf), lambda b, i, j: (layer, half, 0, fill(j) + nf)),
                  pl.BlockSpec((None, None, ff, tn), lambda b, i, j: (layer, half, 0, drain(j)))],
        out_specs=pl.BlockSpec((None, tm, tn), lambda b, i, j: (b, i, drain(j))),
        scratch_shapes=[pltpu.VMEM((tm, d), BF16), pltpu.VMEM((tm, ff), BF16)],
        compiler_params=_cparams("arbitrary", "arbitrary", "arbitrary"),
        name="ffn",
    )(x, g.reshape(1, d), mod, mod, x, mod, w_in, w_in, w_out)


def _proj_kernel(*refs, n_rope, with_gates, n_scaled, scale):
    x_ref, g_ref, sh_ref, sc_ref, w_ref = refs[:5]
    rest = refs[5:]
    if n_rope:
        cs_ref, sn_ref = rest[:2]
        rest = rest[2:]
    if with_gates:
        wg_ref, bg_ref, o_ref, og_ref, hn_ref = rest
    else:
        o_ref, hn_ref = rest
    n = pl.program_id(2)

    @pl.when(n == 0)
    def _():
        _adaln_rows(hn_ref, x_ref, g_ref, sh_ref, sc_ref)
        if with_gates:
            og_ref[...] = jnp.dot(hn_ref[...], wg_ref[...], preferred_element_type=F32) + bg_ref[...]

    def columns(rope):
        hn = hn_ref[...]
        tn = w_ref.shape[1]
        sub = _sub_cols(tn)
        for c in range(tn // sub):
            acc = jnp.dot(hn, w_ref[:, c * sub:(c + 1) * sub], preferred_element_type=F32)
            if n_scaled:
                acc = acc * jnp.where(n < n_scaled, scale, 1.0)
            if not rope:
                o_ref[:, c * sub:(c + 1) * sub] = acc.astype(o_ref.dtype)
            else:
                cs, sn = cs_ref[...], sn_ref[...]
                hd = cs.shape[1]
                for h in range(sub // hd):
                    r = acc[:, h * hd:(h + 1) * hd]
                    r = r * cs + pltpu.roll(r, hd // 2, axis=1) * sn
                    o_ref[:, c * sub + h * hd:c * sub + (h + 1) * hd] = r.astype(o_ref.dtype)

    if not n_rope:
        columns(False)
    else:
        @pl.when(n >= n_rope)
        def _():
            columns(False)

        @pl.when(n < n_rope)
        def _():
            columns(True)


def _proj(x, g, mod, k0, w, out_dtype, rope=None, rope_cols=0, gates=None, scale=1.0, scale_cols=0):
    bsz, rows, d = x.shape
    nout = w.shape[1]
    tm = _row_tile(rows, ROW_TILE_STREAM)
    tn = _col_tile(math.gcd(math.gcd(nout, rope_cols), scale_cols), 1024)
    n_rope = rope_cols // tn
    row = pl.BlockSpec((None, tm, d), lambda b, i, n: (b, i, 0))
    in_specs = [row, pl.BlockSpec((1, d), lambda b, i, n: (0, 0)),
                _mod_spec(k0, d), _mod_spec(k0 + 1, d),
                pl.BlockSpec((d, tn), lambda b, i, n: (0, n))]
    args = [x, g.reshape(1, d), mod, mod, w]
    if n_rope:
        cs, sn = rope
        hd = cs.shape[1]
        in_specs += [pl.BlockSpec((tm, hd), lambda b, i, n: (i, 0))] * 2
        args += [cs, sn]
    out_shape = jax.ShapeDtypeStruct((bsz, rows, nout), out_dtype)
    out_specs = pl.BlockSpec((None, tm, tn), lambda b, i, n: (b, i, n))
    if gates is not None:
        wg, bg = gates
        ng = wg.shape[1]
        in_specs += [pl.BlockSpec((d, ng), lambda b, i, n: (0, 0)),
                     pl.BlockSpec((1, ng), lambda b, i, n: (0, 0))]
        args += [wg, bg]
        out_shape = (out_shape, jax.ShapeDtypeStruct((bsz, rows, ng), F32))
        out_specs = (out_specs, pl.BlockSpec((None, tm, ng), lambda b, i, n: (b, i, 0)))
    return pl.pallas_call(
        functools.partial(_proj_kernel, n_rope=n_rope, with_gates=gates is not None,
                          n_scaled=scale_cols // tn, scale=scale),
        out_shape=out_shape,
        grid=(bsz, rows // tm, nout // tn),
        in_specs=in_specs,
        out_specs=out_specs,
        scratch_shapes=[pltpu.VMEM((tm, d), BF16)],
        compiler_params=_cparams("parallel", "parallel", "arbitrary"),
        name="proj",
    )(*args)


def _out_proj_kernel(y_ref, w_ref, h_ref, gate_ref, o_ref):
    acc = jnp.dot(y_ref[...], w_ref[...], preferred_element_type=F32)
    o_ref[...] = h_ref[...] + gate_ref[...] * acc


def _out_proj(y, w, h, mod, k):
    bsz, rows, d = h.shape
    dk = y.shape[2]
    tm = _row_tile(rows, ROW_TILE)
    return pl.pallas_call(
        _out_proj_kernel,
        out_shape=jax.ShapeDtypeStruct(h.shape, F32),
        grid=(bsz, rows // tm),
        in_specs=[pl.BlockSpec((None, tm, dk), lambda b, i: (b, i, 0)),
                  pl.BlockSpec((dk, d), lambda b, i: (0, 0)),
                  pl.BlockSpec((None, tm, d), lambda b, i: (b, i, 0)),
                  _mod_spec(k, d)],
        out_specs=pl.BlockSpec((None, tm, d), lambda b, i: (b, i, 0)),
        compiler_params=_cparams("parallel", "parallel"),
        name="out_proj",
    )(y, w, h, mod)


def _ml_chunk_len(length):
    return min(ML_CHUNK, length)


ML_GATE_SLOTS = 8


def _ml_gate_kernel(g_ref, o_ref, *, lc):
    row = lax.broadcasted_iota(jnp.int32, (lc, lc), 0)
    col = lax.broadcasted_iota(jnp.int32, (lc, lc), 1)
    lower = (col <= row).astype(BF16)
    upper = (col >= row).astype(BF16)
    slot = lax.broadcasted_iota(jnp.int32, (lc, LANES), 1) & (ML_GATE_SLOTS - 1)
    rix = lax.broadcasted_iota(jnp.int32, (lc, LANES), 0)

    def cumsum(tri, parts):
        return sum(jnp.dot(tri, p, preferred_element_type=F32) for p in parts)

    for c in range(g_ref.shape[0] // lc):
        x = g_ref[c * lc:(c + 1) * lc, :]
        ls = jax.nn.log_sigmoid(x)
        hi = ls.astype(BF16)
        r1 = ls - hi.astype(F32)
        mid = r1.astype(BF16)
        lo = (r1 - mid.astype(F32)).astype(BF16)
        pre = cumsum(lower, (hi, mid, lo))
        suf = cumsum(upper, (hi, mid, lo))
        a_f = x - pltpu.roll(pre, LANES - 1, axis=1)
        a_b = x - pltpu.roll(suf, LANES - 1, axis=1)
        cm_f, cm_b = a_f, a_b
        sh = 1
        while sh < lc:
            cm_f = jnp.maximum(cm_f, jnp.where(rix >= sh, pltpu.roll(cm_f, sh, axis=0), NEG_BIG))
            cm_b = jnp.maximum(cm_b, jnp.where(rix < lc - sh, pltpu.roll(cm_b, lc - sh, axis=0), NEG_BIG))
            sh *= 2
        out = jnp.where(slot == 0, a_f,
              jnp.where(slot == 1, pre,
              jnp.where(slot == 2, a_b,
              jnp.where(slot == 3, suf,
              jnp.where(slot == 4, pltpu.roll(cm_f, 4, axis=1), pltpu.roll(cm_b, 3, axis=1))))))
        o_ref[:, c * lc:(c + 1) * lc] = jnp.transpose(out)


def _ml_gates(g):
    bsz, rows, w = g.shape
    tm = _row_tile(rows, ROW_TILE_STREAM)
    return pl.pallas_call(
        functools.partial(_ml_gate_kernel, lc=_ml_chunk_len(rows)),
        out_shape=jax.ShapeDtypeStruct((bsz, w, rows), F32),
        grid=(bsz, rows // tm),
        in_specs=[pl.BlockSpec((None, tm, w), lambda b, i: (b, i, 0))],
        out_specs=pl.BlockSpec((None, w, tm), lambda b, i: (b, 0, i)),
        compiler_params=_cparams("parallel", "parallel"),
        name="ml_gates",
    )(g)


def _ml_chunk(q, k, v_aug, a_col, b_col, cm_col, a_row, c_mem, m_run, tri, last):
    dd = range(2)
    hd = q[0].shape[1]
    nt = (((1,), (1,)), ((), ()))
    tn = (((0,), (0,)), ((), ()))
    b_end = [b_col[d][last[d]:last[d] + 1, :] for d in dd]
    m_new = [b_end[d] + jnp.maximum(m_run[d], cm_col[d][last[d]:last[d] + 1, :]) for d in dd]
    r_col = [jnp.maximum(m_run[d], cm_col[d]) for d in dd]
    s = [lax.dot_general(q[d], k[d], nt, preferred_element_type=F32) for d in dd]
    c_bf = [c_mem[d].astype(BF16) for d in dd]
    qc = [jnp.dot(q[d], c_bf[d], preferred_element_type=F32) for d in dd]
    wk = [k[d] * jnp.exp(a_col[d] + (b_end[d] - m_new[d])).astype(BF16) for d in dd]
    e = [jnp.exp(jnp.where(tri[d], a_row[d] - r_col[d], NEG_BIG)) for d in dd]
    upd = [lax.dot_general(wk[d], v_aug[d], tn, preferred_element_type=F32) for d in dd]
    p = [(s[d] * e[d]).astype(BF16) for d in dd]
    dec = [jnp.exp(b_end[d] + m_run[d] - m_new[d]) for d in dd]
    dq = [jnp.exp(m_run[d] - r_col[d]) for d in dd]
    acc = [jnp.dot(p[d], v_aug[d], preferred_element_type=F32) + dq[d] * qc[d] for d in dd]
    c_new = [dec[d] * c_mem[d] + upd[d] for d in dd]
    lim = [jnp.exp(-(b_col[d] + r_col[d])) for d in dd]
    h = [acc[d][:, :hd] / jnp.maximum(jnp.abs(acc[d][:, hd:hd + 1]), lim[d]) for d in dd]
    return h, c_new, m_new


def _mlstm_kernel(cq, ck, cv, co, lq, lk, lv, lo, cgr, lgr, gh_ref,
                  yc_ref, yl_ref, hf_ref, hb_ref, c_ref, m_ref):
    c_ref[...] = jnp.zeros_like(c_ref)
    m_ref[...] = jnp.full_like(m_ref, NEG_BIG)

    def scan(q_ref, k_ref, v_ref, o_ref, gr_ref, y_ref):
        lc = _ml_chunk_len(q_ref.shape[0])
        nc = q_ref.shape[0] // lc
        rows = lax.broadcasted_iota(jnp.int32, (lc, lc), 0)
        cols = lax.broadcasted_iota(jnp.int32, (lc, lc), 1)
        tri = (cols <= rows, cols >= rows)
        ones_blk = jnp.where(lax.broadcasted_iota(jnp.int32, (lc, LANES), 1) == 0, 1.0, 0.0).astype(BF16)

        def body(c, carry):
            sl = (pl.ds(pl.multiple_of(c * lc, lc), lc), pl.ds(pl.multiple_of((nc - 1 - c) * lc, lc), lc))
            dd = range(2)
            gc = [jnp.transpose(gr_ref[:, sl[d]]) for d in dd]
            h, c_new, m_new = _ml_chunk(
                [q_ref[sl[d], :] for d in dd], [k_ref[sl[d], :] for d in dd],
                [jnp.concatenate([v_ref[sl[d], :], ones_blk], axis=1) for d in dd],
                [gc[d][:, 2 * d:2 * d + 1] for d in dd], [gc[d][:, 2 * d + 1:2 * d + 2] for d in dd],
                [gc[d][:, 4 + d:5 + d] for d in dd], [gr_ref[2 * d:2 * d + 1, sl[d]] for d in dd],
                [c_ref[d] for d in dd], [m_ref[d] for d in dd], tri, (lc - 1, 0))
            for d, out_ref in enumerate((hf_ref, hb_ref)):
                c_ref[d] = c_new[d]
                m_ref[d] = m_new[d]
                out_ref[sl[d], :] = h[d]
            return carry

        lax.fori_loop(0, nc, body, 0)

        lm = ML_MERGE_ROWS

        def merge(c, carry):
            sl = pl.ds(pl.multiple_of(c * lm, lm), lm)
            hs = hf_ref[sl, :] + hb_ref[sl, :]
            hs = hs * lax.rsqrt(jnp.mean(hs * hs, axis=-1, keepdims=True) + EPS) * gh_ref[...]
            y_ref[sl, :] = (jax.nn.sigmoid(o_ref[sl, :].astype(F32)) * hs).astype(y_ref.dtype)
            return carry

        lax.fori_loop(0, q_ref.shape[0] // lm, merge, 0)

    scan(cq, ck, cv, co, cgr, yc_ref)
    scan(lq, lk, lv, lo, lgr, yl_ref)


def _mlstm_scan(zc, zl, gc, gl, g_head):
    bsz, n, d4 = zl.shape
    lctx = zc.shape[1]
    d = d4 // 4
    nh = ML_HEADS
    hd = d // nh
    assert n % _ml_chunk_len(n) == 0 and n % ML_MERGE_ROWS == 0 and lctx % ML_MERGE_ROWS == 0

    def zspec(length, k):
        return pl.BlockSpec((None, length, hd), lambda b, h: (b, 0, k * nh + h))

    def gspec(length):
        return pl.BlockSpec((None, ML_GATE_SLOTS, length), lambda b, h: (b, h, 0))

    lmax = max(n, lctx)
    return pl.pallas_call(
        _mlstm_kernel,
        out_shape=(jax.ShapeDtypeStruct((bsz, lctx, d), BF16),
                   jax.ShapeDtypeStruct((bsz, n, d), BF16)),
        grid=(bsz, nh),
        in_specs=([zspec(lctx, k) for k in range(4)] + [zspec(n, k) for k in range(4)]
                  + [gspec(lctx), gspec(n)]
                  + [pl.BlockSpec((None, 1, hd), lambda b, h: (h, 0, 0))]),
        out_specs=(pl.BlockSpec((None, lctx, hd), lambda b, h: (b, 0, h)),
                   pl.BlockSpec((None, n, hd), lambda b, h: (b, 0, h))),
        scratch_shapes=[pltpu.VMEM((lmax, hd), F32), pltpu.VMEM((lmax, hd), F32),
                        pltpu.VMEM((2, hd, hd + LANES), F32), pltpu.VMEM((2, 1, 1), F32)],
        compiler_params=_cparams("parallel", "parallel"),
        name="mlstm_scan",
    )(zc, zc, zc, zc, zl, zl, zl, zl, gc, gl, g_head.reshape(nh, 1, hd))


def _mlstm_mixer(h, hc, g, mod, mod_c, w_in, b_gate, g_head, w_out, with_ctx):
    bsz, n, d = h.shape
    lctx = hc.shape[1] // bsz
    w_main = w_in[:, :4 * d].astype(BF16)
    nh = ML_HEADS
    ngp = -(-nh * ML_GATE_SLOTS // LANES) * LANES

    def head_major(t):
        t = jnp.swapaxes(t.reshape(t.shape[:-1] + (4, nh)), -1, -2)
        t = jnp.pad(t, [(0, 0)] * (t.ndim - 1) + [(0, ML_GATE_SLOTS - 4)])
        t = t.reshape(t.shape[:-2] + (nh * ML_GATE_SLOTS,))
        return jnp.pad(t, [(0, 0)] * (t.ndim - 1) + [(0, ngp - nh * ML_GATE_SLOTS)])

    w_gate = head_major(w_in[:, 4 * d:]).astype(BF16)
    bias = head_major(b_gate.reshape(1, 4 * nh))
    q_scale = (d // ML_HEADS) ** -0.5
    zl, gl = _proj(h, g, mod, 3, w_main, BF16, gates=(w_gate, bias), scale=q_scale, scale_cols=d)
    zc, gc = _proj(hc, g, mod_c, 3, w_main, BF16, gates=(w_gate, bias), scale=q_scale, scale_cols=d)
    yc, yl = _mlstm_scan(zc.reshape(bsz, lctx, 4 * d), zl,
                         _ml_gates(gc.reshape(bsz, lctx, ngp)), _ml_gates(gl), g_head)
    w_o = w_out.astype(BF16)
    h = _out_proj(yl, w_o, h, mod, 5)
    if with_ctx:
        hc = _out_proj(yc.reshape(1, bsz * lctx, d), w_o, hc, mod_c, 5)
    return h, hc


def _wa_kernel(sink_ref, q_ref, *refs, local, scale):
    if local:
        k_ref, v_ref, kc_ref, vc_ref, o_ref = refs
    else:
        kc_ref, vc_ref, o_ref = refs
    tq = q_ref.shape[0]
    hd = kc_ref.shape[1] // WA_KV_HEADS
    grp = WA_Q_HEADS // WA_KV_HEADS
    if local:
        n = k_ref.shape[0]
        span = 3 * WA_BLOCK
        start = pl.program_id(1) * tq
        ks = pl.multiple_of(jnp.clip(start - WA_BLOCK, 0, n - span), WA_BLOCK)
        q_pos = start + (lax.broadcasted_iota(jnp.int32, (grp * tq, span), 0) & (tq - 1))
        k_pos = ks + lax.broadcasted_iota(jnp.int32, (grp * tq, span), 1)
        ok = jnp.abs(q_pos - k_pos) <= WA_WINDOW
    dims = (((1,), (1,)), ((), ()))
    heads = range(WA_KV_HEADS)
    c2 = scale * LOG2E
    cs = [slice(kv * hd, (kv + 1) * hd) for kv in heads]
    q4 = [jnp.concatenate([q_ref[:, (kv * grp + j) * hd:(kv * grp + j + 1) * hd] for j in range(grp)], axis=0)
          for kv in heads]
    sink = [jnp.concatenate([jnp.full((tq, 1), sink_ref[0, kv * grp + j], F32) for j in range(grp)], axis=0)
            * (1.0 / scale) for kv in heads]
    s_ctx = [lax.dot_general(q4[kv], kc_ref[:, cs[kv]], dims, preferred_element_type=F32) for kv in heads]
    m = [jnp.maximum(jnp.max(s_ctx[kv], axis=1, keepdims=True), sink[kv]) for kv in heads]
    if local:
        s_loc = [jnp.where(ok, lax.dot_general(q4[kv], k_ref[pl.ds(ks, span), cs[kv]], dims,
                                               preferred_element_type=F32), NEG_BIG) for kv in heads]
        m = [jnp.maximum(m[kv], jnp.max(s_loc[kv], axis=1, keepdims=True)) for kv in heads]
        p_loc = [jnp.exp2((s_loc[kv] - m[kv]) * c2) for kv in heads]
    p_ctx = [jnp.exp2((s_ctx[kv] - m[kv]) * c2) for kv in heads]
    den = [jnp.sum(p_ctx[kv], axis=1, keepdims=True) + jnp.exp2((sink[kv] - m[kv]) * c2) for kv in heads]
    if local:
        den = [den[kv] + jnp.sum(p_loc[kv], axis=1, keepdims=True) for kv in heads]
    inv = [1.0 / den[kv] for kv in heads]
    out = [jnp.dot((p_ctx[kv] * inv[kv]).astype(BF16), vc_ref[:, cs[kv]], preferred_element_type=F32)
           for kv in heads]
    if local:
        out = [out[kv] + jnp.dot((p_loc[kv] * inv[kv]).astype(BF16), v_ref[pl.ds(ks, span), cs[kv]],
                                 preferred_element_type=F32) for kv in heads]
    for kv in heads:
        for j in range(grp):
            hq = kv * grp + j
            o_ref[:, hq * hd:(hq + 1) * hd] = out[kv][j * tq:(j + 1) * tq].astype(o_ref.dtype)


def _wa_attention(sink, zq, zkv, zc, local):
    bsz, lq, _ = zq.shape
    lctx = zc.shape[1]
    hd = zq.shape[2] // (WA_Q_HEADS + 2 * WA_KV_HEADS)
    qd, kd = WA_Q_HEADS * hd, WA_KV_HEADS * hd
    tq = WA_BLOCK
    kblk = qd // kd
    in_specs = [pl.BlockSpec(memory_space=pltpu.SMEM),
                pl.BlockSpec((None, tq, qd), lambda b, i: (b, i, 0))]
    args = [sink.reshape(1, WA_Q_HEADS), zq]
    if local:
        n = zkv.shape[1]
        in_specs += [pl.BlockSpec((None, n, kd), lambda b, i: (b, 0, kblk)),
                     pl.BlockSpec((None, n, kd), lambda b, i: (b, 0, kblk + 1))]
        args += [zkv, zkv]
    in_specs += [pl.BlockSpec((None, lctx, kd), lambda b, i: (b, 0, kblk)),
                 pl.BlockSpec((None, lctx, kd), lambda b, i: (b, 0, kblk + 1))]
    args += [zc, zc]
    return pl.pallas_call(
        functools.partial(_wa_kernel, local=local, scale=hd ** -0.5),
        out_shape=jax.ShapeDtypeStruct((bsz, lq, qd), BF16),
        grid=(bsz, lq // tq),
        in_specs=in_specs,
        out_specs=pl.BlockSpec((None, tq, qd), lambda b, i: (b, i, 0)),
        compiler_params=_cparams("parallel", "parallel"),
        name="wa_attention",
    )(*args)


def _rope_tables(n, hd):
    rows = n // GRID_W
    row = jnp.repeat(jnp.arange(rows, dtype=F32), GRID_W)
    col = jnp.tile(jnp.arange(GRID_W, dtype=F32), rows)
    n_freq = hd // 4
    inv = ROPE_BASE ** (-jnp.arange(n_freq, dtype=F32) / n_freq)
    ang = jnp.concatenate([row[:, None] * inv, col[:, None] * inv], axis=-1)
    cos, sin = jnp.cos(ang), jnp.sin(ang)
    return jnp.concatenate([cos, cos], axis=-1), jnp.concatenate([-sin, sin], axis=-1)


def _wa_mixer(h, hc, g, mod, mod_c, w_in, sink, w_out, with_ctx):
    bsz, n, d = h.shape
    lctx = hc.shape[1] // bsz
    hd = d // WA_Q_HEADS
    w = w_in.astype(BF16)
    rope_cols = (WA_Q_HEADS + WA_KV_HEADS) * hd
    zl = _proj(h, g, mod, 3, w, BF16, rope=_rope_tables(n, hd), rope_cols=rope_cols)
    zc = _proj(hc, g, mod_c, 3, w, BF16).reshape(bsz, lctx, -1)
    sink = sink.astype(F32)
    w_o = w_out.astype(BF16)
    h = _out_proj(_wa_attention(sink, zl, zl, zc, True), w_o, h, mod, 5)
    if with_ctx:
        yc = _wa_attention(sink, zc, None, zc, False)
        hc = _out_proj(yc.reshape(1, bsz * lctx, d), w_o, hc, mod_c, 5)
    return h, hc


def _s5_proj_kernel(x_ref, g_ref, sh_ref, sc_ref, w_ref, of_ref, hn_ref):
    n = pl.program_id(2)

    @pl.when(n == 0)
    def _():
        _adaln_rows(hn_ref, x_ref, g_ref, sh_ref, sc_ref)

    hn = hn_ref[...]
    tm, tn = hn.shape[0], w_ref.shape[1]
    sub = _sub_cols(tn)
    for c in range(tn // sub):
        acc = jnp.dot(hn, w_ref[:, c * sub:(c + 1) * sub], preferred_element_type=F32)
        for k in range(sub // LANES):
            blk = acc[:, k * LANES:(k + 1) * LANES]
            of_ref[c * (sub // LANES) + k] = blk.reshape(tm // S5_CHUNK, S5_CHUNK, LANES)


def _s5_proj(x, g, mod, k0, w, shared_mod):
    bsz, rows, d = x.shape
    tm = _row_tile(rows, ROW_TILE_STREAM)
    tn = _col_tile(d, 1024)
    lc = S5_CHUNK
    return pl.pallas_call(
        _s5_proj_kernel,
        out_shape=jax.ShapeDtypeStruct((d // LANES, rows // lc, bsz, lc, LANES), F32),
        grid=(bsz, rows // tm, d // tn),
        in_specs=[pl.BlockSpec((None, tm, d), lambda b, i, n: (b, i, 0)),
                  pl.BlockSpec((1, d), lambda b, i, n: (0, 0)),
                  _mod_spec(k0, d, shared_mod), _mod_spec(k0 + 1, d, shared_mod),
                  pl.BlockSpec((d, tn), lambda b, i, n: (0, n))],
        out_specs=pl.BlockSpec((tn // LANES, tm // lc, None, lc, LANES), lambda b, i, n: (n, i, b, 0, 0)),
        scratch_shapes=[pltpu.VMEM((tm, d), BF16)],
        compiler_params=_cparams("parallel", "parallel", "arbitrary"),
        name="s5_proj",
    )(x, g.reshape(1, d), mod, mod, w)


S5_PACKETS = LANES // S5_GROUP
S5_REGROUP_ROWS = 16


def _packet_transpose(vs):
    lane = lax.broadcasted_iota(jnp.int32, vs[0].shape, 1)
    d = S5_PACKETS // 2
    while d:
        low = (lane & (d * S5_GROUP)) == 0
        nxt = list(vs)
        for i in range(S5_PACKETS):
            if not i & d:
                nxt[i] = jnp.where(low, vs[i], pltpu.roll(vs[i + d], d * S5_GROUP, axis=1))
                nxt[i + d] = jnp.where(low, pltpu.roll(vs[i], LANES - d * S5_GROUP, axis=1), vs[i + d])
        vs = nxt
        d //= 2
    return vs


def _s5_group_kernel(u_ref, o_ref):
    def body(r, carry):
        rows = pl.ds(pl.multiple_of(r * S5_REGROUP_ROWS, S5_REGROUP_ROWS), S5_REGROUP_ROWS)
        for hf in range(S5_CHUNK // S5_PACKETS):
            base = hf * S5_PACKETS
            vs = [pltpu.bitcast(u_ref[rows, (base + t) * LANES:(base + t + 1) * LANES].astype(o_ref.dtype),
                                jnp.uint32) for t in range(S5_PACKETS)]
            for gq, v in enumerate(_packet_transpose(vs)):
                o_ref[gq, rows, hf * LANES:(hf + 1) * LANES] = pltpu.bitcast(v, o_ref.dtype)
        return carry

    lax.fori_loop(0, u_ref.shape[0] // S5_REGROUP_ROWS, body, 0, unroll=4)


def _s5_group(u):
    g8, rows, wide = u.shape
    tr = _row_tile(rows, ROW_TILE)
    w = wide // S5_PACKETS
    return pl.pallas_call(
        _s5_group_kernel,
        out_shape=jax.ShapeDtypeStruct((g8 * S5_PACKETS, rows, w), BF16),
        grid=(g8, rows // tr),
        in_specs=[pl.BlockSpec((None, tr, wide), lambda g, i: (g, i, 0))],
        out_specs=pl.BlockSpec((S5_PACKETS, tr, w), lambda g, i: (g, i, 0)),
        compiler_params=_cparams("parallel", "parallel"),
        name="s5_group",
    )(u)


def _s5_ungroup_kernel(y_ref, u_ref, ds_ref, o_ref):
    def body(r, carry):
        rows = pl.ds(pl.multiple_of(r * S5_REGROUP_ROWS, S5_REGROUP_ROWS), S5_REGROUP_ROWS)
        for hf in range(S5_CHUNK // S5_PACKETS):
            base = hf * S5_PACKETS
            vs = [pltpu.bitcast(y_ref[gq, rows, hf * LANES:(hf + 1) * LANES], jnp.uint32)
                  for gq in range(S5_PACKETS)]
            for t, v in enumerate(_packet_transpose(vs)):
                cols = slice((base + t) * LANES, (base + t + 1) * LANES)
                y = pltpu.bitcast(v, F32)
                o_ref[rows, cols] = jax.nn.gelu(y + ds_ref[...] * u_ref[rows, cols]).astype(o_ref.dtype)
        return carry

    lax.fori_loop(0, o_ref.shape[0] // S5_REGROUP_ROWS, body, 0, unroll=2)


def _s5_ungroup(y, u, d_skip):
    ng, rows, w = y.shape
    tr = _row_tile(rows, ROW_TILE)
    wide = w * S5_PACKETS
    wide_spec = pl.BlockSpec((None, tr, wide), lambda g, i: (g, i, 0))
    return pl.pallas_call(
        _s5_ungroup_kernel,
        out_shape=jax.ShapeDtypeStruct((ng // S5_PACKETS, rows, wide), BF16),
        grid=(ng // S5_PACKETS, rows // tr),
        in_specs=[pl.BlockSpec((S5_PACKETS, tr, w), lambda g, i: (g, i, 0)), wide_spec,
                  pl.BlockSpec((None, 1, LANES), lambda g, i: (g, 0, 0))],
        out_specs=wide_spec,
        compiler_params=_cparams("parallel", "parallel"),
        name="s5_ungroup",
    )(y, u, d_skip)


def _s5_kernel(uc_ref, ul_ref, kl_ref, q_ref, p_ref, a_ref, yc_ref, yl_ref,
               sc_ref, sl_ref, xc_ref, xl_ref, t_ref, *, bsz):
    half = sc_ref.shape[2] // 2
    nch = kl_ref.shape[1]
    for ti in range(S5_CHUNK):
        for to in range(S5_CHUNK):
            t_ref[ti * nch:(ti + 1) * nch, to * nch:(to + 1) * nch] = kl_ref[to - ti + S5_CHUNK - 1]
    for u_ref, s_ref in ((uc_ref, sc_ref), (ul_ref, sl_ref)):
        for d in range(2):
            s_ref[d] = jnp.dot(u_ref[...], q_ref[d], preferred_element_type=F32)
    aa = [a_ref[d, 0:1, :] for d in range(2)]
    ab = [a_ref[d, 1:2, :] for d in range(2)]

    def run(s_ref, x_ref, carry):
        n = s_ref.shape[1] // bsz

        def step(d, j, w):
            rows = pl.ds(pl.multiple_of(j * bsz, bsz), bsz)
            x_ref[d, rows, :] = w[:, :half]
            other = jnp.concatenate([w[:, half:], w[:, :half]], axis=1)
            return aa[d] * w + ab[d] * other + s_ref[d, rows, :]

        def body(i, ws):
            return step(0, i, ws[0]), step(1, n - 1 - i, ws[1])

        return lax.fori_loop(0, n, body, carry)

    zero = jnp.zeros((bsz, 2 * half), F32)
    run(sl_ref, xl_ref, run(sc_ref, xc_ref, (zero, zero)))
    t_m = t_ref[...].astype(BF16)
    for u_ref, x_ref, y_ref in ((uc_ref, xc_ref, yc_ref), (ul_ref, xl_ref, yl_ref)):
        y = jnp.dot(u_ref[...], t_m, preferred_element_type=F32)
        for d in range(2):
            y = y + lax.dot_general(x_ref[d].astype(BF16), p_ref[d], (((1,), (1,)), ((), ())),
                                    preferred_element_type=F32)
        y_ref[...] = y


def _s5_scan(uc, ul, k_lag, q_m, p_m, a_m, bsz):
    ng, rc, w = uc.shape
    rl = ul.shape[1]
    st2 = p_m.shape[3]
    nlag, nch = k_lag.shape[1:3]

    def rows_spec(r):
        return pl.BlockSpec((None, r, w), lambda g: (g, 0, 0))

    return pl.pallas_call(
        functools.partial(_s5_kernel, bsz=bsz),
        out_shape=(jax.ShapeDtypeStruct((ng, rc, w), F32), jax.ShapeDtypeStruct((ng, rl, w), F32)),
        grid=(ng,),
        in_specs=[rows_spec(rc), rows_spec(rl),
                  pl.BlockSpec((None, nlag, nch, nch), lambda g: (g, 0, 0, 0)),
                  pl.BlockSpec((2, None, w, 2 * st2), lambda g: (0, g, 0, 0)),
                  pl.BlockSpec((2, None, w, st2), lambda g: (0, g, 0, 0)),
                  pl.BlockSpec((2, None, 2, 2 * st2), lambda g: (0, g, 0, 0))],
        out_specs=(rows_spec(rc), rows_spec(rl)),
        scratch_shapes=[pltpu.VMEM((2, rc, 2 * st2), F32), pltpu.VMEM((2, rl, 2 * st2), F32),
                        pltpu.VMEM((2, rc, st2), F32), pltpu.VMEM((2, rl, st2), F32),
                        pltpu.VMEM((w, w), F32)],
        compiler_params=_cparams("parallel"),
        name="s5_scan",
    )(uc, ul, k_lag, q_m, p_m, a_m)


def _s5_operators(lam_re, lam_im, log_dt, b_re, b_im, c_re, c_im):
    lc = S5_CHUNK
    dt = jnp.exp(log_dt)[..., None]
    mag = jnp.exp(lam_re * dt)
    lb_re, lb_im = mag * jnp.cos(lam_im * dt), mag * jnp.sin(lam_im * dt)
    den = lam_re * lam_re + lam_im * lam_im
    nr, ni = lb_re - 1.0, lb_im
    fr = (nr * lam_re + ni * lam_im) / den
    fi = (ni * lam_re - nr * lam_im) / den
    bb_re = fr[..., None] * b_re - fi[..., None] * b_im
    bb_im = fr[..., None] * b_im + fi[..., None] * b_re
    k = jnp.arange(lc + 1, dtype=F32)[:, None, None, None]
    pmag = jnp.exp(k * (lam_re * dt))
    pw_re, pw_im = pmag * jnp.cos(k * (lam_im * dt)), pmag * jnp.sin(k * (lam_im * dt))
    ngrp, nch = lam_re.shape[1], b_re.shape[3]
    nst = lam_re.shape[2]
    pw_t = (jnp.transpose(pw_re, (1, 2, 0, 3)), jnp.transpose(pw_im, (1, 2, 0, 3)))
    bb_t = (jnp.transpose(bb_re, (0, 1, 3, 2)), jnp.transpose(bb_im, (0, 1, 3, 2)))

    lb_re = pw_t[0][:, :, :, None, :] * bb_t[0][:, :, None] - pw_t[1][:, :, :, None, :] * bb_t[1][:, :, None]
    lb_im = pw_t[0][:, :, :, None, :] * bb_t[1][:, :, None] + pw_t[1][:, :, :, None, :] * bb_t[0][:, :, None]
    kern = jnp.sum(lb_re[:, :, :, :, None, :] * c_re[:, :, None, None]
                   - lb_im[:, :, :, :, None, :] * c_im[:, :, None, None], axis=-1)
    k_lag = jnp.concatenate([jnp.flip(kern[1, :, 1:lc], axis=1), (kern[0, :, 0] + kern[1, :, 0])[:, None],
                             kern[0, :, 1:lc]], axis=1)

    def per_dir(d):
        e_in = (lc - 1 - jnp.arange(lc)) if d == 0 else jnp.arange(lc)
        l_re, l_im = (pw[d][:, e_in][:, :, None, :] for pw in pw_t)
        bt_re, bt_im = (bb[d][:, None] for bb in bb_t)
        q_re = (l_re * bt_re - l_im * bt_im).reshape(ngrp, lc * nch, nst)
        q_im = (l_re * bt_im + l_im * bt_re).reshape(ngrp, lc * nch, nst)
        q_m = jnp.concatenate([q_re, q_im, q_im, q_re], axis=-1)
        e_out = (jnp.arange(lc) + 1) if d == 0 else (lc - jnp.arange(lc))
        l_re, l_im = (pw[d][:, e_out][:, :, None, :] for pw in pw_t)
        cd_re, cd_im = c_re[d][:, None], c_im[d][:, None]
        p_re = (cd_re * l_re - cd_im * l_im).reshape(ngrp, lc * nch, nst)
        p_im = (cd_re * l_im + cd_im * l_re).reshape(ngrp, lc * nch, nst)
        p_m = jnp.concatenate([p_re, -p_im], axis=-1)
        a_re, a_im = pw_re[lc, d], pw_im[lc, d]
        a_m = jnp.stack([jnp.concatenate([a_re] * 4, axis=-1),
                         jnp.concatenate([-a_im, a_im, a_im, -a_im], axis=-1)], axis=1)
        return q_m, p_m, a_m

    q_m, p_m, a_m = (jnp.stack([x, y]) for x, y in zip(per_dir(0), per_dir(1)))
    return k_lag, q_m.astype(BF16), p_m.astype(BF16), a_m


def _s5_glu_kernel(zf_ref, wa_ref, wg_ref, h_ref, gate_ref, o_ref, z_ref):
    n = pl.program_id(2)

    @pl.when(n == 0)
    def _():
        tm = z_ref.shape[0]
        for k in range(zf_ref.shape[0]):
            z_ref[:, k * LANES:(k + 1) * LANES] = zf_ref[k].reshape(tm, LANES)

    z = z_ref[...]
    tn = wa_ref.shape[1]
    sub = _sub_cols(tn)
    for c in range(tn // sub):
        cols = slice(c * sub, (c + 1) * sub)
        a = jnp.dot(z, wa_ref[:, cols], preferred_element_type=F32)
        gt = jnp.dot(z, wg_ref[:, cols], preferred_element_type=F32)
        o_ref[:, cols] = h_ref[:, cols] + gate_ref[:, cols] * (a * jax.nn.sigmoid(gt))


def _s5_glu(z, w, h, mod, k, shared_mod):
    bsz, rows, d = h.shape
    tm = _row_tile(rows, ROW_TILE_STREAM)
    tn = _col_tile(d, 512)
    nn = d // tn
    lc = S5_CHUNK
    col = pl.BlockSpec((None, tm, tn), lambda b, i, n: (b, i, n))
    return pl.pallas_call(
        _s5_glu_kernel,
        out_shape=jax.ShapeDtypeStruct(h.shape, F32),
        grid=(bsz, rows // tm, nn),
        in_specs=[pl.BlockSpec((d // LANES, tm // lc, None, lc, LANES), lambda b, i, n: (0, i, b, 0, 0)),
                  pl.BlockSpec((d, tn), lambda b, i, n: (0, n)),
                  pl.BlockSpec((d, tn), lambda b, i, n: (0, n + nn)),
                  col, pl.BlockSpec((None, None, 1, tn),
                                    lambda b, i, n: (0 if shared_mod else b, k, 0, n))],
        out_specs=col,
        scratch_shapes=[pltpu.VMEM((tm, d), BF16)],
        compiler_params=_cparams("parallel", "parallel", "arbitrary"),
        name="s5_glu",
    )(z, w, w, h, mod)


def _s5_mixer(h, hc, g, mod, mod_c, w_in, lam_re, lam_im, log_dt, b_re, b_im, c_re, c_im,
              d_skip, w_out, with_ctx):
    bsz, n, d = h.shape
    lctx = hc.shape[1] // bsz
    lc = S5_CHUNK
    g8 = d // LANES
    w = w_in.astype(BF16)
    hc3 = hc.reshape(bsz, lctx, d)
    u_l = _s5_proj(h, g, mod, 3, w, False).reshape(g8, n // lc * bsz, lc * LANES)
    u_c = _s5_proj(hc3, g, mod_c, 3, w, True).reshape(g8, lctx // lc * bsz, lc * LANES)
    ops = _s5_operators(lam_re, lam_im, log_dt, b_re, b_im, c_re, c_im)
    yp_c, yp_l = _s5_scan(_s5_group(u_c), _s5_group(u_l), *ops, bsz=bsz)
    ds = d_skip.astype(F32).reshape(g8, 1, LANES)
    w_o = w_out.astype(BF16)
    z_l = _s5_ungroup(yp_l, u_l, ds).reshape(g8, n // lc, bsz, lc, LANES)
    h = _s5_glu(z_l, w_o, h, mod, 5, False)
    if with_ctx:
        z_c = _s5_ungroup(yp_c, u_c, ds).reshape(g8, lctx // lc, bsz, lc, LANES)
        hc = _s5_glu(z_c, w_o, hc3, mod_c, 5, True).reshape(1, bsz * lctx, d)
    return h, hc


def _rmsnorm_kernel(x_ref, g_ref, o_ref):
    x = x_ref[...]
    o_ref[...] = x * lax.rsqrt(jnp.mean(x * x, axis=-1, keepdims=True) + EPS) * g_ref[...]


def _final_norm(x, g):
    bsz, rows, d = x.shape
    tm = _row_tile(rows, ROW_TILE)
    row = pl.BlockSpec((None, tm, d), lambda b, i: (b, i, 0))
    return pl.pallas_call(
        _rmsnorm_kernel,
        out_shape=jax.ShapeDtypeStruct(x.shape, F32),
        grid=(bsz, rows // tm),
        in_specs=[row, pl.BlockSpec((1, d), lambda b, i: (0, 0))],
        out_specs=row,
        compiler_params=_cparams("parallel", "parallel"),
        name="final_norm",
    )(x, g.reshape(1, d))


def kernel(x, c, ctx, c_ctx, w_ada, b_ada, g_norm, w_ffn_in, w_ffn_out, g_final, ml_w_in, ml_b_gate, ml_g_head, ml_w_out, wa_w_in, wa_sink, wa_w_out, s5_w_in, s5_lam_re, s5_lam_im, s5_log_dt, s5_b_re, s5_b_im, s5_c_re, s5_c_im, s5_d_skip, s5_w_out):
    bsz, n, d = x.shape
    depth = w_ada.shape[0]
    lctx = ctx.shape[1]
    n_rows = -(-(bsz + 1) // 16) * 16
    c_rows = jnp.concatenate([c, c_ctx[None], jnp.zeros((n_rows - bsz - 1, d), F32)], axis=0)
    mods = _mod_table(c_rows, w_ada, b_ada).reshape(depth, n_rows, N_MOD, 1, d)
    w_in, w_out = _to_bf16(w_ffn_in), _to_bf16(w_ffn_out)
    h = x
    hc = ctx.reshape(1, bsz * lctx, d)
    for layer in range(depth):
        has_next = layer < depth - 1
        mod, mod_c = mods[layer, :bsz], mods[layer, bsz:bsz + 1]
        g = g_norm[layer]
        h = _ffn(h, g[0], mod, 0, w_in, w_out, layer, 0)
        hc = _ffn(hc, g[0], mod_c, 0, w_in, w_out, layer, 0)
        kind, idx = layer % 3, layer // 3
        if kind == 0:
            h, hc = _mlstm_mixer(h, hc, g[1], mod, mod_c, ml_w_in[idx], ml_b_gate[idx],
                                 ml_g_head[idx], ml_w_out[idx], has_next)
        elif kind == 1:
            h, hc = _wa_mixer(h, hc, g[1], mod, mod_c, wa_w_in[idx], wa_sink[idx],
                              wa_w_out[idx], has_next)
        else:
            h, hc = _s5_mixer(h, hc, g[1], mod, mod_c, s5_w_in[idx], s5_lam_re[idx], s5_lam_im[idx],
                              s5_log_dt[idx], s5_b_re[idx], s5_b_im[idx], s5_c_re[idx],
                              s5_c_im[idx], s5_d_skip[idx], s5_w_out[idx], has_next)
        h = _ffn(h, g[2], mod, 6, w_in, w_out, layer, 1)
        if has_next:
            hc = _ffn(hc, g[2], mod_c, 6, w_in, w_out, layer, 1)
    return _final_norm(h, g_final)
```

```python
import functools
import math

import jax
import jax.numpy as jnp
from jax import lax
from jax.experimental import pallas as pl
from jax.experimental.pallas import tpu as pltpu

F32 = jnp.float32
BF16 = jnp.bfloat16

EPS = 1e-6
NEG_BIG = -1e30
LOG2E = math.log2(math.e)
N_MOD = 9
ML_HEADS = 8
ML_CHUNK = 512
ML_MERGE_ROWS = 256
WA_Q_HEADS = 16
WA_KV_HEADS = 4
WA_WINDOW = 128
WA_BLOCK = 128
GRID_W = 64
ROPE_BASE = 10000.0
S5_GROUP = 16
S5_STATE = 64
S5_CHUNK = 16

LANES = 128
MXU_COLS = 256
ROW_TILE = 512
ROW_TILE_STREAM = 1024
CAST_BLOCK_BYTES = 8 * 1024 * 1024
VMEM_LIMIT = 56 * 1024 * 1024


def _cparams(*sem):
    return pltpu.CompilerParams(dimension_semantics=sem, vmem_limit_bytes=VMEM_LIMIT)


def _row_tile(rows, want):
    return want if rows % want == 0 else rows


def _sub_cols(cols):
    return MXU_COLS if cols % MXU_COLS == 0 else cols


def _col_tile(cols, want):
    t = min(want, cols)
    while cols % t:
        t -= LANES
    return t


def _adaln(x, g, shift, scale):
    var = jnp.mean(x * x, axis=-1, keepdims=True)
    return (x * lax.rsqrt(var + EPS) * g) * (1.0 + scale) + shift


ADALN_ROWS = 128


def _adaln_rows(hn_ref, x_ref, g_ref, sh_ref, sc_ref):
    rows = x_ref.shape[0]
    step = ADALN_ROWS if rows % ADALN_ROWS == 0 else rows

    def body(r, carry):
        sl = pl.ds(pl.multiple_of(r * step, step), step)
        hn_ref[sl, :] = _adaln(x_ref[sl, :], g_ref[...], sh_ref[...], sc_ref[...]).astype(BF16)
        return carry

    lax.fori_loop(0, rows // step, body, 0)


def _mod_spec(k, d, shared=False):
    return pl.BlockSpec((None, None, 1, d), lambda b, *_: (0 if shared else b, k, 0, 0))


def _mod_kernel(c_ref, w_ref, b_ref, o_ref):
    c = c_ref[...]
    s = (c * jax.nn.sigmoid(c)).astype(BF16)
    o_ref[...] = jnp.dot(s, w_ref[...].astype(BF16), preferred_element_type=F32) + b_ref[...]


def _mod_table(c_rows, w_ada, b_ada):
    depth, d, nd = w_ada.shape
    r = c_rows.shape[0]
    tn = _col_tile(nd, 1024)
    return pl.pallas_call(
        _mod_kernel,
        out_shape=jax.ShapeDtypeStruct((depth, r, nd), F32),
        grid=(depth, nd // tn),
        in_specs=[pl.BlockSpec((r, d), lambda l, n: (0, 0)),
                  pl.BlockSpec((None, d, tn), lambda l, n: (l, 0, n)),
                  pl.BlockSpec((None, 1, tn), lambda l, n: (l, 0, n))],
        out_specs=pl.BlockSpec((None, r, tn), lambda l, n: (l, 0, n)),
        compiler_params=_cparams("parallel", "parallel"),
        name="mod_table",
    )(c_rows, w_ada, b_ada.reshape(depth, 1, nd))


def _ffn_kernel(xn_ref, g_ref, sh_ref, sc_ref, x_ref, gate_ref, wa_ref, wg_ref, wo_ref, o_ref,
                hn_ref, act_ref, *, nf, nd):
    j = pl.program_id(2)
    tf = wa_ref.shape[1]

    @pl.when((j == 0) & (pl.program_id(0) == 0) & (pl.program_id(1) == 0))
    def _():
        _adaln_rows(hn_ref, xn_ref, g_ref, sh_ref, sc_ref)

    @pl.when(j < nf)
    def _():
        hn = hn_ref[...]
        sub = _sub_cols(tf)
        for c in range(tf // sub):
            a = jnp.dot(hn, wa_ref[:, c * sub:(c + 1) * sub], preferred_element_type=F32)
            gt = jnp.dot(hn, wg_ref[:, c * sub:(c + 1) * sub], preferred_element_type=F32)
            cols = pl.ds(pl.multiple_of(j * tf + c * sub, sub), sub)
            act_ref[:, cols] = (a * (gt * jax.nn.sigmoid(gt))).astype(BF16)

    @pl.when(j >= nf)
    def _():
        y = jnp.dot(act_ref[...], wo_ref[...], preferred_element_type=F32)
        o_ref[...] = x_ref[...] + (0.5 * gate_ref[...]) * y
        rp = xn_ref.shape[0] // nd
        rows = pl.ds(pl.multiple_of((j - nf) * rp, rp), rp)
        hn_ref[rows, :] = _adaln(xn_ref[rows, :], g_ref[...], sh_ref[...], sc_ref[...]).astype(BF16)


def _cast_kernel(x_ref, o_ref):
    o_ref[...] = x_ref[...].astype(o_ref.dtype)


def _to_bf16(w):
    shape = w.shape
    cols = shape[-1]
    w2 = w.reshape(-1, cols)
    tr = 1 << ((CAST_BLOCK_BYTES // (4 * cols)).bit_length() - 1)
    while w2.shape[0] % tr:
        tr //= 2
    blk = pl.BlockSpec((tr, cols), lambda i: (i, 0))
    out = pl.pallas_call(
        _cast_kernel,
        out_shape=jax.ShapeDtypeStruct(w2.shape, BF16),
        grid=(w2.shape[0] // tr,),
        in_specs=[blk],
        out_specs=blk,
        compiler_params=_cparams("parallel"),
        name="to_bf16",
    )(w2)
    return out.reshape(shape)


def _ffn(x, g, mod, k0, w_in, w_out, layer, half):
    bsz, rows, d = x.shape
    ff = w_out.shape[2]
    tm = _row_tile(rows, ROW_TILE_STREAM)
    tf = _col_tile(ff, 512)
    tn = _col_tile(d, 256)
    nf = ff // tf
    nt = rows // tm
    nd = d // tn
    steps = nf + nd
    assert tm % nd == 0

    def fill(j):
        return jnp.minimum(j, nf - 1)

    def drain(j):
        return jnp.maximum(j - nf, 0)

    def ahead(b, i, j):
        r = jnp.minimum(b * nt + i + (j >= nf).astype(jnp.int32), bsz * nt - 1)
        return r // nt, r % nt

    def mod_ahead(k):
        return pl.BlockSpec((None, None, 1, d), lambda b, i, j: (ahead(b, i, j)[0], k, 0, 0))

    return pl.pallas_call(
        functools.partial(_ffn_kernel, nf=nf, nd=nd),
        out_shape=jax.ShapeDtypeStruct(x.shape, F32),
        grid=(bsz, nt, steps),
        in_specs=[pl.BlockSpec((None, tm, d), lambda b, i, j: (*ahead(b, i, j), 0)),
                  pl.BlockSpec((1, d), lambda b, i, j: (0, 0)),
                  mod_ahead(k0), mod_ahead(k0 + 1),
                  pl.BlockSpec((None, tm, tn), lambda b, i, j: (b, i, drain(j))),
                  pl.BlockSpec((None, None, 1, tn), lambda b, i, j: (b, k0 + 2, 0, drain(j))),
                  pl.BlockSpec((None, None, d, tf), lambda b, i, j: (layer, half, 0, fill(j))),
                  pl.BlockSpec((None, None, d, tf), lambda b, i, j: (layer, half, 0, fill(j) + nf)),
                  pl.BlockSpec((None, None, ff, tn), lambda b, i, j: (layer, half, 0, drain(j)))],
        out_specs=pl.BlockSpec((None, tm, tn), lambda b, i, j: (b, i, drain(j))),
        scratch_shapes=[pltpu.VMEM((tm, d), BF16), pltpu.VMEM((tm, ff), BF16)],
        compiler_params=_cparams("arbitrary", "arbitrary", "arbitrary"),
        name="ffn",
    )(x, g.reshape(1, d), mod, mod, x, mod, w_in, w_in, w_out)


def _proj_kernel(*refs, n_rope, with_gates, n_scaled, scale):
    x_ref, g_ref, sh_ref, sc_ref, w_ref = refs[:5]
    rest = refs[5:]
    if n_rope:
        cs_ref, sn_ref = rest[:2]
        rest = rest[2:]
    if with_gates:
        wg_ref, bg_ref, o_ref, og_ref, hn_ref = rest
    else:
        o_ref, hn_ref = rest
    n = pl.program_id(2)

    @pl.when(n == 0)
    def _():
        _adaln_rows(hn_ref, x_ref, g_ref, sh_ref, sc_ref)
        if with_gates:
            og_ref[...] = jnp.dot(hn_ref[...], wg_ref[...], preferred_element_type=F32) + bg_ref[...]

    def columns(rope):
        hn = hn_ref[...]
        tn = w_ref.shape[1]
        sub = _sub_cols(tn)
        for c in range(tn // sub):
            acc = jnp.dot(hn, w_ref[:, c * sub:(c + 1) * sub], preferred_element_type=F32)
            if n_scaled:
                acc = acc * jnp.where(n < n_scaled, scale, 1.0)
            if not rope:
                o_ref[:, c * sub:(c + 1) * sub] = acc.astype(o_ref.dtype)
            else:
                cs, sn = cs_ref[...], sn_ref[...]
                hd = cs.shape[1]
                for h in range(sub // hd):
                    r = acc[:, h * hd:(h + 1) * hd]
                    r = r * cs + pltpu.roll(r, hd // 2, axis=1) * sn
                    o_ref[:, c * sub + h * hd:c * sub + (h + 1) * hd] = r.astype(o_ref.dtype)

    if not n_rope:
        columns(False)
    else:
        @pl.when(n >= n_rope)
        def _():
            columns(False)

        @pl.when(n < n_rope)
        def _():
            columns(True)


def _proj(x, g, mod, k0, w, out_dtype, rope=None, rope_cols=0, gates=None, scale=1.0, scale_cols=0):
    bsz, rows, d = x.shape
    nout = w.shape[1]
    tm = _row_tile(rows, ROW_TILE_STREAM)
    tn = _col_tile(math.gcd(math.gcd(nout, rope_cols), scale_cols), 1024)
    n_rope = rope_cols // tn
    row = pl.BlockSpec((None, tm, d), lambda b, i, n: (b, i, 0))
    in_specs = [row, pl.BlockSpec((1, d), lambda b, i, n: (0, 0)),
                _mod_spec(k0, d), _mod_spec(k0 + 1, d),
                pl.BlockSpec((d, tn), lambda b, i, n: (0, n))]
    args = [x, g.reshape(1, d), mod, mod, w]
    if n_rope:
        cs, sn = rope
        hd = cs.shape[1]
        in_specs += [pl.BlockSpec((tm, hd), lambda b, i, n: (i, 0))] * 2
        args += [cs, sn]
    out_shape = jax.ShapeDtypeStruct((bsz, rows, nout), out_dtype)
    out_specs = pl.BlockSpec((None, tm, tn), lambda b, i, n: (b, i, n))
    if gates is not None:
        wg, bg = gates
        ng = wg.shape[1]
        in_specs += [pl.BlockSpec((d, ng), lambda b, i, n: (0, 0)),
                     pl.BlockSpec((1, ng), lambda b, i, n: (0, 0))]
        args += [wg, bg]
        out_shape = (out_shape, jax.ShapeDtypeStruct((bsz, rows, ng), F32))
        out_specs = (out_specs, pl.BlockSpec((None, tm, ng), lambda b, i, n: (b, i, 0)))
    return pl.pallas_call(
        functools.partial(_proj_kernel, n_rope=n_rope, with_gates=gates is not None,
                          n_scaled=scale_cols // tn, scale=scale),
        out_shape=out_shape,
        grid=(bsz, rows // tm, nout // tn),
        in_specs=in_specs,
        out_specs=out_specs,
        scratch_shapes=[pltpu.VMEM((tm, d), BF16)],
        compiler_params=_cparams("parallel", "parallel", "arbitrary"),
        name="proj",
    )(*args)


def _out_proj_kernel(y_ref, w_ref, h_ref, gate_ref, o_ref):
    acc = jnp.dot(y_ref[...], w_ref[...], preferred_element_type=F32)
    o_ref[...] = h_ref[...] + gate_ref[...] * acc


def _out_proj(y, w, h, mod, k):
    bsz, rows, d = h.shape
    dk = y.shape[2]
    tm = _row_tile(rows, ROW_TILE)
    return pl.pallas_call(
        _out_proj_kernel,
        out_shape=jax.ShapeDtypeStruct(h.shape, F32),
        grid=(bsz, rows // tm),
        in_specs=[pl.BlockSpec((None, tm, dk), lambda b, i: (b, i, 0)),
                  pl.BlockSpec((dk, d), lambda b, i: (0, 0)),
                  pl.BlockSpec((None, tm, d), lambda b, i: (b, i, 0)),
                  _mod_spec(k, d)],
        out_specs=pl.BlockSpec((None, tm, d), lambda b, i: (b, i, 0)),
        compiler_params=_cparams("parallel", "parallel"),
        name="out_proj",
    )(y, w, h, mod)


def _ml_chunk_len(length):
    return min(ML_CHUNK, length)


ML_GATE_SLOTS = 8


def _ml_gate_kernel(g_ref, o_ref, *, lc):
    row = lax.broadcasted_iota(jnp.int32, (lc, lc), 0)
    col = lax.broadcasted_iota(jnp.int32, (lc, lc), 1)
    lower = (col <= row).astype(BF16)
    upper = (col >= row).astype(BF16)
    slot = lax.broadcasted_iota(jnp.int32, (lc, LANES), 1) & (ML_GATE_SLOTS - 1)
    rix = lax.broadcasted_iota(jnp.int32, (lc, LANES), 0)

    def cumsum(tri, parts):
        return sum(jnp.dot(tri, p, preferred_element_type=F32) for p in parts)

    for c in range(g_ref.shape[0] // lc):
        x = g_ref[c * lc:(c + 1) * lc, :]
        ls = jax.nn.log_sigmoid(x)
        hi = ls.astype(BF16)
        r1 = ls - hi.astype(F32)
        mid = r1.astype(BF16)
        lo = (r1 - mid.astype(F32)).astype(BF16)
        pre = cumsum(lower, (hi, mid, lo))
        suf = cumsum(upper, (hi, mid, lo))
        a_f = x - pltpu.roll(pre, LANES - 1, axis=1)
        a_b = x - pltpu.roll(suf, LANES - 1, axis=1)
        cm_f, cm_b = a_f, a_b
        sh = 1
        while sh < lc:
            cm_f = jnp.maximum(cm_f, jnp.where(rix >= sh, pltpu.roll(cm_f, sh, axis=0), NEG_BIG))
            cm_b = jnp.maximum(cm_b, jnp.where(rix < lc - sh, pltpu.roll(cm_b, lc - sh, axis=0), NEG_BIG))
            sh *= 2
        out = jnp.where(slot == 0, a_f,
              jnp.where(slot == 1, pre,
              jnp.where(slot == 2, a_b,
              jnp.where(slot == 3, suf,
              jnp.where(slot == 4, pltpu.roll(cm_f, 4, axis=1), pltpu.roll(cm_b, 3, axis=1))))))
        o_ref[:, c * lc:(c + 1) * lc] = jnp.transpose(out)


def _ml_gates(g):
    bsz, rows, w = g.shape
    tm = _row_tile(rows, ROW_TILE_STREAM)
    return pl.pallas_call(
        functools.partial(_ml_gate_kernel, lc=_ml_chunk_len(rows)),
        out_shape=jax.ShapeDtypeStruct((bsz, w, rows), F32),
        grid=(bsz, rows // tm),
        in_specs=[pl.BlockSpec((None, tm, w), lambda b, i: (b, i, 0))],
        out_specs=pl.BlockSpec((None, w, tm), lambda b, i: (b, 0, i)),
        compiler_params=_cparams("parallel", "parallel"),
        name="ml_gates",
    )(g)


def _ml_chunk(q, k, v_aug, a_col, b_col, cm_col, a_row, c_mem, m_run, tri, last):
    dd = range(2)
    hd = q[0].shape[1]
    nt = (((1,), (1,)), ((), ()))
    tn = (((0,), (0,)), ((), ()))
    b_end = [b_col[d][last[d]:last[d] + 1, :] for d in dd]
    m_new = [b_end[d] + jnp.maximum(m_run[d], cm_col[d][last[d]:last[d] + 1, :]) for d in dd]
    r_col = [jnp.maximum(m_run[d], cm_col[d]) for d in dd]
    s = [lax.dot_general(q[d], k[d], nt, preferred_element_type=F32) for d in dd]
    c_bf = [c_mem[d].astype(BF16) for d in dd]
    qc = [jnp.dot(q[d], c_bf[d], preferred_element_type=F32) for d in dd]
    wk = [k[d] * jnp.exp(a_col[d] + (b_end[d] - m_new[d])).astype(BF16) for d in dd]
    e = [jnp.exp(jnp.where(tri[d], a_row[d] - r_col[d], NEG_BIG)) for d in dd]
    upd = [lax.dot_general(wk[d], v_aug[d], tn, preferred_element_type=F32) for d in dd]
    p = [(s[d] * e[d]).astype(BF16) for d in dd]
    dec = [jnp.exp(b_end[d] + m_run[d] - m_new[d]) for d in dd]
    dq = [jnp.exp(m_run[d] - r_col[d]) for d in dd]
    acc = [jnp.dot(p[d], v_aug[d], preferred_element_type=F32) + dq[d] * qc[d] for d in dd]
    c_new = [dec[d] * c_mem[d] + upd[d] for d in dd]
    lim = [jnp.exp(-(b_col[d] + r_col[d])) for d in dd]
    h = [acc[d][:, :hd] / jnp.maximum(jnp.abs(acc[d][:, hd:hd + 1]), lim[d]) for d in dd]
    return h, c_new, m_new


def _mlstm_kernel(cq, ck, cv, co, lq, lk, lv, lo, cgr, lgr, gh_ref,
                  yc_ref, yl_ref, hf_ref, hb_ref, c_ref, m_ref):
    c_ref[...] = jnp.zeros_like(c_ref)
    m_ref[...] = jnp.full_like(m_ref, NEG_BIG)

    def scan(q_ref, k_ref, v_ref, o_ref, gr_ref, y_ref):
        lc = _ml_chunk_len(q_ref.shape[0])
        nc = q_ref.shape[0] // lc
        rows = lax.broadcasted_iota(jnp.int32, (lc, lc), 0)
        cols = lax.broadcasted_iota(jnp.int32, (lc, lc), 1)
        tri = (cols <= rows, cols >= rows)
        ones_blk = jnp.where(lax.broadcasted_iota(jnp.int32, (lc, LANES), 1) == 0, 1.0, 0.0).astype(BF16)

        def body(c, carry):
            sl = (pl.ds(pl.multiple_of(c * lc, lc), lc), pl.ds(pl.multiple_of((nc - 1 - c) * lc, lc), lc))
            dd = range(2)
            gc = [jnp.transpose(gr_ref[:, sl[d]]) for d in dd]
            h, c_new, m_new = _ml_chunk(
                [q_ref[sl[d], :] for d in dd], [k_ref[sl[d], :] for d in dd],
                [jnp.concatenate([v_ref[sl[d], :], ones_blk], axis=1) for d in dd],
                [gc[d][:, 2 * d:2 * d + 1] for d in dd], [gc[d][:, 2 * d + 1:2 * d + 2] for d in dd],
                [gc[d][:, 4 + d:5 + d] for d in dd], [gr_ref[2 * d:2 * d + 1, sl[d]] for d in dd],
                [c_ref[d] for d in dd], [m_ref[d] for d in dd], tri, (lc - 1, 0))
            for d, out_ref in enumerate((hf_ref, hb_ref)):
                c_ref[d] = c_new[d]
                m_ref[d] = m_new[d]
                out_ref[sl[d], :] = h[d]
            return carry

        lax.fori_loop(0, nc, body, 0)

        lm = ML_MERGE_ROWS

        def merge(c, carry):
            sl = pl.ds(pl.multiple_of(c * lm, lm), lm)
            hs = hf_ref[sl, :] + hb_ref[sl, :]
            hs = hs * lax.rsqrt(jnp.mean(hs * hs, axis=-1, keepdims=True) + EPS) * gh_ref[...]
            y_ref[sl, :] = (jax.nn.sigmoid(o_ref[sl, :].astype(F32)) * hs).astype(y_ref.dtype)
            return carry

        lax.fori_loop(0, q_ref.shape[0] // lm, merge, 0)

    scan(cq, ck, cv, co, cgr, yc_ref)
    scan(lq, lk, lv, lo, lgr, yl_ref)


def _mlstm_scan(zc, zl, gc, gl, g_head):
    bsz, n, d4 = zl.shape
    lctx = zc.shape[1]
    d = d4 // 4
    nh = ML_HEADS
    hd = d // nh
    assert n % _ml_chunk_len(n) == 0 and n % ML_MERGE_ROWS == 0 and lctx % ML_MERGE_ROWS == 0

    def zspec(length, k):
        return pl.BlockSpec((None, length, hd), lambda b, h: (b, 0, k * nh + h))

    def gspec(length):
        return pl.BlockSpec((None, ML_GATE_SLOTS, length), lambda b, h: (b, h, 0))

    lmax = max(n, lctx)
    return pl.pallas_call(
        _mlstm_kernel,
        out_shape=(jax.ShapeDtypeStruct((bsz, lctx, d), BF16),
                   jax.ShapeDtypeStruct((bsz, n, d), BF16)),
        grid=(bsz, nh),
        in_specs=([zspec(lctx, k) for k in range(4)] + [zspec(n, k) for k in range(4)]
                  + [gspec(lctx), gspec(n)]
                  + [pl.BlockSpec((None, 1, hd), lambda b, h: (h, 0, 0))]),
        out_specs=(pl.BlockSpec((None, lctx, hd), lambda b, h: (b, 0, h)),
                   pl.BlockSpec((None, n, hd), lambda b, h: (b, 0, h))),
        scratch_shapes=[pltpu.VMEM((lmax, hd), F32), pltpu.VMEM((lmax, hd), F32),
                        pltpu.VMEM((2, hd, hd + LANES), F32), pltpu.VMEM((2, 1, 1), F32)],
        compiler_params=_cparams("parallel", "parallel"),
        name="mlstm_scan",
    )(zc, zc, zc, zc, zl, zl, zl, zl, gc, gl, g_head.reshape(nh, 1, hd))


def _mlstm_mixer(h, hc, g, mod, mod_c, w_in, b_gate, g_head, w_out, with_ctx):
    bsz, n, d = h.shape
    lctx = hc.shape[1] // bsz
    w_main = w_in[:, :4 * d].astype(BF16)
    nh = ML_HEADS
    ngp = -(-nh * ML_GATE_SLOTS // LANES) * LANES

    def head_major(t):
        t = jnp.swapaxes(t.reshape(t.shape[:-1] + (4, nh)), -1, -2)
        t = jnp.pad(t, [(0, 0)] * (t.ndim - 1) + [(0, ML_GATE_SLOTS - 4)])
        t = t.reshape(t.shape[:-2] + (nh * ML_GATE_SLOTS,))
        return jnp.pad(t, [(0, 0)] * (t.ndim - 1) + [(0, ngp - nh * ML_GATE_SLOTS)])

    w_gate = head_major(w_in[:, 4 * d:]).astype(BF16)
    bias = head_major(b_gate.reshape(1, 4 * nh))
    q_scale = (d // ML_HEADS) ** -0.5
    zl, gl = _proj(h, g, mod, 3, w_main, BF16, gates=(w_gate, bias), scale=q_scale, scale_cols=d)
    zc, gc = _proj(hc, g, mod_c, 3, w_main, BF16, gates=(w_gate, bias), scale=q_scale, scale_cols=d)
    yc, yl = _mlstm_scan(zc.reshape(bsz, lctx, 4 * d), zl,
                         _ml_gates(gc.reshape(bsz, lctx, ngp)), _ml_gates(gl), g_head)
    w_o = w_out.astype(BF16)
    h = _out_proj(yl, w_o, h, mod, 5)
    if with_ctx:
        hc = _out_proj(yc.reshape(1, bsz * lctx, d), w_o, hc, mod_c, 5)
    return h, hc


def _wa_kernel(sink_ref, q_ref, *refs, local, scale):
    if local:
        k_ref, v_ref, kc_ref, vc_ref, o_ref = refs
    else:
        kc_ref, vc_ref, o_ref = refs
    tq = q_ref.shape[0]
    hd = kc_ref.shape[1] // WA_KV_HEADS
    grp = WA_Q_HEADS // WA_KV_HEADS
    if local:
        n = k_ref.shape[0]
        span = 3 * WA_BLOCK
        start = pl.program_id(1) * tq
        ks = pl.multiple_of(jnp.clip(start - WA_BLOCK, 0, n - span), WA_BLOCK)
        q_pos = start + (lax.broadcasted_iota(jnp.int32, (grp * tq, span), 0) & (tq - 1))
        k_pos = ks + lax.broadcasted_iota(jnp.int32, (grp * tq, span), 1)
        ok = jnp.abs(q_pos - k_pos) <= WA_WINDOW
    dims = (((1,), (1,)), ((), ()))
    heads = range(WA_KV_HEADS)
    c2 = scale * LOG2E
    cs = [slice(kv * hd, (kv + 1) * hd) for kv in heads]
    q4 = [jnp.concatenate([q_ref[:, (kv * grp + j) * hd:(kv * grp + j + 1) * hd] for j in range(grp)], axis=0)
          for kv in heads]
    sink = [jnp.concatenate([jnp.full((tq, 1), sink_ref[0, kv * grp + j], F32) for j in range(grp)], axis=0)
            * (1.0 / scale) for kv in heads]
    s_ctx = [lax.dot_general(q4[kv], kc_ref[:, cs[kv]], dims, preferred_element_type=F32) for kv in heads]
    m = [jnp.maximum(jnp.max(s_ctx[kv], axis=1, keepdims=True), sink[kv]) for kv in heads]
    if local:
        s_loc = [jnp.where(ok, lax.dot_general(q4[kv], k_ref[pl.ds(ks, span), cs[kv]], dims,
                                               preferred_element_type=F32), NEG_BIG) for kv in heads]
        m = [jnp.maximum(m[kv], jnp.max(s_loc[kv], axis=1, keepdims=True)) for kv in heads]
        p_loc = [jnp.exp2((s_loc[kv] - m[kv]) * c2) for kv in heads]
    p_ctx = [jnp.exp2((s_ctx[kv] - m[kv]) * c2) for kv in heads]
    den = [jnp.sum(p_ctx[kv], axis=1, keepdims=True) + jnp.exp2((sink[kv] - m[kv]) * c2) for kv in heads]
    if local:
        den = [den[kv] + jnp.sum(p_loc[kv], axis=1, keepdims=True) for kv in heads]
    inv = [1.0 / den[kv] for kv in heads]
    out = [jnp.dot((p_ctx[kv] * inv[kv]).astype(BF16), vc_ref[:, cs[kv]], preferred_element_type=F32)
           for kv in heads]
    if local:
        out = [out[kv] + jnp.dot((p_loc[kv] * inv[kv]).astype(BF16), v_ref[pl.ds(ks, span), cs[kv]],
                                 preferred_element_type=F32) for kv in heads]
    for kv in heads:
        for j in range(grp):
            hq = kv * grp + j
            o_ref[:, hq * hd:(hq + 1) * hd] = out[kv][j * tq:(j + 1) * tq].astype(o_ref.dtype)


def _wa_attention(sink, zq, zkv, zc, local):
    bsz, lq, _ = zq.shape
    lctx = zc.shape[1]
    hd = zq.shape[2] // (WA_Q_HEADS + 2 * WA_KV_HEADS)
    qd, kd = WA_Q_HEADS * hd, WA_KV_HEADS * hd
    tq = WA_BLOCK
    kblk = qd // kd
    in_specs = [pl.BlockSpec(memory_space=pltpu.SMEM),
                pl.BlockSpec((None, tq, qd), lambda b, i: (b, i, 0))]
    args = [sink.reshape(1, WA_Q_HEADS), zq]
    if local:
        n = zkv.shape[1]
        in_specs += [pl.BlockSpec((None, n, kd), lambda b, i: (b, 0, kblk)),
                     pl.BlockSpec((None, n, kd), lambda b, i: (b, 0, kblk + 1))]
        args += [zkv, zkv]
    in_specs += [pl.BlockSpec((None, lctx, kd), lambda b, i: (b, 0, kblk)),
                 pl.BlockSpec((None, lctx, kd), lambda b, i: (b, 0, kblk + 1))]
    args += [zc, zc]
    return pl.pallas_call(
        functools.partial(_wa_kernel, local=local, scale=hd ** -0.5),
        out_shape=jax.ShapeDtypeStruct((bsz, lq, qd), BF16),
        grid=(bsz, lq // tq),
        in_specs=in_specs,
        out_specs=pl.BlockSpec((None, tq, qd), lambda b, i: (b, i, 0)),
        compiler_params=_cparams("parallel", "parallel"),
        name="wa_attention",
    )(*args)


def _rope_tables(n, hd):
    rows = n // GRID_W
    row = jnp.repeat(jnp.arange(rows, dtype=F32), GRID_W)
    col = jnp.tile(jnp.arange(GRID_W, dtype=F32), rows)
    n_freq = hd // 4
    inv = ROPE_BASE ** (-jnp.arange(n_freq, dtype=F32) / n_freq)
    ang = jnp.concatenate([row[:, None] * inv, col[:, None] * inv], axis=-1)
    cos, sin = jnp.cos(ang), jnp.sin(ang)
    return jnp.concatenate([cos, cos], axis=-1), jnp.concatenate([-sin, sin], axis=-1)


def _wa_mixer(h, hc, g, mod, mod_c, w_in, sink, w_out, with_ctx):
    bsz, n, d = h.shape
    lctx = hc.shape[1] // bsz
    hd = d // WA_Q_HEADS
    w = w_in.astype(BF16)
    rope_cols = (WA_Q_HEADS + WA_KV_HEADS) * hd
    zl = _proj(h, g, mod, 3, w, BF16, rope=_rope_tables(n, hd), rope_cols=rope_cols)
    zc = _proj(hc, g, mod_c, 3, w, BF16).reshape(bsz, lctx, -1)
    sink = sink.astype(F32)
    w_o = w_out.astype(BF16)
    h = _out_proj(_wa_attention(sink, zl, zl, zc, True), w_o, h, mod, 5)
    if with_ctx:
        yc = _wa_attention(sink, zc, None, zc, False)
        hc = _out_proj(yc.reshape(1, bsz * lctx, d), w_o, hc, mod_c, 5)
    return h, hc


def _s5_proj_kernel(x_ref, g_ref, sh_ref, sc_ref, w_ref, of_ref, hn_ref):
    n = pl.program_id(2)

    @pl.when(n == 0)
    def _():
        _adaln_rows(hn_ref, x_ref, g_ref, sh_ref, sc_ref)

    hn = hn_ref[...]
    tm, tn = hn.shape[0], w_ref.shape[1]
    sub = _sub_cols(tn)
    for c in range(tn // sub):
        acc = jnp.dot(hn, w_ref[:, c * sub:(c + 1) * sub], preferred_element_type=F32)
        for k in range(sub // LANES):
            blk = acc[:, k * LANES:(k + 1) * LANES]
            of_ref[c * (sub // LANES) + k] = blk.reshape(tm // S5_CHUNK, S5_CHUNK, LANES)


def _s5_proj(x, g, mod, k0, w, shared_mod):
    bsz, rows, d = x.shape
    tm = _row_tile(rows, ROW_TILE_STREAM)
    tn = _col_tile(d, 1024)
    lc = S5_CHUNK
    return pl.pallas_call(
        _s5_proj_kernel,
        out_shape=jax.ShapeDtypeStruct((d // LANES, rows // lc, bsz, lc, LANES), F32),
        grid=(bsz, rows // tm, d // tn),
        in_specs=[pl.BlockSpec((None, tm, d), lambda b, i, n: (b, i, 0)),
                  pl.BlockSpec((1, d), lambda b, i, n: (0, 0)),
                  _mod_spec(k0, d, shared_mod), _mod_spec(k0 + 1, d, shared_mod),
                  pl.BlockSpec((d, tn), lambda b, i, n: (0, n))],
        out_specs=pl.BlockSpec((tn // LANES, tm // lc, None, lc, LANES), lambda b, i, n: (n, i, b, 0, 0)),
        scratch_shapes=[pltpu.VMEM((tm, d), BF16)],
        compiler_params=_cparams("parallel", "parallel", "arbitrary"),
        name="s5_proj",
    )(x, g.reshape(1, d), mod, mod, w)


S5_PACKETS = LANES // S5_GROUP
S5_REGROUP_ROWS = 16


def _packet_transpose(vs):
    lane = lax.broadcasted_iota(jnp.int32, vs[0].shape, 1)
    d = S5_PACKETS // 2
    while d:
        low = (lane & (d * S5_GROUP)) == 0
        nxt = list(vs)
        for i in range(S5_PACKETS):
            if not i & d:
                nxt[i] = jnp.where(low, vs[i], pltpu.roll(vs[i + d], d * S5_GROUP, axis=1))
                nxt[i + d] = jnp.where(low, pltpu.roll(vs[i], LANES - d * S5_GROUP, axis=1), vs[i + d])
        vs = nxt
        d //= 2
    return vs


def _s5_group_kernel(u_ref, o_ref):
    def body(r, carry):
        rows = pl.ds(pl.multiple_of(r * S5_REGROUP_ROWS, S5_REGROUP_ROWS), S5_REGROUP_ROWS)
        for hf in range(S5_CHUNK // S5_PACKETS):
            base = hf * S5_PACKETS
            vs = [pltpu.bitcast(u_ref[rows, (base + t) * LANES:(base + t + 1) * LANES].astype(o_ref.dtype),
                                jnp.uint32) for t in range(S5_PACKETS)]
            for gq, v in enumerate(_packet_transpose(vs)):
                o_ref[gq, rows, hf * LANES:(hf + 1) * LANES] = pltpu.bitcast(v, o_ref.dtype)
        return carry

    lax.fori_loop(0, u_ref.shape[0] // S5_REGROUP_ROWS, body, 0, unroll=4)


def _s5_group(u):
    g8, rows, wide = u.shape
    tr = _row_tile(rows, ROW_TILE)
    w = wide // S5_PACKETS
    return pl.pallas_call(
        _s5_group_kernel,
        out_shape=jax.ShapeDtypeStruct((g8 * S5_PACKETS, rows, w), BF16),
        grid=(g8, rows // tr),
        in_specs=[pl.BlockSpec((None, tr, wide), lambda g, i: (g, i, 0))],
        out_specs=pl.BlockSpec((S5_PACKETS, tr, w), lambda g, i: (g, i, 0)),
        compiler_params=_cparams("parallel", "parallel"),
        name="s5_group",
    )(u)


def _s5_ungroup_kernel(y_ref, u_ref, ds_ref, o_ref):
    def body(r, carry):
        rows = pl.ds(pl.multiple_of(r * S5_REGROUP_ROWS, S5_REGROUP_ROWS), S5_REGROUP_ROWS)
        for hf in range(S5_CHUNK // S5_PACKETS):
            base = hf * S5_PACKETS
            vs = [pltpu.bitcast(y_ref[gq, rows, hf * LANES:(hf + 1) * LANES], jnp.uint32)
                  for gq in range(S5_PACKETS)]
            for t, v in enumerate(_packet_transpose(vs)):
                cols = slice((base + t) * LANES, (base + t + 1) * LANES)
                y = pltpu.bitcast(v, F32)
                o_ref[rows, cols] = jax.nn.gelu(y + ds_ref[...] * u_ref[rows, cols]).astype(o_ref.dtype)
        return carry

    lax.fori_loop(0, o_ref.shape[0] // S5_REGROUP_ROWS, body, 0, unroll=2)


def _s5_ungroup(y, u, d_skip):
    ng, rows, w = y.shape
    tr = _row_tile(rows, ROW_TILE)
    wide = w * S5_PACKETS
    wide_spec = pl.BlockSpec((None, tr, wide), lambda g, i: (g, i, 0))
    return pl.pallas_call(
        _s5_ungroup_kernel,
        out_shape=jax.ShapeDtypeStruct((ng // S5_PACKETS, rows, wide), BF16),
        grid=(ng // S5_PACKETS, rows // tr),
        in_specs=[pl.BlockSpec((S5_PACKETS, tr, w), lambda g, i: (g, i, 0)), wide_spec,
                  pl.BlockSpec((None, 1, LANES), lambda g, i: (g, 0, 0))],
        out_specs=wide_spec,
        compiler_params=_cparams("parallel", "parallel"),
        name="s5_ungroup",
    )(y, u, d_skip)


S5_SCAN_GROUPS = 2


def _s5_kernel(uc_ref, ul_ref, kl_ref, q_ref, p_ref, a_ref, yc_ref, yl_ref,
               sc_ref, sl_ref, xc_ref, xl_ref, t_ref, *, bsz):
    gs = range(uc_ref.shape[0])
    dd = range(2)
    half = sc_ref.shape[3] // 2
    nch = kl_ref.shape[2]
    for ti in range(S5_CHUNK):
        for to in range(S5_CHUNK):
            for g in gs:
                t_ref[g, ti * nch:(ti + 1) * nch, to * nch:(to + 1) * nch] = kl_ref[g, to - ti + S5_CHUNK - 1]
    for u_ref, s_ref in ((uc_ref, sc_ref), (ul_ref, sl_ref)):
        for g in gs:
            for d in dd:
                s_ref[g, d] = jnp.dot(u_ref[g], q_ref[d, g], preferred_element_type=F32)
    aa = [[a_ref[d, g, 0:1, :] for d in dd] for g in gs]
    ab = [[a_ref[d, g, 1:2, :] for d in dd] for g in gs]

    def run(s_ref, x_ref, carry):
        n = s_ref.shape[2] // bsz

        def step(g, d, j, w):
            rows = pl.ds(pl.multiple_of(j * bsz, bsz), bsz)
            x_ref[g, d, rows, :] = w[:, :half]
            other = jnp.concatenate([w[:, half:], w[:, :half]], axis=1)
            return aa[g][d] * w + ab[g][d] * other + s_ref[g, d, rows, :]

        def body(i, ws):
            return tuple((step(g, 0, i, ws[g][0]), step(g, 1, n - 1 - i, ws[g][1])) for g in gs)

        return lax.fori_loop(0, n, body, carry)

    zero = jnp.zeros((bsz, 2 * half), F32)
    run(sl_ref, xl_ref, run(sc_ref, xc_ref, tuple((zero, zero) for _ in gs)))
    t_m = [t_ref[g].astype(BF16) for g in gs]
    for u_ref, x_ref, y_ref in ((uc_ref, xc_ref, yc_ref), (ul_ref, xl_ref, yl_ref)):
        for g in gs:
            y = jnp.dot(u_ref[g], t_m[g], preferred_element_type=F32)
            for d in dd:
                y = y + lax.dot_general(x_ref[g, d].astype(BF16), p_ref[d, g], (((1,), (1,)), ((), ())),
                                        preferred_element_type=F32)
            y_ref[g] = y


def _s5_scan(uc, ul, k_lag, q_m, p_m, a_m, bsz):
    ng, rc, w = uc.shape
    rl = ul.shape[1]
    st2 = p_m.shape[3]
    nlag, nch = k_lag.shape[1:3]
    gb = S5_SCAN_GROUPS if ng % S5_SCAN_GROUPS == 0 else 1

    def rows_spec(r):
        return pl.BlockSpec((gb, r, w), lambda g: (g, 0, 0))

    return pl.pallas_call(
        functools.partial(_s5_kernel, bsz=bsz),
        out_shape=(jax.ShapeDtypeStruct((ng, rc, w), F32), jax.ShapeDtypeStruct((ng, rl, w), F32)),
        grid=(ng // gb,),
        in_specs=[rows_spec(rc), rows_spec(rl),
                  pl.BlockSpec((gb, nlag, nch, nch), lambda g: (g, 0, 0, 0)),
                  pl.BlockSpec((2, gb, w, 2 * st2), lambda g: (0, g, 0, 0)),
                  pl.BlockSpec((2, gb, w, st2), lambda g: (0, g, 0, 0)),
                  pl.BlockSpec((2, gb, 2, 2 * st2), lambda g: (0, g, 0, 0))],
        out_specs=(rows_spec(rc), rows_spec(rl)),
        scratch_shapes=[pltpu.VMEM((gb, 2, rc, 2 * st2), F32), pltpu.VMEM((gb, 2, rl, 2 * st2), F32),
                        pltpu.VMEM((gb, 2, rc, st2), F32), pltpu.VMEM((gb, 2, rl, st2), F32),
                        pltpu.VMEM((gb, w, w), F32)],
        compiler_params=_cparams("parallel"),
        name="s5_scan",
    )(uc, ul, k_lag, q_m, p_m, a_m)


def _s5_operators(lam_re, lam_im, log_dt, b_re, b_im, c_re, c_im):
    lc = S5_CHUNK
    dt = jnp.exp(log_dt)[..., None]
    mag = jnp.exp(lam_re * dt)
    lb_re, lb_im = mag * jnp.cos(lam_im * dt), mag * jnp.sin(lam_im * dt)
    den = lam_re * lam_re + lam_im * lam_im
    nr, ni = lb_re - 1.0, lb_im
    fr = (nr * lam_re + ni * lam_im) / den
    fi = (ni * lam_re - nr * lam_im) / den
    bb_re = fr[..., None] * b_re - fi[..., None] * b_im
    bb_im = fr[..., None] * b_im + fi[..., None] * b_re
    k = jnp.arange(lc + 1, dtype=F32)[:, None, None, None]
    pmag = jnp.exp(k * (lam_re * dt))
    pw_re, pw_im = pmag * jnp.cos(k * (lam_im * dt)), pmag * jnp.sin(k * (lam_im * dt))
    ngrp, nch = lam_re.shape[1], b_re.shape[3]
    nst = lam_re.shape[2]
    pw_t = (jnp.transpose(pw_re, (1, 2, 0, 3)), jnp.transpose(pw_im, (1, 2, 0, 3)))
    bb_t = (jnp.transpose(bb_re, (0, 1, 3, 2)), jnp.transpose(bb_im, (0, 1, 3, 2)))

    lb_re = pw_t[0][:, :, :, None, :] * bb_t[0][:, :, None] - pw_t[1][:, :, :, None, :] * bb_t[1][:, :, None]
    lb_im = pw_t[0][:, :, :, None, :] * bb_t[1][:, :, None] + pw_t[1][:, :, :, None, :] * bb_t[0][:, :, None]
    kern = jnp.sum(lb_re[:, :, :, :, None, :] * c_re[:, :, None, None]
                   - lb_im[:, :, :, :, None, :] * c_im[:, :, None, None], axis=-1)
    k_lag = jnp.concatenate([jnp.flip(kern[1, :, 1:lc], axis=1), (kern[0, :, 0] + kern[1, :, 0])[:, None],
                             kern[0, :, 1:lc]], axis=1)

    def per_dir(d):
        e_in = (lc - 1 - jnp.arange(lc)) if d == 0 else jnp.arange(lc)
        l_re, l_im = (pw[d][:, e_in][:, :, None, :] for pw in pw_t)
        bt_re, bt_im = (bb[d][:, None] for bb in bb_t)
        q_re = (l_re * bt_re - l_im * bt_im).reshape(ngrp, lc * nch, nst)
        q_im = (l_re * bt_im + l_im * bt_re).reshape(ngrp, lc * nch, nst)
        q_m = jnp.concatenate([q_re, q_im, q_im, q_re], axis=-1)
        e_out = (jnp.arange(lc) + 1) if d == 0 else (lc - jnp.arange(lc))
        l_re, l_im = (pw[d][:, e_out][:, :, None, :] for pw in pw_t)
        cd_re, cd_im = c_re[d][:, None], c_im[d][:, None]
        p_re = (cd_re * l_re - cd_im * l_im).reshape(ngrp, lc * nch, nst)
        p_im = (cd_re * l_im + cd_im * l_re).reshape(ngrp, lc * nch, nst)
        p_m = jnp.concatenate([p_re, -p_im], axis=-1)
        a_re, a_im = pw_re[lc, d], pw_im[lc, d]
        a_m = jnp.stack([jnp.concatenate([a_re] * 4, axis=-1),
                         jnp.concatenate([-a_im, a_im, a_im, -a_im], axis=-1)], axis=1)
        return q_m, p_m, a_m

    q_m, p_m, a_m = (jnp.stack([x, y]) for x, y in zip(per_dir(0), per_dir(1)))
    return k_lag, q_m.astype(BF16), p_m.astype(BF16), a_m


def _s5_glu_kernel(zf_ref, wa_ref, wg_ref, h_ref, gate_ref, o_ref, z_ref):
    n = pl.program_id(2)

    @pl.when(n == 0)
    def _():
        tm = z_ref.shape[0]
        for k in range(zf_ref.shape[0]):
            z_ref[:, k * LANES:(k + 1) * LANES] = zf_ref[k].reshape(tm, LANES)

    z = z_ref[...]
    tn = wa_ref.shape[1]
    sub = _sub_cols(tn)
    for c in range(tn // sub):
        cols = slice(c * sub, (c + 1) * sub)
        a = jnp.dot(z, wa_ref[:, cols], preferred_element_type=F32)
        gt = jnp.dot(z, wg_ref[:, cols], preferred_element_type=F32)
        o_ref[:, cols] = h_ref[:, cols] + gate_ref[:, cols] * (a * jax.nn.sigmoid(gt))


def _s5_glu(z, w, h, mod, k, shared_mod):
    bsz, rows, d = h.shape
    tm = _row_tile(rows, ROW_TILE_STREAM)
    tn = _col_tile(d, 512)
    nn = d // tn
    lc = S5_CHUNK
    col = pl.BlockSpec((None, tm, tn), lambda b, i, n: (b, i, n))
    return pl.pallas_call(
        _s5_glu_kernel,
        out_shape=jax.ShapeDtypeStruct(h.shape, F32),
        grid=(bsz, rows // tm, nn),
        in_specs=[pl.BlockSpec((d // LANES, tm // lc, None, lc, LANES), lambda b, i, n: (0, i, b, 0, 0)),
                  pl.BlockSpec((d, tn), lambda b, i, n: (0, n)),
                  pl.BlockSpec((d, tn), lambda b, i, n: (0, n + nn)),
                  col, pl.BlockSpec((None, None, 1, tn),
                                    lambda b, i, n: (0 if shared_mod else b, k, 0, n))],
        out_specs=col,
        scratch_shapes=[pltpu.VMEM((tm, d), BF16)],
        compiler_params=_cparams("parallel", "parallel", "arbitrary"),
        name="s5_glu",
    )(z, w, w, h, mod)


def _s5_mixer(h, hc, g, mod, mod_c, w_in, lam_re, lam_im, log_dt, b_re, b_im, c_re, c_im,
              d_skip, w_out, with_ctx):
    bsz, n, d = h.shape
    lctx = hc.shape[1] // bsz
    lc = S5_CHUNK
    g8 = d // LANES
    w = w_in.astype(BF16)
    hc3 = hc.reshape(bsz, lctx, d)
    u_l = _s5_proj(h, g, mod, 3, w, False).reshape(g8, n // lc * bsz, lc * LANES)
    u_c = _s5_proj(hc3, g, mod_c, 3, w, True).reshape(g8, lctx // lc * bsz, lc * LANES)
    ops = _s5_operators(lam_re, lam_im, log_dt, b_re, b_im, c_re, c_im)
    yp_c, yp_l = _s5_scan(_s5_group(u_c), _s5_group(u_l), *ops, bsz=bsz)
    ds = d_skip.astype(F32).reshape(g8, 1, LANES)
    w_o = w_out.astype(BF16)
    z_l = _s5_ungroup(yp_l, u_l, ds).reshape(g8, n // lc, bsz, lc, LANES)
    h = _s5_glu(z_l, w_o, h, mod, 5, False)
    if with_ctx:
        z_c = _s5_ungroup(yp_c, u_c, ds).reshape(g8, lctx // lc, bsz, lc, LANES)
        hc = _s5_glu(z_c, w_o, hc3, mod_c, 5, True).reshape(1, bsz * lctx, d)
    return h, hc


def _rmsnorm_kernel(x_ref, g_ref, o_ref):
    x = x_ref[...]
    o_ref[...] = x * lax.rsqrt(jnp.mean(x * x, axis=-1, keepdims=True) + EPS) * g_ref[...]


def _final_norm(x, g):
    bsz, rows, d = x.shape
    tm = _row_tile(rows, ROW_TILE)
    row = pl.BlockSpec((None, tm, d), lambda b, i: (b, i, 0))
    return pl.pallas_call(
        _rmsnorm_kernel,
        out_shape=jax.ShapeDtypeStruct(x.shape, F32),
        grid=(bsz, rows // tm),
        in_specs=[row, pl.BlockSpec((1, d), lambda b, i: (0, 0))],
        out_specs=row,
        compiler_params=_cparams("parallel", "parallel"),
        name="final_norm",
    )(x, g.reshape(1, d))


def kernel(x, c, ctx, c_ctx, w_ada, b_ada, g_norm, w_ffn_in, w_ffn_out, g_final, ml_w_in, ml_b_gate, ml_g_head, ml_w_out, wa_w_in, wa_sink, wa_w_out, s5_w_in, s5_lam_re, s5_lam_im, s5_log_dt, s5_b_re, s5_b_im, s5_c_re, s5_c_im, s5_d_skip, s5_w_out):
    bsz, n, d = x.shape
    depth = w_ada.shape[0]
    lctx = ctx.shape[1]
    n_rows = -(-(bsz + 1) // 16) * 16
    c_rows = jnp.concatenate([c, c_ctx[None], jnp.zeros((n_rows - bsz - 1, d), F32)], axis=0)
    mods = _mod_table(c_rows, w_ada, b_ada).reshape(depth, n_rows, N_MOD, 1, d)
    w_in, w_out = _to_bf16(w_ffn_in), _to_bf16(w_ffn_out)
    h = x
    hc = ctx.reshape(1, bsz * lctx, d)
    for layer in range(depth):
        has_next = layer < depth - 1
        mod, mod_c = mods[layer, :bsz], mods[layer, bsz:bsz + 1]
        g = g_norm[layer]
        h = _ffn(h, g[0], mod, 0, w_in, w_out, layer, 0)
        hc = _ffn(hc, g[0], mod_c, 0, w_in, w_out, layer, 0)
        kind, idx = layer % 3, layer // 3
        if kind == 0:
            h, hc = _mlstm_mixer(h, hc, g[1], mod, mod_c, ml_w_in[idx], ml_b_gate[idx],
                                 ml_g_head[idx], ml_w_out[idx], has_next)
        elif kind == 1:
            h, hc = _wa_mixer(h, hc, g[1], mod, mod_c, wa_w_in[idx], wa_sink[idx],
                              wa_w_out[idx], has_next)
        else:
            h, hc = _s5_mixer(h, hc, g[1], mod, mod_c, s5_w_in[idx], s5_lam_re[idx], s5_lam_im[idx],
                              s5_log_dt[idx], s5_b_re[idx], s5_b_im[idx], s5_c_re[idx],
                              s5_c_im[idx], s5_d_skip[idx], s5_w_out[idx], has_next)
        h = _ffn(h, g[2], mod, 6, w_in, w_out, layer, 1)
        if has_next:
            hc = _ffn(hc, g[2], mod_c, 6, w_in, w_out, layer, 1)
    return _final_norm(h, g_final)
```

```python
import functools
import math

import jax
import jax.numpy as jnp
from jax import lax
from jax.experimental import pallas as pl
from jax.experimental.pallas import tpu as pltpu

F32 = jnp.float32
BF16 = jnp.bfloat16

EPS = 1e-6
NEG_BIG = -1e30
LOG2E = math.log2(math.e)
N_MOD = 9
ML_HEADS = 8
ML_CHUNK = 512
ML_MERGE_ROWS = 256
WA_Q_HEADS = 16
WA_KV_HEADS = 4
WA_WINDOW = 128
WA_BLOCK = 128
GRID_W = 64
ROPE_BASE = 10000.0
S5_GROUP = 16
S5_STATE = 64
S5_CHUNK = 16

LANES = 128
MXU_COLS = 256
ROW_TILE = 512
ROW_TILE_STREAM = 1024
CAST_BLOCK_BYTES = 8 * 1024 * 1024
VMEM_LIMIT = 56 * 1024 * 1024


def _cparams(*sem):
    return pltpu.CompilerParams(dimension_semantics=sem, vmem_limit_bytes=VMEM_LIMIT)


def _row_tile(rows, want):
    return want if rows % want == 0 else rows


def _sub_cols(cols):
    return MXU_COLS if cols % MXU_COLS == 0 else cols


def _col_tile(cols, want):
    t = min(want, cols)
    while cols % t:
        t -= LANES
    return t


def _adaln(x, g, shift, scale):
    var = jnp.mean(x * x, axis=-1, keepdims=True)
    return (x * lax.rsqrt(var + EPS) * g) * (1.0 + scale) + shift


ADALN_ROWS = 128


def _adaln_rows(hn_ref, x_ref, g_ref, sh_ref, sc_ref):
    rows = x_ref.shape[0]
    step = ADALN_ROWS if rows % ADALN_ROWS == 0 else rows

    def body(r, carry):
        sl = pl.ds(pl.multiple_of(r * step, step), step)
        hn_ref[sl, :] = _adaln(x_ref[sl, :], g_ref[...], sh_ref[...], sc_ref[...]).astype(BF16)
        return carry

    lax.fori_loop(0, rows // step, body, 0)


def _mod_spec(k, d, shared=False):
    return pl.BlockSpec((None, None, 1, d), lambda b, *_: (0 if shared else b, k, 0, 0))


def _mod_kernel(c_ref, w_ref, b_ref, o_ref):
    c = c_ref[...]
    s = (c * jax.nn.sigmoid(c)).astype(BF16)
    o_ref[...] = jnp.dot(s, w_ref[...].astype(BF16), preferred_element_type=F32) + b_ref[...]


def _mod_table(c_rows, w_ada, b_ada):
    depth, d, nd = w_ada.shape
    r = c_rows.shape[0]
    tn = _col_tile(nd, 1024)
    return pl.pallas_call(
        _mod_kernel,
        out_shape=jax.ShapeDtypeStruct((depth, r, nd), F32),
        grid=(depth, nd // tn),
        in_specs=[pl.BlockSpec((r, d), lambda l, n: (0, 0)),
                  pl.BlockSpec((None, d, tn), lambda l, n: (l, 0, n)),
                  pl.BlockSpec((None, 1, tn), lambda l, n: (l, 0, n))],
        out_specs=pl.BlockSpec((None, r, tn), lambda l, n: (l, 0, n)),
        compiler_params=_cparams("parallel", "parallel"),
        name="mod_table",
    )(c_rows, w_ada, b_ada.reshape(depth, 1, nd))


def _adaln_tile_kernel(x_ref, g_ref, sh_ref, sc_ref, o_ref):
    _adaln_rows(o_ref, x_ref, g_ref, sh_ref, sc_ref)


def _adaln_tile(x, g, mod, k0, tm):
    d = x.shape[2]
    return pl.pallas_call(
        _adaln_tile_kernel,
        out_shape=jax.ShapeDtypeStruct((tm, d), BF16),
        grid=(1,),
        in_specs=[pl.BlockSpec((None, tm, d), lambda b: (0, 0, 0)),
                  pl.BlockSpec((1, d), lambda b: (0, 0)),
                  _mod_spec(k0, d), _mod_spec(k0 + 1, d)],
        out_specs=pl.BlockSpec((tm, d), lambda b: (0, 0)),
        compiler_params=_cparams("arbitrary"),
        name="adaln_tile",
    )(x, g, mod, mod)


def _ffn_kernel(hn0_ref, xr_ref, g_ref, sh_ref, sc_ref, x_ref, gate_ref, wa_ref, wg_ref, wo_ref, o_ref,
                hn_ref, act_ref, *, nf):
    j = pl.program_id(2)
    tf = wa_ref.shape[1]

    @pl.when((j == 0) & (pl.program_id(0) == 0) & (pl.program_id(1) == 0))
    def _():
        hn_ref[...] = hn0_ref[...]

    @pl.when(j < nf)
    def _():
        hn = hn_ref[...]
        sub = _sub_cols(tf)
        for c in range(tf // sub):
            a = jnp.dot(hn, wa_ref[:, c * sub:(c + 1) * sub], preferred_element_type=F32)
            gt = jnp.dot(hn, wg_ref[:, c * sub:(c + 1) * sub], preferred_element_type=F32)
            cols = pl.ds(pl.multiple_of(j * tf + c * sub, sub), sub)
            act_ref[:, cols] = (a * (gt * jax.nn.sigmoid(gt))).astype(BF16)

    @pl.when(j >= nf)
    def _():
        y = jnp.dot(act_ref[...], wo_ref[...], preferred_element_type=F32)
        o_ref[...] = x_ref[...] + (0.5 * gate_ref[...]) * y
        rp = xr_ref.shape[0]
        rows = pl.ds(pl.multiple_of((j - nf) * rp, rp), rp)
        hn_ref[rows, :] = _adaln(xr_ref[...], g_ref[...], sh_ref[...], sc_ref[...]).astype(BF16)


def _cast_kernel(x_ref, o_ref):
    o_ref[...] = x_ref[...].astype(o_ref.dtype)


def _to_bf16(w):
    shape = w.shape
    cols = shape[-1]
    w2 = w.reshape(-1, cols)
    tr = 1 << ((CAST_BLOCK_BYTES // (4 * cols)).bit_length() - 1)
    while w2.shape[0] % tr:
        tr //= 2
    blk = pl.BlockSpec((tr, cols), lambda i: (i, 0))
    out = pl.pallas_call(
        _cast_kernel,
        out_shape=jax.ShapeDtypeStruct(w2.shape, BF16),
        grid=(w2.shape[0] // tr,),
        in_specs=[blk],
        out_specs=blk,
        compiler_params=_cparams("parallel"),
        name="to_bf16",
    )(w2)
    return out.reshape(shape)


def _ffn(x, g, mod, k0, w_in, w_out, layer, half):
    bsz, rows, d = x.shape
    ff = w_out.shape[2]
    tm = _row_tile(rows, ROW_TILE_STREAM)
    tf = _col_tile(ff, 512)
    tn = _col_tile(d, 512)
    nf = ff // tf
    nt = rows // tm
    nd = d // tn
    steps = nf + nd
    rp = tm // nd
    assert rp * nd == tm

    def fill(j):
        return jnp.minimum(j, nf - 1)

    def drain(j):
        return jnp.maximum(j - nf, 0)

    def ahead(b, i, j):
        r = jnp.minimum(b * nt + i + (j >= nf).astype(jnp.int32), bsz * nt - 1)
        return r // nt, r % nt

    def mod_ahead(k):
        return pl.BlockSpec((None, None, 1, d), lambda b, i, j: (ahead(b, i, j)[0], k, 0, 0))

    return pl.pallas_call(
        functools.partial(_ffn_kernel, nf=nf),
        out_shape=jax.ShapeDtypeStruct(x.shape, F32),
        grid=(bsz, nt, steps),
        in_specs=[pl.BlockSpec((tm, d), lambda b, i, j: (0, 0)),
                  pl.BlockSpec((None, rp, d),
                               lambda b, i, j: (ahead(b, i, j)[0], ahead(b, i, j)[1] * nd + drain(j), 0)),
                  pl.BlockSpec((1, d), lambda b, i, j: (0, 0)),
                  mod_ahead(k0), mod_ahead(k0 + 1),
                  pl.BlockSpec((None, tm, tn), lambda b, i, j: (b, i, drain(j))),
                  pl.BlockSpec((None, None, 1, tn), lambda b, i, j: (b, k0 + 2, 0, drain(j))),
                  pl.BlockSpec((None, None, d, tf), lambda b, i, j: (layer, half, 0, fill(j))),
                  pl.BlockSpec((None, None, d, tf), lambda b, i, j: (layer, half, 0, fill(j) + nf)),
                  pl.BlockSpec((None, None, ff, tn), lambda b, i, j: (layer, half, 0, drain(j)))],
        out_specs=pl.BlockSpec((None, tm, tn), lambda b, i, j: (b, i, drain(j))),
        scratch_shapes=[pltpu.VMEM((tm, d), BF16), pltpu.VMEM((tm, ff), BF16)],
        compiler_params=_cparams("arbitrary", "arbitrary", "arbitrary"),
        name="ffn",
    )(_adaln_tile(x, g.reshape(1, d), mod, k0, tm), x, g.reshape(1, d), mod, mod, x, mod, w_in, w_in, w_out)


def _proj_kernel(*refs, n_rope, with_gates, n_scaled, scale):
    x_ref, g_ref, sh_ref, sc_ref, w_ref = refs[:5]
    rest = refs[5:]
    if n_rope:
        cs_ref, sn_ref = rest[:2]
        rest = rest[2:]
    if with_gates:
        wg_ref, bg_ref, o_ref, og_ref, hn_ref = rest
    else:
        o_ref, hn_ref = rest
    n = pl.program_id(2)

    @pl.when(n == 0)
    def _():
        _adaln_rows(hn_ref, x_ref, g_ref, sh_ref, sc_ref)
        if with_gates:
            og_ref[...] = jnp.dot(hn_ref[...], wg_ref[...], preferred_element_type=F32) + bg_ref[...]

    def columns(rope):
        hn = hn_ref[...]
        tn = w_ref.shape[1]
        sub = _sub_cols(tn)
        for c in range(tn // sub):
            acc = jnp.dot(hn, w_ref[:, c * sub:(c + 1) * sub], preferred_element_type=F32)
            if n_scaled:
                acc = acc * jnp.where(n < n_scaled, scale, 1.0)
            if not rope:
                o_ref[:, c * sub:(c + 1) * sub] = acc.astype(o_ref.dtype)
            else:
                cs, sn = cs_ref[...], sn_ref[...]
                hd = cs.shape[1]
                for h in range(sub // hd):
                    r = acc[:, h * hd:(h + 1) * hd]
                    r = r * cs + pltpu.roll(r, hd // 2, axis=1) * sn
                    o_ref[:, c * sub + h * hd:c * sub + (h + 1) * hd] = r.astype(o_ref.dtype)

    if not n_rope:
        columns(False)
    else:
        @pl.when(n >= n_rope)
        def _():
            columns(False)

        @pl.when(n < n_rope)
        def _():
            columns(True)


def _proj(x, g, mod, k0, w, out_dtype, rope=None, rope_cols=0, gates=None, scale=1.0, scale_cols=0):
    bsz, rows, d = x.shape
    nout = w.shape[1]
    tm = _row_tile(rows, ROW_TILE_STREAM)
    tn = _col_tile(math.gcd(math.gcd(nout, rope_cols), scale_cols), 1024)
    n_rope = rope_cols // tn
    row = pl.BlockSpec((None, tm, d), lambda b, i, n: (b, i, 0))
    in_specs = [row, pl.BlockSpec((1, d), lambda b, i, n: (0, 0)),
                _mod_spec(k0, d), _mod_spec(k0 + 1, d),
                pl.BlockSpec((d, tn), lambda b, i, n: (0, n))]
    args = [x, g.reshape(1, d), mod, mod, w]
    if n_rope:
        cs, sn = rope
        hd = cs.shape[1]
        in_specs += [pl.BlockSpec((tm, hd), lambda b, i, n: (i, 0))] * 2
        args += [cs, sn]
    out_shape = jax.ShapeDtypeStruct((bsz, rows, nout), out_dtype)
    out_specs = pl.BlockSpec((None, tm, tn), lambda b, i, n: (b, i, n))
    if gates is not None:
        wg, bg = gates
        ng = wg.shape[1]
        in_specs += [pl.BlockSpec((d, ng), lambda b, i, n: (0, 0)),
                     pl.BlockSpec((1, ng), lambda b, i, n: (0, 0))]
        args += [wg, bg]
        out_shape = (out_shape, jax.ShapeDtypeStruct((bsz, rows, ng), F32))
        out_specs = (out_specs, pl.BlockSpec((None, tm, ng), lambda b, i, n: (b, i, 0)))
    return pl.pallas_call(
        functools.partial(_proj_kernel, n_rope=n_rope, with_gates=gates is not None,
                          n_scaled=scale_cols // tn, scale=scale),
        out_shape=out_shape,
        grid=(bsz, rows // tm, nout // tn),
        in_specs=in_specs,
        out_specs=out_specs,
        scratch_shapes=[pltpu.VMEM((tm, d), BF16)],
        compiler_params=_cparams("parallel", "parallel", "arbitrary"),
        name="proj",
    )(*args)


def _out_proj_kernel(y_ref, w_ref, h_ref, gate_ref, o_ref):
    acc = jnp.dot(y_ref[...], w_ref[...], preferred_element_type=F32)
    o_ref[...] = h_ref[...] + gate_ref[...] * acc


def _out_proj(y, w, h, mod, k):
    bsz, rows, d = h.shape
    dk = y.shape[2]
    tm = _row_tile(rows, ROW_TILE)
    return pl.pallas_call(
        _out_proj_kernel,
        out_shape=jax.ShapeDtypeStruct(h.shape, F32),
        grid=(bsz, rows // tm),
        in_specs=[pl.BlockSpec((None, tm, dk), lambda b, i: (b, i, 0)),
                  pl.BlockSpec((dk, d), lambda b, i: (0, 0)),
                  pl.BlockSpec((None, tm, d), lambda b, i: (b, i, 0)),
                  _mod_spec(k, d)],
        out_specs=pl.BlockSpec((None, tm, d), lambda b, i: (b, i, 0)),
        compiler_params=_cparams("parallel", "parallel"),
        name="out_proj",
    )(y, w, h, mod)


def _ml_chunk_len(length):
    return min(ML_CHUNK, length)


ML_GATE_SLOTS = 8


def _ml_gate_kernel(g_ref, o_ref, *, lc):
    row = lax.broadcasted_iota(jnp.int32, (lc, lc), 0)
    col = lax.broadcasted_iota(jnp.int32, (lc, lc), 1)
    lower = (col <= row).astype(BF16)
    upper = (col >= row).astype(BF16)
    slot = lax.broadcasted_iota(jnp.int32, (lc, LANES), 1) & (ML_GATE_SLOTS - 1)
    rix = lax.broadcasted_iota(jnp.int32, (lc, LANES), 0)

    def cumsum(tri, parts):
        return sum(jnp.dot(tri, p, preferred_element_type=F32) for p in parts)

    for c in range(g_ref.shape[0] // lc):
        x = g_ref[c * lc:(c + 1) * lc, :]
        ls = jax.nn.log_sigmoid(x)
        hi = ls.astype(BF16)
        r1 = ls - hi.astype(F32)
        mid = r1.astype(BF16)
        lo = (r1 - mid.astype(F32)).astype(BF16)
        pre = cumsum(lower, (hi, mid, lo))
        suf = cumsum(upper, (hi, mid, lo))
        a_f = x - pltpu.roll(pre, LANES - 1, axis=1)
        a_b = x - pltpu.roll(suf, LANES - 1, axis=1)
        cm_f, cm_b = a_f, a_b
        sh = 1
        while sh < lc:
            cm_f = jnp.maximum(cm_f, jnp.where(rix >= sh, pltpu.roll(cm_f, sh, axis=0), NEG_BIG))
            cm_b = jnp.maximum(cm_b, jnp.where(rix < lc - sh, pltpu.roll(cm_b, lc - sh, axis=0), NEG_BIG))
            sh *= 2
        out = jnp.where(slot == 0, a_f,
              jnp.where(slot == 1, pre,
              jnp.where(slot == 2, a_b,
              jnp.where(slot == 3, suf,
              jnp.where(slot == 4, pltpu.roll(cm_f, 4, axis=1), pltpu.roll(cm_b, 3, axis=1))))))
        o_ref[:, c * lc:(c + 1) * lc] = jnp.transpose(out)


def _ml_gates(g):
    bsz, rows, w = g.shape
    tm = _row_tile(rows, ROW_TILE_STREAM)
    return pl.pallas_call(
        functools.partial(_ml_gate_kernel, lc=_ml_chunk_len(rows)),
        out_shape=jax.ShapeDtypeStruct((bsz, w, rows), F32),
        grid=(bsz, rows // tm),
        in_specs=[pl.BlockSpec((None, tm, w), lambda b, i: (b, i, 0))],
        out_specs=pl.BlockSpec((None, w, tm), lambda b, i: (b, 0, i)),
        compiler_params=_cparams("parallel", "parallel"),
        name="ml_gates",
    )(g)


def _ml_chunk(q, k, v_aug, a_col, b_col, cm_col, a_row, c_mem, m_run, tri, last):
    dd = range(2)
    hd = q[0].shape[1]
    nt = (((1,), (1,)), ((), ()))
    tn = (((0,), (0,)), ((), ()))
    b_end = [b_col[d][last[d]:last[d] + 1, :] for d in dd]
    m_new = [b_end[d] + jnp.maximum(m_run[d], cm_col[d][last[d]:last[d] + 1, :]) for d in dd]
    r_col = [jnp.maximum(m_run[d], cm_col[d]) for d in dd]
    s = [lax.dot_general(q[d], k[d], nt, preferred_element_type=F32) for d in dd]
    c_bf = [c_mem[d].astype(BF16) for d in dd]
    qc = [jnp.dot(q[d], c_bf[d], preferred_element_type=F32) for d in dd]
    wk = [k[d] * jnp.exp(a_col[d] + (b_end[d] - m_new[d])).astype(BF16) for d in dd]
    e = [jnp.exp(jnp.where(tri[d], a_row[d] - r_col[d], NEG_BIG)) for d in dd]
    upd = [lax.dot_general(wk[d], v_aug[d], tn, preferred_element_type=F32) for d in dd]
    p = [(s[d] * e[d]).astype(BF16) for d in dd]
    dec = [jnp.exp(b_end[d] + m_run[d] - m_new[d]) for d in dd]
    dq = [jnp.exp(m_run[d] - r_col[d]) for d in dd]
    acc = [jnp.dot(p[d], v_aug[d], preferred_element_type=F32) + dq[d] * qc[d] for d in dd]
    c_new = [dec[d] * c_mem[d] + upd[d] for d in dd]
    lim = [jnp.exp(-(b_col[d] + r_col[d])) for d in dd]
    h = [acc[d][:, :hd] / jnp.maximum(jnp.abs(acc[d][:, hd:hd + 1]), lim[d]) for d in dd]
    return h, c_new, m_new


def _mlstm_kernel(cq, ck, cv, co, lq, lk, lv, lo, cgr, lgr, gh_ref,
                  yc_ref, yl_ref, hf_ref, hb_ref, c_ref, m_ref):
    c_ref[...] = jnp.zeros_like(c_ref)
    m_ref[...] = jnp.full_like(m_ref, NEG_BIG)

    def scan(q_ref, k_ref, v_ref, o_ref, gr_ref, y_ref):
        lc = _ml_chunk_len(q_ref.shape[0])
        nc = q_ref.shape[0] // lc
        rows = lax.broadcasted_iota(jnp.int32, (lc, lc), 0)
        cols = lax.broadcasted_iota(jnp.int32, (lc, lc), 1)
        tri = (cols <= rows, cols >= rows)
        ones_blk = jnp.where(lax.broadcasted_iota(jnp.int32, (lc, LANES), 1) == 0, 1.0, 0.0).astype(BF16)

        def body(c, carry):
            sl = (pl.ds(pl.multiple_of(c * lc, lc), lc), pl.ds(pl.multiple_of((nc - 1 - c) * lc, lc), lc))
            dd = range(2)
            gc = [jnp.transpose(gr_ref[:, sl[d]]) for d in dd]
            h, c_new, m_new = _ml_chunk(
                [q_ref[sl[d], :] for d in dd], [k_ref[sl[d], :] for d in dd],
                [jnp.concatenate([v_ref[sl[d], :], ones_blk], axis=1) for d in dd],
                [gc[d][:, 2 * d:2 * d + 1] for d in dd], [gc[d][:, 2 * d + 1:2 * d + 2] for d in dd],
                [gc[d][:, 4 + d:5 + d] for d in dd], [gr_ref[2 * d:2 * d + 1, sl[d]] for d in dd],
                [c_ref[d] for d in dd], [m_ref[d] for d in dd], tri, (lc - 1, 0))
            for d, out_ref in enumerate((hf_ref, hb_ref)):
                c_ref[d] = c_new[d]
                m_ref[d] = m_new[d]
                out_ref[sl[d], :] = h[d]
            return carry

        lax.fori_loop(0, nc, body, 0)

        lm = ML_MERGE_ROWS

        def merge(c, carry):
            sl = pl.ds(pl.multiple_of(c * lm, lm), lm)
            hs = hf_ref[sl, :] + hb_ref[sl, :]
            hs = hs * lax.rsqrt(jnp.mean(hs * hs, axis=-1, keepdims=True) + EPS) * gh_ref[...]
            y_ref[sl, :] = (jax.nn.sigmoid(o_ref[sl, :].astype(F32)) * hs).astype(y_ref.dtype)
            return carry

        lax.fori_loop(0, q_ref.shape[0] // lm, merge, 0)

    scan(cq, ck, cv, co, cgr, yc_ref)
    scan(lq, lk, lv, lo, lgr, yl_ref)


def _mlstm_scan(zc, zl, gc, gl, g_head):
    bsz, n, d4 = zl.shape
    lctx = zc.shape[1]
    d = d4 // 4
    nh = ML_HEADS
    hd = d // nh
    assert n % _ml_chunk_len(n) == 0 and n % ML_MERGE_ROWS == 0 and lctx % ML_MERGE_ROWS == 0

    def zspec(length, k):
        return pl.BlockSpec((None, length, hd), lambda b, h: (b, 0, k * nh + h))

    def gspec(length):
        return pl.BlockSpec((None, ML_GATE_SLOTS, length), lambda b, h: (b, h, 0))

    lmax = max(n, lctx)
    return pl.pallas_call(
        _mlstm_kernel,
        out_shape=(jax.ShapeDtypeStruct((bsz, lctx, d), BF16),
                   jax.ShapeDtypeStruct((bsz, n, d), BF16)),
        grid=(bsz, nh),
        in_specs=([zspec(lctx, k) for k in range(4)] + [zspec(n, k) for k in range(4)]
                  + [gspec(lctx), gspec(n)]
                  + [pl.BlockSpec((None, 1, hd), lambda b, h: (h, 0, 0))]),
        out_specs=(pl.BlockSpec((None, lctx, hd), lambda b, h: (b, 0, h)),
                   pl.BlockSpec((None, n, hd), lambda b, h: (b, 0, h))),
        scratch_shapes=[pltpu.VMEM((lmax, hd), F32), pltpu.VMEM((lmax, hd), F32),
                        pltpu.VMEM((2, hd, hd + LANES), F32), pltpu.VMEM((2, 1, 1), F32)],
        compiler_params=_cparams("parallel", "parallel"),
        name="mlstm_scan",
    )(zc, zc, zc, zc, zl, zl, zl, zl, gc, gl, g_head.reshape(nh, 1, hd))


def _mlstm_mixer(h, hc, g, mod, mod_c, w_in, b_gate, g_head, w_out, with_ctx):
    bsz, n, d = h.shape
    lctx = hc.shape[1] // bsz
    w_main = w_in[:, :4 * d].astype(BF16)
    nh = ML_HEADS
    ngp = -(-nh * ML_GATE_SLOTS // LANES) * LANES

    def head_major(t):
        t = jnp.swapaxes(t.reshape(t.shape[:-1] + (4, nh)), -1, -2)
        t = jnp.pad(t, [(0, 0)] * (t.ndim - 1) + [(0, ML_GATE_SLOTS - 4)])
        t = t.reshape(t.shape[:-2] + (nh * ML_GATE_SLOTS,))
        return jnp.pad(t, [(0, 0)] * (t.ndim - 1) + [(0, ngp - nh * ML_GATE_SLOTS)])

    w_gate = head_major(w_in[:, 4 * d:]).astype(BF16)
    bias = head_major(b_gate.reshape(1, 4 * nh))
    q_scale = (d // ML_HEADS) ** -0.5
    zl, gl = _proj(h, g, mod, 3, w_main, BF16, gates=(w_gate, bias), scale=q_scale, scale_cols=d)
    zc, gc = _proj(hc, g, mod_c, 3, w_main, BF16, gates=(w_gate, bias), scale=q_scale, scale_cols=d)
    yc, yl = _mlstm_scan(zc.reshape(bsz, lctx, 4 * d), zl,
                         _ml_gates(gc.reshape(bsz, lctx, ngp)), _ml_gates(gl), g_head)
    w_o = w_out.astype(BF16)
    h = _out_proj(yl, w_o, h, mod, 5)
    if with_ctx:
        hc = _out_proj(yc.reshape(1, bsz * lctx, d), w_o, hc, mod_c, 5)
    return h, hc


def _wa_kernel(sink_ref, q_ref, *refs, local, scale):
    if local:
        k_ref, v_ref, kc_ref, vc_ref, o_ref = refs
    else:
        kc_ref, vc_ref, o_ref = refs
    tq = q_ref.shape[0]
    hd = kc_ref.shape[1] // WA_KV_HEADS
    grp = WA_Q_HEADS // WA_KV_HEADS
    if local:
        n = k_ref.shape[0]
        span = 3 * WA_BLOCK
        start = pl.program_id(1) * tq
        ks = pl.multiple_of(jnp.clip(start - WA_BLOCK, 0, n - span), WA_BLOCK)
        q_pos = start + (lax.broadcasted_iota(jnp.int32, (grp * tq, span), 0) & (tq - 1))
        k_pos = ks + lax.broadcasted_iota(jnp.int32, (grp * tq, span), 1)
        ok = jnp.abs(q_pos - k_pos) <= WA_WINDOW
    dims = (((1,), (1,)), ((), ()))
    heads = range(WA_KV_HEADS)
    c2 = scale * LOG2E
    cs = [slice(kv * hd, (kv + 1) * hd) for kv in heads]
    q4 = [jnp.concatenate([q_ref[:, (kv * grp + j) * hd:(kv * grp + j + 1) * hd] for j in range(grp)], axis=0)
          for kv in heads]
    sink = [jnp.concatenate([jnp.full((tq, 1), sink_ref[0, kv * grp + j], F32) for j in range(grp)], axis=0)
            * (1.0 / scale) for kv in heads]
    s_ctx = [lax.dot_general(q4[kv], kc_ref[:, cs[kv]], dims, preferred_element_type=F32) for kv in heads]
    m = [jnp.maximum(jnp.max(s_ctx[kv], axis=1, keepdims=True), sink[kv]) for kv in heads]
    if local:
        s_loc = [jnp.where(ok, lax.dot_general(q4[kv], k_ref[pl.ds(ks, span), cs[kv]], dims,
                                               preferred_element_type=F32), NEG_BIG) for kv in heads]
        m = [jnp.maximum(m[kv], jnp.max(s_loc[kv], axis=1, keepdims=True)) for kv in heads]
        p_loc = [jnp.exp2((s_loc[kv] - m[kv]) * c2) for kv in heads]
    p_ctx = [jnp.exp2((s_ctx[kv] - m[kv]) * c2) for kv in heads]
    den = [jnp.sum(p_ctx[kv], axis=1, keepdims=True) + jnp.exp2((sink[kv] - m[kv]) * c2) for kv in heads]
    if local:
        den = [den[kv] + jnp.sum(p_loc[kv], axis=1, keepdims=True) for kv in heads]
    inv = [1.0 / den[kv] for kv in heads]
    out = [jnp.dot((p_ctx[kv] * inv[kv]).astype(BF16), vc_ref[:, cs[kv]], preferred_element_type=F32)
           for kv in heads]
    if local:
        out = [out[kv] + jnp.dot((p_loc[kv] * inv[kv]).astype(BF16), v_ref[pl.ds(ks, span), cs[kv]],
                                 preferred_element_type=F32) for kv in heads]
    for kv in heads:
        for j in range(grp):
            hq = kv * grp + j
            o_ref[:, hq * hd:(hq + 1) * hd] = out[kv][j * tq:(j + 1) * tq].astype(o_ref.dtype)


def _wa_attention(sink, zq, zkv, zc, local):
    bsz, lq, _ = zq.shape
    lctx = zc.shape[1]
    hd = zq.shape[2] // (WA_Q_HEADS + 2 * WA_KV_HEADS)
    qd, kd = WA_Q_HEADS * hd, WA_KV_HEADS * hd
    tq = WA_BLOCK
    kblk = qd // kd
    in_specs = [pl.BlockSpec(memory_space=pltpu.SMEM),
                pl.BlockSpec((None, tq, qd), lambda b, i: (b, i, 0))]
    args = [sink.reshape(1, WA_Q_HEADS), zq]
    if local:
        n = zkv.shape[1]
        in_specs += [pl.BlockSpec((None, n, kd), lambda b, i: (b, 0, kblk)),
                     pl.BlockSpec((None, n, kd), lambda b, i: (b, 0, kblk + 1))]
        args += [zkv, zkv]
    in_specs += [pl.BlockSpec((None, lctx, kd), lambda b, i: (b, 0, kblk)),
                 pl.BlockSpec((None, lctx, kd), lambda b, i: (b, 0, kblk + 1))]
    args += [zc, zc]
    return pl.pallas_call(
        functools.partial(_wa_kernel, local=local, scale=hd ** -0.5),
        out_shape=jax.ShapeDtypeStruct((bsz, lq, qd), BF16),
        grid=(bsz, lq // tq),
        in_specs=in_specs,
        out_specs=pl.BlockSpec((None, tq, qd), lambda b, i: (b, i, 0)),
        compiler_params=_cparams("parallel", "parallel"),
        name="wa_attention",
    )(*args)


def _rope_tables(n, hd):
    rows = n // GRID_W
    row = jnp.repeat(jnp.arange(rows, dtype=F32), GRID_W)
    col = jnp.tile(jnp.arange(GRID_W, dtype=F32), rows)
    n_freq = hd // 4
    inv = ROPE_BASE ** (-jnp.arange(n_freq, dtype=F32) / n_freq)
    ang = jnp.concatenate([row[:, None] * inv, col[:, None] * inv], axis=-1)
    cos, sin = jnp.cos(ang), jnp.sin(ang)
    return jnp.concatenate([cos, cos], axis=-1), jnp.concatenate([-sin, sin], axis=-1)


def _wa_mixer(h, hc, g, mod, mod_c, w_in, sink, w_out, with_ctx):
    bsz, n, d = h.shape
    lctx = hc.shape[1] // bsz
    hd = d // WA_Q_HEADS
    w = w_in.astype(BF16)
    rope_cols = (WA_Q_HEADS + WA_KV_HEADS) * hd
    zl = _proj(h, g, mod, 3, w, BF16, rope=_rope_tables(n, hd), rope_cols=rope_cols)
    zc = _proj(hc, g, mod_c, 3, w, BF16).reshape(bsz, lctx, -1)
    sink = sink.astype(F32)
    w_o = w_out.astype(BF16)
    h = _out_proj(_wa_attention(sink, zl, zl, zc, True), w_o, h, mod, 5)
    if with_ctx:
        yc = _wa_attention(sink, zc, None, zc, False)
        hc = _out_proj(yc.reshape(1, bsz * lctx, d), w_o, hc, mod_c, 5)
    return h, hc


def _s5_proj_kernel(x_ref, g_ref, sh_ref, sc_ref, w_ref, of_ref, hn_ref):
    n = pl.program_id(2)

    @pl.when(n == 0)
    def _():
        _adaln_rows(hn_ref, x_ref, g_ref, sh_ref, sc_ref)

    hn = hn_ref[...]
    tm, tn = hn.shape[0], w_ref.shape[1]
    sub = _sub_cols(tn)
    for c in range(tn // sub):
        acc = jnp.dot(hn, w_ref[:, c * sub:(c + 1) * sub], preferred_element_type=F32)
        for k in range(sub // LANES):
            blk = acc[:, k * LANES:(k + 1) * LANES]
            of_ref[c * (sub // LANES) + k] = blk.reshape(tm // S5_CHUNK, S5_CHUNK, LANES)


def _s5_proj(x, g, mod, k0, w, shared_mod):
    bsz, rows, d = x.shape
    tm = _row_tile(rows, ROW_TILE_STREAM)
    tn = _col_tile(d, 1024)
    lc = S5_CHUNK
    return pl.pallas_call(
        _s5_proj_kernel,
        out_shape=jax.ShapeDtypeStruct((d // LANES, rows // lc, bsz, lc, LANES), F32),
        grid=(bsz, rows // tm, d // tn),
        in_specs=[pl.BlockSpec((None, tm, d), lambda b, i, n: (b, i, 0)),
                  pl.BlockSpec((1, d), lambda b, i, n: (0, 0)),
                  _mod_spec(k0, d, shared_mod), _mod_spec(k0 + 1, d, shared_mod),
                  pl.BlockSpec((d, tn), lambda b, i, n: (0, n))],
        out_specs=pl.BlockSpec((tn // LANES, tm // lc, None, lc, LANES), lambda b, i, n: (n, i, b, 0, 0)),
        scratch_shapes=[pltpu.VMEM((tm, d), BF16)],
        compiler_params=_cparams("parallel", "parallel", "arbitrary"),
        name="s5_proj",
    )(x, g.reshape(1, d), mod, mod, w)


S5_PACKETS = LANES // S5_GROUP
S5_REGROUP_ROWS = 16


def _packet_transpose(vs):
    lane = lax.broadcasted_iota(jnp.int32, vs[0].shape, 1)
    d = S5_PACKETS // 2
    while d:
        low = (lane & (d * S5_GROUP)) == 0
        nxt = list(vs)
        for i in range(S5_PACKETS):
            if not i & d:
                nxt[i] = jnp.where(low, vs[i], pltpu.roll(vs[i + d], d * S5_GROUP, axis=1))
                nxt[i + d] = jnp.where(low, pltpu.roll(vs[i], LANES - d * S5_GROUP, axis=1), vs[i + d])
        vs = nxt
        d //= 2
    return vs


def _s5_group_kernel(u_ref, o_ref):
    def body(r, carry):
        rows = pl.ds(pl.multiple_of(r * S5_REGROUP_ROWS, S5_REGROUP_ROWS), S5_REGROUP_ROWS)
        for hf in range(S5_CHUNK // S5_PACKETS):
            base = hf * S5_PACKETS
            vs = [pltpu.bitcast(u_ref[rows, (base + t) * LANES:(base + t + 1) * LANES].astype(o_ref.dtype),
                                jnp.uint32) for t in range(S5_PACKETS)]
            for gq, v in enumerate(_packet_transpose(vs)):
                o_ref[gq, rows, hf * LANES:(hf + 1) * LANES] = pltpu.bitcast(v, o_ref.dtype)
        return carry

    lax.fori_loop(0, u_ref.shape[0] // S5_REGROUP_ROWS, body, 0, unroll=4)


def _s5_group(u):
    g8, rows, wide = u.shape
    tr = _row_tile(rows, ROW_TILE)
    w = wide // S5_PACKETS
    return pl.pallas_call(
        _s5_group_kernel,
        out_shape=jax.ShapeDtypeStruct((g8 * S5_PACKETS, rows, w), BF16),
        grid=(g8, rows // tr),
        in_specs=[pl.BlockSpec((None, tr, wide), lambda g, i: (g, i, 0))],
        out_specs=pl.BlockSpec((S5_PACKETS, tr, w), lambda g, i: (g, i, 0)),
        compiler_params=_cparams("parallel", "parallel"),
        name="s5_group",
    )(u)


def _s5_ungroup_kernel(y_ref, u_ref, ds_ref, o_ref):
    def body(r, carry):
        rows = pl.ds(pl.multiple_of(r * S5_REGROUP_ROWS, S5_REGROUP_ROWS), S5_REGROUP_ROWS)
        for hf in range(S5_CHUNK // S5_PACKETS):
            base = hf * S5_PACKETS
            vs = [pltpu.bitcast(y_ref[gq, rows, hf * LANES:(hf + 1) * LANES], jnp.uint32)
                  for gq in range(S5_PACKETS)]
            for t, v in enumerate(_packet_transpose(vs)):
                cols = slice((base + t) * LANES, (base + t + 1) * LANES)
                y = pltpu.bitcast(v, F32)
                o_ref[rows, cols] = jax.nn.gelu(y + ds_ref[...] * u_ref[rows, cols]).astype(o_ref.dtype)
        return carry

    lax.fori_loop(0, o_ref.shape[0] // S5_REGROUP_ROWS, body, 0, unroll=2)


def _s5_ungroup(y, u, d_skip):
    ng, rows, w = y.shape
    tr = _row_tile(rows, ROW_TILE)
    wide = w * S5_PACKETS
    wide_spec = pl.BlockSpec((None, tr, wide), lambda g, i: (g, i, 0))
    return pl.pallas_call(
        _s5_ungroup_kernel,
        out_shape=jax.ShapeDtypeStruct((ng // S5_PACKETS, rows, wide), BF16),
        grid=(ng // S5_PACKETS, rows // tr),
        in_specs=[pl.BlockSpec((S5_PACKETS, tr, w), lambda g, i: (g, i, 0)), wide_spec,
                  pl.BlockSpec((None, 1, LANES), lambda g, i: (g, 0, 0))],
        out_specs=wide_spec,
        compiler_params=_cparams("parallel", "parallel"),
        name="s5_ungroup",
    )(y, u, d_skip)


S5_SCAN_GROUPS = 2


def _s5_kernel(uc_ref, ul_ref, kl_ref, q_ref, p_ref, a_ref, yc_ref, yl_ref,
               sc_ref, sl_ref, xc_ref, xl_ref, t_ref, *, bsz):
    gs = range(uc_ref.shape[0])
    dd = range(2)
    half = sc_ref.shape[3] // 2
    nch = kl_ref.shape[2]
    for ti in range(S5_CHUNK):
        for to in range(S5_CHUNK):
            for g in gs:
                t_ref[g, ti * nch:(ti + 1) * nch, to * nch:(to + 1) * nch] = kl_ref[g, to - ti + S5_CHUNK - 1]
    for u_ref, s_ref in ((uc_ref, sc_ref), (ul_ref, sl_ref)):
        for g in gs:
            for d in dd:
                s_ref[g, d] = jnp.dot(u_ref[g], q_ref[d, g], preferred_element_type=F32)
    aa = [[a_ref[d, g, 0:1, :] for d in dd] for g in gs]
    ab = [[a_ref[d, g, 1:2, :] for d in dd] for g in gs]

    def run(s_ref, x_ref, carry):
        n = s_ref.shape[2] // bsz

        def step(g, d, j, w):
            rows = pl.ds(pl.multiple_of(j * bsz, bsz), bsz)
            x_ref[g, d, rows, :] = w[:, :half]
            other = jnp.concatenate([w[:, half:], w[:, :half]], axis=1)
            return aa[g][d] * w + ab[g][d] * other + s_ref[g, d, rows, :]

        def body(i, ws):
            return tuple((step(g, 0, i, ws[g][0]), step(g, 1, n - 1 - i, ws[g][1])) for g in gs)

        return lax.fori_loop(0, n, body, carry)

    zero = jnp.zeros((bsz, 2 * half), F32)
    run(sl_ref, xl_ref, run(sc_ref, xc_ref, tuple((zero, zero) for _ in gs)))
    t_m = [t_ref[g].astype(BF16) for g in gs]
    for u_ref, x_ref, y_ref in ((uc_ref, xc_ref, yc_ref), (ul_ref, xl_ref, yl_ref)):
        for g in gs:
            y = jnp.dot(u_ref[g], t_m[g], preferred_element_type=F32)
            for d in dd:
                y = y + lax.dot_general(x_ref[g, d].astype(BF16), p_ref[d, g], (((1,), (1,)), ((), ())),
                                        preferred_element_type=F32)
            y_ref[g] = y


def _s5_scan(uc, ul, k_lag, q_m, p_m, a_m, bsz):
    ng, rc, w = uc.shape
    rl = ul.shape[1]
    st2 = p_m.shape[3]
    nlag, nch = k_lag.shape[1:3]
    gb = S5_SCAN_GROUPS if ng % S5_SCAN_GROUPS == 0 else 1

    def rows_spec(r):
        return pl.BlockSpec((gb, r, w), lambda g: (g, 0, 0))

    return pl.pallas_call(
        functools.partial(_s5_kernel, bsz=bsz),
        out_shape=(jax.ShapeDtypeStruct((ng, rc, w), F32), jax.ShapeDtypeStruct((ng, rl, w), F32)),
        grid=(ng // gb,),
        in_specs=[rows_spec(rc), rows_spec(rl),
                  pl.BlockSpec((gb, nlag, nch, nch), lambda g: (g, 0, 0, 0)),
                  pl.BlockSpec((2, gb, w, 2 * st2), lambda g: (0, g, 0, 0)),
                  pl.BlockSpec((2, gb, w, st2), lambda g: (0, g, 0, 0)),
                  pl.BlockSpec((2, gb, 2, 2 * st2), lambda g: (0, g, 0, 0))],
        out_specs=(rows_spec(rc), rows_spec(rl)),
        scratch_shapes=[pltpu.VMEM((gb, 2, rc, 2 * st2), F32), pltpu.VMEM((gb, 2, rl, 2 * st2), F32),
                        pltpu.VMEM((gb, 2, rc, st2), F32), pltpu.VMEM((gb, 2, rl, st2), F32),
                        pltpu.VMEM((gb, w, w), F32)],
        compiler_params=_cparams("parallel"),
        name="s5_scan",
    )(uc, ul, k_lag, q_m, p_m, a_m)


def _s5_operators(lam_re, lam_im, log_dt, b_re, b_im, c_re, c_im):
    lc = S5_CHUNK
    dt = jnp.exp(log_dt)[..., None]
    mag = jnp.exp(lam_re * dt)
    lb_re, lb_im = mag * jnp.cos(lam_im * dt), mag * jnp.sin(lam_im * dt)
    den = lam_re * lam_re + lam_im * lam_im
    nr, ni = lb_re - 1.0, lb_im
    fr = (nr * lam_re + ni * lam_im) / den
    fi = (ni * lam_re - nr * lam_im) / den
    bb_re = fr[..., None] * b_re - fi[..., None] * b_im
    bb_im = fr[..., None] * b_im + fi[..., None] * b_re
    k = jnp.arange(lc + 1, dtype=F32)[:, None, None, None]
    pmag = jnp.exp(k * (lam_re * dt))
    pw_re, pw_im = pmag * jnp.cos(k * (lam_im * dt)), pmag * jnp.sin(k * (lam_im * dt))
    ngrp, nch = lam_re.shape[1], b_re.shape[3]
    nst = lam_re.shape[2]
    pw_t = (jnp.transpose(pw_re, (1, 2, 0, 3)), jnp.transpose(pw_im, (1, 2, 0, 3)))
    bb_t = (jnp.transpose(bb_re, (0, 1, 3, 2)), jnp.transpose(bb_im, (0, 1, 3, 2)))

    lb_re = pw_t[0][:, :, :, None, :] * bb_t[0][:, :, None] - pw_t[1][:, :, :, None, :] * bb_t[1][:, :, None]
    lb_im = pw_t[0][:, :, :, None, :] * bb_t[1][:, :, None] + pw_t[1][:, :, :, None, :] * bb_t[0][:, :, None]
    kern = jnp.sum(lb_re[:, :, :, :, None, :] * c_re[:, :, None, None]
                   - lb_im[:, :, :, :, None, :] * c_im[:, :, None, None], axis=-1)
    k_lag = jnp.concatenate([jnp.flip(kern[1, :, 1:lc], axis=1), (kern[0, :, 0] + kern[1, :, 0])[:, None],
                             kern[0, :, 1:lc]], axis=1)

    def per_dir(d):
        e_in = (lc - 1 - jnp.arange(lc)) if d == 0 else jnp.arange(lc)
        l_re, l_im = (pw[d][:, e_in][:, :, None, :] for pw in pw_t)
        bt_re, bt_im = (bb[d][:, None] for bb in bb_t)
        q_re = (l_re * bt_re - l_im * bt_im).reshape(ngrp, lc * nch, nst)
        q_im = (l_re * bt_im + l_im * bt_re).reshape(ngrp, lc * nch, nst)
        q_m = jnp.concatenate([q_re, q_im, q_im, q_re], axis=-1)
        e_out = (jnp.arange(lc) + 1) if d == 0 else (lc - jnp.arange(lc))
        l_re, l_im = (pw[d][:, e_out][:, :, None, :] for pw in pw_t)
        cd_re, cd_im = c_re[d][:, None], c_im[d][:, None]
        p_re = (cd_re * l_re - cd_im * l_im).reshape(ngrp, lc * nch, nst)
        p_im = (cd_re * l_im + cd_im * l_re).reshape(ngrp, lc * nch, nst)
        p_m = jnp.concatenate([p_re, -p_im], axis=-1)
        a_re, a_im = pw_re[lc, d], pw_im[lc, d]
        a_m = jnp.stack([jnp.concatenate([a_re] * 4, axis=-1),
                         jnp.concatenate([-a_im, a_im, a_im, -a_im], axis=-1)], axis=1)
        return q_m, p_m, a_m

    q_m, p_m, a_m = (jnp.stack([x, y]) for x, y in zip(per_dir(0), per_dir(1)))
    return k_lag, q_m.astype(BF16), p_m.astype(BF16), a_m


def _s5_glu_kernel(zf_ref, wa_ref, wg_ref, h_ref, gate_ref, o_ref, z_ref):
    n = pl.program_id(2)

    @pl.when(n == 0)
    def _():
        tm = z_ref.shape[0]
        for k in range(zf_ref.shape[0]):
            z_ref[:, k * LANES:(k + 1) * LANES] = zf_ref[k].reshape(tm, LANES)

    z = z_ref[...]
    tn = wa_ref.shape[1]
    sub = _sub_cols(tn)
    for c in range(tn // sub):
        cols = slice(c * sub, (c + 1) * sub)
        a = jnp.dot(z, wa_ref[:, cols], preferred_element_type=F32)
        gt = jnp.dot(z, wg_ref[:, cols], preferred_element_type=F32)
        o_ref[:, cols] = h_ref[:, cols] + gate_ref[:, cols] * (a * jax.nn.sigmoid(gt))


def _s5_glu(z, w, h, mod, k, shared_mod):
    bsz, rows, d = h.shape
    tm = _row_tile(rows, ROW_TILE_STREAM)
    tn = _col_tile(d, 512)
    nn = d // tn
    lc = S5_CHUNK
    col = pl.BlockSpec((None, tm, tn), lambda b, i, n: (b, i, n))
    return pl.pallas_call(
        _s5_glu_kernel,
        out_shape=jax.ShapeDtypeStruct(h.shape, F32),
        grid=(bsz, rows // tm, nn),
        in_specs=[pl.BlockSpec((d // LANES, tm // lc, None, lc, LANES), lambda b, i, n: (0, i, b, 0, 0)),
                  pl.BlockSpec((d, tn), lambda b, i, n: (0, n)),
                  pl.BlockSpec((d, tn), lambda b, i, n: (0, n + nn)),
                  col, pl.BlockSpec((None, None, 1, tn),
                                    lambda b, i, n: (0 if shared_mod else b, k, 0, n))],
        out_specs=col,
        scratch_shapes=[pltpu.VMEM((tm, d), BF16)],
        compiler_params=_cparams("parallel", "parallel", "arbitrary"),
        name="s5_glu",
    )(z, w, w, h, mod)


def _s5_mixer(h, hc, g, mod, mod_c, w_in, lam_re, lam_im, log_dt, b_re, b_im, c_re, c_im,
              d_skip, w_out, with_ctx):
    bsz, n, d = h.shape
    lctx = hc.shape[1] // bsz
    lc = S5_CHUNK
    g8 = d // LANES
    w = w_in.astype(BF16)
    hc3 = hc.reshape(bsz, lctx, d)
    u_l = _s5_proj(h, g, mod, 3, w, False).reshape(g8, n // lc * bsz, lc * LANES)
    u_c = _s5_proj(hc3, g, mod_c, 3, w, True).reshape(g8, lctx // lc * bsz, lc * LANES)
    ops = _s5_operators(lam_re, lam_im, log_dt, b_re, b_im, c_re, c_im)
    yp_c, yp_l = _s5_scan(_s5_group(u_c), _s5_group(u_l), *ops, bsz=bsz)
    ds = d_skip.astype(F32).reshape(g8, 1, LANES)
    w_o = w_out.astype(BF16)
    z_l = _s5_ungroup(yp_l, u_l, ds).reshape(g8, n // lc, bsz, lc, LANES)
    h = _s5_glu(z_l, w_o, h, mod, 5, False)
    if with_ctx:
        z_c = _s5_ungroup(yp_c, u_c, ds).reshape(g8, lctx // lc, bsz, lc, LANES)
        hc = _s5_glu(z_c, w_o, hc3, mod_c, 5, True).reshape(1, bsz * lctx, d)
    return h, hc


def _rmsnorm_kernel(x_ref, g_ref, o_ref):
    x = x_ref[...]
    o_ref[...] = x * lax.rsqrt(jnp.mean(x * x, axis=-1, keepdims=True) + EPS) * g_ref[...]


def _final_norm(x, g):
    bsz, rows, d = x.shape
    tm = _row_tile(rows, ROW_TILE)
    row = pl.BlockSpec((None, tm, d), lambda b, i: (b, i, 0))
    return pl.pallas_call(
        _rmsnorm_kernel,
        out_shape=jax.ShapeDtypeStruct(x.shape, F32),
        grid=(bsz, rows // tm),
        in_specs=[row, pl.BlockSpec((1, d), lambda b, i: (0, 0))],
        out_specs=row,
        compiler_params=_cparams("parallel", "parallel"),
        name="final_norm",
    )(x, g.reshape(1, d))


def kernel(x, c, ctx, c_ctx, w_ada, b_ada, g_norm, w_ffn_in, w_ffn_out, g_final, ml_w_in, ml_b_gate, ml_g_head, ml_w_out, wa_w_in, wa_sink, wa_w_out, s5_w_in, s5_lam_re, s5_lam_im, s5_log_dt, s5_b_re, s5_b_im, s5_c_re, s5_c_im, s5_d_skip, s5_w_out):
    bsz, n, d = x.shape
    depth = w_ada.shape[0]
    lctx = ctx.shape[1]
    n_rows = -(-(bsz + 1) // 16) * 16
    c_rows = jnp.concatenate([c, c_ctx[None], jnp.zeros((n_rows - bsz - 1, d), F32)], axis=0)
    mods = _mod_table(c_rows, w_ada, b_ada).reshape(depth, n_rows, N_MOD, 1, d)
    w_in, w_out = _to_bf16(w_ffn_in), _to_bf16(w_ffn_out)
    h = x
    hc = ctx.reshape(1, bsz * lctx, d)
    for layer in range(depth):
        has_next = layer < depth - 1
        mod, mod_c = mods[layer, :bsz], mods[layer, bsz:bsz + 1]
        g = g_norm[layer]
        h = _ffn(h, g[0], mod, 0, w_in, w_out, layer, 0)
        hc = _ffn(hc, g[0], mod_c, 0, w_in, w_out, layer, 0)
        kind, idx = layer % 3, layer // 3
        if kind == 0:
            h, hc = _mlstm_mixer(h, hc, g[1], mod, mod_c, ml_w_in[idx], ml_b_gate[idx],
                                 ml_g_head[idx], ml_w_out[idx], has_next)
        elif kind == 1:
            h, hc = _wa_mixer(h, hc, g[1], mod, mod_c, wa_w_in[idx], wa_sink[idx],
                              wa_w_out[idx], has_next)
        else:
            h, hc = _s5_mixer(h, hc, g[1], mod, mod_c, s5_w_in[idx], s5_lam_re[idx], s5_lam_im[idx],
                              s5_log_dt[idx], s5_b_re[idx], s5_b_im[idx], s5_c_re[idx],
                              s5_c_im[idx], s5_d_skip[idx], s5_w_out[idx], has_next)
        h = _ffn(h, g[2], mod, 6, w_in, w_out, layer, 1)
        if has_next:
            hc = _ffn(hc, g[2], mod_c, 6, w_in, w_out, layer, 1)
    return _final_norm(h, g_final)
```

```python
import functools
import math

import jax
import jax.numpy as jnp
from jax import lax
from jax.experimental import pallas as pl
from jax.experimental.pallas import tpu as pltpu

F32 = jnp.float32
BF16 = jnp.bfloat16

EPS = 1e-6
NEG_BIG = -1e30
LOG2E = math.log2(math.e)
N_MOD = 9
ML_HEADS = 8
ML_CHUNK = 512
ML_MERGE_ROWS = 256
WA_Q_HEADS = 16
WA_KV_HEADS = 4
WA_WINDOW = 128
WA_BLOCK = 128
GRID_W = 64
ROPE_BASE = 10000.0
S5_GROUP = 16
S5_STATE = 64
S5_CHUNK = 16

LANES = 128
MXU_COLS = 256
ROW_TILE = 512
ROW_TILE_STREAM = 1024
CAST_BLOCK_BYTES = 8 * 1024 * 1024
VMEM_LIMIT = 56 * 1024 * 1024


def _cparams(*sem):
    return pltpu.CompilerParams(dimension_semantics=sem, vmem_limit_bytes=VMEM_LIMIT)


def _row_tile(rows, want):
    return want if rows % want == 0 else rows


def _sub_cols(cols):
    return MXU_COLS if cols % MXU_COLS == 0 else cols


def _col_tile(cols, want):
    t = min(want, cols)
    while cols % t:
        t -= LANES
    return t


def _adaln(x, g, shift, scale):
    var = jnp.mean(x * x, axis=-1, keepdims=True)
    return (x * lax.rsqrt(var + EPS) * g) * (1.0 + scale) + shift


ADALN_ROWS = 128


def _adaln_rows(hn_ref, x_ref, g_ref, sh_ref, sc_ref):
    rows = x_ref.shape[0]
    step = ADALN_ROWS if rows % ADALN_ROWS == 0 else rows

    def body(r, carry):
        sl = pl.ds(pl.multiple_of(r * step, step), step)
        hn_ref[sl, :] = _adaln(x_ref[sl, :], g_ref[...], sh_ref[...], sc_ref[...]).astype(BF16)
        return carry

    lax.fori_loop(0, rows // step, body, 0)


def _mod_spec(k, d, shared=False):
    return pl.BlockSpec((None, None, 1, d), lambda b, *_: (0 if shared else b, k, 0, 0))


def _mod_kernel(c_ref, w_ref, b_ref, o_ref):
    c = c_ref[...]
    s = (c * jax.nn.sigmoid(c)).astype(BF16)
    o_ref[...] = jnp.dot(s, w_ref[...].astype(BF16), preferred_element_type=F32) + b_ref[...]


def _mod_table(c_rows, w_ada, b_ada):
    depth, d, nd = w_ada.shape
    r = c_rows.shape[0]
    tn = _col_tile(nd, 1024)
    return pl.pallas_call(
        _mod_kernel,
        out_shape=jax.ShapeDtypeStruct((depth, r, nd), F32),
        grid=(depth, nd // tn),
        in_specs=[pl.BlockSpec((r, d), lambda l, n: (0, 0)),
                  pl.BlockSpec((None, d, tn), lambda l, n: (l, 0, n)),
                  pl.BlockSpec((None, 1, tn), lambda l, n: (l, 0, n))],
        out_specs=pl.BlockSpec((None, r, tn), lambda l, n: (l, 0, n)),
        compiler_params=_cparams("parallel", "parallel"),
        name="mod_table",
    )(c_rows, w_ada, b_ada.reshape(depth, 1, nd))


def _adaln_tile_kernel(x_ref, g_ref, sh_ref, sc_ref, o_ref):
    _adaln_rows(o_ref, x_ref, g_ref, sh_ref, sc_ref)


def _adaln_tile(x, g, mod, k0, tm):
    d = x.shape[2]
    return pl.pallas_call(
        _adaln_tile_kernel,
        out_shape=jax.ShapeDtypeStruct((tm, d), BF16),
        grid=(1,),
        in_specs=[pl.BlockSpec((None, tm, d), lambda b: (0, 0, 0)),
                  pl.BlockSpec((1, d), lambda b: (0, 0)),
                  _mod_spec(k0, d), _mod_spec(k0 + 1, d)],
        out_specs=pl.BlockSpec((tm, d), lambda b: (0, 0)),
        compiler_params=_cparams("arbitrary"),
        name="adaln_tile",
    )(x, g, mod, mod)


def _ffn_kernel(hn0_ref, xr_ref, g_ref, sh_ref, sc_ref, x_ref, gate_ref, wa_ref, wg_ref, wo_ref, o_ref,
                hn_ref, act_ref, *, nf):
    j = pl.program_id(2)
    tf = wa_ref.shape[1]

    @pl.when((j == 0) & (pl.program_id(0) == 0) & (pl.program_id(1) == 0))
    def _():
        hn_ref[...] = hn0_ref[...]

    @pl.when(j < nf)
    def _():
        hn = hn_ref[...]
        sub = _sub_cols(tf)
        for c in range(tf // sub):
            a = jnp.dot(hn, wa_ref[:, c * sub:(c + 1) * sub], preferred_element_type=F32)
            gt = jnp.dot(hn, wg_ref[:, c * sub:(c + 1) * sub], preferred_element_type=F32)
            cols = pl.ds(pl.multiple_of(j * tf + c * sub, sub), sub)
            act_ref[:, cols] = (a * (gt * jax.nn.sigmoid(gt))).astype(BF16)

    @pl.when(j >= nf)
    def _():
        y = jnp.dot(act_ref[...], wo_ref[...], preferred_element_type=F32)
        o_ref[...] = x_ref[...] + (0.5 * gate_ref[...]) * y
        rp = xr_ref.shape[0]
        rows = pl.ds(pl.multiple_of((j - nf) * rp, rp), rp)
        hn_ref[rows, :] = _adaln(xr_ref[...], g_ref[...], sh_ref[...], sc_ref[...]).astype(BF16)


def _cast_kernel(x_ref, o_ref):
    o_ref[...] = x_ref[...].astype(o_ref.dtype)


def _to_bf16(w):
    shape = w.shape
    cols = shape[-1]
    w2 = w.reshape(-1, cols)
    tr = 1 << ((CAST_BLOCK_BYTES // (4 * cols)).bit_length() - 1)
    while w2.shape[0] % tr:
        tr //= 2
    blk = pl.BlockSpec((tr, cols), lambda i: (i, 0))
    out = pl.pallas_call(
        _cast_kernel,
        out_shape=jax.ShapeDtypeStruct(w2.shape, BF16),
        grid=(w2.shape[0] // tr,),
        in_specs=[blk],
        out_specs=blk,
        compiler_params=_cparams("parallel"),
        name="to_bf16",
    )(w2)
    return out.reshape(shape)


def _ffn(x, g, mod, k0, w_in, w_out, layer, half):
    bsz, rows, d = x.shape
    ff = w_out.shape[2]
    tm = _row_tile(rows, ROW_TILE_STREAM)
    tf = _col_tile(ff, 512)
    tn = _col_tile(d, 512)
    nf = ff // tf
    nt = rows // tm
    nd = d // tn
    steps = nf + nd
    rp = tm // nd
    assert rp * nd == tm

    def fill(j):
        return jnp.minimum(j, nf - 1)

    def drain(j):
        return jnp.maximum(j - nf, 0)

    def ahead(b, i, j):
        r = jnp.minimum(b * nt + i + (j >= nf).astype(jnp.int32), bsz * nt - 1)
        return r // nt, r % nt

    def mod_ahead(k):
        return pl.BlockSpec((None, None, 1, d), lambda b, i, j: (ahead(b, i, j)[0], k, 0, 0))

    return pl.pallas_call(
        functools.partial(_ffn_kernel, nf=nf),
        out_shape=jax.ShapeDtypeStruct(x.shape, F32),
        grid=(bsz, nt, steps),
        in_specs=[pl.BlockSpec((tm, d), lambda b, i, j: (0, 0)),
                  pl.BlockSpec((None, rp, d),
                               lambda b, i, j: (ahead(b, i, j)[0], ahead(b, i, j)[1] * nd + drain(j), 0)),
                  pl.BlockSpec((1, d), lambda b, i, j: (0, 0)),
                  mod_ahead(k0), mod_ahead(k0 + 1),
                  pl.BlockSpec((None, tm, tn), lambda b, i, j: (b, i, drain(j))),
                  pl.BlockSpec((None, None, 1, tn), lambda b, i, j: (b, k0 + 2, 0, drain(j))),
                  pl.BlockSpec((None, None, d, tf), lambda b, i, j: (layer, half, 0, fill(j))),
                  pl.BlockSpec((None, None, d, tf), lambda b, i, j: (layer, half, 0, fill(j) + nf)),
                  pl.BlockSpec((None, None, ff, tn), lambda b, i, j: (layer, half, 0, drain(j)))],
        out_specs=pl.BlockSpec((None, tm, tn), lambda b, i, j: (b, i, drain(j))),
        scratch_shapes=[pltpu.VMEM((tm, d), BF16), pltpu.VMEM((tm, ff), BF16)],
        compiler_params=_cparams("arbitrary", "arbitrary", "arbitrary"),
        name="ffn",
    )(_adaln_tile(x, g.reshape(1, d), mod, k0, tm), x, g.reshape(1, d), mod, mod, x, mod, w_in, w_in, w_out)


def _proj_kernel(*refs, n_rope, with_gates, n_scaled, scale):
    x_ref, g_ref, sh_ref, sc_ref, w_ref = refs[:5]
    rest = refs[5:]
    if n_rope:
        cs_ref, sn_ref = rest[:2]
        rest = rest[2:]
    if with_gates:
        wg_ref, bg_ref, o_ref, og_ref, hn_ref = rest
    else:
        o_ref, hn_ref = rest
    n = pl.program_id(2)

    @pl.when(n == 0)
    def _():
        _adaln_rows(hn_ref, x_ref, g_ref, sh_ref, sc_ref)
        if with_gates:
            og_ref[...] = jnp.dot(hn_ref[...], wg_ref[...], preferred_element_type=F32) + bg_ref[...]

    def columns(rope):
        hn = hn_ref[...]
        tn = w_ref.shape[1]
        sub = _sub_cols(tn)
        for c in range(tn // sub):
            acc = jnp.dot(hn, w_ref[:, c * sub:(c + 1) * sub], preferred_element_type=F32)
            if n_scaled:
                acc = acc * jnp.where(n < n_scaled, scale, 1.0)
            if not rope:
                o_ref[:, c * sub:(c + 1) * sub] = acc.astype(o_ref.dtype)
            else:
                cs, sn = cs_ref[...], sn_ref[...]
                hd = cs.shape[1]
                for h in range(sub // hd):
                    r = acc[:, h * hd:(h + 1) * hd]
                    r = r * cs + pltpu.roll(r, hd // 2, axis=1) * sn
                    o_ref[:, c * sub + h * hd:c * sub + (h + 1) * hd] = r.astype(o_ref.dtype)

    if not n_rope:
        columns(False)
    else:
        @pl.when(n >= n_rope)
        def _():
            columns(False)

        @pl.when(n < n_rope)
        def _():
            columns(True)


def _proj(x, g, mod, k0, w, out_dtype, rope=None, rope_cols=0, gates=None, scale=1.0, scale_cols=0):
    bsz, rows, d = x.shape
    nout = w.shape[1]
    tm = _row_tile(rows, ROW_TILE_STREAM)
    tn = _col_tile(math.gcd(math.gcd(nout, rope_cols), scale_cols), 2048)
    n_rope = rope_cols // tn
    row = pl.BlockSpec((None, tm, d), lambda b, i, n: (b, i, 0))
    in_specs = [row, pl.BlockSpec((1, d), lambda b, i, n: (0, 0)),
                _mod_spec(k0, d), _mod_spec(k0 + 1, d),
                pl.BlockSpec((d, tn), lambda b, i, n: (0, n))]
    args = [x, g.reshape(1, d), mod, mod, w]
    if n_rope:
        cs, sn = rope
        hd = cs.shape[1]
        in_specs += [pl.BlockSpec((tm, hd), lambda b, i, n: (i, 0))] * 2
        args += [cs, sn]
    out_shape = jax.ShapeDtypeStruct((bsz, rows, nout), out_dtype)
    out_specs = pl.BlockSpec((None, tm, tn), lambda b, i, n: (b, i, n))
    if gates is not None:
        wg, bg = gates
        ng = wg.shape[1]
        in_specs += [pl.BlockSpec((d, ng), lambda b, i, n: (0, 0)),
                     pl.BlockSpec((1, ng), lambda b, i, n: (0, 0))]
        args += [wg, bg]
        out_shape = (out_shape, jax.ShapeDtypeStruct((bsz, rows, ng), F32))
        out_specs = (out_specs, pl.BlockSpec((None, tm, ng), lambda b, i, n: (b, i, 0)))
    return pl.pallas_call(
        functools.partial(_proj_kernel, n_rope=n_rope, with_gates=gates is not None,
                          n_scaled=scale_cols // tn, scale=scale),
        out_shape=out_shape,
        grid=(bsz, rows // tm, nout // tn),
        in_specs=in_specs,
        out_specs=out_specs,
        scratch_shapes=[pltpu.VMEM((tm, d), BF16)],
        compiler_params=_cparams("parallel", "parallel", "arbitrary"),
        name="proj",
    )(*args)


def _out_proj_kernel(y_ref, w_ref, h_ref, gate_ref, o_ref):
    acc = jnp.dot(y_ref[...], w_ref[...], preferred_element_type=F32)
    o_ref[...] = h_ref[...] + gate_ref[...] * acc


def _out_proj(y, w, h, mod, k):
    bsz, rows, d = h.shape
    dk = y.shape[2]
    tm = _row_tile(rows, ROW_TILE)
    return pl.pallas_call(
        _out_proj_kernel,
        out_shape=jax.ShapeDtypeStruct(h.shape, F32),
        grid=(bsz, rows // tm),
        in_specs=[pl.BlockSpec((None, tm, dk), lambda b, i: (b, i, 0)),
                  pl.BlockSpec((dk, d), lambda b, i: (0, 0)),
                  pl.BlockSpec((None, tm, d), lambda b, i: (b, i, 0)),
                  _mod_spec(k, d)],
        out_specs=pl.BlockSpec((None, tm, d), lambda b, i: (b, i, 0)),
        compiler_params=_cparams("parallel", "parallel"),
        name="out_proj",
    )(y, w, h, mod)


def _ml_chunk_len(length):
    return min(ML_CHUNK, length)


ML_GATE_SLOTS = 8


def _ml_gate_kernel(g_ref, o_ref, *, lc):
    row = lax.broadcasted_iota(jnp.int32, (lc, lc), 0)
    col = lax.broadcasted_iota(jnp.int32, (lc, lc), 1)
    lower = (col <= row).astype(BF16)
    upper = (col >= row).astype(BF16)
    slot = lax.broadcasted_iota(jnp.int32, (lc, LANES), 1) & (ML_GATE_SLOTS - 1)
    rix = lax.broadcasted_iota(jnp.int32, (lc, LANES), 0)

    def cumsum(tri, parts):
        return sum(jnp.dot(tri, p, preferred_element_type=F32) for p in parts)

    for c in range(g_ref.shape[0] // lc):
        x = g_ref[c * lc:(c + 1) * lc, :]
        ls = jax.nn.log_sigmoid(x)
        hi = ls.astype(BF16)
        r1 = ls - hi.astype(F32)
        mid = r1.astype(BF16)
        lo = (r1 - mid.astype(F32)).astype(BF16)
        pre = cumsum(lower, (hi, mid, lo))
        suf = cumsum(upper, (hi, mid, lo))
        a_f = x - pltpu.roll(pre, LANES - 1, axis=1)
        a_b = x - pltpu.roll(suf, LANES - 1, axis=1)
        cm_f, cm_b = a_f, a_b
        sh = 1
        while sh < lc:
            cm_f = jnp.maximum(cm_f, jnp.where(rix >= sh, pltpu.roll(cm_f, sh, axis=0), NEG_BIG))
            cm_b = jnp.maximum(cm_b, jnp.where(rix < lc - sh, pltpu.roll(cm_b, lc - sh, axis=0), NEG_BIG))
            sh *= 2
        out = jnp.where(slot == 0, a_f,
              jnp.where(slot == 1, pre,
              jnp.where(slot == 2, a_b,
              jnp.where(slot == 3, suf,
              jnp.where(slot == 4, pltpu.roll(cm_f, 4, axis=1), pltpu.roll(cm_b, 3, axis=1))))))
        o_ref[:, c * lc:(c + 1) * lc] = jnp.transpose(out)


def _ml_gates(g):
    bsz, rows, w = g.shape
    tm = _row_tile(rows, ROW_TILE_STREAM)
    return pl.pallas_call(
        functools.partial(_ml_gate_kernel, lc=_ml_chunk_len(rows)),
        out_shape=jax.ShapeDtypeStruct((bsz, w, rows), F32),
        grid=(bsz, rows // tm),
        in_specs=[pl.BlockSpec((None, tm, w), lambda b, i: (b, i, 0))],
        out_specs=pl.BlockSpec((None, w, tm), lambda b, i: (b, 0, i)),
        compiler_params=_cparams("parallel", "parallel"),
        name="ml_gates",
    )(g)


def _ml_chunk(q, k, v_aug, a_col, b_col, cm_col, a_row, c_mem, m_run, tri, last):
    dd = range(2)
    hd = q[0].shape[1]
    nt = (((1,), (1,)), ((), ()))
    tn = (((0,), (0,)), ((), ()))
    b_end = [b_col[d][last[d]:last[d] + 1, :] for d in dd]
    m_new = [b_end[d] + jnp.maximum(m_run[d], cm_col[d][last[d]:last[d] + 1, :]) for d in dd]
    r_col = [jnp.maximum(m_run[d], cm_col[d]) for d in dd]
    s = [lax.dot_general(q[d], k[d], nt, preferred_element_type=F32) for d in dd]
    c_bf = [c_mem[d].astype(BF16) for d in dd]
    qc = [jnp.dot(q[d], c_bf[d], preferred_element_type=F32) for d in dd]
    wk = [k[d] * jnp.exp(a_col[d] + (b_end[d] - m_new[d])).astype(BF16) for d in dd]
    e = [jnp.exp(jnp.where(tri[d], a_row[d] - r_col[d], NEG_BIG)) for d in dd]
    upd = [lax.dot_general(wk[d], v_aug[d], tn, preferred_element_type=F32) for d in dd]
    p = [(s[d] * e[d]).astype(BF16) for d in dd]
    dec = [jnp.exp(b_end[d] + m_run[d] - m_new[d]) for d in dd]
    dq = [jnp.exp(m_run[d] - r_col[d]) for d in dd]
    acc = [jnp.dot(p[d], v_aug[d], preferred_element_type=F32) + dq[d] * qc[d] for d in dd]
    c_new = [dec[d] * c_mem[d] + upd[d] for d in dd]
    lim = [jnp.exp(-(b_col[d] + r_col[d])) for d in dd]
    h = [acc[d][:, :hd] / jnp.maximum(jnp.abs(acc[d][:, hd:hd + 1]), lim[d]) for d in dd]
    return h, c_new, m_new


def _mlstm_kernel(cq, ck, cv, co, lq, lk, lv, lo, cgr, lgr, gh_ref,
                  yc_ref, yl_ref, hf_ref, hb_ref, c_ref, m_ref):
    c_ref[...] = jnp.zeros_like(c_ref)
    m_ref[...] = jnp.full_like(m_ref, NEG_BIG)

    def scan(q_ref, k_ref, v_ref, o_ref, gr_ref, y_ref):
        lc = _ml_chunk_len(q_ref.shape[0])
        nc = q_ref.shape[0] // lc
        rows = lax.broadcasted_iota(jnp.int32, (lc, lc), 0)
        cols = lax.broadcasted_iota(jnp.int32, (lc, lc), 1)
        tri = (cols <= rows, cols >= rows)
        ones_blk = jnp.where(lax.broadcasted_iota(jnp.int32, (lc, LANES), 1) == 0, 1.0, 0.0).astype(BF16)

        def body(c, carry):
            sl = (pl.ds(pl.multiple_of(c * lc, lc), lc), pl.ds(pl.multiple_of((nc - 1 - c) * lc, lc), lc))
            dd = range(2)
            gc = [jnp.transpose(gr_ref[:, sl[d]]) for d in dd]
            h, c_new, m_new = _ml_chunk(
                [q_ref[sl[d], :] for d in dd], [k_ref[sl[d], :] for d in dd],
                [jnp.concatenate([v_ref[sl[d], :], ones_blk], axis=1) for d in dd],
                [gc[d][:, 2 * d:2 * d + 1] for d in dd], [gc[d][:, 2 * d + 1:2 * d + 2] for d in dd],
                [gc[d][:, 4 + d:5 + d] for d in dd], [gr_ref[2 * d:2 * d + 1, sl[d]] for d in dd],
                [c_ref[d] for d in dd], [m_ref[d] for d in dd], tri, (lc - 1, 0))
            for d, out_ref in enumerate((hf_ref, hb_ref)):
                c_ref[d] = c_new[d]
                m_ref[d] = m_new[d]
                out_ref[sl[d], :] = h[d]
            return carry

        lax.fori_loop(0, nc, body, 0)

        lm = ML_MERGE_ROWS

        def merge(c, carry):
            sl = pl.ds(pl.multiple_of(c * lm, lm), lm)
            hs = hf_ref[sl, :] + hb_ref[sl, :]
            hs = hs * lax.rsqrt(jnp.mean(hs * hs, axis=-1, keepdims=True) + EPS) * gh_ref[...]
            y_ref[sl, :] = (jax.nn.sigmoid(o_ref[sl, :].astype(F32)) * hs).astype(y_ref.dtype)
            return carry

        lax.fori_loop(0, q_ref.shape[0] // lm, merge, 0)

    scan(cq, ck, cv, co, cgr, yc_ref)
    scan(lq, lk, lv, lo, lgr, yl_ref)


def _mlstm_scan(zc, zl, gc, gl, g_head):
    bsz, n, d4 = zl.shape
    lctx = zc.shape[1]
    d = d4 // 4
    nh = ML_HEADS
    hd = d // nh
    assert n % _ml_chunk_len(n) == 0 and n % ML_MERGE_ROWS == 0 and lctx % ML_MERGE_ROWS == 0

    def zspec(length, k):
        return pl.BlockSpec((None, length, hd), lambda b, h: (b, 0, k * nh + h))

    def gspec(length):
        return pl.BlockSpec((None, ML_GATE_SLOTS, length), lambda b, h: (b, h, 0))

    lmax = max(n, lctx)
    return pl.pallas_call(
        _mlstm_kernel,
        out_shape=(jax.ShapeDtypeStruct((bsz, lctx, d), BF16),
                   jax.ShapeDtypeStruct((bsz, n, d), BF16)),
        grid=(bsz, nh),
        in_specs=([zspec(lctx, k) for k in range(4)] + [zspec(n, k) for k in range(4)]
                  + [gspec(lctx), gspec(n)]
                  + [pl.BlockSpec((None, 1, hd), lambda b, h: (h, 0, 0))]),
        out_specs=(pl.BlockSpec((None, lctx, hd), lambda b, h: (b, 0, h)),
                   pl.BlockSpec((None, n, hd), lambda b, h: (b, 0, h))),
        scratch_shapes=[pltpu.VMEM((lmax, hd), F32), pltpu.VMEM((lmax, hd), F32),
                        pltpu.VMEM((2, hd, hd + LANES), F32), pltpu.VMEM((2, 1, 1), F32)],
        compiler_params=_cparams("parallel", "parallel"),
        name="mlstm_scan",
    )(zc, zc, zc, zc, zl, zl, zl, zl, gc, gl, g_head.reshape(nh, 1, hd))


def _mlstm_mixer(h, hc, g, mod, mod_c, w_in, b_gate, g_head, w_out, with_ctx):
    bsz, n, d = h.shape
    lctx = hc.shape[1] // bsz
    w_main = w_in[:, :4 * d].astype(BF16)
    nh = ML_HEADS
    ngp = -(-nh * ML_GATE_SLOTS // LANES) * LANES

    def head_major(t):
        t = jnp.swapaxes(t.reshape(t.shape[:-1] + (4, nh)), -1, -2)
        t = jnp.pad(t, [(0, 0)] * (t.ndim - 1) + [(0, ML_GATE_SLOTS - 4)])
        t = t.reshape(t.shape[:-2] + (nh * ML_GATE_SLOTS,))
        return jnp.pad(t, [(0, 0)] * (t.ndim - 1) + [(0, ngp - nh * ML_GATE_SLOTS)])

    w_gate = head_major(w_in[:, 4 * d:]).astype(BF16)
    bias = head_major(b_gate.reshape(1, 4 * nh))
    q_scale = (d // ML_HEADS) ** -0.5
    zl, gl = _proj(h, g, mod, 3, w_main, BF16, gates=(w_gate, bias), scale=q_scale, scale_cols=d)
    zc, gc = _proj(hc, g, mod_c, 3, w_main, BF16, gates=(w_gate, bias), scale=q_scale, scale_cols=d)
    yc, yl = _mlstm_scan(zc.reshape(bsz, lctx, 4 * d), zl,
                         _ml_gates(gc.reshape(bsz, lctx, ngp)), _ml_gates(gl), g_head)
    w_o = w_out.astype(BF16)
    h = _out_proj(yl, w_o, h, mod, 5)
    if with_ctx:
        hc = _out_proj(yc.reshape(1, bsz * lctx, d), w_o, hc, mod_c, 5)
    return h, hc


def _wa_kernel(sink_ref, q_ref, *refs, local, scale):
    if local:
        k_ref, v_ref, kc_ref, vc_ref, o_ref = refs
    else:
        kc_ref, vc_ref, o_ref = refs
    tq = q_ref.shape[0]
    hd = kc_ref.shape[1] // WA_KV_HEADS
    grp = WA_Q_HEADS // WA_KV_HEADS
    if local:
        n = k_ref.shape[0]
        span = 3 * WA_BLOCK
        start = pl.program_id(1) * tq
        ks = pl.multiple_of(jnp.clip(start - WA_BLOCK, 0, n - span), WA_BLOCK)
        q_pos = start + (lax.broadcasted_iota(jnp.int32, (grp * tq, span), 0) & (tq - 1))
        k_pos = ks + lax.broadcasted_iota(jnp.int32, (grp * tq, span), 1)
        ok = jnp.abs(q_pos - k_pos) <= WA_WINDOW
    dims = (((1,), (1,)), ((), ()))
    heads = range(WA_KV_HEADS)
    c2 = scale * LOG2E
    cs = [slice(kv * hd, (kv + 1) * hd) for kv in heads]
    q4 = [jnp.concatenate([q_ref[:, (kv * grp + j) * hd:(kv * grp + j + 1) * hd] for j in range(grp)], axis=0)
          for kv in heads]
    sink = [jnp.concatenate([jnp.full((tq, 1), sink_ref[0, kv * grp + j], F32) for j in range(grp)], axis=0)
            * (1.0 / scale) for kv in heads]
    s_ctx = [lax.dot_general(q4[kv], kc_ref[:, cs[kv]], dims, preferred_element_type=F32) for kv in heads]
    m = [jnp.maximum(jnp.max(s_ctx[kv], axis=1, keepdims=True), sink[kv]) for kv in heads]
    if local:
        s_loc = [jnp.where(ok, lax.dot_general(q4[kv], k_ref[pl.ds(ks, span), cs[kv]], dims,
                                               preferred_element_type=F32), NEG_BIG) for kv in heads]
        m = [jnp.maximum(m[kv], jnp.max(s_loc[kv], axis=1, keepdims=True)) for kv in heads]
        p_loc = [jnp.exp2((s_loc[kv] - m[kv]) * c2) for kv in heads]
    p_ctx = [jnp.exp2((s_ctx[kv] - m[kv]) * c2) for kv in heads]
    den = [jnp.sum(p_ctx[kv], axis=1, keepdims=True) + jnp.exp2((sink[kv] - m[kv]) * c2) for kv in heads]
    if local:
        den = [den[kv] + jnp.sum(p_loc[kv], axis=1, keepdims=True) for kv in heads]
    inv = [1.0 / den[kv] for kv in heads]
    out = [jnp.dot((p_ctx[kv] * inv[kv]).astype(BF16), vc_ref[:, cs[kv]], preferred_element_type=F32)
           for kv in heads]
    if local:
        out = [out[kv] + jnp.dot((p_loc[kv] * inv[kv]).astype(BF16), v_ref[pl.ds(ks, span), cs[kv]],
                                 preferred_element_type=F32) for kv in heads]
    for kv in heads:
        for j in range(grp):
            hq = kv * grp + j
            o_ref[:, hq * hd:(hq + 1) * hd] = out[kv][j * tq:(j + 1) * tq].astype(o_ref.dtype)


def _wa_attention(sink, zq, zkv, zc, local):
    bsz, lq, _ = zq.shape
    lctx = zc.shape[1]
    hd = zq.shape[2] // (WA_Q_HEADS + 2 * WA_KV_HEADS)
    qd, kd = WA_Q_HEADS * hd, WA_KV_HEADS * hd
    tq = WA_BLOCK
    kblk = qd // kd
    in_specs = [pl.BlockSpec(memory_space=pltpu.SMEM),
                pl.BlockSpec((None, tq, qd), lambda b, i: (b, i, 0))]
    args = [sink.reshape(1, WA_Q_HEADS), zq]
    if local:
        n = zkv.shape[1]
        in_specs += [pl.BlockSpec((None, n, kd), lambda b, i: (b, 0, kblk)),
                     pl.BlockSpec((None, n, kd), lambda b, i: (b, 0, kblk + 1))]
        args += [zkv, zkv]
    in_specs += [pl.BlockSpec((None, lctx, kd), lambda b, i: (b, 0, kblk)),
                 pl.BlockSpec((None, lctx, kd), lambda b, i: (b, 0, kblk + 1))]
    args += [zc, zc]
    return pl.pallas_call(
        functools.partial(_wa_kernel, local=local, scale=hd ** -0.5),
        out_shape=jax.ShapeDtypeStruct((bsz, lq, qd), BF16),
        grid=(bsz, lq // tq),
        in_specs=in_specs,
        out_specs=pl.BlockSpec((None, tq, qd), lambda b, i: (b, i, 0)),
        compiler_params=_cparams("parallel", "parallel"),
        name="wa_attention",
    )(*args)


def _rope_tables(n, hd):
    rows = n // GRID_W
    row = jnp.repeat(jnp.arange(rows, dtype=F32), GRID_W)
    col = jnp.tile(jnp.arange(GRID_W, dtype=F32), rows)
    n_freq = hd // 4
    inv = ROPE_BASE ** (-jnp.arange(n_freq, dtype=F32) / n_freq)
    ang = jnp.concatenate([row[:, None] * inv, col[:, None] * inv], axis=-1)
    cos, sin = jnp.cos(ang), jnp.sin(ang)
    return jnp.concatenate([cos, cos], axis=-1), jnp.concatenate([-sin, sin], axis=-1)


def _wa_mixer(h, hc, g, mod, mod_c, w_in, sink, w_out, with_ctx):
    bsz, n, d = h.shape
    lctx = hc.shape[1] // bsz
    hd = d // WA_Q_HEADS
    w = w_in.astype(BF16)
    rope_cols = (WA_Q_HEADS + WA_KV_HEADS) * hd
    zl = _proj(h, g, mod, 3, w, BF16, rope=_rope_tables(n, hd), rope_cols=rope_cols)
    zc = _proj(hc, g, mod_c, 3, w, BF16).reshape(bsz, lctx, -1)
    sink = sink.astype(F32)
    w_o = w_out.astype(BF16)
    h = _out_proj(_wa_attention(sink, zl, zl, zc, True), w_o, h, mod, 5)
    if with_ctx:
        yc = _wa_attention(sink, zc, None, zc, False)
        hc = _out_proj(yc.reshape(1, bsz * lctx, d), w_o, hc, mod_c, 5)
    return h, hc


def _s5_proj_kernel(x_ref, g_ref, sh_ref, sc_ref, w_ref, of_ref, hn_ref):
    n = pl.program_id(2)

    @pl.when(n == 0)
    def _():
        _adaln_rows(hn_ref, x_ref, g_ref, sh_ref, sc_ref)

    hn = hn_ref[...]
    tm, tn = hn.shape[0], w_ref.shape[1]
    sub = _sub_cols(tn)
    for c in range(tn // sub):
        acc = jnp.dot(hn, w_ref[:, c * sub:(c + 1) * sub], preferred_element_type=F32)
        for k in range(sub // LANES):
            blk = acc[:, k * LANES:(k + 1) * LANES]
            of_ref[c * (sub // LANES) + k] = blk.reshape(tm // S5_CHUNK, S5_CHUNK, LANES)


def _s5_proj(x, g, mod, k0, w, shared_mod):
    bsz, rows, d = x.shape
    tm = _row_tile(rows, ROW_TILE_STREAM)
    tn = _col_tile(d, 1024)
    lc = S5_CHUNK
    return pl.pallas_call(
        _s5_proj_kernel,
        out_shape=jax.ShapeDtypeStruct((d // LANES, rows // lc, bsz, lc, LANES), F32),
        grid=(bsz, rows // tm, d // tn),
        in_specs=[pl.BlockSpec((None, tm, d), lambda b, i, n: (b, i, 0)),
                  pl.BlockSpec((1, d), lambda b, i, n: (0, 0)),
                  _mod_spec(k0, d, shared_mod), _mod_spec(k0 + 1, d, shared_mod),
                  pl.BlockSpec((d, tn), lambda b, i, n: (0, n))],
        out_specs=pl.BlockSpec((tn // LANES, tm // lc, None, lc, LANES), lambda b, i, n: (n, i, b, 0, 0)),
        scratch_shapes=[pltpu.VMEM((tm, d), BF16)],
        compiler_params=_cparams("parallel", "parallel", "arbitrary"),
        name="s5_proj",
    )(x, g.reshape(1, d), mod, mod, w)


S5_PACKETS = LANES // S5_GROUP
S5_REGROUP_ROWS = 16


def _packet_transpose(vs):
    lane = lax.broadcasted_iota(jnp.int32, vs[0].shape, 1)
    d = S5_PACKETS // 2
    while d:
        low = (lane & (d * S5_GROUP)) == 0
        nxt = list(vs)
        for i in range(S5_PACKETS):
            if not i & d:
                nxt[i] = jnp.where(low, vs[i], pltpu.roll(vs[i + d], d * S5_GROUP, axis=1))
                nxt[i + d] = jnp.where(low, pltpu.roll(vs[i], LANES - d * S5_GROUP, axis=1), vs[i + d])
        vs = nxt
        d //= 2
    return vs


def _s5_group_kernel(u_ref, o_ref):
    def body(r, carry):
        rows = pl.ds(pl.multiple_of(r * S5_REGROUP_ROWS, S5_REGROUP_ROWS), S5_REGROUP_ROWS)
        for hf in range(S5_CHUNK // S5_PACKETS):
            base = hf * S5_PACKETS
            vs = [pltpu.bitcast(u_ref[rows, (base + t) * LANES:(base + t + 1) * LANES].astype(o_ref.dtype),
                                jnp.uint32) for t in range(S5_PACKETS)]
            for gq, v in enumerate(_packet_transpose(vs)):
                o_ref[gq, rows, hf * LANES:(hf + 1) * LANES] = pltpu.bitcast(v, o_ref.dtype)
        return carry

    lax.fori_loop(0, u_ref.shape[0] // S5_REGROUP_ROWS, body, 0, unroll=4)


def _s5_group(u):
    g8, rows, wide = u.shape
    tr = _row_tile(rows, ROW_TILE)
    w = wide // S5_PACKETS
    return pl.pallas_call(
        _s5_group_kernel,
        out_shape=jax.ShapeDtypeStruct((g8 * S5_PACKETS, rows, w), BF16),
        grid=(g8, rows // tr),
        in_specs=[pl.BlockSpec((None, tr, wide), lambda g, i: (g, i, 0))],
        out_specs=pl.BlockSpec((S5_PACKETS, tr, w), lambda g, i: (g, i, 0)),
        compiler_params=_cparams("parallel", "parallel"),
        name="s5_group",
    )(u)


def _s5_ungroup_kernel(y_ref, u_ref, ds_ref, o_ref):
    def body(r, carry):
        rows = pl.ds(pl.multiple_of(r * S5_REGROUP_ROWS, S5_REGROUP_ROWS), S5_REGROUP_ROWS)
        for hf in range(S5_CHUNK // S5_PACKETS):
            base = hf * S5_PACKETS
            vs = [pltpu.bitcast(y_ref[gq, rows, hf * LANES:(hf + 1) * LANES], jnp.uint32)
                  for gq in range(S5_PACKETS)]
            for t, v in enumerate(_packet_transpose(vs)):
                cols = slice((base + t) * LANES, (base + t + 1) * LANES)
                y = pltpu.bitcast(v, F32)
                o_ref[rows, cols] = jax.nn.gelu(y + ds_ref[...] * u_ref[rows, cols]).astype(o_ref.dtype)
        return carry

    lax.fori_loop(0, o_ref.shape[0] // S5_REGROUP_ROWS, body, 0, unroll=2)


def _s5_ungroup(y, u, d_skip):
    ng, rows, w = y.shape
    tr = _row_tile(rows, ROW_TILE)
    wide = w * S5_PACKETS
    wide_spec = pl.BlockSpec((None, tr, wide), lambda g, i: (g, i, 0))
    return pl.pallas_call(
        _s5_ungroup_kernel,
        out_shape=jax.ShapeDtypeStruct((ng // S5_PACKETS, rows, wide), BF16),
        grid=(ng // S5_PACKETS, rows // tr),
        in_specs=[pl.BlockSpec((S5_PACKETS, tr, w), lambda g, i: (g, i, 0)), wide_spec,
                  pl.BlockSpec((None, 1, LANES), lambda g, i: (g, 0, 0))],
        out_specs=wide_spec,
        compiler_params=_cparams("parallel", "parallel"),
        name="s5_ungroup",
    )(y, u, d_skip)


S5_SCAN_GROUPS = 2


def _s5_kernel(uc_ref, ul_ref, kl_ref, q_ref, p_ref, a_ref, yc_ref, yl_ref,
               sc_ref, sl_ref, xc_ref, xl_ref, t_ref, *, bsz):
    gs = range(uc_ref.shape[0])
    dd = range(2)
    half = sc_ref.shape[3] // 2
    nch = kl_ref.shape[2]
    for ti in range(S5_CHUNK):
        for to in range(S5_CHUNK):
            for g in gs:
                t_ref[g, ti * nch:(ti + 1) * nch, to * nch:(to + 1) * nch] = kl_ref[g, to - ti + S5_CHUNK - 1]
    for u_ref, s_ref in ((uc_ref, sc_ref), (ul_ref, sl_ref)):
        for g in gs:
            for d in dd:
                s_ref[g, d] = jnp.dot(u_ref[g], q_ref[d, g], preferred_element_type=F32)
    aa = [[a_ref[d, g, 0:1, :] for d in dd] for g in gs]
    ab = [[a_ref[d, g, 1:2, :] for d in dd] for g in gs]

    def run(s_ref, x_ref, carry):
        n = s_ref.shape[2] // bsz

        def step(g, d, j, w):
            rows = pl.ds(pl.multiple_of(j * bsz, bsz), bsz)
            x_ref[g, d, rows, :] = w[:, :half]
            other = jnp.concatenate([w[:, half:], w[:, :half]], axis=1)
            return aa[g][d] * w + ab[g][d] * other + s_ref[g, d, rows, :]

        def body(i, ws):
            return tuple((step(g, 0, i, ws[g][0]), step(g, 1, n - 1 - i, ws[g][1])) for g in gs)

        return lax.fori_loop(0, n, body, carry)

    zero = jnp.zeros((bsz, 2 * half), F32)
    run(sl_ref, xl_ref, run(sc_ref, xc_ref, tuple((zero, zero) for _ in gs)))
    t_m = [t_ref[g].astype(BF16) for g in gs]
    for u_ref, x_ref, y_ref in ((uc_ref, xc_ref, yc_ref), (ul_ref, xl_ref, yl_ref)):
        for g in gs:
            y = jnp.dot(u_ref[g], t_m[g], preferred_element_type=F32)
            for d in dd:
                y = y + lax.dot_general(x_ref[g, d].astype(BF16), p_ref[d, g], (((1,), (1,)), ((), ())),
                                        preferred_element_type=F32)
            y_ref[g] = y


def _s5_scan(uc, ul, k_lag, q_m, p_m, a_m, bsz):
    ng, rc, w = uc.shape
    rl = ul.shape[1]
    st2 = p_m.shape[3]
    nlag, nch = k_lag.shape[1:3]
    gb = S5_SCAN_GROUPS if ng % S5_SCAN_GROUPS == 0 else 1

    def rows_spec(r):
        return pl.BlockSpec((gb, r, w), lambda g: (g, 0, 0))

    return pl.pallas_call(
        functools.partial(_s5_kernel, bsz=bsz),
        out_shape=(jax.ShapeDtypeStruct((ng, rc, w), F32), jax.ShapeDtypeStruct((ng, rl, w), F32)),
        grid=(ng // gb,),
        in_specs=[rows_spec(rc), rows_spec(rl),
                  pl.BlockSpec((gb, nlag, nch, nch), lambda g: (g, 0, 0, 0)),
                  pl.BlockSpec((2, gb, w, 2 * st2), lambda g: (0, g, 0, 0)),
                  pl.BlockSpec((2, gb, w, st2), lambda g: (0, g, 0, 0)),
                  pl.BlockSpec((2, gb, 2, 2 * st2), lambda g: (0, g, 0, 0))],
        out_specs=(rows_spec(rc), rows_spec(rl)),
        scratch_shapes=[pltpu.VMEM((gb, 2, rc, 2 * st2), F32), pltpu.VMEM((gb, 2, rl, 2 * st2), F32),
                        pltpu.VMEM((gb, 2, rc, st2), F32), pltpu.VMEM((gb, 2, rl, st2), F32),
                        pltpu.VMEM((gb, w, w), F32)],
        compiler_params=_cparams("parallel"),
        name="s5_scan",
    )(uc, ul, k_lag, q_m, p_m, a_m)


def _s5_operators(lam_re, lam_im, log_dt, b_re, b_im, c_re, c_im):
    lc = S5_CHUNK
    dt = jnp.exp(log_dt)[..., None]
    mag = jnp.exp(lam_re * dt)
    lb_re, lb_im = mag * jnp.cos(lam_im * dt), mag * jnp.sin(lam_im * dt)
    den = lam_re * lam_re + lam_im * lam_im
    nr, ni = lb_re - 1.0, lb_im
    fr = (nr * lam_re + ni * lam_im) / den
    fi = (ni * lam_re - nr * lam_im) / den
    bb_re = fr[..., None] * b_re - fi[..., None] * b_im
    bb_im = fr[..., None] * b_im + fi[..., None] * b_re
    k = jnp.arange(lc + 1, dtype=F32)[:, None, None, None]
    pmag = jnp.exp(k * (lam_re * dt))
    pw_re, pw_im = pmag * jnp.cos(k * (lam_im * dt)), pmag * jnp.sin(k * (lam_im * dt))
    ngrp, nch = lam_re.shape[1], b_re.shape[3]
    nst = lam_re.shape[2]
    pw_t = (jnp.transpose(pw_re, (1, 2, 0, 3)), jnp.transpose(pw_im, (1, 2, 0, 3)))
    bb_t = (jnp.transpose(bb_re, (0, 1, 3, 2)), jnp.transpose(bb_im, (0, 1, 3, 2)))

    lb_re = pw_t[0][:, :, :, None, :] * bb_t[0][:, :, None] - pw_t[1][:, :, :, None, :] * bb_t[1][:, :, None]
    lb_im = pw_t[0][:, :, :, None, :] * bb_t[1][:, :, None] + pw_t[1][:, :, :, None, :] * bb_t[0][:, :, None]
    kern = jnp.sum(lb_re[:, :, :, :, None, :] * c_re[:, :, None, None]
                   - lb_im[:, :, :, :, None, :] * c_im[:, :, None, None], axis=-1)
    k_lag = jnp.concatenate([jnp.flip(kern[1, :, 1:lc], axis=1), (kern[0, :, 0] + kern[1, :, 0])[:, None],
                             kern[0, :, 1:lc]], axis=1)

    def per_dir(d):
        e_in = (lc - 1 - jnp.arange(lc)) if d == 0 else jnp.arange(lc)
        l_re, l_im = (pw[d][:, e_in][:, :, None, :] for pw in pw_t)
        bt_re, bt_im = (bb[d][:, None] for bb in bb_t)
        q_re = (l_re * bt_re - l_im * bt_im).reshape(ngrp, lc * nch, nst)
        q_im = (l_re * bt_im + l_im * bt_re).reshape(ngrp, lc * nch, nst)
        q_m = jnp.concatenate([q_re, q_im, q_im, q_re], axis=-1)
        e_out = (jnp.arange(lc) + 1) if d == 0 else (lc - jnp.arange(lc))
        l_re, l_im = (pw[d][:, e_out][:, :, None, :] for pw in pw_t)
        cd_re, cd_im = c_re[d][:, None], c_im[d][:, None]
        p_re = (cd_re * l_re - cd_im * l_im).reshape(ngrp, lc * nch, nst)
        p_im = (cd_re * l_im + cd_im * l_re).reshape(ngrp, lc * nch, nst)
        p_m = jnp.concatenate([p_re, -p_im], axis=-1)
        a_re, a_im = pw_re[lc, d], pw_im[lc, d]
        a_m = jnp.stack([jnp.concatenate([a_re] * 4, axis=-1),
                         jnp.concatenate([-a_im, a_im, a_im, -a_im], axis=-1)], axis=1)
        return q_m, p_m, a_m

    q_m, p_m, a_m = (jnp.stack([x, y]) for x, y in zip(per_dir(0), per_dir(1)))
    return k_lag, q_m.astype(BF16), p_m.astype(BF16), a_m


def _s5_glu_kernel(zf_ref, wa_ref, wg_ref, h_ref, gate_ref, o_ref, z_ref):
    n = pl.program_id(2)

    @pl.when(n == 0)
    def _():
        tm = z_ref.shape[0]
        for k in range(zf_ref.shape[0]):
            z_ref[:, k * LANES:(k + 1) * LANES] = zf_ref[k].reshape(tm, LANES)

    z = z_ref[...]
    tn = wa_ref.shape[1]
    sub = _sub_cols(tn)
    for c in range(tn // sub):
        cols = slice(c * sub, (c + 1) * sub)
        a = jnp.dot(z, wa_ref[:, cols], preferred_element_type=F32)
        gt = jnp.dot(z, wg_ref[:, cols], preferred_element_type=F32)
        o_ref[:, cols] = h_ref[:, cols] + gate_ref[:, cols] * (a * jax.nn.sigmoid(gt))


def _s5_glu(z, w, h, mod, k, shared_mod):
    bsz, rows, d = h.shape
    tm = _row_tile(rows, ROW_TILE_STREAM)
    tn = _col_tile(d, 512)
    nn = d // tn
    lc = S5_CHUNK
    col = pl.BlockSpec((None, tm, tn), lambda b, i, n: (b, i, n))
    return pl.pallas_call(
        _s5_glu_kernel,
        out_shape=jax.ShapeDtypeStruct(h.shape, F32),
        grid=(bsz, rows // tm, nn),
        in_specs=[pl.BlockSpec((d // LANES, tm // lc, None, lc, LANES), lambda b, i, n: (0, i, b, 0, 0)),
                  pl.BlockSpec((d, tn), lambda b, i, n: (0, n)),
                  pl.BlockSpec((d, tn), lambda b, i, n: (0, n + nn)),
                  col, pl.BlockSpec((None, None, 1, tn),
                                    lambda b, i, n: (0 if shared_mod else b, k, 0, n))],
        out_specs=col,
        scratch_shapes=[pltpu.VMEM((tm, d), BF16)],
        compiler_params=_cparams("parallel", "parallel", "arbitrary"),
        name="s5_glu",
    )(z, w, w, h, mod)


def _s5_mixer(h, hc, g, mod, mod_c, w_in, lam_re, lam_im, log_dt, b_re, b_im, c_re, c_im,
              d_skip, w_out, with_ctx):
    bsz, n, d = h.shape
    lctx = hc.shape[1] // bsz
    lc = S5_CHUNK
    g8 = d // LANES
    w = w_in.astype(BF16)
    hc3 = hc.reshape(bsz, lctx, d)
    u_l = _s5_proj(h, g, mod, 3, w, False).reshape(g8, n // lc * bsz, lc * LANES)
    u_c = _s5_proj(hc3, g, mod_c, 3, w, True).reshape(g8, lctx // lc * bsz, lc * LANES)
    ops = _s5_operators(lam_re, lam_im, log_dt, b_re, b_im, c_re, c_im)
    yp_c, yp_l = _s5_scan(_s5_group(u_c), _s5_group(u_l), *ops, bsz=bsz)
    ds = d_skip.astype(F32).reshape(g8, 1, LANES)
    w_o = w_out.astype(BF16)
    z_l = _s5_ungroup(yp_l, u_l, ds).reshape(g8, n // lc, bsz, lc, LANES)
    h = _s5_glu(z_l, w_o, h, mod, 5, False)
    if with_ctx:
        z_c = _s5_ungroup(yp_c, u_c, ds).reshape(g8, lctx // lc, bsz, lc, LANES)
        hc = _s5_glu(z_c, w_o, hc3, mod_c, 5, True).reshape(1, bsz * lctx, d)
    return h, hc


def _rmsnorm_kernel(x_ref, g_ref, o_ref):
    x = x_ref[...]
    o_ref[...] = x * lax.rsqrt(jnp.mean(x * x, axis=-1, keepdims=True) + EPS) * g_ref[...]


def _final_norm(x, g):
    bsz, rows, d = x.shape
    tm = _row_tile(rows, ROW_TILE)
    row = pl.BlockSpec((None, tm, d), lambda b, i: (b, i, 0))
    return pl.pallas_call(
        _rmsnorm_kernel,
        out_shape=jax.ShapeDtypeStruct(x.shape, F32),
        grid=(bsz, rows // tm),
        in_specs=[row, pl.BlockSpec((1, d), lambda b, i: (0, 0))],
        out_specs=row,
        compiler_params=_cparams("parallel", "parallel"),
        name="final_norm",
    )(x, g.reshape(1, d))


def kernel(x, c, ctx, c_ctx, w_ada, b_ada, g_norm, w_ffn_in, w_ffn_out, g_final, ml_w_in, ml_b_gate, ml_g_head, ml_w_out, wa_w_in, wa_sink, wa_w_out, s5_w_in, s5_lam_re, s5_lam_im, s5_log_dt, s5_b_re, s5_b_im, s5_c_re, s5_c_im, s5_d_skip, s5_w_out):
    bsz, n, d = x.shape
    depth = w_ada.shape[0]
    lctx = ctx.shape[1]
    n_rows = -(-(bsz + 1) // 16) * 16
    c_rows = jnp.concatenate([c, c_ctx[None], jnp.zeros((n_rows - bsz - 1, d), F32)], axis=0)
    mods = _mod_table(c_rows, w_ada, b_ada).reshape(depth, n_rows, N_MOD, 1, d)
    w_in, w_out = _to_bf16(w_ffn_in), _to_bf16(w_ffn_out)
    h = x
    hc = ctx.reshape(1, bsz * lctx, d)
    for layer in range(depth):
        has_next = layer < depth - 1
        mod, mod_c = mods[layer, :bsz], mods[layer, bsz:bsz + 1]
        g = g_norm[layer]
        h = _ffn(h, g[0], mod, 0, w_in, w_out, layer, 0)
        hc = _ffn(hc, g[0], mod_c, 0, w_in, w_out, layer, 0)
        kind, idx = layer % 3, layer // 3
        if kind == 0:
            h, hc = _mlstm_mixer(h, hc, g[1], mod, mod_c, ml_w_in[idx], ml_b_gate[idx],
                                 ml_g_head[idx], ml_w_out[idx], has_next)
        elif kind == 1:
            h, hc = _wa_mixer(h, hc, g[1], mod, mod_c, wa_w_in[idx], wa_sink[idx],
                              wa_w_out[idx], has_next)
        else:
            h, hc = _s5_mixer(h, hc, g[1], mod, mod_c, s5_w_in[idx], s5_lam_re[idx], s5_lam_im[idx],
                              s5_log_dt[idx], s5_b_re[idx], s5_b_im[idx], s5_c_re[idx],
                              s5_c_im[idx], s5_d_skip[idx], s5_w_out[idx], has_next)
        h = _ffn(h, g[2], mod, 6, w_in, w_out, layer, 1)
        if has_next:
            hc = _ffn(hc, g[2], mod_c, 6, w_in, w_out, layer, 1)
    return _final_norm(h, g_final)
```

```python
import functools
import math

import jax
import jax.numpy as jnp
from jax import lax
from jax.experimental import pallas as pl
from jax.experimental.pallas import tpu as pltpu

F32 = jnp.float32
BF16 = jnp.bfloat16

EPS = 1e-6
NEG_BIG = -1e30
LOG2E = math.log2(math.e)
N_MOD = 9
ML_HEADS = 8
ML_CHUNK = 512
ML_MERGE_ROWS = 256
WA_Q_HEADS = 16
WA_KV_HEADS = 4
WA_WINDOW = 128
WA_BLOCK = 128
GRID_W = 64
ROPE_BASE = 10000.0
S5_GROUP = 16
S5_STATE = 64
S5_CHUNK = 16

LANES = 128
MXU_COLS = 256
ROW_TILE = 512
ROW_TILE_STREAM = 1024
CAST_BLOCK_BYTES = 8 * 1024 * 1024
VMEM_LIMIT = 56 * 1024 * 1024


def _cparams(*sem):
    return pltpu.CompilerParams(dimension_semantics=sem, vmem_limit_bytes=VMEM_LIMIT)


def _row_tile(rows, want):
    return want if rows % want == 0 else rows


def _sub_cols(cols):
    return MXU_COLS if cols % MXU_COLS == 0 else cols


def _col_tile(cols, want):
    t = min(want, cols)
    while cols % t:
        t -= LANES
    return t


def _adaln(x, g, shift, scale):
    var = jnp.mean(x * x, axis=-1, keepdims=True)
    return (x * lax.rsqrt(var + EPS) * g) * (1.0 + scale) + shift


ADALN_ROWS = 128


def _adaln_rows(hn_ref, x_ref, g_ref, sh_ref, sc_ref):
    rows = x_ref.shape[0]
    step = ADALN_ROWS if rows % ADALN_ROWS == 0 else rows

    def body(r, carry):
        sl = pl.ds(pl.multiple_of(r * step, step), step)
        hn_ref[sl, :] = _adaln(x_ref[sl, :], g_ref[...], sh_ref[...], sc_ref[...]).astype(BF16)
        return carry

    lax.fori_loop(0, rows // step, body, 0)


def _mod_spec(k, d, shared=False):
    return pl.BlockSpec((None, None, 1, d), lambda b, *_: (0 if shared else b, k, 0, 0))


def _mod_kernel(c_ref, w_ref, b_ref, o_ref):
    c = c_ref[...]
    s = (c * jax.nn.sigmoid(c)).astype(BF16)
    o_ref[...] = jnp.dot(s, w_ref[...].astype(BF16), preferred_element_type=F32) + b_ref[...]


def _mod_table(c_rows, w_ada, b_ada):
    depth, d, nd = w_ada.shape
    r = c_rows.shape[0]
    tn = _col_tile(nd, 1024)
    return pl.pallas_call(
        _mod_kernel,
        out_shape=jax.ShapeDtypeStruct((depth, r, nd), F32),
        grid=(depth, nd // tn),
        in_specs=[pl.BlockSpec((r, d), lambda l, n: (0, 0)),
                  pl.BlockSpec((None, d, tn), lambda l, n: (l, 0, n)),
                  pl.BlockSpec((None, 1, tn), lambda l, n: (l, 0, n))],
        out_specs=pl.BlockSpec((None, r, tn), lambda l, n: (l, 0, n)),
        compiler_params=_cparams("parallel", "parallel"),
        name="mod_table",
    )(c_rows, w_ada, b_ada.reshape(depth, 1, nd))


def _adaln_tile_kernel(x_ref, g_ref, sh_ref, sc_ref, o_ref):
    _adaln_rows(o_ref, x_ref, g_ref, sh_ref, sc_ref)


def _adaln_tile(x, g, mod, k0, tm):
    d = x.shape[2]
    return pl.pallas_call(
        _adaln_tile_kernel,
        out_shape=jax.ShapeDtypeStruct((tm, d), BF16),
        grid=(1,),
        in_specs=[pl.BlockSpec((None, tm, d), lambda b: (0, 0, 0)),
                  pl.BlockSpec((1, d), lambda b: (0, 0)),
                  _mod_spec(k0, d), _mod_spec(k0 + 1, d)],
        out_specs=pl.BlockSpec((tm, d), lambda b: (0, 0)),
        compiler_params=_cparams("arbitrary"),
        name="adaln_tile",
    )(x, g, mod, mod)


def _ffn_kernel(hn0_ref, xr_ref, g_ref, sh_ref, sc_ref, x_ref, gate_ref, wa_ref, wg_ref, wo_ref, o_ref,
                hn_ref, act_ref, *, nf):
    j = pl.program_id(2)
    tf = wa_ref.shape[1]

    @pl.when((j == 0) & (pl.program_id(0) == 0) & (pl.program_id(1) == 0))
    def _():
        hn_ref[...] = hn0_ref[...]

    @pl.when(j < nf)
    def _():
        hn = hn_ref[...]
        sub = _sub_cols(tf)
        for c in range(tf // sub):
            a = jnp.dot(hn, wa_ref[:, c * sub:(c + 1) * sub], preferred_element_type=F32)
            gt = jnp.dot(hn, wg_ref[:, c * sub:(c + 1) * sub], preferred_element_type=F32)
            cols = pl.ds(pl.multiple_of(j * tf + c * sub, sub), sub)
            act_ref[:, cols] = (a * (gt * jax.nn.sigmoid(gt))).astype(BF16)

    @pl.when(j >= nf)
    def _():
        y = jnp.dot(act_ref[...], wo_ref[...], preferred_element_type=F32)
        o_ref[...] = x_ref[...] + (0.5 * gate_ref[...]) * y
        rp = xr_ref.shape[0]
        rows = pl.ds(pl.multiple_of((j - nf) * rp, rp), rp)
        hn_ref[rows, :] = _adaln(xr_ref[...], g_ref[...], sh_ref[...], sc_ref[...]).astype(BF16)


def _cast_kernel(x_ref, o_ref):
    o_ref[...] = x_ref[...].astype(o_ref.dtype)


def _to_bf16(w):
    shape = w.shape
    cols = shape[-1]
    w2 = w.reshape(-1, cols)
    tr = 1 << ((CAST_BLOCK_BYTES // (4 * cols)).bit_length() - 1)
    while w2.shape[0] % tr:
        tr //= 2
    blk = pl.BlockSpec((tr, cols), lambda i: (i, 0))
    out = pl.pallas_call(
        _cast_kernel,
        out_shape=jax.ShapeDtypeStruct(w2.shape, BF16),
        grid=(w2.shape[0] // tr,),
        in_specs=[blk],
        out_specs=blk,
        compiler_params=_cparams("parallel"),
        name="to_bf16",
    )(w2)
    return out.reshape(shape)


def _ffn(x, g, mod, k0, w_in, w_out, layer, half):
    bsz, rows, d = x.shape
    ff = w_out.shape[2]
    tm = _row_tile(rows, ROW_TILE_STREAM)
    tf = _col_tile(ff, 512)
    tn = _col_tile(d, 512)
    nf = ff // tf
    nt = rows // tm
    nd = d // tn
    steps = nf + nd
    rp = tm // nd
    assert rp * nd == tm

    def fill(j):
        return jnp.minimum(j, nf - 1)

    def drain(j):
        return jnp.maximum(j - nf, 0)

    def ahead(b, i, j):
        r = jnp.minimum(b * nt + i + (j >= nf).astype(jnp.int32), bsz * nt - 1)
        return r // nt, r % nt

    def mod_ahead(k):
        return pl.BlockSpec((None, None, 1, d), lambda b, i, j: (ahead(b, i, j)[0], k, 0, 0))

    return pl.pallas_call(
        functools.partial(_ffn_kernel, nf=nf),
        out_shape=jax.ShapeDtypeStruct(x.shape, F32),
        grid=(bsz, nt, steps),
        in_specs=[pl.BlockSpec((tm, d), lambda b, i, j: (0, 0)),
                  pl.BlockSpec((None, rp, d),
                               lambda b, i, j: (ahead(b, i, j)[0], ahead(b, i, j)[1] * nd + drain(j), 0)),
                  pl.BlockSpec((1, d), lambda b, i, j: (0, 0)),
                  mod_ahead(k0), mod_ahead(k0 + 1),
                  pl.BlockSpec((None, tm, tn), lambda b, i, j: (b, i, drain(j))),
                  pl.BlockSpec((None, None, 1, tn), lambda b, i, j: (b, k0 + 2, 0, drain(j))),
                  pl.BlockSpec((None, None, d, tf), lambda b, i, j: (layer, half, 0, fill(j))),
                  pl.BlockSpec((None, None, d, tf), lambda b, i, j: (layer, half, 0, fill(j) + nf)),
                  pl.BlockSpec((None, None, ff, tn), lambda b, i, j: (layer, half, 0, drain(j)))],
        out_specs=pl.BlockSpec((None, tm, tn), lambda b, i, j: (b, i, drain(j))),
        scratch_shapes=[pltpu.VMEM((tm, d), BF16), pltpu.VMEM((tm, ff), BF16)],
        compiler_params=_cparams("arbitrary", "arbitrary", "arbitrary"),
        name="ffn",
    )(_adaln_tile(x, g.reshape(1, d), mod, k0, tm), x, g.reshape(1, d), mod, mod, x, mod, w_in, w_in, w_out)


def _proj_kernel(*refs, n_rope, with_gates, n_scaled, scale):
    x_ref, g_ref, sh_ref, sc_ref, w_ref = refs[:5]
    rest = refs[5:]
    if n_rope:
        cs_ref, sn_ref = rest[:2]
        rest = rest[2:]
    if with_gates:
        wg_ref, bg_ref, o_ref, og_ref, hn_ref = rest
    else:
        o_ref, hn_ref = rest
    n = pl.program_id(2)

    @pl.when(n == 0)
    def _():
        _adaln_rows(hn_ref, x_ref, g_ref, sh_ref, sc_ref)
        if with_gates:
            og_ref[...] = jnp.dot(hn_ref[...], wg_ref[...], preferred_element_type=F32) + bg_ref[...]

    def columns(rope):
        hn = hn_ref[...]
        tn = w_ref.shape[1]
        sub = _sub_cols(tn)
        for c in range(tn // sub):
            acc = jnp.dot(hn, w_ref[:, c * sub:(c + 1) * sub], preferred_element_type=F32)
            if n_scaled:
                acc = acc * jnp.where(n < n_scaled, scale, 1.0)
            if not rope:
                o_ref[:, c * sub:(c + 1) * sub] = acc.astype(o_ref.dtype)
            else:
                cs, sn = cs_ref[...], sn_ref[...]
                hd = cs.shape[1]
                for h in range(sub // hd):
                    r = acc[:, h * hd:(h + 1) * hd]
                    r = r * cs + pltpu.roll(r, hd // 2, axis=1) * sn
                    o_ref[:, c * sub + h * hd:c * sub + (h + 1) * hd] = r.astype(o_ref.dtype)

    if not n_rope:
        columns(False)
    else:
        @pl.when(n >= n_rope)
        def _():
            columns(False)

        @pl.when(n < n_rope)
        def _():
            columns(True)


def _proj(x, g, mod, k0, w, out_dtype, rope=None, rope_cols=0, gates=None, scale=1.0, scale_cols=0):
    bsz, rows, d = x.shape
    nout = w.shape[1]
    tm = _row_tile(rows, ROW_TILE_STREAM)
    tn = _col_tile(math.gcd(math.gcd(nout, rope_cols), scale_cols), 2048)
    n_rope = rope_cols // tn
    row = pl.BlockSpec((None, tm, d), lambda b, i, n: (b, i, 0))
    in_specs = [row, pl.BlockSpec((1, d), lambda b, i, n: (0, 0)),
                _mod_spec(k0, d), _mod_spec(k0 + 1, d),
                pl.BlockSpec((d, tn), lambda b, i, n: (0, n))]
    args = [x, g.reshape(1, d), mod, mod, w]
    if n_rope:
        cs, sn = rope
        hd = cs.shape[1]
        in_specs += [pl.BlockSpec((tm, hd), lambda b, i, n: (i, 0))] * 2
        args += [cs, sn]
    out_shape = jax.ShapeDtypeStruct((bsz, rows, nout), out_dtype)
    out_specs = pl.BlockSpec((None, tm, tn), lambda b, i, n: (b, i, n))
    if gates is not None:
        wg, bg = gates
        ng = wg.shape[1]
        in_specs += [pl.BlockSpec((d, ng), lambda b, i, n: (0, 0)),
                     pl.BlockSpec((1, ng), lambda b, i, n: (0, 0))]
        args += [wg, bg]
        out_shape = (out_shape, jax.ShapeDtypeStruct((bsz, rows, ng), F32))
        out_specs = (out_specs, pl.BlockSpec((None, tm, ng), lambda b, i, n: (b, i, 0)))
    return pl.pallas_call(
        functools.partial(_proj_kernel, n_rope=n_rope, with_gates=gates is not None,
                          n_scaled=scale_cols // tn, scale=scale),
        out_shape=out_shape,
        grid=(bsz, rows // tm, nout // tn),
        in_specs=in_specs,
        out_specs=out_specs,
        scratch_shapes=[pltpu.VMEM((tm, d), BF16)],
        compiler_params=_cparams("parallel", "parallel", "arbitrary"),
        name="proj",
    )(*args)


def _out_proj_kernel(y_ref, w_ref, h_ref, gate_ref, o_ref):
    y = y_ref[...]
    d = w_ref.shape[1]
    sub = _sub_cols(d)
    for c in range(d // sub):
        cols = slice(c * sub, (c + 1) * sub)
        acc = jnp.dot(y, w_ref[:, cols], preferred_element_type=F32)
        o_ref[:, cols] = h_ref[:, cols] + gate_ref[:, cols] * acc


def _out_proj(y, w, h, mod, k):
    bsz, rows, d = h.shape
    dk = y.shape[2]
    tm = _row_tile(rows, ROW_TILE_STREAM)
    return pl.pallas_call(
        _out_proj_kernel,
        out_shape=jax.ShapeDtypeStruct(h.shape, F32),
        grid=(bsz, rows // tm),
        in_specs=[pl.BlockSpec((None, tm, dk), lambda b, i: (b, i, 0)),
                  pl.BlockSpec((dk, d), lambda b, i: (0, 0), pipeline_mode=pl.Buffered(1)),
                  pl.BlockSpec((None, tm, d), lambda b, i: (b, i, 0)),
                  _mod_spec(k, d)],
        out_specs=pl.BlockSpec((None, tm, d), lambda b, i: (b, i, 0)),
        compiler_params=_cparams("parallel", "parallel"),
        name="out_proj",
    )(y, w, h, mod)


def _ml_chunk_len(length):
    return min(ML_CHUNK, length)


ML_GATE_SLOTS = 8


def _ml_gate_kernel(g_ref, o_ref, *, lc):
    row = lax.broadcasted_iota(jnp.int32, (lc, lc), 0)
    col = lax.broadcasted_iota(jnp.int32, (lc, lc), 1)
    lower = (col <= row).astype(BF16)
    upper = (col >= row).astype(BF16)
    slot = lax.broadcasted_iota(jnp.int32, (lc, LANES), 1) & (ML_GATE_SLOTS - 1)
    rix = lax.broadcasted_iota(jnp.int32, (lc, LANES), 0)

    def cumsum(tri, parts):
        return sum(jnp.dot(tri, p, preferred_element_type=F32) for p in parts)

    for c in range(g_ref.shape[0] // lc):
        x = g_ref[c * lc:(c + 1) * lc, :]
        ls = jax.nn.log_sigmoid(x)
        hi = ls.astype(BF16)
        r1 = ls - hi.astype(F32)
        mid = r1.astype(BF16)
        lo = (r1 - mid.astype(F32)).astype(BF16)
        pre = cumsum(lower, (hi, mid, lo))
        suf = cumsum(upper, (hi, mid, lo))
        a_f = x - pltpu.roll(pre, LANES - 1, axis=1)
        a_b = x - pltpu.roll(suf, LANES - 1, axis=1)
        cm_f, cm_b = a_f, a_b
        sh = 1
        while sh < lc:
            cm_f = jnp.maximum(cm_f, jnp.where(rix >= sh, pltpu.roll(cm_f, sh, axis=0), NEG_BIG))
            cm_b = jnp.maximum(cm_b, jnp.where(rix < lc - sh, pltpu.roll(cm_b, lc - sh, axis=0), NEG_BIG))
            sh *= 2
        out = jnp.where(slot == 0, a_f,
              jnp.where(slot == 1, pre,
              jnp.where(slot == 2, a_b,
              jnp.where(slot == 3, suf,
              jnp.where(slot == 4, pltpu.roll(cm_f, 4, axis=1), pltpu.roll(cm_b, 3, axis=1))))))
        o_ref[:, c * lc:(c + 1) * lc] = jnp.transpose(out)


def _ml_gates(g):
    bsz, rows, w = g.shape
    tm = _row_tile(rows, ROW_TILE_STREAM)
    return pl.pallas_call(
        functools.partial(_ml_gate_kernel, lc=_ml_chunk_len(rows)),
        out_shape=jax.ShapeDtypeStruct((bsz, w, rows), F32),
        grid=(bsz, rows // tm),
        in_specs=[pl.BlockSpec((None, tm, w), lambda b, i: (b, i, 0))],
        out_specs=pl.BlockSpec((None, w, tm), lambda b, i: (b, 0, i)),
        compiler_params=_cparams("parallel", "parallel"),
        name="ml_gates",
    )(g)


def _ml_chunk(q, k, v_aug, a_col, b_col, cm_col, a_row, c_mem, m_run, tri, last):
    dd = range(2)
    hd = q[0].shape[1]
    nt = (((1,), (1,)), ((), ()))
    tn = (((0,), (0,)), ((), ()))
    b_end = [b_col[d][last[d]:last[d] + 1, :] for d in dd]
    m_new = [b_end[d] + jnp.maximum(m_run[d], cm_col[d][last[d]:last[d] + 1, :]) for d in dd]
    r_col = [jnp.maximum(m_run[d], cm_col[d]) for d in dd]
    s = [lax.dot_general(q[d], k[d], nt, preferred_element_type=F32) for d in dd]
    c_bf = [c_mem[d].astype(BF16) for d in dd]
    qc = [jnp.dot(q[d], c_bf[d], preferred_element_type=F32) for d in dd]
    wk = [k[d] * jnp.exp(a_col[d] + (b_end[d] - m_new[d])).astype(BF16) for d in dd]
    e = [jnp.exp(jnp.where(tri[d], a_row[d] - r_col[d], NEG_BIG)) for d in dd]
    upd = [lax.dot_general(wk[d], v_aug[d], tn, preferred_element_type=F32) for d in dd]
    p = [(s[d] * e[d]).astype(BF16) for d in dd]
    dec = [jnp.exp(b_end[d] + m_run[d] - m_new[d]) for d in dd]
    dq = [jnp.exp(m_run[d] - r_col[d]) for d in dd]
    acc = [jnp.dot(p[d], v_aug[d], preferred_element_type=F32) + dq[d] * qc[d] for d in dd]
    c_new = [dec[d] * c_mem[d] + upd[d] for d in dd]
    lim = [jnp.exp(-(b_col[d] + r_col[d])) for d in dd]
    h = [acc[d][:, :hd] / jnp.maximum(jnp.abs(acc[d][:, hd:hd + 1]), lim[d]) for d in dd]
    return h, c_new, m_new


def _mlstm_kernel(cq, ck, cv, co, lq, lk, lv, lo, cgr, lgr, gh_ref,
                  yc_ref, yl_ref, hf_ref, hb_ref, c_ref, m_ref):
    c_ref[...] = jnp.zeros_like(c_ref)
    m_ref[...] = jnp.full_like(m_ref, NEG_BIG)

    def scan(q_ref, k_ref, v_ref, o_ref, gr_ref, y_ref):
        lc = _ml_chunk_len(q_ref.shape[0])
        nc = q_ref.shape[0] // lc
        rows = lax.broadcasted_iota(jnp.int32, (lc, lc), 0)
        cols = lax.broadcasted_iota(jnp.int32, (lc, lc), 1)
        tri = (cols <= rows, cols >= rows)
        ones_blk = jnp.where(lax.broadcasted_iota(jnp.int32, (lc, LANES), 1) == 0, 1.0, 0.0).astype(BF16)

        def body(c, carry):
            sl = (pl.ds(pl.multiple_of(c * lc, lc), lc), pl.ds(pl.multiple_of((nc - 1 - c) * lc, lc), lc))
            dd = range(2)
            gc = [jnp.transpose(gr_ref[:, sl[d]]) for d in dd]
            h, c_new, m_new = _ml_chunk(
                [q_ref[sl[d], :] for d in dd], [k_ref[sl[d], :] for d in dd],
                [jnp.concatenate([v_ref[sl[d], :], ones_blk], axis=1) for d in dd],
                [gc[d][:, 2 * d:2 * d + 1] for d in dd], [gc[d][:, 2 * d + 1:2 * d + 2] for d in dd],
                [gc[d][:, 4 + d:5 + d] for d in dd], [gr_ref[2 * d:2 * d + 1, sl[d]] for d in dd],
                [c_ref[d] for d in dd], [m_ref[d] for d in dd], tri, (lc - 1, 0))
            for d, out_ref in enumerate((hf_ref, hb_ref)):
                c_ref[d] = c_new[d]
                m_ref[d] = m_new[d]
                out_ref[sl[d], :] = h[d]
            return carry

        lax.fori_loop(0, nc, body, 0)

        lm = ML_MERGE_ROWS

        def merge(c, carry):
            sl = pl.ds(pl.multiple_of(c * lm, lm), lm)
            hs = hf_ref[sl, :] + hb_ref[sl, :]
            hs = hs * lax.rsqrt(jnp.mean(hs * hs, axis=-1, keepdims=True) + EPS) * gh_ref[...]
            y_ref[sl, :] = (jax.nn.sigmoid(o_ref[sl, :].astype(F32)) * hs).astype(y_ref.dtype)
            return carry

        lax.fori_loop(0, q_ref.shape[0] // lm, merge, 0)

    scan(cq, ck, cv, co, cgr, yc_ref)
    scan(lq, lk, lv, lo, lgr, yl_ref)


def _mlstm_scan(zc, zl, gc, gl, g_head):
    bsz, n, d4 = zl.shape
    lctx = zc.shape[1]
    d = d4 // 4
    nh = ML_HEADS
    hd = d // nh
    assert n % _ml_chunk_len(n) == 0 and n % ML_MERGE_ROWS == 0 and lctx % ML_MERGE_ROWS == 0

    def zspec(length, k):
        return pl.BlockSpec((None, length, hd), lambda b, h: (b, 0, k * nh + h))

    def gspec(length):
        return pl.BlockSpec((None, ML_GATE_SLOTS, length), lambda b, h: (b, h, 0))

    lmax = max(n, lctx)
    return pl.pallas_call(
        _mlstm_kernel,
        out_shape=(jax.ShapeDtypeStruct((bsz, lctx, d), BF16),
                   jax.ShapeDtypeStruct((bsz, n, d), BF16)),
        grid=(bsz, nh),
        in_specs=([zspec(lctx, k) for k in range(4)] + [zspec(n, k) for k in range(4)]
                  + [gspec(lctx), gspec(n)]
                  + [pl.BlockSpec((None, 1, hd), lambda b, h: (h, 0, 0))]),
        out_specs=(pl.BlockSpec((None, lctx, hd), lambda b, h: (b, 0, h)),
                   pl.BlockSpec((None, n, hd), lambda b, h: (b, 0, h))),
        scratch_shapes=[pltpu.VMEM((lmax, hd), F32), pltpu.VMEM((lmax, hd), F32),
                        pltpu.VMEM((2, hd, hd + LANES), F32), pltpu.VMEM((2, 1, 1), F32)],
        compiler_params=_cparams("parallel", "parallel"),
        name="mlstm_scan",
    )(zc, zc, zc, zc, zl, zl, zl, zl, gc, gl, g_head.reshape(nh, 1, hd))


def _mlstm_mixer(h, hc, g, mod, mod_c, w_in, b_gate, g_head, w_out, with_ctx):
    bsz, n, d = h.shape
    lctx = hc.shape[1] // bsz
    w_main = w_in[:, :4 * d].astype(BF16)
    nh = ML_HEADS
    ngp = -(-nh * ML_GATE_SLOTS // LANES) * LANES

    def head_major(t):
        t = jnp.swapaxes(t.reshape(t.shape[:-1] + (4, nh)), -1, -2)
        t = jnp.pad(t, [(0, 0)] * (t.ndim - 1) + [(0, ML_GATE_SLOTS - 4)])
        t = t.reshape(t.shape[:-2] + (nh * ML_GATE_SLOTS,))
        return jnp.pad(t, [(0, 0)] * (t.ndim - 1) + [(0, ngp - nh * ML_GATE_SLOTS)])

    w_gate = head_major(w_in[:, 4 * d:]).astype(BF16)
    bias = head_major(b_gate.reshape(1, 4 * nh))
    q_scale = (d // ML_HEADS) ** -0.5
    zl, gl = _proj(h, g, mod, 3, w_main, BF16, gates=(w_gate, bias), scale=q_scale, scale_cols=d)
    zc, gc = _proj(hc, g, mod_c, 3, w_main, BF16, gates=(w_gate, bias), scale=q_scale, scale_cols=d)
    yc, yl = _mlstm_scan(zc.reshape(bsz, lctx, 4 * d), zl,
                         _ml_gates(gc.reshape(bsz, lctx, ngp)), _ml_gates(gl), g_head)
    w_o = w_out.astype(BF16)
    h = _out_proj(yl, w_o, h, mod, 5)
    if with_ctx:
        hc = _out_proj(yc.reshape(1, bsz * lctx, d), w_o, hc, mod_c, 5)
    return h, hc


def _wa_kernel(sink_ref, q_ref, *refs, local, scale):
    if local:
        k_ref, v_ref, kc_ref, vc_ref, o_ref = refs
    else:
        kc_ref, vc_ref, o_ref = refs
    tq = q_ref.shape[0]
    hd = kc_ref.shape[1] // WA_KV_HEADS
    grp = WA_Q_HEADS // WA_KV_HEADS
    if local:
        n = k_ref.shape[0]
        span = 3 * WA_BLOCK
        start = pl.program_id(1) * tq
        ks = pl.multiple_of(jnp.clip(start - WA_BLOCK, 0, n - span), WA_BLOCK)
        q_pos = start + (lax.broadcasted_iota(jnp.int32, (grp * tq, span), 0) & (tq - 1))
        k_pos = ks + lax.broadcasted_iota(jnp.int32, (grp * tq, span), 1)
        ok = jnp.abs(q_pos - k_pos) <= WA_WINDOW
    dims = (((1,), (1,)), ((), ()))
    heads = range(WA_KV_HEADS)
    c2 = scale * LOG2E
    cs = [slice(kv * hd, (kv + 1) * hd) for kv in heads]
    q4 = [jnp.concatenate([q_ref[:, (kv * grp + j) * hd:(kv * grp + j + 1) * hd] for j in range(grp)], axis=0)
          for kv in heads]
    sink = [jnp.concatenate([jnp.full((tq, 1), sink_ref[0, kv * grp + j], F32) for j in range(grp)], axis=0)
            * (1.0 / scale) for kv in heads]
    s_ctx = [lax.dot_general(q4[kv], kc_ref[:, cs[kv]], dims, preferred_element_type=F32) for kv in heads]
    m = [jnp.maximum(jnp.max(s_ctx[kv], axis=1, keepdims=True), sink[kv]) for kv in heads]
    if local:
        s_loc = [jnp.where(ok, lax.dot_general(q4[kv], k_ref[pl.ds(ks, span), cs[kv]], dims,
                                               preferred_element_type=F32), NEG_BIG) for kv in heads]
        m = [jnp.maximum(m[kv], jnp.max(s_loc[kv], axis=1, keepdims=True)) for kv in heads]
        p_loc = [jnp.exp2((s_loc[kv] - m[kv]) * c2) for kv in heads]
    p_ctx = [jnp.exp2((s_ctx[kv] - m[kv]) * c2) for kv in heads]
    den = [jnp.sum(p_ctx[kv], axis=1, keepdims=True) + jnp.exp2((sink[kv] - m[kv]) * c2) for kv in heads]
    if local:
        den = [den[kv] + jnp.sum(p_loc[kv], axis=1, keepdims=True) for kv in heads]
    inv = [1.0 / den[kv] for kv in heads]
    out = [jnp.dot((p_ctx[kv] * inv[kv]).astype(BF16), vc_ref[:, cs[kv]], preferred_element_type=F32)
           for kv in heads]
    if local:
        out = [out[kv] + jnp.dot((p_loc[kv] * inv[kv]).astype(BF16), v_ref[pl.ds(ks, span), cs[kv]],
                                 preferred_element_type=F32) for kv in heads]
    for kv in heads:
        for j in range(grp):
            hq = kv * grp + j
            o_ref[:, hq * hd:(hq + 1) * hd] = out[kv][j * tq:(j + 1) * tq].astype(o_ref.dtype)


def _wa_attention(sink, zq, zkv, zc, local):
    bsz, lq, _ = zq.shape
    lctx = zc.shape[1]
    hd = zq.shape[2] // (WA_Q_HEADS + 2 * WA_KV_HEADS)
    qd, kd = WA_Q_HEADS * hd, WA_KV_HEADS * hd
    tq = WA_BLOCK
    kblk = qd // kd
    in_specs = [pl.BlockSpec(memory_space=pltpu.SMEM),
                pl.BlockSpec((None, tq, qd), lambda b, i: (b, i, 0))]
    args = [sink.reshape(1, WA_Q_HEADS), zq]
    if local:
        n = zkv.shape[1]
        in_specs += [pl.BlockSpec((None, n, kd), lambda b, i: (b, 0, kblk)),
                     pl.BlockSpec((None, n, kd), lambda b, i: (b, 0, kblk + 1))]
        args += [zkv, zkv]
    in_specs += [pl.BlockSpec((None, lctx, kd), lambda b, i: (b, 0, kblk)),
                 pl.BlockSpec((None, lctx, kd), lambda b, i: (b, 0, kblk + 1))]
    args += [zc, zc]
    return pl.pallas_call(
        functools.partial(_wa_kernel, local=local, scale=hd ** -0.5),
        out_shape=jax.ShapeDtypeStruct((bsz, lq, qd), BF16),
        grid=(bsz, lq // tq),
        in_specs=in_specs,
        out_specs=pl.BlockSpec((None, tq, qd), lambda b, i: (b, i, 0)),
        compiler_params=_cparams("parallel", "parallel"),
        name="wa_attention",
    )(*args)


def _rope_tables(n, hd):
    rows = n // GRID_W
    row = jnp.repeat(jnp.arange(rows, dtype=F32), GRID_W)
    col = jnp.tile(jnp.arange(GRID_W, dtype=F32), rows)
    n_freq = hd // 4
    inv = ROPE_BASE ** (-jnp.arange(n_freq, dtype=F32) / n_freq)
    ang = jnp.concatenate([row[:, None] * inv, col[:, None] * inv], axis=-1)
    cos, sin = jnp.cos(ang), jnp.sin(ang)
    return jnp.concatenate([cos, cos], axis=-1), jnp.concatenate([-sin, sin], axis=-1)


def _wa_mixer(h, hc, g, mod, mod_c, w_in, sink, w_out, with_ctx):
    bsz, n, d = h.shape
    lctx = hc.shape[1] // bsz
    hd = d // WA_Q_HEADS
    w = w_in.astype(BF16)
    rope_cols = (WA_Q_HEADS + WA_KV_HEADS) * hd
    zl = _proj(h, g, mod, 3, w, BF16, rope=_rope_tables(n, hd), rope_cols=rope_cols)
    zc = _proj(hc, g, mod_c, 3, w, BF16).reshape(bsz, lctx, -1)
    sink = sink.astype(F32)
    w_o = w_out.astype(BF16)
    h = _out_proj(_wa_attention(sink, zl, zl, zc, True), w_o, h, mod, 5)
    if with_ctx:
        yc = _wa_attention(sink, zc, None, zc, False)
        hc = _out_proj(yc.reshape(1, bsz * lctx, d), w_o, hc, mod_c, 5)
    return h, hc


def _s5_proj_kernel(x_ref, g_ref, sh_ref, sc_ref, w_ref, of_ref, hn_ref):
    n = pl.program_id(2)

    @pl.when(n == 0)
    def _():
        _adaln_rows(hn_ref, x_ref, g_ref, sh_ref, sc_ref)

    hn = hn_ref[...]
    tm, tn = hn.shape[0], w_ref.shape[1]
    sub = _sub_cols(tn)
    for c in range(tn // sub):
        acc = jnp.dot(hn, w_ref[:, c * sub:(c + 1) * sub], preferred_element_type=F32)
        for k in range(sub // LANES):
            blk = acc[:, k * LANES:(k + 1) * LANES]
            of_ref[c * (sub // LANES) + k] = blk.reshape(tm // S5_CHUNK, S5_CHUNK, LANES)


def _s5_proj(x, g, mod, k0, w, shared_mod):
    bsz, rows, d = x.shape
    tm = _row_tile(rows, ROW_TILE_STREAM)
    tn = _col_tile(d, 1024)
    lc = S5_CHUNK
    return pl.pallas_call(
        _s5_proj_kernel,
        out_shape=jax.ShapeDtypeStruct((d // LANES, rows // lc, bsz, lc, LANES), F32),
        grid=(bsz, rows // tm, d // tn),
        in_specs=[pl.BlockSpec((None, tm, d), lambda b, i, n: (b, i, 0)),
                  pl.BlockSpec((1, d), lambda b, i, n: (0, 0)),
                  _mod_spec(k0, d, shared_mod), _mod_spec(k0 + 1, d, shared_mod),
                  pl.BlockSpec((d, tn), lambda b, i, n: (0, n))],
        out_specs=pl.BlockSpec((tn // LANES, tm // lc, None, lc, LANES), lambda b, i, n: (n, i, b, 0, 0)),
        scratch_shapes=[pltpu.VMEM((tm, d), BF16)],
        compiler_params=_cparams("parallel", "parallel", "arbitrary"),
        name="s5_proj",
    )(x, g.reshape(1, d), mod, mod, w)


S5_PACKETS = LANES // S5_GROUP
S5_REGROUP_ROWS = 16


def _packet_transpose(vs):
    lane = lax.broadcasted_iota(jnp.int32, vs[0].shape, 1)
    d = S5_PACKETS // 2
    while d:
        low = (lane & (d * S5_GROUP)) == 0
        nxt = list(vs)
        for i in range(S5_PACKETS):
            if not i & d:
                nxt[i] = jnp.where(low, vs[i], pltpu.roll(vs[i + d], d * S5_GROUP, axis=1))
                nxt[i + d] = jnp.where(low, pltpu.roll(vs[i], LANES - d * S5_GROUP, axis=1), vs[i + d])
        vs = nxt
        d //= 2
    return vs


def _s5_group_kernel(u_ref, o_ref):
    def body(r, carry):
        rows = pl.ds(pl.multiple_of(r * S5_REGROUP_ROWS, S5_REGROUP_ROWS), S5_REGROUP_ROWS)
        for hf in range(S5_CHUNK // S5_PACKETS):
            base = hf * S5_PACKETS
            vs = [pltpu.bitcast(u_ref[rows, (base + t) * LANES:(base + t + 1) * LANES].astype(o_ref.dtype),
                                jnp.uint32) for t in range(S5_PACKETS)]
            for gq, v in enumerate(_packet_transpose(vs)):
                o_ref[gq, rows, hf * LANES:(hf + 1) * LANES] = pltpu.bitcast(v, o_ref.dtype)
        return carry

    lax.fori_loop(0, u_ref.shape[0] // S5_REGROUP_ROWS, body, 0, unroll=4)


def _s5_group(u):
    g8, rows, wide = u.shape
    tr = _row_tile(rows, ROW_TILE)
    w = wide // S5_PACKETS
    return pl.pallas_call(
        _s5_group_kernel,
        out_shape=jax.ShapeDtypeStruct((g8 * S5_PACKETS, rows, w), BF16),
        grid=(g8, rows // tr),
        in_specs=[pl.BlockSpec((None, tr, wide), lambda g, i: (g, i, 0))],
        out_specs=pl.BlockSpec((S5_PACKETS, tr, w), lambda g, i: (g, i, 0)),
        compiler_params=_cparams("parallel", "parallel"),
        name="s5_group",
    )(u)


def _s5_ungroup_kernel(y_ref, u_ref, ds_ref, o_ref):
    def body(r, carry):
        rows = pl.ds(pl.multiple_of(r * S5_REGROUP_ROWS, S5_REGROUP_ROWS), S5_REGROUP_ROWS)
        for hf in range(S5_CHUNK // S5_PACKETS):
            base = hf * S5_PACKETS
            vs = [pltpu.bitcast(y_ref[gq, rows, hf * LANES:(hf + 1) * LANES], jnp.uint32)
                  for gq in range(S5_PACKETS)]
            for t, v in enumerate(_packet_transpose(vs)):
                cols = slice((base + t) * LANES, (base + t + 1) * LANES)
                y = pltpu.bitcast(v, F32)
                o_ref[rows, cols] = jax.nn.gelu(y + ds_ref[...] * u_ref[rows, cols]).astype(o_ref.dtype)
        return carry

    lax.fori_loop(0, o_ref.shape[0] // S5_REGROUP_ROWS, body, 0, unroll=2)


def _s5_ungroup(y, u, d_skip):
    ng, rows, w = y.shape
    tr = _row_tile(rows, ROW_TILE)
    wide = w * S5_PACKETS
    wide_spec = pl.BlockSpec((None, tr, wide), lambda g, i: (g, i, 0))
    return pl.pallas_call(
        _s5_ungroup_kernel,
        out_shape=jax.ShapeDtypeStruct((ng // S5_PACKETS, rows, wide), BF16),
        grid=(ng // S5_PACKETS, rows // tr),
        in_specs=[pl.BlockSpec((S5_PACKETS, tr, w), lambda g, i: (g, i, 0)), wide_spec,
                  pl.BlockSpec((None, 1, LANES), lambda g, i: (g, 0, 0))],
        out_specs=wide_spec,
        compiler_params=_cparams("parallel", "parallel"),
        name="s5_ungroup",
    )(y, u, d_skip)


S5_SCAN_GROUPS = 2


def _s5_kernel(uc_ref, ul_ref, kl_ref, q_ref, p_ref, a_ref, yc_ref, yl_ref,
               sc_ref, sl_ref, xc_ref, xl_ref, t_ref, *, bsz):
    gs = range(uc_ref.shape[0])
    dd = range(2)
    half = sc_ref.shape[3] // 2
    nch = kl_ref.shape[2]
    for ti in range(S5_CHUNK):
        for to in range(S5_CHUNK):
            for g in gs:
                t_ref[g, ti * nch:(ti + 1) * nch, to * nch:(to + 1) * nch] = kl_ref[g, to - ti + S5_CHUNK - 1]
    for u_ref, s_ref in ((uc_ref, sc_ref), (ul_ref, sl_ref)):
        for g in gs:
            for d in dd:
                s_ref[g, d] = jnp.dot(u_ref[g], q_ref[d, g], preferred_element_type=F32)
    aa = [[a_ref[d, g, 0:1, :] for d in dd] for g in gs]
    ab = [[a_ref[d, g, 1:2, :] for d in dd] for g in gs]

    def run(s_ref, x_ref, carry):
        n = s_ref.shape[2] // bsz

        def step(g, d, j, w):
            rows = pl.ds(pl.multiple_of(j * bsz, bsz), bsz)
            x_ref[g, d, rows, :] = w[:, :half]
            other = jnp.concatenate([w[:, half:], w[:, :half]], axis=1)
            return aa[g][d] * w + ab[g][d] * other + s_ref[g, d, rows, :]

        def body(i, ws):
            return tuple((step(g, 0, i, ws[g][0]), step(g, 1, n - 1 - i, ws[g][1])) for g in gs)

        return lax.fori_loop(0, n, body, carry)

    zero = jnp.zeros((bsz, 2 * half), F32)
    run(sl_ref, xl_ref, run(sc_ref, xc_ref, tuple((zero, zero) for _ in gs)))
    t_m = [t_ref[g].astype(BF16) for g in gs]
    for u_ref, x_ref, y_ref in ((uc_ref, xc_ref, yc_ref), (ul_ref, xl_ref, yl_ref)):
        for g in gs:
            y = jnp.dot(u_ref[g], t_m[g], preferred_element_type=F32)
            for d in dd:
                y = y + lax.dot_general(x_ref[g, d].astype(BF16), p_ref[d, g], (((1,), (1,)), ((), ())),
                                        preferred_element_type=F32)
            y_ref[g] = y


def _s5_scan(uc, ul, k_lag, q_m, p_m, a_m, bsz):
    ng, rc, w = uc.shape
    rl = ul.shape[1]
    st2 = p_m.shape[3]
    nlag, nch = k_lag.shape[1:3]
    gb = S5_SCAN_GROUPS if ng % S5_SCAN_GROUPS == 0 else 1

    def rows_spec(r):
        return pl.BlockSpec((gb, r, w), lambda g: (g, 0, 0))

    return pl.pallas_call(
        functools.partial(_s5_kernel, bsz=bsz),
        out_shape=(jax.ShapeDtypeStruct((ng, rc, w), F32), jax.ShapeDtypeStruct((ng, rl, w), F32)),
        grid=(ng // gb,),
        in_specs=[rows_spec(rc), rows_spec(rl),
                  pl.BlockSpec((gb, nlag, nch, nch), lambda g: (g, 0, 0, 0)),
                  pl.BlockSpec((2, gb, w, 2 * st2), lambda g: (0, g, 0, 0)),
                  pl.BlockSpec((2, gb, w, st2), lambda g: (0, g, 0, 0)),
                  pl.BlockSpec((2, gb, 2, 2 * st2), lambda g: (0, g, 0, 0))],
        out_specs=(rows_spec(rc), rows_spec(rl)),
        scratch_shapes=[pltpu.VMEM((gb, 2, rc, 2 * st2), F32), pltpu.VMEM((gb, 2, rl, 2 * st2), F32),
                        pltpu.VMEM((gb, 2, rc, st2), F32), pltpu.VMEM((gb, 2, rl, st2), F32),
                        pltpu.VMEM((gb, w, w), F32)],
        compiler_params=_cparams("parallel"),
        name="s5_scan",
    )(uc, ul, k_lag, q_m, p_m, a_m)


def _s5_operators(lam_re, lam_im, log_dt, b_re, b_im, c_re, c_im):
    lc = S5_CHUNK
    dt = jnp.exp(log_dt)[..., None]
    mag = jnp.exp(lam_re * dt)
    lb_re, lb_im = mag * jnp.cos(lam_im * dt), mag * jnp.sin(lam_im * dt)
    den = lam_re * lam_re + lam_im * lam_im
    nr, ni = lb_re - 1.0, lb_im
    fr = (nr * lam_re + ni * lam_im) / den
    fi = (ni * lam_re - nr * lam_im) / den
    bb_re = fr[..., None] * b_re - fi[..., None] * b_im
    bb_im = fr[..., None] * b_im + fi[..., None] * b_re
    k = jnp.arange(lc + 1, dtype=F32)[:, None, None, None]
    pmag = jnp.exp(k * (lam_re * dt))
    pw_re, pw_im = pmag * jnp.cos(k * (lam_im * dt)), pmag * jnp.sin(k * (lam_im * dt))
    ngrp, nch = lam_re.shape[1], b_re.shape[3]
    nst = lam_re.shape[2]
    pw_t = (jnp.transpose(pw_re, (1, 2, 0, 3)), jnp.transpose(pw_im, (1, 2, 0, 3)))
    bb_t = (jnp.transpose(bb_re, (0, 1, 3, 2)), jnp.transpose(bb_im, (0, 1, 3, 2)))

    lb_re = pw_t[0][:, :, :, None, :] * bb_t[0][:, :, None] - pw_t[1][:, :, :, None, :] * bb_t[1][:, :, None]
    lb_im = pw_t[0][:, :, :, None, :] * bb_t[1][:, :, None] + pw_t[1][:, :, :, None, :] * bb_t[0][:, :, None]
    kern = jnp.sum(lb_re[:, :, :, :, None, :] * c_re[:, :, None, None]
                   - lb_im[:, :, :, :, None, :] * c_im[:, :, None, None], axis=-1)
    k_lag = jnp.concatenate([jnp.flip(kern[1, :, 1:lc], axis=1), (kern[0, :, 0] + kern[1, :, 0])[:, None],
                             kern[0, :, 1:lc]], axis=1)

    def per_dir(d):
        e_in = (lc - 1 - jnp.arange(lc)) if d == 0 else jnp.arange(lc)
        l_re, l_im = (pw[d][:, e_in][:, :, None, :] for pw in pw_t)
        bt_re, bt_im = (bb[d][:, None] for bb in bb_t)
        q_re = (l_re * bt_re - l_im * bt_im).reshape(ngrp, lc * nch, nst)
        q_im = (l_re * bt_im + l_im * bt_re).reshape(ngrp, lc * nch, nst)
        q_m = jnp.concatenate([q_re, q_im, q_im, q_re], axis=-1)
        e_out = (jnp.arange(lc) + 1) if d == 0 else (lc - jnp.arange(lc))
        l_re, l_im = (pw[d][:, e_out][:, :, None, :] for pw in pw_t)
        cd_re, cd_im = c_re[d][:, None], c_im[d][:, None]
        p_re = (cd_re * l_re - cd_im * l_im).reshape(ngrp, lc * nch, nst)
        p_im = (cd_re * l_im + cd_im * l_re).reshape(ngrp, lc * nch, nst)
        p_m = jnp.concatenate([p_re, -p_im], axis=-1)
        a_re, a_im = pw_re[lc, d], pw_im[lc, d]
        a_m = jnp.stack([jnp.concatenate([a_re] * 4, axis=-1),
                         jnp.concatenate([-a_im, a_im, a_im, -a_im], axis=-1)], axis=1)
        return q_m, p_m, a_m

    q_m, p_m, a_m = (jnp.stack([x, y]) for x, y in zip(per_dir(0), per_dir(1)))
    return k_lag, q_m.astype(BF16), p_m.astype(BF16), a_m


def _s5_glu_kernel(zf_ref, wa_ref, wg_ref, h_ref, gate_ref, o_ref, z_ref):
    n = pl.program_id(2)

    @pl.when(n == 0)
    def _():
        tm = z_ref.shape[0]
        for k in range(zf_ref.shape[0]):
            z_ref[:, k * LANES:(k + 1) * LANES] = zf_ref[k].reshape(tm, LANES)

    z = z_ref[...]
    tn = wa_ref.shape[1]
    sub = _sub_cols(tn)
    for c in range(tn // sub):
        cols = slice(c * sub, (c + 1) * sub)
        a = jnp.dot(z, wa_ref[:, cols], preferred_element_type=F32)
        gt = jnp.dot(z, wg_ref[:, cols], preferred_element_type=F32)
        o_ref[:, cols] = h_ref[:, cols] + gate_ref[:, cols] * (a * jax.nn.sigmoid(gt))


def _s5_glu(z, w, h, mod, k, shared_mod):
    bsz, rows, d = h.shape
    tm = _row_tile(rows, ROW_TILE_STREAM)
    tn = _col_tile(d, 512)
    nn = d // tn
    lc = S5_CHUNK
    col = pl.BlockSpec((None, tm, tn), lambda b, i, n: (b, i, n))
    return pl.pallas_call(
        _s5_glu_kernel,
        out_shape=jax.ShapeDtypeStruct(h.shape, F32),
        grid=(bsz, rows // tm, nn),
        in_specs=[pl.BlockSpec((d // LANES, tm // lc, None, lc, LANES), lambda b, i, n: (0, i, b, 0, 0)),
                  pl.BlockSpec((d, tn), lambda b, i, n: (0, n)),
                  pl.BlockSpec((d, tn), lambda b, i, n: (0, n + nn)),
                  col, pl.BlockSpec((None, None, 1, tn),
                                    lambda b, i, n: (0 if shared_mod else b, k, 0, n))],
        out_specs=col,
        scratch_shapes=[pltpu.VMEM((tm, d), BF16)],
        compiler_params=_cparams("parallel", "parallel", "arbitrary"),
        name="s5_glu",
    )(z, w, w, h, mod)


def _s5_mixer(h, hc, g, mod, mod_c, w_in, lam_re, lam_im, log_dt, b_re, b_im, c_re, c_im,
              d_skip, w_out, with_ctx):
    bsz, n, d = h.shape
    lctx = hc.shape[1] // bsz
    lc = S5_CHUNK
    g8 = d // LANES
    w = w_in.astype(BF16)
    hc3 = hc.reshape(bsz, lctx, d)
    u_l = _s5_proj(h, g, mod, 3, w, False).reshape(g8, n // lc * bsz, lc * LANES)
    u_c = _s5_proj(hc3, g, mod_c, 3, w, True).reshape(g8, lctx // lc * bsz, lc * LANES)
    ops = _s5_operators(lam_re, lam_im, log_dt, b_re, b_im, c_re, c_im)
    yp_c, yp_l = _s5_scan(_s5_group(u_c), _s5_group(u_l), *ops, bsz=bsz)
    ds = d_skip.astype(F32).reshape(g8, 1, LANES)
    w_o = w_out.astype(BF16)
    z_l = _s5_ungroup(yp_l, u_l, ds).reshape(g8, n // lc, bsz, lc, LANES)
    h = _s5_glu(z_l, w_o, h, mod, 5, False)
    if with_ctx:
        z_c = _s5_ungroup(yp_c, u_c, ds).reshape(g8, lctx // lc, bsz, lc, LANES)
        hc = _s5_glu(z_c, w_o, hc3, mod_c, 5, True).reshape(1, bsz * lctx, d)
    return h, hc


def _rmsnorm_kernel(x_ref, g_ref, o_ref):
    x = x_ref[...]
    o_ref[...] = x * lax.rsqrt(jnp.mean(x * x, axis=-1, keepdims=True) + EPS) * g_ref[...]


def _final_norm(x, g):
    bsz, rows, d = x.shape
    tm = _row_tile(rows, ROW_TILE)
    row = pl.BlockSpec((None, tm, d), lambda b, i: (b, i, 0))
    return pl.pallas_call(
        _rmsnorm_kernel,
        out_shape=jax.ShapeDtypeStruct(x.shape, F32),
        grid=(bsz, rows // tm),
        in_specs=[row, pl.BlockSpec((1, d), lambda b, i: (0, 0))],
        out_specs=row,
        compiler_params=_cparams("parallel", "parallel"),
        name="final_norm",
    )(x, g.reshape(1, d))


def kernel(x, c, ctx, c_ctx, w_ada, b_ada, g_norm, w_ffn_in, w_ffn_out, g_final, ml_w_in, ml_b_gate, ml_g_head, ml_w_out, wa_w_in, wa_sink, wa_w_out, s5_w_in, s5_lam_re, s5_lam_im, s5_log_dt, s5_b_re, s5_b_im, s5_c_re, s5_c_im, s5_d_skip, s5_w_out):
    bsz, n, d = x.shape
    depth = w_ada.shape[0]
    lctx = ctx.shape[1]
    n_rows = -(-(bsz + 1) // 16) * 16
    c_rows = jnp.concatenate([c, c_ctx[None], jnp.zeros((n_rows - bsz - 1, d), F32)], axis=0)
    mods = _mod_table(c_rows, w_ada, b_ada).reshape(depth, n_rows, N_MOD, 1, d)
    w_in, w_out = _to_bf16(w_ffn_in), _to_bf16(w_ffn_out)
    h = x
    hc = ctx.reshape(1, bsz * lctx, d)
    for layer in range(depth):
        has_next = layer < depth - 1
        mod, mod_c = mods[layer, :bsz], mods[layer, bsz:bsz + 1]
        g = g_norm[layer]
        h = _ffn(h, g[0], mod, 0, w_in, w_out, layer, 0)
        hc = _ffn(hc, g[0], mod_c, 0, w_in, w_out, layer, 0)
        kind, idx = layer % 3, layer // 3
        if kind == 0:
            h, hc = _mlstm_mixer(h, hc, g[1], mod, mod_c, ml_w_in[idx], ml_b_gate[idx],
                                 ml_g_head[idx], ml_w_out[idx], has_next)
        elif kind == 1:
            h, hc = _wa_mixer(h, hc, g[1], mod, mod_c, wa_w_in[idx], wa_sink[idx],
                              wa_w_out[idx], has_next)
        else:
            h, hc = _s5_mixer(h, hc, g[1], mod, mod_c, s5_w_in[idx], s5_lam_re[idx], s5_lam_im[idx],
                              s5_log_dt[idx], s5_b_re[idx], s5_b_im[idx], s5_c_re[idx],
                              s5_c_im[idx], s5_d_skip[idx], s5_w_out[idx], has_next)
        h = _ffn(h, g[2], mod, 6, w_in, w_out, layer, 1)
        if has_next:
            hc = _ffn(hc, g[2], mod_c, 6, w_in, w_out, layer, 1)
    return _final_norm(h, g_final)
```
